```python
import math
import jax, jax.numpy as jnp
from jax import lax
import numpy as np

D_MODEL = 2048
BATCH = 2
SEQ = 8192
DEPTH = 2

GRID_W = 64
CTX_LEN = 256
HEAD_DIM = 128
N_MIXERS = 4
MIX_W = D_MODEL // N_MIXERS
H_A = MIX_W // HEAD_DIM
H_B = MIX_W // HEAD_DIM
KV_B = H_B // 2
G_B = H_B // KV_B
G_C = MIX_W // HEAD_DIM
H_D = MIX_W // HEAD_DIM
CONV_W = 5
CHUNK_A = 64
CHUNK_D = 64
CHUNK_C = 128
Q_BLOCK = 128
ROPE_THETA = 10000.0
N_EXPERTS = 16
N_EXPERT_GROUPS = 4
EXPERTS_PER_GROUP = N_EXPERTS // N_EXPERT_GROUPS
TOP_K = 2
D_FF_EXPERT = 1408
MOE_BLOCK = 128
EPS = 1e-6
GATE_CLIP = 30.0
SPLIT_SIZES = (3 * MIX_W, MIX_W, 2 * H_A, 2 * H_A, MIX_W, 2 * KV_B * HEAD_DIM, MIX_W, MIX_W, MIX_W, MIX_W, MIX_W, 2 * MIX_W)
IN_COLS = sum(SPLIT_SIZES)

kernel_name = 'hybrid_dit_deltanet_gqa_gmlp_hgrn2_moe'


def _layernorm(x, gain=None, bias=None):
    x32 = x.astype(jnp.float32)
    xc = x32 - jnp.mean(x32, -1, keepdims=True)
    y = xc * lax.rsqrt(jnp.mean(xc * xc, -1, keepdims=True) + EPS)
    if gain is not None:
        y = y * gain.astype(jnp.float32) + bias.astype(jnp.float32)
    return y.astype(x.dtype)


def _rmsnorm(x, gain):
    x32 = x.astype(jnp.float32)
    y = x32 * lax.rsqrt(jnp.mean(x32 * x32, -1, keepdims=True) + EPS)
    return (y * gain.astype(jnp.float32)).astype(x.dtype)


def _l2norm(x):
    return x * lax.rsqrt(jnp.sum(x * x, -1, keepdims=True) + EPS)


def _modulate(x, shift, scale):
    return _layernorm(x) * (1.0 + scale) + shift


def _masked_exp(diff, mask):
    return jnp.where(mask, jnp.exp(jnp.where(mask, diff, 0.0)), 0.0)


def _centred_dwconv(x, w):
    k = w.shape[0]
    return lax.conv_general_dilated(x, w[:, None, :].astype(x.dtype), (1,), [(k // 2, k // 2)],
                                    dimension_numbers=('NWC', 'WIO', 'NWC'), feature_group_count=x.shape[-1])


def _to_chunks(a, c):
    b, t, h, d = a.shape
    return jnp.moveaxis(a.reshape(b, t // c, c, h, d), (1, 3), (0, 2))


def _from_chunks(a):
    n, b, h, c, d = a.shape
    return jnp.moveaxis(a, (0, 2), (1, 3)).reshape(b, n * c, h, d)


def _ctx_then_latent(run, ctx_in, lat_in, s0, reverse):
    flip = (lambda a: jnp.flip(a, axis=1)) if reverse else (lambda a: a)
    o_c, s_c = run(*[flip(a) for a in ctx_in], s0)
    o_l, _ = run(*[flip(a) for a in lat_in], s_c)
    return flip(o_c), flip(o_l)


def _gated_out(o, gate, gain):
    b, t, h, d = o.shape
    y = _rmsnorm(o, gain) * jax.nn.silu(gate.astype(jnp.float32).reshape(b, t, h, d))
    return y.reshape(b, t, h * d).astype(gate.dtype)


def _gated_delta_chunked(q, k, v, beta, g, s0):
    c = CHUNK_A
    dv = v.shape[-1]
    qc = _to_chunks(q * q.shape[-1] ** -0.5, c)
    kc, vc = _to_chunks(k, c), _to_chunks(v, c)
    bc = _to_chunks(beta[..., None], c)
    gcum = jnp.cumsum(_to_chunks(g[..., None], c), axis=3)
    incl = jnp.tril(jnp.ones((c, c), bool))
    strict = jnp.tril(jnp.ones((c, c), bool), -1)
    decay = _masked_exp(gcum - jnp.swapaxes(gcum, -1, -2), incl)
    kb = kc * bc
    lower = jnp.where(strict, jnp.einsum('nbhid,nbhjd->nbhij', kb, kc) * decay, 0.0)
    a_mat = lower + jnp.eye(c, dtype=jnp.float32)
    rhs = jnp.concatenate([vc * bc, kb * jnp.exp(gcum)], axis=-1)
    sol = lax.linalg.triangular_solve(a_mat, rhs, left_side=True, lower=True, unit_diagonal=True)
    u_val, w_dec = sol[..., :dv], sol[..., dv:]
    attn = jnp.einsum('nbhid,nbhjd->nbhij', qc, kc) * decay
    q_dec = qc * jnp.exp(gcum)
    g_last = gcum[..., -1:, :]
    k_dec = kc * jnp.exp(g_last - gcum)

    def step(state, xs):
        u_, w_, at_, qd_, kd_, gl_ = xs
        v_new = u_ - jnp.einsum('bhcd,bhde->bhce', w_, state)
        o = jnp.einsum('bhcd,bhde->bhce', qd_, state) + jnp.einsum('bhij,bhje->bhie', at_, v_new)
        state = state * jnp.exp(gl_) + jnp.einsum('bhcd,bhce->bhde', kd_, v_new)
        return state, o

    s_final, o = lax.scan(step, s0, (u_val, w_dec, attn, q_dec, k_dec, g_last))
    return _from_chunks(o), s_final


def _delta_prep(qkv, beta_raw, dec_raw, conv_w, a_log, dt_bias):
    b, t, _ = qkv.shape
    qkv = jax.nn.silu(_centred_dwconv(qkv, conv_w)).astype(jnp.float32)
    q, k, v = [a.reshape(b, t, H_A, HEAD_DIM) for a in jnp.split(qkv, 3, axis=-1)]
    beta = jax.nn.sigmoid(beta_raw.astype(jnp.float32)).reshape(b, t, 2, H_A)
    g = -jnp.exp(a_log.astype(jnp.float32)) * jax.nn.softplus(
        dec_raw.astype(jnp.float32).reshape(b, t, 2, H_A) + dt_bias.astype(jnp.float32))
    return _l2norm(q), _l2norm(k), v, beta, g


def _mixer_a(lat, ctx, conv_w, a_log, dt_bias, gain, with_ctx):
    qkv_l, gate_l, beta_l, dec_l = lat
    qkv_c, gate_c, beta_c, dec_c = ctx
    pl = _delta_prep(qkv_l, beta_l, dec_l, conv_w, a_log, dt_bias)
    pc = _delta_prep(qkv_c, beta_c, dec_c, conv_w, a_log, dt_bias)
    s0 = jnp.zeros((qkv_l.shape[0], H_A, HEAD_DIM, HEAD_DIM), jnp.float32)
    dirs = [_ctx_then_latent(_gated_delta_chunked,
                             pc[:3] + (pc[3][:, :, d], pc[4][:, :, d]),
                             pl[:3] + (pl[3][:, :, d], pl[4][:, :, d]), s0, d == 1) for d in range(2)]
    y_l = _gated_out(dirs[0][1] + dirs[1][1], gate_l, gain)
    y_c = _gated_out(dirs[0][0] + dirs[1][0], gate_c, gain) if with_ctx else None
    return y_l, y_c


def _axial_rope(n_tokens):
    rows = n_tokens // GRID_W
    row = jnp.repeat(jnp.arange(rows, dtype=jnp.float32), GRID_W)
    col = jnp.tile(jnp.arange(GRID_W, dtype=jnp.float32), rows)
    n_freq = HEAD_DIM // 4
    inv_freq = ROPE_THETA ** (-jnp.arange(n_freq, dtype=jnp.float32) / n_freq)
    ang = jnp.concatenate([row[:, None] * inv_freq, col[:, None] * inv_freq], axis=-1)
    return jnp.cos(ang), jnp.sin(ang)


def _apply_rope(x, cos, sin):
    x1, x2 = jnp.split(x, 2, axis=-1)
    cos = cos[None, :, None, :].astype(x.dtype)
    sin = sin[None, :, None, :].astype(x.dtype)
    return jnp.concatenate([x1 * cos - x2 * sin, x2 * cos + x1 * sin], axis=-1)


def _softmax_attend(q, k, v):
    s = jnp.einsum('bqkgd,bskd->bkgqs', q, k, preferred_element_type=jnp.float32) * (HEAD_DIM ** -0.5)
    p = jax.nn.softmax(s, axis=-1).astype(v.dtype)
    return jnp.einsum('bkgqs,bskd->bqkgd', p, v)


def _mixer_b(q_l, kv_l, q_c, kv_c, qk_gain, with_ctx):
    b, s, _ = q_l.shape

    def heads(q, kv):
        t = q.shape[1]
        q = _rmsnorm(q.reshape(b, t, H_B, HEAD_DIM), qk_gain[0])
        k, v = jnp.split(kv.reshape(b, t, 2 * KV_B, HEAD_DIM), 2, axis=2)
        return q, _rmsnorm(k, qk_gain[1]), v

    q, k, v = heads(q_l, kv_l)
    cos, sin = _axial_rope(s)
    q, k = _apply_rope(q, cos, sin), _apply_rope(k, cos, sin)
    qc, kc, vc = heads(q_c, kv_c)
    k_all = jnp.concatenate([k, kc], axis=1)
    v_all = jnp.concatenate([v, vc], axis=1)
    n_blk = s // Q_BLOCK
    q_blocks = jnp.moveaxis(q.reshape(b, n_blk, Q_BLOCK, KV_B, G_B, HEAD_DIM), 1, 0)
    o = lax.map(lambda qb: _softmax_attend(qb, k_all, v_all), q_blocks)
    y_l = jnp.moveaxis(o, 0, 1).reshape(b, s, MIX_W)
    y_c = None
    if with_ctx:
        t_c = q_c.shape[1]
        y_c = _softmax_attend(qc.reshape(b, t_c, KV_B, G_B, HEAD_DIM), kc, vc).reshape(b, t_c, MIX_W)
    return y_l, y_c


def _mixer_c(u, v, ws, bs):
    b, t, _ = u.shape
    u = jax.nn.gelu(u)
    v = _layernorm(jax.nn.gelu(v))
    vb = v.reshape(b, t // CHUNK_C, CHUNK_C, G_C, MIX_W // G_C)
    vm = jnp.einsum('gpq,bnqgc->bnpgc', ws, vb) + bs.T[:, :, None]
    return u * vm.reshape(b, t, MIX_W)


def _hgrn2_chunked(q, k, v, log_f, s0):
    c = CHUNK_D
    qc, kc, vc = _to_chunks(q, c), _to_chunks(k, c), _to_chunks(v, c)
    bcum = jnp.cumsum(_to_chunks(log_f, c), axis=3)
    incl = jnp.tril(jnp.ones((c, c), bool))[:, :, None]

    def step(state, xs):
        q_, k_, v_, b_ = xs
        rel = _masked_exp(b_[:, :, :, None, :] - b_[:, :, None, :, :], incl)
        attn = jnp.einsum('bhtd,bhtsd,bhsd->bhts', q_, rel, k_)
        b_last = b_[:, :, -1:, :]
        o = jnp.einsum('bhtd,bhde->bhte', q_ * jnp.exp(b_), state) + jnp.einsum('bhts,bhse->bhte', attn, v_)
        state = state * jnp.exp(b_last)[:, :, 0, :, None] + jnp.einsum('bhsd,bhse->bhde', k_ * jnp.exp(b_last - b_), v_)
        return state, o

    s_final, o = lax.scan(step, s0, (qc, kc, vc, bcum))
    return _from_chunks(o), s_final


def _hgrn2_prep(q, i, f, lb):
    b, t, _ = q.shape
    z = jnp.clip(f.astype(jnp.float32).reshape(b, t, 2, H_D, HEAD_DIM), -GATE_CLIP, GATE_CLIP)
    lbh = lb.reshape(H_D, HEAD_DIM)
    log_f = jnp.log(lbh + (1.0 - lbh) * jax.nn.sigmoid(z))
    k = (1.0 - lbh) * jax.nn.sigmoid(-z)
    return (q.astype(jnp.float32).reshape(b, t, H_D, HEAD_DIM),
            i.astype(jnp.float32).reshape(b, t, H_D, HEAD_DIM), k, log_f)


def _mixer_d(lat, ctx, lb, gain, with_ctx):
    q_l, i_l, g_l, f_l = lat
    q_c, i_c, g_c, f_c = ctx
    pl = _hgrn2_prep(q_l, i_l, f_l, lb)
    pc = _hgrn2_prep(q_c, i_c, f_c, lb)
    s0 = jnp.zeros((q_l.shape[0], H_D, HEAD_DIM, HEAD_DIM), jnp.float32)
    dirs = [_ctx_then_latent(_hgrn2_chunked,
                             (pc[0], pc[2][:, :, d], pc[1], pc[3][:, :, d]),
                             (pl[0], pl[2][:, :, d], pl[1], pl[3][:, :, d]), s0, d == 1) for d in range(2)]
    y_l = _gated_out(dirs[0][1] + dirs[1][1], g_l, gain)
    y_c = _gated_out(dirs[0][0] + dirs[1][0], g_c, gain) if with_ctx else None
    return y_l, y_c


def _hybrid_mixer(h_l, h_c, w_in, conv_a, a_log, dt_bias, norm_a, qk_norm_b, ws_c, bs_c, lb, norm_d, w_out, with_ctx):
    cuts = np.cumsum(SPLIT_SIZES)[:-1].tolist()
    pl = jnp.split(h_l @ w_in, cuts, axis=-1)
    pc = jnp.split(h_c @ w_in, cuts, axis=-1)
    ya_l, ya_c = _mixer_a(pl[0:4], pc[0:4], conv_a, a_log, dt_bias, norm_a, with_ctx)
    yb_l, yb_c = _mixer_b(pl[4], pl[5], pc[4], pc[5], qk_norm_b, with_ctx)
    yc_l = _mixer_c(pl[6], pl[7], ws_c, bs_c)
    yd_l, yd_c = _mixer_d(pl[8:12], pc[8:12], lb, norm_d, with_ctx)
    y_l = jnp.concatenate([ya_l, yb_l, yc_l, yd_l], axis=-1) @ w_out
    y_c = None
    if with_ctx:
        yc_c = _mixer_c(pc[6], pc[7], ws_c, bs_c)
        y_c = jnp.concatenate([ya_c, yb_c, yc_c, yd_c], axis=-1) @ w_out
    return y_l, y_c


def _route(h, w_router, b_router):
    t = h.shape[0]
    scores = jax.nn.sigmoid(jnp.dot(h, w_router, preferred_element_type=jnp.float32))
    sel = (scores + b_router.astype(jnp.float32)).reshape(t, N_EXPERT_GROUPS, EXPERTS_PER_GROUP)
    group_score = jnp.sum(lax.top_k(sel, TOP_K)[0], axis=-1)
    best = jnp.argmax(group_score, axis=-1)
    in_group = jnp.take_along_axis(sel, best[:, None, None], axis=1)[:, 0]
    _, local = lax.top_k(in_group, TOP_K)
    idx = best[:, None] * EXPERTS_PER_GROUP + local
    w = jnp.take_along_axis(scores, idx, axis=1)
    return idx, w / jnp.sum(w, axis=-1, keepdims=True)


def _moe(h, idx, gate_w, w1, w3, w2):
    t, d = h.shape
    n_assign = t * TOP_K
    e_flat = idx.reshape(-1)
    tok_flat = jnp.repeat(jnp.arange(t), TOP_K)
    order = jnp.argsort(e_flat)
    e_s, tok_s = e_flat[order], tok_flat[order]
    w_s = gate_w.reshape(-1)[order]
    counts = jnp.bincount(e_flat, length=N_EXPERTS)
    starts = jnp.cumsum(counts) - counts
    padded = (counts + MOE_BLOCK - 1) // MOE_BLOCK * MOE_BLOCK
    pad_end = jnp.cumsum(padded)
    pad_start = pad_end - padded
    dest = pad_start[e_s] + jnp.arange(n_assign) - starts[e_s]
    n_blocks = -(-n_assign // MOE_BLOCK) + N_EXPERTS
    buf = jnp.zeros((n_blocks * MOE_BLOCK, d), h.dtype).at[dest].set(h[tok_s])
    block_expert = jnp.minimum(jnp.searchsorted(pad_end, jnp.arange(n_blocks) * MOE_BLOCK, side='right'), N_EXPERTS - 1)

    def expert_block(args):
        xb, e = args
        return (jax.nn.silu(xb @ w1[e]) * (xb @ w3[e])) @ w2[e]

    y = lax.map(expert_block, (buf.reshape(n_blocks, MOE_BLOCK, d), block_expert)).reshape(-1, d)
    return jnp.zeros((t, d), h.dtype).at[tok_s].add(w_s[:, None].astype(h.dtype) * y[dest])


def setup_inputs(seed: int = 0) -> dict:
    key = jax.random.key(seed)
    ks = jax.random.split(key, 24)
    f32 = jnp.float32
    d, l = D_MODEL, DEPTH
    beta_init = (8.0 * DEPTH) ** -0.25

    def nrm(k, shape, scale):
        return jax.random.normal(k, shape, f32) * scale

    dt = jnp.exp(jax.random.uniform(ks[11], (l, 2, H_A), f32, math.log(1e-3), math.log(1e-1)))
    return {
        'x': nrm(ks[0], (BATCH, SEQ, d), 1.0),
        'c': nrm(ks[1], (BATCH, d), 1.0),
        'ctx': nrm(ks[2], (BATCH, CTX_LEN, d), 1.0),
        'c_ctx': nrm(ks[3], (d,), 1.0),
        'w_ada': nrm(ks[4], (l, d, 6 * d), d ** -0.5),
        'b_ada': nrm(ks[5], (l, 6 * d), 0.02),
        'ln_g': 1.0 + nrm(ks[6], (l, 2, d), 0.02),
        'ln_b': nrm(ks[7], (l, 2, d), 0.02),
        'w_in': nrm(ks[8], (l, d, IN_COLS), d ** -0.5),
        'conv_a': nrm(ks[9], (l, CONV_W, 3 * MIX_W), CONV_W ** -0.5),
        'a_log': jnp.log(jax.random.uniform(ks[10], (l, 2, H_A), f32, 1.0, 16.0)),
        'dt_bias': dt + jnp.log(-jnp.expm1(-dt)),
        'norm_a': 1.0 + nrm(ks[12], (l, HEAD_DIM), 0.02),
        'qk_norm_b': 1.0 + nrm(ks[13], (l, 2, HEAD_DIM), 0.02),
        'ws_c': nrm(ks[14], (l, G_C, CHUNK_C, CHUNK_C), CHUNK_C ** -0.5),
        'bs_c': 1.0 + nrm(ks[15], (l, G_C, CHUNK_C), 0.02),
        'lb_d': nrm(ks[16], (l, MIX_W), 1.0),
        'norm_d': 1.0 + nrm(ks[17], (l, HEAD_DIM), 0.02),
        'w_out': nrm(ks[18], (l, N_MIXERS * MIX_W, d), (N_MIXERS * MIX_W) ** -0.5 * beta_init),
        'w_router': nrm(ks[19], (d, N_EXPERTS), d ** -0.5),
        'b_router': nrm(ks[20], (N_EXPERTS,), 0.01),
        'w1': nrm(ks[21], (l, N_EXPERTS, d, D_FF_EXPERT), d ** -0.5),
        'w3': nrm(ks[22], (l, N_EXPERTS, d, D_FF_EXPERT), d ** -0.5),
        'w2': nrm(ks[23], (l, N_EXPERTS, D_FF_EXPERT, d), D_FF_EXPERT ** -0.5 * beta_init),
    }


def reference(x, c, ctx, c_ctx, w_ada, b_ada, ln_g, ln_b, w_in, conv_a, a_log, dt_bias, norm_a, qk_norm_b,
              ws_c, bs_c, lb_d, norm_d, w_out, w_router, b_router, w1, w3, w2):
    alpha = (2.0 * DEPTH) ** 0.25
    b, s, d = x.shape
    soft = jax.nn.softmax(lb_d.astype(jnp.float32), axis=0)
    lb_all = jnp.cumsum(soft, axis=0) - soft[0]
    x_l, x_c = x, ctx
    for l in range(DEPTH):
        with_ctx = l < DEPTH - 1
        mod_l = [m[:, None, :] for m in jnp.split(jax.nn.silu(c) @ w_ada[l] + b_ada[l], 6, axis=-1)]
        mod_c = jnp.split(jax.nn.silu(c_ctx) @ w_ada[l] + b_ada[l], 6, axis=-1)
        h_l = _modulate(x_l, mod_l[0], mod_l[1])
        h_c = _modulate(x_c, mod_c[0], mod_c[1])
        y_l, y_c = _hybrid_mixer(h_l, h_c, w_in[l], conv_a[l], a_log[l], dt_bias[l], norm_a[l], qk_norm_b[l],
                                 ws_c[l], bs_c[l], lb_all[l], norm_d[l], w_out[l], with_ctx)
        x_l = _layernorm(alpha * x_l + mod_l[2] * y_l, ln_g[l, 0], ln_b[l, 0])
        h_l = _modulate(x_l, mod_l[3], mod_l[4])
        tokens = h_l.reshape(b * s, d)
        if with_ctx:
            x_c = _layernorm(alpha * x_c + mod_c[2] * y_c, ln_g[l, 0], ln_b[l, 0])
            h_c = _modulate(x_c, mod_c[3], mod_c[4])
            tokens = jnp.concatenate([tokens, h_c.reshape(-1, d)], axis=0)
        idx, gw = _route(tokens, w_router, b_router)
        f = _moe(tokens, idx, gw, w1[l], w3[l], w2[l])
        x_l = _layernorm(alpha * x_l + mod_l[5] * f[: b * s].reshape(b, s, d), ln_g[l, 1], ln_b[l, 1])
        if with_ctx:
            x_c = _layernorm(alpha * x_c + mod_c[5] * f[b * s:].reshape(x_c.shape), ln_g[l, 1], ln_b[l, 1])
    return x_l
```

```python
import functools
import math

import numpy as np
import jax
import jax.numpy as jnp
from jax import lax
from jax.experimental import pallas as pl
from jax.experimental.pallas import tpu as pltpu

F32 = jnp.float32
BF16 = jnp.bfloat16
HIGHEST = lax.Precision.HIGHEST

HEAD_DIM = 128
N_MIXERS = 4
CONV_W = 5
CHUNK = 64
STEP_ROWS = 2 * CHUNK
CHUNK_C = 128
GRID_W = 64
ROPE_THETA = 10000.0
N_EXPERT_GROUPS = 4
TOP_K = 2
EPS = 1e-6
GATE_CLIP = 30.0
N_LEVELS = 6
HALO = 8
V7X_VMEM_LIMIT = 56 * 1024 * 1024


def _pick(n, cands):
    for c in cands:
        if n % c == 0:
            return c
    raise ValueError(f"no tile in {cands} divides {n}")


def _cparams(sem, vmem=None):
    return pltpu.CompilerParams(dimension_semantics=sem, vmem_limit_bytes=vmem or V7X_VMEM_LIMIT)


def _dot(a, b, precision=None):
    return jnp.dot(a, b, preferred_element_type=F32, precision=precision)


def _dot_nt(a, b, precision=None):
    return lax.dot_general(a, b, (((1,), (1,)), ((), ())), preferred_element_type=F32, precision=precision)


def _dot_tn(a, b, precision=None):
    return lax.dot_general(a, b, (((0,), (0,)), ((), ())), preferred_element_type=F32, precision=precision)


def _sigmoid(x):
    return 1.0 / (1.0 + jnp.exp(-x))


def _silu(x):
    return x * _sigmoid(x)


def _gelu_tanh(x):
    return 0.5 * x * (1.0 + jnp.tanh(math.sqrt(2.0 / math.pi) * (x + 0.044715 * (x * x * x))))


def _ln(x):
    mu = jnp.mean(x, axis=-1, keepdims=True)
    xc = x - mu
    return xc * lax.rsqrt(jnp.mean(xc * xc, axis=-1, keepdims=True) + EPS)


def _mod_row(i, nb, nctx, n_batch):
    return jnp.where(i % nb < nctx, n_batch, i // nb)


def _mod_kernel(c_ref, w_ref, b_ref, o_ref):
    c = c_ref[...]
    o_ref[0] = _dot(_silu(c), w_ref[0], HIGHEST) + b_ref[0]


def _mod_vectors(cc, w_ada, b_ada):
    n_layers, d, n6 = w_ada.shape
    tn = _pick(n6, (1024, 512, 128))
    return pl.pallas_call(
        _mod_kernel,
        grid=(n_layers, n6 // tn),
        in_specs=[pl.BlockSpec((8, d), lambda l, j: (0, 0)),
                  pl.BlockSpec((1, d, tn), lambda l, j: (l, 0, j)),
                  pl.BlockSpec((1, 1, tn), lambda l, j: (l, 0, j))],
        out_specs=pl.BlockSpec((1, 8, tn), lambda l, j: (l, 0, j)),
        out_shape=jax.ShapeDtypeStruct((n_layers, 8, n6), F32),
        compiler_params=_cparams(("parallel", "parallel")),
        name="mod_vectors",
    )(cc, w_ada, b_ada.reshape(n_layers, 1, n6))


def _lnmod_kernel(x_ref, m_ref, h_ref):
    y = _ln(x_ref[...])
    h_ref[...] = (y * (1.0 + m_ref[0, 1:2, :]) + m_ref[0, 0:1, :]).astype(h_ref.dtype)


def _ln_modulate(x, mods, tm, nb, nctx, n_batch):
    t, d = x.shape
    return pl.pallas_call(
        _lnmod_kernel,
        grid=(t // tm,),
        in_specs=[pl.BlockSpec((tm, d), lambda i: (i, 0)),
                  pl.BlockSpec((1, 6, d), lambda i: (_mod_row(i, nb, nctx, n_batch), 0, 0))],
        out_specs=pl.BlockSpec((tm, d), lambda i: (i, 0)),
        out_shape=jax.ShapeDtypeStruct((t, d), BF16),
        compiler_params=_cparams(("parallel",)),
        name="ln_modulate",
    )(x, mods)


def _matmul_kernel(a_ref, w_ref, o_ref):
    o_ref[...] = _dot(a_ref[...], w_ref[...]).astype(o_ref.dtype)


def _matmul(a, w, out_dtype=F32):
    m, k = a.shape
    n = w.shape[1]
    tm = _pick(m, (1536, 1024, 768, 512, 256, 128))
    tn = _pick(n, (1152, 1024, 768, 512, 256, 128))
    return pl.pallas_call(
        _matmul_kernel,
        grid=(m // tm, n // tn),
        in_specs=[pl.BlockSpec((tm, k), lambda i, j: (i, 0)),
                  pl.BlockSpec((k, tn), lambda i, j: (0, j))],
        out_specs=pl.BlockSpec((tm, tn), lambda i, j: (i, j)),
        out_shape=jax.ShapeDtypeStruct((m, n), out_dtype),
        compiler_params=_cparams(("parallel", "parallel")),
        name="in_proj",
    )(a, w)


def _a_prep_kernel(x_ref, xp_ref, xn_ref, sm_ref, cw_ref, na_ref, dtb_ref, trf_ref, trr_ref,
                   q_ref, k_ref, v_ref, gc_ref, gr_ref, ext_ref, *, tm, nb, nctx, mixw):
    i = pl.program_id(0)
    j = i % nb
    first = jnp.logical_or(j == 0, j == nctx)
    last = jnp.logical_or(j == nctx - 1, j == nb - 1)
    ext_ref[HALO:HALO + tm, :] = x_ref[...]
    ext_ref[0:HALO, :] = jnp.where(first, 0.0, xp_ref[...])
    ext_ref[HALO + tm:2 * HALO + tm, :] = jnp.where(last, 0.0, xn_ref[...])
    off = HALO - CONV_W // 2
    acc = cw_ref[0:1, :] * ext_ref[off:off + tm, :]
    for t in range(1, CONV_W):
        acc = acc + cw_ref[t:t + 1, :] * ext_ref[off + t:off + t + tm, :]
    y = _silu(acc)
    n_heads = mixw // HEAD_DIM
    for h in range(n_heads):
        for part, ref, scale in ((0, q_ref, HEAD_DIM ** -0.5), (1, k_ref, 1.0)):
            c0 = part * mixw + h * HEAD_DIM
            xh = y[:, c0:c0 + HEAD_DIM]
            nrm = lax.rsqrt(jnp.sum(xh * xh, axis=-1, keepdims=True) + EPS)
            ref[:, h * HEAD_DIM:(h + 1) * HEAD_DIM] = xh * (nrm * scale)
    v_ref[...] = y[:, 2 * mixw:3 * mixw]
    sm = sm_ref[...]
    lane = lax.broadcasted_iota(jnp.int32, sm.shape, 1)
    zz = sm + dtb_ref[...]
    softplus = jnp.maximum(zz, 0.0) + jnp.log(1.0 + jnp.exp(-jnp.abs(zz)))
    g = jnp.where(jnp.logical_and(lane >= 2 * n_heads, lane < 4 * n_heads), na_ref[...] * softplus, 0.0)
    cs_f = _dot(trf_ref[...], g, HIGHEST)
    cs_r = _dot(trr_ref[...], g, HIGHEST)
    gcum = jnp.where(lane < 3 * n_heads, cs_f, cs_r)
    gc_ref[...] = jnp.where(lane < 2 * n_heads, _sigmoid(sm), gcum)
    gr_ref[...] = gcum.T[2 * n_heads:4 * n_heads, :]


def _a_prep(p, conv_w, neg_a, dtb, tm, nb, nctx, mixw, col_small):
    t = p.shape[0]
    c3 = 3 * mixw
    n_heads = mixw // HEAD_DIM
    r = np.arange(tm)
    same = (r[:, None] // CHUNK) == (r[None, :] // CHUNK)
    tri_f = jnp.asarray((same & (r[None, :] <= r[:, None])).astype(np.float32))
    tri_r = jnp.asarray((same & (r[None, :] >= r[:, None])).astype(np.float32))
    hb = tm // HALO
    last_hb = t // HALO - 1
    kern = functools.partial(_a_prep_kernel, tm=tm, nb=nb, nctx=nctx, mixw=mixw)
    return pl.pallas_call(
        kern,
        grid=(t // tm,),
        in_specs=[pl.BlockSpec((tm, c3), lambda i: (i, 0)),
                  pl.BlockSpec((HALO, c3), lambda i: (jnp.maximum(i * hb - 1, 0), 0)),
                  pl.BlockSpec((HALO, c3), lambda i: (jnp.minimum((i + 1) * hb, last_hb), 0)),
                  pl.BlockSpec((tm, 128), lambda i: (i, col_small // 128)),
                  pl.BlockSpec((CONV_W, c3), lambda i: (0, 0)),
                  pl.BlockSpec((1, 128), lambda i: (0, 0)),
                  pl.BlockSpec((1, 128), lambda i: (0, 0)),
                  pl.BlockSpec((tm, tm), lambda i: (0, 0)),
                  pl.BlockSpec((tm, tm), lambda i: (0, 0))],
        out_specs=[pl.BlockSpec((tm, mixw), lambda i: (i, 0)),
                   pl.BlockSpec((tm, mixw), lambda i: (i, 0)),
                   pl.BlockSpec((tm, mixw), lambda i: (i, 0)),
                   pl.BlockSpec((tm, 128), lambda i: (i, 0)),
                   pl.BlockSpec((2 * n_heads, tm), lambda i: (0, i))],
        out_shape=[jax.ShapeDtypeStruct((t, mixw), F32)] * 3
        + [jax.ShapeDtypeStruct((t, 128), F32), jax.ShapeDtypeStruct((2 * n_heads, t), F32)],
        scratch_shapes=[pltpu.VMEM((tm + 2 * HALO, c3), F32)],
        compiler_params=_cparams(("parallel",)),
        name="deltanet_prep",
    )(p, p, p, p, conv_w, neg_a, dtb, tri_f, tri_r)


def _unit_tri_inverse(low, eye):
    p = eye - low
    m = low
    for _ in range(N_LEVELS - 1):
        m = _dot(m, m, HIGHEST)
        p = p + _dot(p, m, HIGHEST)
    return p


def _delta_chunk(q, k, v, beta, gc, gr, s, incl, strict, eye, last_idx):
    a = gc - gr
    decay = jnp.where(incl, jnp.exp(jnp.where(incl, a, 0.0)), 0.0)
    glast = gc[last_idx:last_idx + 1, :]
    eg = jnp.exp(gc)
    kb = k * beta
    kbf = k.astype(BF16)
    low = jnp.where(strict, _dot_nt(kb.astype(BF16), kbf) * decay, 0.0)
    tinv = _unit_tri_inverse(low, eye)
    rhs = jnp.concatenate([v * beta, kb * eg], axis=1)
    sol = _dot(tinv, rhs, HIGHEST)
    hd = q.shape[1]
    u, w = sol[:, :hd], sol[:, hd:]
    attn = _dot_nt(q.astype(BF16), kbf) * decay
    wq = jnp.concatenate([w, q * eg], axis=0).astype(BF16)
    r = _dot(wq, s.astype(BF16))
    c = q.shape[0]
    v_new = u - r[:c]
    vnb = v_new.astype(BF16)
    o = r[c:] + _dot(attn.astype(BF16), vnb)
    kd = (k * jnp.exp(glast - gc)).astype(BF16)
    s_new = s * jnp.exp(glast) + _dot_tn(kd, vnb)
    return o, s_new


def _a_scan_kernel(qf, kf, vf, gcf, grf, qr, kr, vr, gcr, grr, of_ref, or_ref, s_ref, *, n_heads):
    n = pl.program_id(1)

    @pl.when(n == 0)
    def _():
        s_ref[...] = jnp.zeros_like(s_ref)

    row = lax.broadcasted_iota(jnp.int32, (CHUNK, CHUNK), 0)
    col = lax.broadcasted_iota(jnp.int32, (CHUNK, CHUNK), 1)
    eye = (row == col).astype(F32)
    for d, (q_ref, k_ref, v_ref, gc_ref, gr_ref, o_ref) in enumerate(
            ((qf, kf, vf, gcf, grf, of_ref), (qr, kr, vr, gcr, grr, or_ref))):
        incl = (col <= row) if d == 0 else (col >= row)
        strict = (col < row) if d == 0 else (col > row)
        last_idx = CHUNK - 1 if d == 0 else 0
        for ci in ((0, 1) if d == 0 else (1, 0)):
            r0 = ci * CHUNK
            for h in range(n_heads):
                c0 = h * HEAD_DIM
                g_beta = gc_ref[r0:r0 + CHUNK, d * n_heads + h:d * n_heads + h + 1]
                g_cum = gc_ref[r0:r0 + CHUNK, (2 + d) * n_heads + h:(2 + d) * n_heads + h + 1]
                g_row = gr_ref[d * n_heads + h:d * n_heads + h + 1, r0:r0 + CHUNK]
                o, s_new = _delta_chunk(q_ref[r0:r0 + CHUNK, c0:c0 + HEAD_DIM],
                                        k_ref[r0:r0 + CHUNK, c0:c0 + HEAD_DIM],
                                        v_ref[r0:r0 + CHUNK, c0:c0 + HEAD_DIM],
                                        g_beta, g_cum, g_row, s_ref[d, h], incl, strict, eye, last_idx)
                o_ref[r0:r0 + CHUNK, c0:c0 + HEAD_DIM] = o
                s_ref[d, h] = s_new


def _seq_block_maps(nsb, nctx_sb):
    def fwd(b, n):
        return b * nsb + n

    def rev(b, n):
        return b * nsb + jnp.where(n < nctx_sb, nctx_sb - 1 - n, nsb - 1 - (n - nctx_sb))

    return fwd, rev


def _a_scan(q, k, v, gc, gr, n_batch, tb, ctx_len, mixw):
    t = q.shape[0]
    n_heads = mixw // HEAD_DIM
    nsb = tb // STEP_ROWS
    fwd, rev = _seq_block_maps(nsb, ctx_len // STEP_ROWS)
    in_specs = []
    for m in (fwd, rev):
        in_specs += [pl.BlockSpec((STEP_ROWS, mixw), lambda b, n, m=m: (m(b, n), 0))] * 3
        in_specs += [pl.BlockSpec((STEP_ROWS, 128), lambda b, n, m=m: (m(b, n), 0)),
                     pl.BlockSpec((2 * n_heads, STEP_ROWS), lambda b, n, m=m: (0, m(b, n)))]
    return pl.pallas_call(
        functools.partial(_a_scan_kernel, n_heads=n_heads),
        grid=(n_batch, nsb),
        in_specs=in_specs,
        out_specs=[pl.BlockSpec((STEP_ROWS, mixw), lambda b, n: (fwd(b, n), 0)),
                   pl.BlockSpec((STEP_ROWS, mixw), lambda b, n: (rev(b, n), 0))],
        out_shape=[jax.ShapeDtypeStruct((t, mixw), F32)] * 2,
        scratch_shapes=[pltpu.VMEM((2, n_heads, HEAD_DIM, HEAD_DIM), F32)],
        compiler_params=_cparams(("arbitrary", "arbitrary")),
        name="deltanet_scan",
    )(q, k, v, gc, gr, q, k, v, gc, gr)


def _b_prep_kernel(q_ref, kv_ref, gain_ref, cos_ref, sin_ref, qo_ref, ko_ref, vo_ref, *, mixw):
    n_heads = mixw // HEAD_DIM
    n_kv = n_heads // 2
    cosf = cos_ref[...]
    sinf = sin_ref[...]

    def norm_rope(xh, gain, scale):
        yh = xh * lax.rsqrt(jnp.mean(xh * xh, axis=-1, keepdims=True) + EPS) * gain
        return (yh * cosf + pltpu.roll(yh, HEAD_DIM // 2, 1) * sinf) * scale

    for h in range(n_heads):
        c0 = h * HEAD_DIM
        qo_ref[:, c0:c0 + HEAD_DIM] = norm_rope(q_ref[:, c0:c0 + HEAD_DIM], gain_ref[0:1, :],
                                                HEAD_DIM ** -0.5).astype(qo_ref.dtype)
    for h in range(n_kv):
        c0 = h * HEAD_DIM
        ko_ref[:, c0:c0 + HEAD_DIM] = norm_rope(kv_ref[:, c0:c0 + HEAD_DIM], gain_ref[1:2, :], 1.0).astype(ko_ref.dtype)
    vo_ref[...] = kv_ref[:, n_kv * HEAD_DIM:].astype(vo_ref.dtype)


def _b_prep(p, qk_gain, cosf, sinf, tm, nb, mixw, col_q, col_kv):
    t = p.shape[0]
    kvw = mixw // 2
    return pl.pallas_call(
        functools.partial(_b_prep_kernel, mixw=mixw),
        grid=(t // tm,),
        in_specs=[pl.BlockSpec((tm, mixw), lambda i: (i, col_q // mixw)),
                  pl.BlockSpec((tm, mixw), lambda i: (i, col_kv // mixw)),
                  pl.BlockSpec((2, HEAD_DIM), lambda i: (0, 0)),
                  pl.BlockSpec((tm, HEAD_DIM), lambda i: (i % nb, 0)),
                  pl.BlockSpec((tm, HEAD_DIM), lambda i: (i % nb, 0))],
        out_specs=[pl.BlockSpec((tm, mixw), lambda i: (i, 0)),
                   pl.BlockSpec((tm, kvw), lambda i: (i, 0)),
                   pl.BlockSpec((tm, kvw), lambda i: (i, 0))],
        out_shape=[jax.ShapeDtypeStruct((t, mixw), BF16),
                   jax.ShapeDtypeStruct((t, kvw), BF16),
                   jax.ShapeDtypeStruct((t, kvw), BF16)],
        compiler_params=_cparams(("parallel",)),
        name="gqa_prep",
    )(p, p, qk_gain, cosf, sinf)


def _attn_kernel(q_ref, k_ref, v_ref, o_ref, m_ref, l_ref, acc_ref, *, tq, tk, nctx_q, ctx_len):
    qi = pl.program_id(2)
    ki = pl.program_id(3)
    nk = pl.num_programs(3)
    is_ctx = qi < nctx_q

    @pl.when(ki == 0)
    def _():
        m_ref[...] = jnp.full_like(m_ref, -1e30)
        l_ref[...] = jnp.zeros_like(l_ref)
        acc_ref[...] = jnp.zeros_like(acc_ref)

    def step(masked):
        qb = q_ref[...]
        q2 = jnp.concatenate([qb[:, :HEAD_DIM], qb[:, HEAD_DIM:]], axis=0)
        s = _dot_nt(q2, k_ref[...])
        if masked:
            col = lax.broadcasted_iota(jnp.int32, s.shape, 1)
            s = jnp.where(col < ctx_len, s, -1e30)
        m_prev = m_ref[...]
        m_new = jnp.maximum(m_prev, jnp.max(s, axis=-1, keepdims=True))
        p = jnp.exp(s - m_new)
        corr = jnp.exp(m_prev - m_new)
        l_ref[...] = corr * l_ref[...] + jnp.sum(p, axis=-1, keepdims=True)
        acc_ref[...] = corr * acc_ref[...] + _dot(p.astype(BF16), v_ref[...])
        m_ref[...] = m_new

    @pl.when(jnp.logical_not(is_ctx))
    def _():
        step(False)

    @pl.when(jnp.logical_and(is_ctx, ki == 0))
    def _():
        step(True)

    @pl.when(ki == nk - 1)
    def _():
        o = acc_ref[...] / l_ref[...]
        o_ref[...] = jnp.concatenate([o[:tq], o[tq:]], axis=1).astype(o_ref.dtype)


def _attention(q, k, v, n_batch, tb, ctx_len, mixw):
    t = q.shape[0]
    n_kv = mixw // HEAD_DIM // 2
    tq = _pick(math.gcd(ctx_len, tb), (256, 128))
    tk = _pick(tb, (768, 1024, 512, 640, 384, 256, 128))
    assert ctx_len <= tk
    nq, nk = tb // tq, tb // tk
    nctx_q = ctx_len // tq

    def kv_map(b, j, qi, ki):
        return (b * nk + jnp.where(qi < nctx_q, 0, ki), j)

    return pl.pallas_call(
        functools.partial(_attn_kernel, tq=tq, tk=tk, nctx_q=nctx_q, ctx_len=ctx_len),
        grid=(n_batch, n_kv, nq, nk),
        in_specs=[pl.BlockSpec((tq, 2 * HEAD_DIM), lambda b, j, qi, ki: (b * nq + qi, j)),
                  pl.BlockSpec((tk, HEAD_DIM), kv_map),
                  pl.BlockSpec((tk, HEAD_DIM), kv_map)],
        out_specs=pl.BlockSpec((tq, 2 * HEAD_DIM), lambda b, j, qi, ki: (b * nq + qi, j)),
        out_shape=jax.ShapeDtypeStruct((t, mixw), BF16),
        scratch_shapes=[pltpu.VMEM((2 * tq, 1), F32), pltpu.VMEM((2 * tq, 1), F32),
                        pltpu.VMEM((2 * tq, HEAD_DIM), F32)],
        compiler_params=_cparams(("parallel", "parallel", "parallel", "arbitrary")),
        name="gqa_attention",
    )(q, k, v)


def _c_kernel(u_ref, v_ref, ws_ref, bs_ref, o_ref, *, tm, mixw):
    n_groups = mixw // HEAD_DIM
    for ci in range(tm // CHUNK_C):
        r0 = ci * CHUNK_C
        u = _gelu_tanh(u_ref[r0:r0 + CHUNK_C, :])
        vn = _ln(_gelu_tanh(v_ref[r0:r0 + CHUNK_C, :])).astype(BF16)
        for g in range(n_groups):
            c0 = g * HEAD_DIM
            vm = _dot(ws_ref[g], vn[:, c0:c0 + HEAD_DIM]) + bs_ref[:, g:g + 1]
            o_ref[r0:r0 + CHUNK_C, c0:c0 + HEAD_DIM] = (u[:, c0:c0 + HEAD_DIM] * vm).astype(o_ref.dtype)


def _mixer_c(p, ws, bs_t, tm, mixw, col_u, col_v):
    t = p.shape[0]
    n_groups = mixw // HEAD_DIM
    return pl.pallas_call(
        functools.partial(_c_kernel, tm=tm, mixw=mixw),
        grid=(t // tm,),
        in_specs=[pl.BlockSpec((tm, mixw), lambda i: (i, col_u // mixw)),
                  pl.BlockSpec((tm, mixw), lambda i: (i, col_v // mixw)),
                  pl.BlockSpec((n_groups, CHUNK_C, CHUNK_C), lambda i: (0, 0, 0)),
                  pl.BlockSpec((CHUNK_C, 128), lambda i: (0, 0))],
        out_specs=pl.BlockSpec((tm, mixw), lambda i: (i, 0)),
        out_shape=jax.ShapeDtypeStruct((t, mixw), BF16),
        compiler_params=_cparams(("parallel",)),
        name="gmlp",
    )(p, p, ws, bs_t)


def _hgrn_tables():
    tau = np.arange(CHUNK)
    sums_t = [tau[None, :] <= tau[:, None], tau[None, :] > tau[:, None]]
    pair_t = []
    for lvl in range(N_LEVELS):
        m = CHUNK >> (lvl + 1)
        blk = tau // (2 * m)
        upper = (tau % (2 * m)) >= m
        ref = blk * 2 * m + m - 1
        r = tau[None, :]
        w_up = upper[:, None] & (r > ref[:, None]) & (r <= tau[:, None])
        w_lo = (~upper)[:, None] & (r > tau[:, None]) & (r <= ref[:, None])
        sums_t.append(w_up | w_lo)
        pair_t.append((blk[:, None] == blk[None, :]) & upper[:, None] & (~upper)[None, :])
    sums_t = np.concatenate(sums_t, axis=0).astype(np.float32)
    pair_t = np.stack(pair_t).astype(np.float32)
    flip = tau[::-1]
    sums, pair = [], []
    for d in range(2):
        if d == 0:
            s_d, p_d = sums_t, pair_t
        else:
            s_d = sums_t.reshape(-1, CHUNK, CHUNK)[:, flip][:, :, flip].reshape(-1, CHUNK)
            p_d = pair_t[:, flip][:, :, flip]
        sums.append(np.concatenate([s_d, s_d, s_d], axis=1))
        pair.append(p_d)
    return np.stack(sums), np.stack(pair)


def _hgrn_chunk(q, v, fz, lb, st, sums3, pair, eye, last_idx):
    z = jnp.clip(fz, -GATE_CLIP, GATE_CLIP)
    e = jnp.exp(-z)
    sg = 1.0 / (1.0 + e)
    logf = jnp.log(lb + (1.0 - lb) * sg)
    k = (1.0 - lb) * (e * sg)
    hi = logf.astype(BF16)
    r1 = logf - hi.astype(F32)
    mid = r1.astype(BF16)
    lo = (r1 - mid.astype(F32)).astype(BF16)
    x = _dot(sums3, jnp.concatenate([hi, mid, lo], axis=0))
    c = q.shape[0]
    b = x[0:c]
    b_last = b[last_idx:last_idx + 1, :]
    attn = eye * _dot_nt(q.astype(BF16), k.astype(BF16))
    for lvl in range(N_LEVELS):
        zl = jnp.exp(x[(2 + lvl) * c:(3 + lvl) * c])
        attn = attn + pair[lvl] * _dot_nt((q * zl).astype(BF16), (k * zl).astype(BF16))
    o = _dot_nt((q * jnp.exp(b)).astype(BF16), st.astype(BF16)) + _dot(attn.astype(BF16), v.astype(BF16))
    kd = (k * jnp.exp(x[c:2 * c])).astype(BF16)
    st_new = st * jnp.exp(b_last) + _dot_tn(v.astype(BF16), kd)
    return o, st_new


def _d_scan_kernel(qf, vf, ff, qr, vr, fr, lb_ref, sums_ref, pair_ref, of_ref, or_ref, st_ref, *, n_heads):
    n = pl.program_id(1)

    @pl.when(n == 0)
    def _():
        st_ref[...] = jnp.zeros_like(st_ref)

    row = lax.broadcasted_iota(jnp.int32, (CHUNK, CHUNK), 0)
    col = lax.broadcasted_iota(jnp.int32, (CHUNK, CHUNK), 1)
    eye = (row == col).astype(F32)
    for d, (q_ref, v_ref, f_ref, o_ref) in enumerate(((qf, vf, ff, of_ref), (qr, vr, fr, or_ref))):
        last_idx = CHUNK - 1 if d == 0 else 0
        sums3 = sums_ref[d]
        pair = [pair_ref[d, lvl] for lvl in range(N_LEVELS)]
        for ci in ((0, 1) if d == 0 else (1, 0)):
            r0 = ci * CHUNK
            for h in range(n_heads):
                c0 = h * HEAD_DIM
                o, st_new = _hgrn_chunk(q_ref[r0:r0 + CHUNK, c0:c0 + HEAD_DIM],
                                        v_ref[r0:r0 + CHUNK, c0:c0 + HEAD_DIM],
                                        f_ref[r0:r0 + CHUNK, c0:c0 + HEAD_DIM],
                                        lb_ref[0:1, c0:c0 + HEAD_DIM], st_ref[d, h], sums3, pair, eye, last_idx)
                o_ref[r0:r0 + CHUNK, c0:c0 + HEAD_DIM] = o
                st_ref[d, h] = st_new


def _d_scan(p, lb, n_batch, tb, ctx_len, mixw, col_q, col_i, col_f):
    t = p.shape[0]
    n_heads = mixw // HEAD_DIM
    nsb = tb // STEP_ROWS
    fwd, rev = _seq_block_maps(nsb, ctx_len // STEP_ROWS)
    sums_np, pair_np = _hgrn_tables()
    sums = jnp.asarray(sums_np, BF16)
    pair = jnp.asarray(pair_np, F32)
    in_specs = []
    for d, m in enumerate((fwd, rev)):
        in_specs += [pl.BlockSpec((STEP_ROWS, mixw), lambda b, n, m=m: (m(b, n), col_q // mixw)),
                     pl.BlockSpec((STEP_ROWS, mixw), lambda b, n, m=m: (m(b, n), col_i // mixw)),
                     pl.BlockSpec((STEP_ROWS, mixw), lambda b, n, m=m, d=d: (m(b, n), col_f // mixw + d))]
    in_specs += [pl.BlockSpec((1, mixw), lambda b, n: (0, 0)),
                 pl.BlockSpec(sums.shape, lambda b, n: (0, 0, 0)),
                 pl.BlockSpec(pair.shape, lambda b, n: (0, 0, 0, 0))]
    return pl.pallas_call(
        functools.partial(_d_scan_kernel, n_heads=n_heads),
        grid=(n_batch, nsb),
        in_specs=in_specs,
        out_specs=[pl.BlockSpec((STEP_ROWS, mixw), lambda b, n: (fwd(b, n), 0)),
                   pl.BlockSpec((STEP_ROWS, mixw), lambda b, n: (rev(b, n), 0))],
        out_shape=[jax.ShapeDtypeStruct((t, mixw), F32)] * 2,
        scratch_shapes=[pltpu.VMEM((2, n_heads, HEAD_DIM, HEAD_DIM), F32)],
        compiler_params=_cparams(("arbitrary", "arbitrary")),
        name="hgrn2_scan",
    )(p, p, p, p, p, p, lb, sums, pair)


def _route(logits_t, bias, n_experts):
    per = n_experts // N_EXPERT_GROUPS
    scores = _sigmoid(logits_t)
    sel = scores + bias
    rows_sel = [sel[e:e + 1, :] for e in range(n_experts)]
    rows_sc = [scores[e:e + 1, :] for e in range(n_experts)]

    def top2_sum(vals):
        hi = vals[0]
        lo = jnp.full_like(hi, -jnp.inf)
        for x in vals[1:]:
            lo = jnp.maximum(lo, jnp.minimum(hi, x))
            hi = jnp.maximum(hi, x)
        return hi + lo

    best = jnp.zeros_like(rows_sel[0], dtype=jnp.int32)
    best_score = top2_sum(rows_sel[0:per])
    for g in range(1, N_EXPERT_GROUPS):
        gs = top2_sum(rows_sel[g * per:(g + 1) * per])
        better = gs > best_score
        best = jnp.where(better, g, best)
        best_score = jnp.where(better, gs, best_score)
    in_sel, in_sc = [], []
    for j in range(per):
        a = rows_sel[j]
        c = rows_sc[j]
        for g in range(1, N_EXPERT_GROUPS):
            a = jnp.where(best == g, rows_sel[g * per + j], a)
            c = jnp.where(best == g, rows_sc[g * per + j], c)
        in_sel.append(a)
        in_sc.append(c)

    def first_argmax(vals):
        idx = jnp.zeros_like(best)
        top = vals[0]
        for j in range(1, per):
            better = vals[j] > top
            idx = jnp.where(better, j, idx)
            top = jnp.where(better, vals[j], top)
        return idx

    i1 = first_argmax(in_sel)
    i2 = first_argmax([jnp.where(i1 == j, -jnp.inf, in_sel[j]) for j in range(per)])

    def pick(idx):
        w = in_sc[0]
        for j in range(1, per):
            w = jnp.where(idx == j, in_sc[j], w)
        return w

    w1, w2 = pick(i1), pick(i2)
    tot = w1 + w2
    return best * per + i1, best * per + i2, w1 / tot, w2 / tot


def _outproj_kernel(oaf, oar, ga, odf, odr, gd, yb, yc, na, nd, wo, x_ref, m_ref, lng, lnb, wr, br,
                    x1_ref, h2_ref, ei_ref, ew_ref, *, alpha, mixw, tm, n_experts):
    n_heads = mixw // HEAD_DIM

    def gated(of_ref, or_ref, g_ref, gain_ref):
        parts = []
        for h in range(n_heads):
            c0 = h * HEAD_DIM
            o = of_ref[:, c0:c0 + HEAD_DIM] + or_ref[:, c0:c0 + HEAD_DIM]
            y = o * lax.rsqrt(jnp.mean(o * o, axis=-1, keepdims=True) + EPS) * gain_ref[...]
            parts.append((y * _silu(g_ref[:, c0:c0 + HEAD_DIM])).astype(BF16))
        return jnp.concatenate(parts, axis=1)

    ya = gated(oaf, oar, ga, na)
    yd = gated(odf, odr, gd, nd)
    y = (_dot(ya, wo[0:mixw, :]) + _dot(yb[...], wo[mixw:2 * mixw, :])
         + _dot(yc[...], wo[2 * mixw:3 * mixw, :]) + _dot(yd, wo[3 * mixw:4 * mixw, :]))
    x1 = _ln(alpha * x_ref[...] + m_ref[0, 2:3, :] * y) * lng[...] + lnb[...]
    x1_ref[...] = x1
    h2 = _ln(x1) * (1.0 + m_ref[0, 4:5, :]) + m_ref[0, 3:4, :]
    h2_ref[...] = h2
    logits_t = _dot_nt(wr[...], h2, HIGHEST)
    e1, e2, w1, w2 = _route(logits_t, br[:, 0:1], n_experts)
    zi = jnp.zeros((6, tm), jnp.int32)
    ei_ref[...] = jnp.concatenate([e1, e2, zi], axis=0)
    ew_ref[...] = jnp.concatenate([w1, w2, zi.astype(F32)], axis=0)


def _outproj(oaf, oar, odf, odr, p, yb, yc, norm_a, norm_d, w_out, x, mods, ln_g, ln_b, w_router_t, b_router,
             alpha, tm, nb, nctx, n_batch, mixw, col_ga, col_gd):
    t, d = x.shape
    n_experts = w_router_t.shape[0]
    row = lambda i: (i, 0)
    const = lambda i: (0, 0)
    kern = functools.partial(_outproj_kernel, alpha=alpha, mixw=mixw, tm=tm, n_experts=n_experts)
    return pl.pallas_call(
        kern,
        grid=(t // tm,),
        in_specs=[pl.BlockSpec((tm, mixw), row), pl.BlockSpec((tm, mixw), row),
                  pl.BlockSpec((tm, mixw), lambda i: (i, col_ga // mixw)),
                  pl.BlockSpec((tm, mixw), row), pl.BlockSpec((tm, mixw), row),
                  pl.BlockSpec((tm, mixw), lambda i: (i, col_gd // mixw)),
                  pl.BlockSpec((tm, mixw), row), pl.BlockSpec((tm, mixw), row),
                  pl.BlockSpec((1, HEAD_DIM), const), pl.BlockSpec((1, HEAD_DIM), const),
                  pl.BlockSpec((N_MIXERS * mixw, d), const),
                  pl.BlockSpec((tm, d), row),
                  pl.BlockSpec((1, 6, d), lambda i: (_mod_row(i, nb, nctx, n_batch), 0, 0)),
                  pl.BlockSpec((1, d), const), pl.BlockSpec((1, d), const),
                  pl.BlockSpec((n_experts, d), const), pl.BlockSpec((n_experts, 128), const)],
        out_specs=[pl.BlockSpec((tm, d), row), pl.BlockSpec((tm, d), row),
                   pl.BlockSpec((8, tm), lambda i: (0, i)), pl.BlockSpec((8, tm), lambda i: (0, i))],
        out_shape=[jax.ShapeDtypeStruct((t, d), F32), jax.ShapeDtypeStruct((t, d), F32),
                   jax.ShapeDtypeStruct((8, t), jnp.int32), jax.ShapeDtypeStruct((8, t), F32)],
        compiler_params=_cparams(("parallel",)),
        name="out_proj_router",
    )(oaf, oar, p, odf, odr, p, yb, yc, norm_a, norm_d, w_out, x, mods, ln_g, ln_b, w_router_t, b_router)


def _rank_kernel(ei_ref, su_ref, rank_ref, cnt_ref, base_ref, *, n_experts, tm):
    i = pl.program_id(0)

    @pl.when(i == 0)
    def _():
        base_ref[...] = jnp.zeros_like(base_ref)

    eid = lax.broadcasted_iota(jnp.int32, (n_experts, tm), 0)
    o1 = (eid == ei_ref[0:1, :]).astype(F32)
    o2 = (eid == ei_ref[1:2, :]).astype(F32)
    cnt = o1 + o2
    before = _dot(cnt.astype(BF16), su_ref[...]) + base_ref[:, 0:1]
    r1 = jnp.sum(o1 * before, axis=0, keepdims=True)
    r2 = jnp.sum(o2 * before, axis=0, keepdims=True)
    rank_ref[...] = jnp.concatenate([r1, r2, jnp.zeros((6, tm), F32)], axis=0).astype(jnp.int32)
    base_ref[...] = base_ref[...] + jnp.sum(cnt, axis=1, keepdims=True)
    cnt_ref[...] = base_ref[...]


def _slot_ranks(ei, n_experts, tm):
    t = ei.shape[1]
    r = np.arange(tm)
    su = jnp.asarray((r[:, None] < r[None, :]).astype(np.float32), BF16)
    return pl.pallas_call(
        functools.partial(_rank_kernel, n_experts=n_experts, tm=tm),
        grid=(t // tm,),
        in_specs=[pl.BlockSpec((8, tm), lambda i: (0, i)), pl.BlockSpec((tm, tm), lambda i: (0, 0))],
        out_specs=[pl.BlockSpec((8, tm), lambda i: (0, i)), pl.BlockSpec((n_experts, 128), lambda i: (0, 0))],
        out_shape=[jax.ShapeDtypeStruct((8, t), jnp.int32), jax.ShapeDtypeStruct((n_experts, 128), F32)],
        scratch_shapes=[pltpu.VMEM((n_experts, 128), F32)],
        compiler_params=_cparams(("arbitrary",)),
        name="moe_slot_ranks",
    )(ei, su)


def _row_copy(src_ref, src_row, dst_ref, dst_row, sem):
    return pltpu.make_async_copy(src_ref.at[pl.ds(src_row, 1)], dst_ref.at[pl.ds(dst_row, 1)], sem)


def _dispatch_kernel(dest_ref, h_ref, xs_in_ref, xs_ref, sem, *, tm, t):
    del xs_in_ref
    base = pl.program_id(0) * tm

    def issue(r, carry):
        for kk in range(TOP_K):
            _row_copy(h_ref, r, xs_ref, dest_ref[kk * t + base + r], sem).start()
        return carry

    lax.fori_loop(0, tm, issue, 0)

    def drain(r, carry):
        for kk in range(TOP_K):
            _row_copy(h_ref, 0, xs_ref, 0, sem).wait()
        return carry

    lax.fori_loop(0, tm, drain, 0)


def _dispatch(dest, h2, n_slots, tm):
    t, d = h2.shape
    xs0 = jnp.zeros((n_slots, d), h2.dtype)
    return pl.pallas_call(
        functools.partial(_dispatch_kernel, tm=tm, t=t),
        grid_spec=pltpu.PrefetchScalarGridSpec(
            num_scalar_prefetch=1,
            grid=(t // tm,),
            in_specs=[pl.BlockSpec((tm, d), lambda i, dest: (i, 0)),
                      pl.BlockSpec(memory_space=pl.ANY)],
            out_specs=pl.BlockSpec(memory_space=pl.ANY),
            scratch_shapes=[pltpu.SemaphoreType.DMA(())]),
        out_shape=jax.ShapeDtypeStruct((n_slots, d), h2.dtype),
        input_output_aliases={2: 0},
        compiler_params=_cparams(("arbitrary",)),
        name="moe_dispatch",
    )(dest, h2, xs0)


def _expert_kernel(be_ref, na_ref, x_ref, w1_ref, w3_ref, w2_ref, y_ref):
    i = pl.program_id(0)

    @pl.when(i < na_ref[0])
    def _():
        xb = x_ref[...].astype(BF16)
        h1 = _dot(xb, w1_ref[0])
        h3 = _dot(xb, w3_ref[0])
        y_ref[...] = _dot((_silu(h1) * h3).astype(BF16), w2_ref[0])

    @pl.when(i >= na_ref[0])
    def _():
        y_ref[...] = jnp.zeros_like(y_ref)


def _experts(block_expert, n_active, xs, w1, w3, w2, tmx):
    n_slots, d = xs.shape
    f = w1.shape[2]
    return pl.pallas_call(
        _expert_kernel,
        grid_spec=pltpu.PrefetchScalarGridSpec(
            num_scalar_prefetch=2,
            grid=(n_slots // tmx,),
            in_specs=[pl.BlockSpec((tmx, d), lambda i, be, na: (i, 0)),
                      pl.BlockSpec((1, d, f), lambda i, be, na: (be[i], 0, 0)),
                      pl.BlockSpec((1, d, f), lambda i, be, na: (be[i], 0, 0)),
                      pl.BlockSpec((1, f, d), lambda i, be, na: (be[i], 0, 0))],
            out_specs=pl.BlockSpec((tmx, d), lambda i, be, na: (i, 0))),
        out_shape=jax.ShapeDtypeStruct((n_slots, d), F32),
        compiler_params=_cparams(("arbitrary",)),
        name="moe_experts",
    )(block_expert, n_active, xs, w1, w3, w2)


def _combine_kernel(dest_ref, ys_ref, ew_ref, x1_ref, m_ref, lng, lnb, mn_ref, x2_ref, hn_ref, g_ref, sem,
                    *, tm, t, alpha):
    base = pl.program_id(0) * tm

    def issue(r, carry):
        for kk in range(TOP_K):
            _row_copy(ys_ref, dest_ref[kk * t + base + r], g_ref.at[kk], r, sem).start()
        return carry

    lax.fori_loop(0, tm, issue, 0)

    def drain(r, carry):
        for kk in range(TOP_K):
            _row_copy(ys_ref, 0, g_ref.at[kk], 0, sem).wait()
        return carry

    lax.fori_loop(0, tm, drain, 0)
    row = lax.broadcasted_iota(jnp.int32, (tm, tm), 0)
    col = lax.broadcasted_iota(jnp.int32, (tm, tm), 1)
    w_t = _dot_nt((row == col).astype(F32), ew_ref[...], HIGHEST)
    f = w_t[:, 0:1] * g_ref[0] + w_t[:, 1:2] * g_ref[1]
    x2 = _ln(alpha * x1_ref[...] + m_ref[0, 5:6, :] * f) * lng[...] + lnb[...]
    x2_ref[...] = x2
    hn_ref[...] = (_ln(x2) * (1.0 + mn_ref[0, 1:2, :]) + mn_ref[0, 0:1, :]).astype(hn_ref.dtype)


def _combine(dest, ys, ew, x1, mods, ln_g, ln_b, mods_next, alpha, tm, nb, nctx, n_batch):
    t, d = x1.shape
    mrow = lambda i, dest: (_mod_row(i, nb, nctx, n_batch), 0, 0)
    const = lambda i, dest: (0, 0)
    return pl.pallas_call(
        functools.partial(_combine_kernel, tm=tm, t=t, alpha=alpha),
        grid_spec=pltpu.PrefetchScalarGridSpec(
            num_scalar_prefetch=1,
            grid=(t // tm,),
            in_specs=[pl.BlockSpec(memory_space=pl.ANY),
                      pl.BlockSpec((8, tm), lambda i, dest: (0, i)),
                      pl.BlockSpec((tm, d), lambda i, dest: (i, 0)),
                      pl.BlockSpec((1, 6, d), mrow),
                      pl.BlockSpec((1, d), const), pl.BlockSpec((1, d), const),
                      pl.BlockSpec((1, 6, d), mrow)],
            out_specs=[pl.BlockSpec((tm, d), lambda i, dest: (i, 0)),
                       pl.BlockSpec((tm, d), lambda i, dest: (i, 0))],
            scratch_shapes=[pltpu.VMEM((TOP_K, tm, d), F32), pltpu.SemaphoreType.DMA(())]),
        out_shape=[jax.ShapeDtypeStruct((t, d), F32), jax.ShapeDtypeStruct((t, d), BF16)],
        compiler_params=_cparams(("arbitrary",)),
        name="moe_combine",
    )(dest, ys, ew, x1, mods, ln_g, ln_b, mods_next)


def _axial_tables(seq, ctx_len):
    rows = seq // GRID_W
    row = jnp.repeat(jnp.arange(rows, dtype=F32), GRID_W)
    col = jnp.tile(jnp.arange(GRID_W, dtype=F32), rows)
    n_freq = HEAD_DIM // 4
    inv_freq = ROPE_THETA ** (-jnp.arange(n_freq, dtype=F32) / n_freq)
    ang = jnp.concatenate([row[:, None] * inv_freq, col[:, None] * inv_freq], axis=-1)
    cos, sin = jnp.cos(ang), jnp.sin(ang)
    cosf = jnp.concatenate([cos, cos], axis=-1)
    sinf = jnp.concatenate([-sin, sin], axis=-1)
    cosf = jnp.concatenate([jnp.ones((ctx_len, HEAD_DIM), F32), cosf], axis=0)
    sinf = jnp.concatenate([jnp.zeros((ctx_len, HEAD_DIM), F32), sinf], axis=0)
    return cosf, sinf


def kernel(x, c, ctx, c_ctx, w_ada, b_ada, ln_g, ln_b, w_in, conv_a, a_log, dt_bias, norm_a, qk_norm_b, ws_c, bs_c, lb_d, norm_d, w_out, w_router, b_router, w1, w3, w2):
    n_batch, seq, d = x.shape
    ctx_len = ctx.shape[1]
    n_layers = w_in.shape[0]
    n_experts = w_router.shape[1]
    mixw = d // N_MIXERS
    n_heads = mixw // HEAD_DIM
    tb = ctx_len + seq
    t = n_batch * tb
    alpha = (2.0 * n_layers) ** 0.25
    tm = _pick(math.gcd(ctx_len, seq), (256, 128))
    nb, nctx = tb // tm, ctx_len // tm

    col = {"a_qkv": 0, "a_gate": 3 * mixw, "d_f": 4 * mixw, "b_q": 6 * mixw, "b_kv": 7 * mixw, "c_u": 8 * mixw,
           "c_v": 9 * mixw, "d_q": 10 * mixw, "d_i": 11 * mixw, "d_gate": 12 * mixw, "small": 13 * mixw}
    n_proj = 13 * mixw + 256
    src = np.cumsum([0, 3 * mixw, mixw, 2 * n_heads, 2 * n_heads, mixw, mixw, mixw, mixw, mixw, mixw, mixw, 2 * mixw])
    s_qkv, s_ga, s_beta, s_dec, s_bq, s_bkv, s_cu, s_cv, s_dq, s_di, s_dg, s_df, s_end = [int(v) for v in src]

    def permute_w_in(w):
        pad = jnp.zeros((d, n_proj - 13 * mixw - 4 * n_heads), w.dtype)
        return jnp.concatenate([w[:, s_qkv:s_beta], w[:, s_df:s_end], w[:, s_bq:s_df], w[:, s_beta:s_bq], pad],
                               axis=1).astype(BF16)

    cc = jnp.zeros((8, d), F32).at[:n_batch].set(c).at[n_batch].set(c_ctx)
    mods_all = _mod_vectors(cc, w_ada, b_ada)[:, :n_batch + 1].reshape(n_layers, n_batch + 1, 6, d)

    soft = jax.nn.softmax(lb_d.astype(F32), axis=0)
    lb_all = jnp.cumsum(soft, axis=0) - soft[0]
    cosf, sinf = _axial_tables(seq, ctx_len)
    gate_lanes = jnp.zeros((1, 128), F32)
    w_router_t = w_router.T
    b_router_c = jnp.broadcast_to(b_router.astype(F32)[:, None], (n_experts, 128))

    tmx = 256
    n_assign = t * TOP_K
    n_slots = (-(-n_assign // tmx) + n_experts) * tmx

    xu = jnp.concatenate([ctx, x], axis=1).reshape(t, d)
    h = _ln_modulate(xu, mods_all[0], tm, nb, nctx, n_batch)
    for l in range(n_layers):
        mods = mods_all[l]
        p = _matmul(h, permute_w_in(w_in[l]))
        neg_a = gate_lanes.at[0, 2 * n_heads:4 * n_heads].set(-jnp.exp(a_log[l].astype(F32)).reshape(-1))
        dtb = gate_lanes.at[0, 2 * n_heads:4 * n_heads].set(dt_bias[l].astype(F32).reshape(-1))
        qa, ka, va, gca, gra = _a_prep(p, conv_a[l], neg_a, dtb, tm, nb, nctx, mixw, col["small"])
        oaf, oar = _a_scan(qa, ka, va, gca, gra, n_batch, tb, ctx_len, mixw)
        qb, kb, vb = _b_prep(p, qk_norm_b[l], cosf, sinf, tm, nb, mixw, col["b_q"], col["b_kv"])
        yb = _attention(qb, kb, vb, n_batch, tb, ctx_len, mixw)
        bs_t = jnp.zeros((CHUNK_C, 128), F32).at[:, :bs_c.shape[1]].set(bs_c[l].T)
        yc = _mixer_c(p, ws_c[l].astype(BF16), bs_t, tm, mixw, col["c_u"], col["c_v"])
        odf, odr = _d_scan(p, lb_all[l][None, :], n_batch, tb, ctx_len, mixw, col["d_q"], col["d_i"], col["d_f"])
        x1, h2, ei, ew = _outproj(oaf, oar, odf, odr, p, yb, yc, norm_a[l][None, :], norm_d[l][None, :],
                                  w_out[l].astype(BF16), xu, mods, ln_g[l, 0][None, :], ln_b[l, 0][None, :],
                                  w_router_t, b_router_c, alpha, tm, nb, nctx, n_batch, mixw,
                                  col["a_gate"], col["d_gate"])
        rank, counts = _slot_ranks(ei, n_experts, tm)
        cnt = counts[:, 0].astype(jnp.int32)
        padded = (cnt + tmx - 1) // tmx * tmx
        pad_end = jnp.cumsum(padded)
        pad_start = pad_end - padded
        dest = (pad_start[ei[:TOP_K]] + rank[:TOP_K]).reshape(-1)
        n_blocks = n_slots // tmx
        block_expert = jnp.minimum(
            jnp.searchsorted(pad_end, jnp.arange(n_blocks, dtype=jnp.int32) * tmx, side="right"),
            n_experts - 1).astype(jnp.int32)
        n_active = (pad_end[-1:] // tmx).astype(jnp.int32)
        xs = _dispatch(dest, h2, n_slots, tm)
        ys = _experts(block_expert, n_active, xs, w1[l].astype(BF16), w3[l].astype(BF16), w2[l].astype(BF16), tmx)
        mods_next = mods_all[min(l + 1, n_layers - 1)]
        xu, h = _combine(dest, ys, ew, x1, mods, ln_g[l, 1][None, :], ln_b[l, 1][None, :], mods_next, alpha,
                         tm, nb, nctx, n_batch)
    return xu.reshape(n_batch, tb, d)[:, ctx_len:]
```

```python
import functools
import math

import numpy as np
import jax
import jax.numpy as jnp
from jax import lax
from jax.experimental import pallas as pl
from jax.experimental.pallas import tpu as pltpu

F32 = jnp.float32
BF16 = jnp.bfloat16
HIGHEST = lax.Precision.HIGHEST

HEAD_DIM = 128
N_MIXERS = 4
CONV_W = 5
CHUNK = 64
STEP_ROWS = 2 * CHUNK
CHUNK_C = 128
GRID_W = 64
ROPE_THETA = 10000.0
N_EXPERT_GROUPS = 4
TOP_K = 2
EPS = 1e-6
LOG2_E = math.log2(math.e)
GATE_CLIP = 30.0
N_LEVELS = 6
HALO = 8
V7X_VMEM_LIMIT = 56 * 1024 * 1024


def _pick(n, cands):
    for c in cands:
        if n % c == 0:
            return c
    raise ValueError(f"no tile in {cands} divides {n}")


def _cparams(sem, vmem=None):
    return pltpu.CompilerParams(dimension_semantics=sem, vmem_limit_bytes=vmem or V7X_VMEM_LIMIT)


def _dot(a, b, precision=None):
    return jnp.dot(a, b, preferred_element_type=F32, precision=precision)


def _dot_nt(a, b, precision=None):
    return lax.dot_general(a, b, (((1,), (1,)), ((), ())), preferred_element_type=F32, precision=precision)


def _dot_tn(a, b, precision=None):
    return lax.dot_general(a, b, (((0,), (0,)), ((), ())), preferred_element_type=F32, precision=precision)


def _sigmoid(x):
    return 1.0 / (1.0 + jnp.exp(-x))


def _silu(x):
    return x * _sigmoid(x)


def _gelu_tanh(x):
    return 0.5 * x * (1.0 + jnp.tanh(math.sqrt(2.0 / math.pi) * (x + 0.044715 * (x * x * x))))


def _ln(x):
    mu = jnp.mean(x, axis=-1, keepdims=True)
    xc = x - mu
    return xc * lax.rsqrt(jnp.mean(xc * xc, axis=-1, keepdims=True) + EPS)


def _mod_row(i, nb, nctx, n_batch):
    return jnp.where(i % nb < nctx, n_batch, i // nb)


def _mod_kernel(c_ref, w_ref, b_ref, o_ref):
    c = c_ref[...]
    o_ref[0] = _dot(_silu(c), w_ref[0], HIGHEST) + b_ref[0]


def _mod_vectors(cc, w_ada, b_ada):
    n_layers, d, n6 = w_ada.shape
    tn = _pick(n6, (1024, 512, 128))
    return pl.pallas_call(
        _mod_kernel,
        grid=(n_layers, n6 // tn),
        in_specs=[pl.BlockSpec((8, d), lambda l, j: (0, 0)),
                  pl.BlockSpec((1, d, tn), lambda l, j: (l, 0, j)),
                  pl.BlockSpec((1, 1, tn), lambda l, j: (l, 0, j))],
        out_specs=pl.BlockSpec((1, 8, tn), lambda l, j: (l, 0, j)),
        out_shape=jax.ShapeDtypeStruct((n_layers, 8, n6), F32),
        compiler_params=_cparams(("parallel", "parallel")),
        name="mod_vectors",
    )(cc, w_ada, b_ada.reshape(n_layers, 1, n6))


def _lnmod_kernel(x_ref, m_ref, h_ref):
    y = _ln(x_ref[...])
    h_ref[...] = (y * (1.0 + m_ref[0, 1:2, :]) + m_ref[0, 0:1, :]).astype(h_ref.dtype)


def _ln_modulate(x, mods, tm, nb, nctx, n_batch):
    t, d = x.shape
    return pl.pallas_call(
        _lnmod_kernel,
        grid=(t // tm,),
        in_specs=[pl.BlockSpec((tm, d), lambda i: (i, 0)),
                  pl.BlockSpec((1, 6, d), lambda i: (_mod_row(i, nb, nctx, n_batch), 0, 0))],
        out_specs=pl.BlockSpec((tm, d), lambda i: (i, 0)),
        out_shape=jax.ShapeDtypeStruct((t, d), BF16),
        compiler_params=_cparams(("parallel",)),
        name="ln_modulate",
    )(x, mods)


def _matmul_kernel(a_ref, w_ref, o_ref):
    o_ref[...] = _dot(a_ref[...], w_ref[...]).astype(o_ref.dtype)


def _matmul(a, w, out_dtype=F32):
    m, k = a.shape
    n = w.shape[1]
    tm = _pick(m, (1536, 1024, 768, 512, 256, 128))
    tn = _pick(n, (1152, 1024, 768, 512, 256, 128))
    return pl.pallas_call(
        _matmul_kernel,
        grid=(m // tm, n // tn),
        in_specs=[pl.BlockSpec((tm, k), lambda i, j: (i, 0)),
                  pl.BlockSpec((k, tn), lambda i, j: (0, j))],
        out_specs=pl.BlockSpec((tm, tn), lambda i, j: (i, j)),
        out_shape=jax.ShapeDtypeStruct((m, n), out_dtype),
        compiler_params=_cparams(("parallel", "parallel")),
        name="in_proj",
    )(a, w)


def _unit_tri_inverses(lows, eye, pair_refs):
    ts = [eye - low * pair_ref[0] for low, pair_ref in zip(lows, pair_refs)]
    for lvl in range(1, N_LEVELS):
        tbs = [t.astype(BF16) for t in ts]
        xs = [_dot(tb16, (low * pair_ref[lvl]).astype(BF16)) for tb16, low, pair_ref in zip(tbs, lows, pair_refs)]
        ts = [t - _dot(x.astype(BF16), tb16) for t, x, tb16 in zip(ts, xs, tbs)]
    return ts


def _a_prep_kernel(x_ref, xp_ref, xn_ref, sm_ref, cw_ref, na_ref, dtb_ref, trf_ref, trr_ref, pairf_ref, pairr_ref,
                   uf_ref, wqf_ref, kdf_ref, atf_ref, ur_ref, wqr_ref, kdr_ref, atr_ref, eg_ref, ext_ref,
                   *, tm, nb, nctx, mixw):
    i = pl.program_id(0)
    j = i % nb
    first = jnp.logical_or(j == 0, j == nctx)
    last = jnp.logical_or(j == nctx - 1, j == nb - 1)
    ext_ref[HALO:HALO + tm, :] = x_ref[...]
    ext_ref[0:HALO, :] = jnp.where(first, 0.0, xp_ref[...])
    ext_ref[HALO + tm:2 * HALO + tm, :] = jnp.where(last, 0.0, xn_ref[...])
    off = HALO - CONV_W // 2
    acc = cw_ref[0:1, :] * ext_ref[off:off + tm, :]
    for t in range(1, CONV_W):
        acc = acc + cw_ref[t:t + 1, :] * ext_ref[off + t:off + t + tm, :]
    y = _silu(acc)
    n_heads = mixw // HEAD_DIM
    sm = sm_ref[...]
    lane = lax.broadcasted_iota(jnp.int32, sm.shape, 1)
    zz = sm + dtb_ref[...]
    softplus = jnp.maximum(zz, 0.0) + jnp.log(1.0 + jnp.exp(-jnp.abs(zz)))
    g = jnp.where(jnp.logical_and(lane >= 2 * n_heads, lane < 4 * n_heads), na_ref[...] * softplus, 0.0)
    beta_all = _sigmoid(sm)
    tri_f = trf_ref[...]
    tri_r = trr_ref[...]
    cs_f = _dot(tri_f, g, HIGHEST)
    cs_r = _dot(tri_r, g, HIGHEST)
    gcum = jnp.where(lane < 3 * n_heads, cs_f, cs_r)
    gtot = _dot(jnp.maximum(tri_f, tri_r), g, HIGHEST)
    eg_ref[...] = jnp.exp(gtot)
    gcum_t = gcum.T
    row = lax.broadcasted_iota(jnp.int32, (tm, tm), 0)
    col = lax.broadcasted_iota(jnp.int32, (tm, tm), 1)
    eye = (row == col).astype(F32)
    n_chunks = tm // CHUNK
    outs = ((uf_ref, wqf_ref, kdf_ref, atf_ref, tri_f, pairf_ref), (ur_ref, wqr_ref, kdr_ref, atr_ref, tri_r, pairr_ref))
    items = []
    for h in range(n_heads):
        c0 = h * HEAD_DIM
        qh = y[:, c0:c0 + HEAD_DIM]
        kh = y[:, mixw + c0:mixw + c0 + HEAD_DIM]
        vh = y[:, 2 * mixw + c0:2 * mixw + c0 + HEAD_DIM]
        qh = qh * (lax.rsqrt(jnp.sum(qh * qh, axis=-1, keepdims=True) + EPS) * HEAD_DIM ** -0.5)
        kh = kh * lax.rsqrt(jnp.sum(kh * kh, axis=-1, keepdims=True) + EPS)
        kb16 = kh.astype(BF16)
        qk = _dot_nt(qh.astype(BF16), kb16)
        for d in range(2):
            tri = outs[d][4]
            lb = d * n_heads + h
            lg = (2 + d) * n_heads + h
            beta = beta_all[:, lb:lb + 1]
            gc = gcum[:, lg:lg + 1]
            incl = tri > 0.5
            decay = jnp.where(incl, jnp.exp(jnp.where(incl, gc - gcum_t[lg:lg + 1, :], 0.0)), 0.0)
            kbeta = kh * beta
            low = jnp.where(row == col, 0.0, _dot_nt(kbeta.astype(BF16), kb16) * decay)
            rhs = jnp.concatenate([vh * beta, kbeta * jnp.exp(gc)], axis=1).astype(BF16)
            items.append(dict(h=h, d=d, low=low, rhs=rhs, attn=(qk * decay).astype(BF16),
                              qd=(qh * jnp.exp(gc)).astype(BF16),
                              kd=(kh * jnp.exp(gtot[:, lg:lg + 1] - gc)).astype(BF16)))
    tinvs = _unit_tri_inverses([it["low"] for it in items], eye, [outs[it["d"]][5] for it in items])
    for it, tinv in zip(items, tinvs):
        u_ref, wq_ref, kd_ref, at_ref = outs[it["d"]][:4]
        h = it["h"]
        c0 = h * HEAD_DIM
        sol = _dot(tinv.astype(BF16), it["rhs"])
        u_ref[:, c0:c0 + HEAD_DIM] = sol[:, :HEAD_DIM]
        w = sol[:, HEAD_DIM:].astype(BF16)
        kd_ref[:, c0:c0 + HEAD_DIM] = it["kd"]
        for ci in range(n_chunks):
            r0 = ci * CHUNK
            wq_ref[2 * r0:2 * r0 + CHUNK, c0:c0 + HEAD_DIM] = w[r0:r0 + CHUNK]
            wq_ref[2 * r0 + CHUNK:2 * r0 + 2 * CHUNK, c0:c0 + HEAD_DIM] = it["qd"][r0:r0 + CHUNK]
            at_ref[r0:r0 + CHUNK, h * CHUNK:(h + 1) * CHUNK] = it["attn"][r0:r0 + CHUNK, r0:r0 + CHUNK]


def _a_prep(p, conv_w, neg_a, dtb, tm, nb, nctx, mixw, col_small):
    t = p.shape[0]
    c3 = 3 * mixw
    n_heads = mixw // HEAD_DIM
    r = np.arange(tm)
    same = (r[:, None] // CHUNK) == (r[None, :] // CHUNK)
    tri_f = jnp.asarray((same & (r[None, :] <= r[:, None])).astype(np.float32))
    tri_r = jnp.asarray((same & (r[None, :] >= r[:, None])).astype(np.float32))
    pair_f = []
    for lvl in range(N_LEVELS):
        s_blk = 1 << lvl
        joined = ((r[:, None] // (2 * s_blk)) == (r[None, :] // (2 * s_blk))) \
            & ((r[:, None] % (2 * s_blk)) >= s_blk) & ((r[None, :] % (2 * s_blk)) < s_blk)
        pair_f.append(joined.astype(np.float32))
    pair_f = np.stack(pair_f)
    pair_r = jnp.asarray(np.transpose(pair_f, (0, 2, 1)))
    pair_f = jnp.asarray(pair_f)
    hb = tm // HALO
    last_hb = t // HALO - 1
    kern = functools.partial(_a_prep_kernel, tm=tm, nb=nb, nctx=nctx, mixw=mixw)
    per_dir_specs = [pl.BlockSpec((tm, mixw), lambda i: (i, 0)),
                     pl.BlockSpec((2 * tm, mixw), lambda i: (i, 0)),
                     pl.BlockSpec((tm, mixw), lambda i: (i, 0)),
                     pl.BlockSpec((tm, n_heads * CHUNK), lambda i: (i, 0))]
    per_dir_shapes = [jax.ShapeDtypeStruct((t, mixw), F32), jax.ShapeDtypeStruct((2 * t, mixw), BF16),
                      jax.ShapeDtypeStruct((t, mixw), BF16), jax.ShapeDtypeStruct((t, n_heads * CHUNK), BF16)]
    return pl.pallas_call(
        kern,
        grid=(t // tm,),
        in_specs=[pl.BlockSpec((tm, c3), lambda i: (i, 0)),
                  pl.BlockSpec((HALO, c3), lambda i: (jnp.maximum(i * hb - 1, 0), 0)),
                  pl.BlockSpec((HALO, c3), lambda i: (jnp.minimum((i + 1) * hb, last_hb), 0)),
                  pl.BlockSpec((tm, 128), lambda i: (i, col_small // 128)),
                  pl.BlockSpec((CONV_W, c3), lambda i: (0, 0)),
                  pl.BlockSpec((1, 128), lambda i: (0, 0)),
                  pl.BlockSpec((1, 128), lambda i: (0, 0)),
                  pl.BlockSpec((tm, tm), lambda i: (0, 0)),
                  pl.BlockSpec((tm, tm), lambda i: (0, 0)),
                  pl.BlockSpec((N_LEVELS, tm, tm), lambda i: (0, 0, 0)),
                  pl.BlockSpec((N_LEVELS, tm, tm), lambda i: (0, 0, 0))],
        out_specs=per_dir_specs * 2 + [pl.BlockSpec((tm, 128), lambda i: (i, 0))],
        out_shape=per_dir_shapes * 2 + [jax.ShapeDtypeStruct((t, 128), F32)],
        scratch_shapes=[pltpu.VMEM((tm + 2 * HALO, c3), F32)],
        compiler_params=_cparams(("parallel",)),
        name="deltanet_prep",
    )(p, p, p, p, conv_w, neg_a, dtb, tri_f, tri_r, pair_f, pair_r)


def _a_scan_kernel(uf, wqf, kdf, atf, egf, ur, wqr, kdr, atr, egr, of_ref, or_ref, s_ref, sbd_ref, *, n_heads):
    n = pl.program_id(1)

    @pl.when(n == 0)
    def _():
        s_ref[...] = jnp.zeros_like(s_ref)
        sbd_ref[...] = jnp.zeros_like(sbd_ref)

    mixw = n_heads * HEAD_DIM
    lane_head = lax.broadcasted_iota(jnp.int32, (CHUNK, mixw), 1) // HEAD_DIM
    for d, (u_ref, wq_ref, kd_ref, at_ref, eg_ref, o_ref) in enumerate(
            ((uf, wqf, kdf, atf, egf, of_ref), (ur, wqr, kdr, atr, egr, or_ref))):
        for ci in ((0, 1) if d == 0 else (1, 0)):
            r0 = ci * CHUNK
            r = _dot(wq_ref[2 * r0:2 * r0 + 2 * CHUNK, :], sbd_ref[d])
            v_new = u_ref[r0:r0 + CHUNK, :] - r[:CHUNK]
            v_stack = jnp.concatenate([jnp.where(lane_head == h, v_new, 0.0) for h in range(n_heads)],
                                      axis=0).astype(BF16)
            o_ref[r0:r0 + CHUNK, :] = r[CHUNK:] + _dot(at_ref[r0:r0 + CHUNK, :], v_stack)
            vnb = v_new.astype(BF16)
            for h in range(n_heads):
                c0 = h * HEAD_DIM
                lg = (2 + d) * n_heads + h
                s_new = (s_ref[d, h] * eg_ref[r0:r0 + 1, lg:lg + 1]
                         + _dot_tn(kd_ref[r0:r0 + CHUNK, c0:c0 + HEAD_DIM], vnb[:, c0:c0 + HEAD_DIM]))
                s_ref[d, h] = s_new
                sbd_ref[d, c0:c0 + HEAD_DIM, c0:c0 + HEAD_DIM] = s_new.astype(BF16)


def _seq_block_maps(nsb, nctx_sb):
    def fwd(b, n):
        return b * nsb + n

    def rev(b, n):
        return b * nsb + jnp.where(n < nctx_sb, nctx_sb - 1 - n, nsb - 1 - (n - nctx_sb))

    return fwd, rev


def _a_scan(prep, n_batch, tb, ctx_len, mixw):
    uf, wqf, kdf, atf, ur, wqr, kdr, atr, eg = prep
    t = uf.shape[0]
    n_heads = mixw // HEAD_DIM
    nsb = tb // STEP_ROWS
    fwd, rev = _seq_block_maps(nsb, ctx_len // STEP_ROWS)
    in_specs = []
    for m in (fwd, rev):
        in_specs += [pl.BlockSpec((STEP_ROWS, mixw), lambda b, n, m=m: (m(b, n), 0)),
                     pl.BlockSpec((2 * STEP_ROWS, mixw), lambda b, n, m=m: (m(b, n), 0)),
                     pl.BlockSpec((STEP_ROWS, mixw), lambda b, n, m=m: (m(b, n), 0)),
                     pl.BlockSpec((STEP_ROWS, n_heads * CHUNK), lambda b, n, m=m: (m(b, n), 0)),
                     pl.BlockSpec((STEP_ROWS, 128), lambda b, n, m=m: (m(b, n), 0))]
    return pl.pallas_call(
        functools.partial(_a_scan_kernel, n_heads=n_heads),
        grid=(n_batch, nsb),
        in_specs=in_specs,
        out_specs=[pl.BlockSpec((STEP_ROWS, mixw), lambda b, n: (fwd(b, n), 0)),
                   pl.BlockSpec((STEP_ROWS, mixw), lambda b, n: (rev(b, n), 0))],
        out_shape=[jax.ShapeDtypeStruct((t, mixw), F32)] * 2,
        scratch_shapes=[pltpu.VMEM((2, n_heads, HEAD_DIM, HEAD_DIM), F32),
                        pltpu.VMEM((2, mixw, mixw), BF16)],
        compiler_params=_cparams(("arbitrary", "arbitrary")),
        name="deltanet_scan",
    )(uf, wqf, kdf, atf, eg, ur, wqr, kdr, atr, eg)


def _b_prep_kernel(q_ref, kv_ref, gain_ref, cos_ref, sin_ref, qo_ref, ko_ref, vo_ref, *, mixw):
    n_heads = mixw // HEAD_DIM
    n_kv = n_heads // 2
    cosf = cos_ref[...]
    sinf = sin_ref[...]

    def norm_rope(xh, gain, scale):
        yh = xh * lax.rsqrt(jnp.mean(xh * xh, axis=-1, keepdims=True) + EPS) * gain
        return (yh * cosf + pltpu.roll(yh, HEAD_DIM // 2, 1) * sinf) * scale

    for h in range(n_heads):
        c0 = h * HEAD_DIM
        qo_ref[:, c0:c0 + HEAD_DIM] = norm_rope(q_ref[:, c0:c0 + HEAD_DIM], gain_ref[0:1, :],
                                                HEAD_DIM ** -0.5 * LOG2_E).astype(qo_ref.dtype)
    for h in range(n_kv):
        c0 = h * HEAD_DIM
        ko_ref[:, c0:c0 + HEAD_DIM] = norm_rope(kv_ref[:, c0:c0 + HEAD_DIM], gain_ref[1:2, :], 1.0).astype(ko_ref.dtype)
    ones = jnp.ones((q_ref.shape[0], HEAD_DIM), vo_ref.dtype)
    for h in range(n_kv):
        c0 = (n_kv + h) * HEAD_DIM
        vo_ref[:, 2 * h * HEAD_DIM:(2 * h + 1) * HEAD_DIM] = kv_ref[:, c0:c0 + HEAD_DIM].astype(vo_ref.dtype)
        vo_ref[:, (2 * h + 1) * HEAD_DIM:(2 * h + 2) * HEAD_DIM] = ones


def _b_prep(p, qk_gain, cosf, sinf, tm, nb, mixw, col_q, col_kv):
    t = p.shape[0]
    kvw = mixw // 2
    return pl.pallas_call(
        functools.partial(_b_prep_kernel, mixw=mixw),
        grid=(t // tm,),
        in_specs=[pl.BlockSpec((tm, mixw), lambda i: (i, col_q // mixw)),
                  pl.BlockSpec((tm, mixw), lambda i: (i, col_kv // mixw)),
                  pl.BlockSpec((2, HEAD_DIM), lambda i: (0, 0)),
                  pl.BlockSpec((tm, HEAD_DIM), lambda i: (i % nb, 0)),
                  pl.BlockSpec((tm, HEAD_DIM), lambda i: (i % nb, 0))],
        out_specs=[pl.BlockSpec((tm, mixw), lambda i: (i, 0)),
                   pl.BlockSpec((tm, kvw), lambda i: (i, 0)),
                   pl.BlockSpec((tm, 2 * kvw), lambda i: (i, 0))],
        out_shape=[jax.ShapeDtypeStruct((t, mixw), BF16),
                   jax.ShapeDtypeStruct((t, kvw), BF16),
                   jax.ShapeDtypeStruct((t, 2 * kvw), BF16)],
        compiler_params=_cparams(("parallel",)),
        name="gqa_prep",
    )(p, p, qk_gain, cosf, sinf)


def _attn_kernel(q_ref, k_ref, v_ref, o_ref, m_ref, acc_ref, sa_ref, sb_ref, *, tq, tk, nk, nctx_q, ctx_len):
    qi = pl.program_id(2)
    qb = q_ref[...]
    q2 = jnp.concatenate([qb[:, :HEAD_DIM], qb[:, HEAD_DIM:]], axis=0)
    m_ref[...] = jnp.full_like(m_ref, -1e30)
    acc_ref[...] = jnp.zeros_like(acc_ref)

    def scores(i, s_ref):
        r0 = pl.multiple_of(i * tk, tk)
        s_ref[...] = _dot_nt(q2, k_ref[pl.ds(r0, tk), :])

    def absorb(s, vc):
        m_prev = m_ref[...]
        m_new = jnp.maximum(m_prev, jnp.max(s, axis=-1, keepdims=True))
        p = jnp.exp2(s - m_new)
        acc_ref[...] = jnp.exp2(m_prev - m_new) * acc_ref[...] + _dot(p.astype(BF16), vc)
        m_ref[...] = m_new

    def absorb_chunk(i, s_ref):
        r0 = pl.multiple_of(i * tk, tk)
        absorb(s_ref[...], v_ref[pl.ds(r0, tk), :])

    @pl.when(qi >= nctx_q)
    def _():
        scores(0, sa_ref)

        def pair(j, carry):
            scores(2 * j + 1, sb_ref)
            absorb_chunk(2 * j, sa_ref)
            scores(2 * j + 2, sa_ref)
            absorb_chunk(2 * j + 1, sb_ref)
            return carry

        n_pairs = (nk - 1) // 2
        lax.fori_loop(0, n_pairs, pair, 0)
        if nk % 2 == 0:
            scores(nk - 1, sb_ref)
            absorb_chunk(nk - 2, sa_ref)
            absorb_chunk(nk - 1, sb_ref)
        else:
            absorb_chunk(nk - 1, sa_ref)

    @pl.when(qi < nctx_q)
    def _():
        absorb(_dot_nt(q2, k_ref[0:ctx_len, :]), v_ref[0:ctx_len, :])

    acc = acc_ref[...]
    o = acc[:, :HEAD_DIM] / acc[:, HEAD_DIM:HEAD_DIM + 1]
    o_ref[...] = jnp.concatenate([o[:tq], o[tq:]], axis=1).astype(o_ref.dtype)


def _attention(q, k, v, n_batch, tb, ctx_len, mixw):
    t = q.shape[0]
    n_kv = mixw // HEAD_DIM // 2
    tq = _pick(math.gcd(ctx_len, tb), (256, 128))
    tk = _pick(tb, (768, 1024, 512, 640, 384, 256, 128))
    nq, nk = tb // tq, tb // tk
    return pl.pallas_call(
        functools.partial(_attn_kernel, tq=tq, tk=tk, nk=nk, nctx_q=ctx_len // tq, ctx_len=ctx_len),
        grid=(n_batch, n_kv, nq),
        in_specs=[pl.BlockSpec((tq, 2 * HEAD_DIM), lambda b, j, qi: (b * nq + qi, j)),
                  pl.BlockSpec((tb, HEAD_DIM), lambda b, j, qi: (b, j)),
                  pl.BlockSpec((tb, 2 * HEAD_DIM), lambda b, j, qi: (b, j))],
        out_specs=pl.BlockSpec((tq, 2 * HEAD_DIM), lambda b, j, qi: (b * nq + qi, j)),
        out_shape=jax.ShapeDtypeStruct((t, mixw), BF16),
        scratch_shapes=[pltpu.VMEM((2 * tq, 1), F32), pltpu.VMEM((2 * tq, 2 * HEAD_DIM), F32),
                        pltpu.VMEM((2 * tq, tk), F32), pltpu.VMEM((2 * tq, tk), F32)],
        compiler_params=_cparams(("parallel", "parallel", "parallel")),
        name="gqa_attention",
    )(q, k, v)


def _c_kernel(u_ref, v_ref, ws_ref, bs_ref, o_ref, *, tm, mixw):
    n_groups = mixw // HEAD_DIM
    for ci in range(tm // CHUNK_C):
        r0 = ci * CHUNK_C
        u = _gelu_tanh(u_ref[r0:r0 + CHUNK_C, :])
        vn = _ln(_gelu_tanh(v_ref[r0:r0 + CHUNK_C, :])).astype(BF16)
        for g in range(n_groups):
            c0 = g * HEAD_DIM
            vm = _dot(ws_ref[g], vn[:, c0:c0 + HEAD_DIM]) + bs_ref[:, g:g + 1]
            o_ref[r0:r0 + CHUNK_C, c0:c0 + HEAD_DIM] = (u[:, c0:c0 + HEAD_DIM] * vm).astype(o_ref.dtype)


def _mixer_c(p, ws, bs_t, tm, mixw, col_u, col_v):
    t = p.shape[0]
    n_groups = mixw // HEAD_DIM
    return pl.pallas_call(
        functools.partial(_c_kernel, tm=tm, mixw=mixw),
        grid=(t // tm,),
        in_specs=[pl.BlockSpec((tm, mixw), lambda i: (i, col_u // mixw)),
                  pl.BlockSpec((tm, mixw), lambda i: (i, col_v // mixw)),
                  pl.BlockSpec((n_groups, CHUNK_C, CHUNK_C), lambda i: (0, 0, 0)),
                  pl.BlockSpec((CHUNK_C, 128), lambda i: (0, 0))],
        out_specs=pl.BlockSpec((tm, mixw), lambda i: (i, 0)),
        out_shape=jax.ShapeDtypeStruct((t, mixw), BF16),
        compiler_params=_cparams(("parallel",)),
        name="gmlp",
    )(p, p, ws, bs_t)


def _hgrn_tables():
    tau = np.arange(CHUNK)
    sums_t = [tau[None, :] <= tau[:, None], tau[None, :] > tau[:, None]]
    pair_t = []
    for lvl in range(N_LEVELS):
        m = CHUNK >> (lvl + 1)
        blk = tau // (2 * m)
        upper = (tau % (2 * m)) >= m
        ref = blk * 2 * m + m - 1
        r = tau[None, :]
        w_up = upper[:, None] & (r > ref[:, None]) & (r <= tau[:, None])
        w_lo = (~upper)[:, None] & (r > tau[:, None]) & (r <= ref[:, None])
        sums_t.append(w_up | w_lo)
        pair_t.append((blk[:, None] == blk[None, :]) & upper[:, None] & (~upper)[None, :])
    sums_t = np.concatenate(sums_t, axis=0).astype(np.float32)
    pair_t = np.stack(pair_t).astype(np.float32)
    flip = tau[::-1]
    sums, pair = [], []
    for d in range(2):
        if d == 0:
            s_d, p_d = sums_t, pair_t
        else:
            s_d = sums_t.reshape(-1, CHUNK, CHUNK)[:, flip][:, :, flip].reshape(-1, CHUNK)
            p_d = pair_t[:, flip][:, :, flip]
        sums.append(np.concatenate([s_d, s_d, s_d], axis=1))
        pair.append(p_d)
    return np.stack(sums), np.stack(pair)


def _d_scan_kernel(qf, vf, ff, qr, vr, fr, lb_ref, sums_ref, pair_ref, of_ref, or_ref, st_ref, *, n_heads):
    n = pl.program_id(1)

    @pl.when(n == 0)
    def _():
        st_ref[...] = jnp.zeros_like(st_ref)

    c = CHUNK
    row = lax.broadcasted_iota(jnp.int32, (c, c), 0)
    col = lax.broadcasted_iota(jnp.int32, (c, c), 1)
    eye = (row == col).astype(F32)
    inst = []
    for d, (q_ref, v_ref, f_ref, o_ref) in enumerate(((qf, vf, ff, of_ref), (qr, vr, fr, or_ref))):
        for pos, ci in enumerate((0, 1) if d == 0 else (1, 0)):
            for h in range(n_heads):
                inst.append((d, pos, ci * c, h * HEAD_DIM, h, q_ref, v_ref, f_ref, o_ref))
    qs = [q_ref[r0:r0 + c, c0:c0 + HEAD_DIM] for (d, pos, r0, c0, h, q_ref, v_ref, f_ref, o_ref) in inst]
    vs = [v_ref[r0:r0 + c, c0:c0 + HEAD_DIM].astype(BF16) for (d, pos, r0, c0, h, q_ref, v_ref, f_ref, o_ref) in inst]
    es = [jnp.exp(-jnp.clip(f_ref[r0:r0 + c, c0:c0 + HEAD_DIM], -GATE_CLIP, GATE_CLIP))
          for (d, pos, r0, c0, h, q_ref, v_ref, f_ref, o_ref) in inst]
    sgs = [1.0 / (1.0 + e) for e in es]
    lbs = [lb_ref[0:1, it[3]:it[3] + HEAD_DIM] for it in inst]
    logfs = [jnp.log(lb + (1.0 - lb) * sg) for lb, sg in zip(lbs, sgs)]
    ks = [(1.0 - lb) * (e * sg) for lb, e, sg in zip(lbs, es, sgs)]
    his = [x.astype(BF16) for x in logfs]
    r1s = [x - hi.astype(F32) for x, hi in zip(logfs, his)]
    mids = [x.astype(BF16) for x in r1s]
    los = [(x - mid.astype(F32)).astype(BF16) for x, mid in zip(r1s, mids)]
    xs = [_dot(sums_ref[it[0]], jnp.concatenate([hi, mid, lo], axis=0)) for it, hi, mid, lo in zip(inst, his, mids, los)]
    kbs = [k.astype(BF16) for k in ks]
    attns = [eye * _dot_nt(q.astype(BF16), kb) for q, kb in zip(qs, kbs)]
    for lvl in range(N_LEVELS):
        zls = [jnp.exp(x[(2 + lvl) * c:(3 + lvl) * c]) for x in xs]
        attns = [a + pair_ref[it[0], lvl] * _dot_nt((q * zl).astype(BF16), (k * zl).astype(BF16))
                 for a, it, q, k, zl in zip(attns, inst, qs, ks, zls)]
    qds = [(q * jnp.exp(x[0:c])).astype(BF16) for q, x in zip(qs, xs)]
    kds = [(k * jnp.exp(x[c:2 * c])).astype(BF16) for k, x in zip(ks, xs)]
    intra = [_dot(a.astype(BF16), v) for a, v in zip(attns, vs)]
    kvs = [_dot_tn(v, kd) for v, kd in zip(vs, kds)]
    states = [[st_ref[d, h] for h in range(n_heads)] for d in range(2)]
    for pos in range(2):
        for i, (d, p_i, r0, c0, h, q_ref, v_ref, f_ref, o_ref) in enumerate(inst):
            if p_i != pos:
                continue
            st = states[d][h]
            o_ref[r0:r0 + c, c0:c0 + HEAD_DIM] = _dot_nt(qds[i], st.astype(BF16)) + intra[i]
            last = c - 1 if d == 0 else 0
            states[d][h] = st * jnp.exp(xs[i][last:last + 1, :]) + kvs[i]
    for d in range(2):
        for h in range(n_heads):
            st_ref[d, h] = states[d][h]


def _d_scan(p, lb, n_batch, tb, ctx_len, mixw, col_q, col_i, col_f):
    t = p.shape[0]
    n_heads = mixw // HEAD_DIM
    nsb = tb // STEP_ROWS
    fwd, rev = _seq_block_maps(nsb, ctx_len // STEP_ROWS)
    sums_np, pair_np = _hgrn_tables()
    sums = jnp.asarray(sums_np, BF16)
    pair = jnp.asarray(pair_np, F32)
    in_specs = []
    for d, m in enumerate((fwd, rev)):
        in_specs += [pl.BlockSpec((STEP_ROWS, mixw), lambda b, n, m=m: (m(b, n), col_q // mixw)),
                     pl.BlockSpec((STEP_ROWS, mixw), lambda b, n, m=m: (m(b, n), col_i // mixw)),
                     pl.BlockSpec((STEP_ROWS, mixw), lambda b, n, m=m, d=d: (m(b, n), col_f // mixw + d))]
    in_specs += [pl.BlockSpec((1, mixw), lambda b, n: (0, 0)),
                 pl.BlockSpec(sums.shape, lambda b, n: (0, 0, 0)),
                 pl.BlockSpec(pair.shape, lambda b, n: (0, 0, 0, 0))]
    return pl.pallas_call(
        functools.partial(_d_scan_kernel, n_heads=n_heads),
        grid=(n_batch, nsb),
        in_specs=in_specs,
        out_specs=[pl.BlockSpec((STEP_ROWS, mixw), lambda b, n: (fwd(b, n), 0)),
                   pl.BlockSpec((STEP_ROWS, mixw), lambda b, n: (rev(b, n), 0))],
        out_shape=[jax.ShapeDtypeStruct((t, mixw), F32)] * 2,
        scratch_shapes=[pltpu.VMEM((2, n_heads, HEAD_DIM, HEAD_DIM), F32)],
        compiler_params=_cparams(("arbitrary", "arbitrary")),
        name="hgrn2_scan",
    )(p, p, p, p, p, p, lb, sums, pair)


def _route(logits_t, bias, n_experts):
    per = n_experts // N_EXPERT_GROUPS
    scores = _sigmoid(logits_t)
    sel = scores + bias
    rows_sel = [sel[e:e + 1, :] for e in range(n_experts)]
    rows_sc = [scores[e:e + 1, :] for e in range(n_experts)]

    def top2_sum(vals):
        hi = vals[0]
        lo = jnp.full_like(hi, -jnp.inf)
        for x in vals[1:]:
            lo = jnp.maximum(lo, jnp.minimum(hi, x))
            hi = jnp.maximum(hi, x)
        return hi + lo

    best = jnp.zeros_like(rows_sel[0], dtype=jnp.int32)
    best_score = top2_sum(rows_sel[0:per])
    for g in range(1, N_EXPERT_GROUPS):
        gs = top2_sum(rows_sel[g * per:(g + 1) * per])
        better = gs > best_score
        best = jnp.where(better, g, best)
        best_score = jnp.where(better, gs, best_score)
    in_sel, in_sc = [], []
    for j in range(per):
        a = rows_sel[j]
        c = rows_sc[j]
        for g in range(1, N_EXPERT_GROUPS):
            a = jnp.where(best == g, rows_sel[g * per + j], a)
            c = jnp.where(best == g, rows_sc[g * per + j], c)
        in_sel.append(a)
        in_sc.append(c)

    def first_argmax(vals):
        idx = jnp.zeros_like(best)
        top = vals[0]
        for j in range(1, per):
            better = vals[j] > top
            idx = jnp.where(better, j, idx)
            top = jnp.where(better, vals[j], top)
        return idx

    i1 = first_argmax(in_sel)
    i2 = first_argmax([jnp.where(i1 == j, -jnp.inf, in_sel[j]) for j in range(per)])

    def pick(idx):
        w = in_sc[0]
        for j in range(1, per):
            w = jnp.where(idx == j, in_sc[j], w)
        return w

    w1, w2 = pick(i1), pick(i2)
    tot = w1 + w2
    return best * per + i1, best * per + i2, w1 / tot, w2 / tot


def _outproj_kernel(oaf, oar, ga, odf, odr, gd, yb, yc, na, nd, wo, x_ref, m_ref, lng, lnb, wr, br,
                    x1_ref, h2_ref, ei_ref, ew_ref, *, alpha, mixw, tm, n_experts):
    n_heads = mixw // HEAD_DIM

    def gated(of_ref, or_ref, g_ref, gain_ref):
        parts = []
        for h in range(n_heads):
            c0 = h * HEAD_DIM
            o = of_ref[:, c0:c0 + HEAD_DIM] + or_ref[:, c0:c0 + HEAD_DIM]
            y = o * lax.rsqrt(jnp.mean(o * o, axis=-1, keepdims=True) + EPS) * gain_ref[...]
            parts.append((y * _silu(g_ref[:, c0:c0 + HEAD_DIM])).astype(BF16))
        return jnp.concatenate(parts, axis=1)

    ya = gated(oaf, oar, ga, na)
    yd = gated(odf, odr, gd, nd)
    y = (_dot(ya, wo[0:mixw, :]) + _dot(yb[...], wo[mixw:2 * mixw, :])
         + _dot(yc[...], wo[2 * mixw:3 * mixw, :]) + _dot(yd, wo[3 * mixw:4 * mixw, :]))
    x1 = _ln(alpha * x_ref[...] + m_ref[0, 2:3, :] * y) * lng[...] + lnb[...]
    x1_ref[...] = x1
    h2 = _ln(x1) * (1.0 + m_ref[0, 4:5, :]) + m_ref[0, 3:4, :]
    h2_ref[...] = h2
    logits_t = _dot(h2, wr[...], HIGHEST).T[:n_experts]
    e1, e2, w1, w2 = _route(logits_t, br[:, 0:1], n_experts)
    zi = jnp.zeros((6, tm), jnp.int32)
    ei_ref[...] = jnp.concatenate([e1, e2, zi], axis=0)
    ew_ref[...] = jnp.concatenate([w1, w2, zi.astype(F32)], axis=0)


def _outproj(oaf, oar, odf, odr, p, yb, yc, norm_a, norm_d, w_out, x, mods, ln_g, ln_b, w_router_p, b_router,
             alpha, tm, nb, nctx, n_batch, mixw, col_ga, col_gd):
    t, d = x.shape
    n_experts = b_router.shape[0]
    row = lambda i: (i, 0)
    const = lambda i: (0, 0)
    kern = functools.partial(_outproj_kernel, alpha=alpha, mixw=mixw, tm=tm, n_experts=n_experts)
    return pl.pallas_call(
        kern,
        grid=(t // tm,),
        in_specs=[pl.BlockSpec((tm, mixw), row), pl.BlockSpec((tm, mixw), row),
                  pl.BlockSpec((tm, mixw), lambda i: (i, col_ga // mixw)),
                  pl.BlockSpec((tm, mixw), row), pl.BlockSpec((tm, mixw), row),
                  pl.BlockSpec((tm, mixw), lambda i: (i, col_gd // mixw)),
                  pl.BlockSpec((tm, mixw), row), pl.BlockSpec((tm, mixw), row),
                  pl.BlockSpec((1, HEAD_DIM), const), pl.BlockSpec((1, HEAD_DIM), const),
                  pl.BlockSpec((N_MIXERS * mixw, d), const),
                  pl.BlockSpec((tm, d), row),
                  pl.BlockSpec((1, 6, d), lambda i: (_mod_row(i, nb, nctx, n_batch), 0, 0)),
                  pl.BlockSpec((1, d), const), pl.BlockSpec((1, d), const),
                  pl.BlockSpec((d, 128), const), pl.BlockSpec((n_experts, 128), const)],
        out_specs=[pl.BlockSpec((tm, d), row), pl.BlockSpec((tm, d), row),
                   pl.BlockSpec((8, tm), lambda i: (0, i)), pl.BlockSpec((8, tm), lambda i: (0, i))],
        out_shape=[jax.ShapeDtypeStruct((t, d), F32), jax.ShapeDtypeStruct((t, d), F32),
                   jax.ShapeDtypeStruct((8, t), jnp.int32), jax.ShapeDtypeStruct((8, t), F32)],
        compiler_params=_cparams(("parallel",)),
        name="out_proj_router",
    )(oaf, oar, p, odf, odr, p, yb, yc, norm_a, norm_d, w_out, x, mods, ln_g, ln_b, w_router_p, b_router)


def _rank_kernel(ei_ref, su_ref, rank_ref, cnt_ref, base_ref, *, n_experts, tm):
    i = pl.program_id(0)

    @pl.when(i == 0)
    def _():
        base_ref[...] = jnp.zeros_like(base_ref)

    eid = lax.broadcasted_iota(jnp.int32, (n_experts, tm), 0)
    o1 = (eid == ei_ref[0:1, :]).astype(F32)
    o2 = (eid == ei_ref[1:2, :]).astype(F32)
    cnt = o1 + o2
    before = _dot(cnt.astype(BF16), su_ref[...]) + base_ref[:, 0:1]
    r1 = jnp.sum(o1 * before, axis=0, keepdims=True)
    r2 = jnp.sum(o2 * before, axis=0, keepdims=True)
    rank_ref[...] = jnp.concatenate([r1, r2, jnp.zeros((6, tm), F32)], axis=0).astype(jnp.int32)
    base_ref[...] = base_ref[...] + jnp.sum(cnt, axis=1, keepdims=True)
    cnt_ref[...] = base_ref[...]


def _slot_ranks(ei, n_experts, tm):
    t = ei.shape[1]
    r = np.arange(tm)
    su = jnp.asarray((r[:, None] < r[None, :]).astype(np.float32), BF16)
    return pl.pallas_call(
        functools.partial(_rank_kernel, n_experts=n_experts, tm=tm),
        grid=(t // tm,),
        in_specs=[pl.BlockSpec((8, tm), lambda i: (0, i)), pl.BlockSpec((tm, tm), lambda i: (0, 0))],
        out_specs=[pl.BlockSpec((8, tm), lambda i: (0, i)), pl.BlockSpec((n_experts, 128), lambda i: (0, 0))],
        out_shape=[jax.ShapeDtypeStruct((8, t), jnp.int32), jax.ShapeDtypeStruct((n_experts, 128), F32)],
        scratch_shapes=[pltpu.VMEM((n_experts, 128), F32)],
        compiler_params=_cparams(("arbitrary",)),
        name="moe_slot_ranks",
    )(ei, su)


def _row_copy(src_ref, src_row, dst_ref, dst_row, sem):
    return pltpu.make_async_copy(src_ref.at[pl.ds(src_row, 1)], dst_ref.at[pl.ds(dst_row, 1)], sem)


def _dispatch_kernel(dest_ref, h_ref, xs_in_ref, xs_ref, sem, *, tm, t):
    del xs_in_ref
    base = pl.program_id(0) * tm

    def issue(r, carry):
        for kk in range(TOP_K):
            _row_copy(h_ref, r, xs_ref, dest_ref[kk * t + base + r], sem).start()
        return carry

    lax.fori_loop(0, tm, issue, 0)

    def drain(r, carry):
        for kk in range(TOP_K):
            _row_copy(h_ref, 0, xs_ref, 0, sem).wait()
        return carry

    lax.fori_loop(0, tm, drain, 0)


def _dispatch(dest, h2, n_slots, tm):
    t, d = h2.shape
    xs0 = jnp.zeros((n_slots, d), h2.dtype)
    return pl.pallas_call(
        functools.partial(_dispatch_kernel, tm=tm, t=t),
        grid_spec=pltpu.PrefetchScalarGridSpec(
            num_scalar_prefetch=1,
            grid=(t // tm,),
            in_specs=[pl.BlockSpec((tm, d), lambda i, dest: (i, 0)),
                      pl.BlockSpec(memory_space=pl.ANY)],
            out_specs=pl.BlockSpec(memory_space=pl.ANY),
            scratch_shapes=[pltpu.SemaphoreType.DMA(())]),
        out_shape=jax.ShapeDtypeStruct((n_slots, d), h2.dtype),
        input_output_aliases={2: 0},
        compiler_params=_cparams(("arbitrary",)),
        name="moe_dispatch",
    )(dest, h2, xs0)


def _expert_kernel(be_ref, na_ref, x_ref, w1_ref, w3_ref, w2_ref, y_ref):
    i = pl.program_id(0)

    @pl.when(i < na_ref[0])
    def _():
        xb = x_ref[...].astype(BF16)
        h1 = _dot(xb, w1_ref[0])
        h3 = _dot(xb, w3_ref[0])
        y_ref[...] = _dot((_silu(h1) * h3).astype(BF16), w2_ref[0])

    @pl.when(i >= na_ref[0])
    def _():
        y_ref[...] = jnp.zeros_like(y_ref)


def _experts(block_expert, n_active, xs, w1, w3, w2, tmx):
    n_slots, d = xs.shape
    f = w1.shape[2]
    return pl.pallas_call(
        _expert_kernel,
        grid_spec=pltpu.PrefetchScalarGridSpec(
            num_scalar_prefetch=2,
            grid=(n_slots // tmx,),
            in_specs=[pl.BlockSpec((tmx, d), lambda i, be, na: (i, 0)),
                      pl.BlockSpec((1, d, f), lambda i, be, na: (be[i], 0, 0)),
                      pl.BlockSpec((1, d, f), lambda i, be, na: (be[i], 0, 0)),
                      pl.BlockSpec((1, f, d), lambda i, be, na: (be[i], 0, 0))],
            out_specs=pl.BlockSpec((tmx, d), lambda i, be, na: (i, 0))),
        out_shape=jax.ShapeDtypeStruct((n_slots, d), F32),
        compiler_params=_cparams(("arbitrary",)),
        name="moe_experts",
    )(block_expert, n_active, xs, w1, w3, w2)


def _combine_kernel(dest_ref, ys_ref, ew_ref, x1_ref, m_ref, lng, lnb, mn_ref, x2_ref, hn_ref, g_ref, sem,
                    *, tm, t, alpha):
    base = pl.program_id(0) * tm

    def issue(r, carry):
        for kk in range(TOP_K):
            _row_copy(ys_ref, dest_ref[kk * t + base + r], g_ref.at[kk], r, sem).start()
        return carry

    lax.fori_loop(0, tm, issue, 0)

    def drain(r, carry):
        for kk in range(TOP_K):
            _row_copy(ys_ref, 0, g_ref.at[kk], 0, sem).wait()
        return carry

    lax.fori_loop(0, tm, drain, 0)
    row = lax.broadcasted_iota(jnp.int32, (tm, tm), 0)
    col = lax.broadcasted_iota(jnp.int32, (tm, tm), 1)
    w_t = _dot_nt((row == col).astype(F32), ew_ref[...], HIGHEST)
    f = w_t[:, 0:1] * g_ref[0] + w_t[:, 1:2] * g_ref[1]
    x2 = _ln(alpha * x1_ref[...] + m_ref[0, 5:6, :] * f) * lng[...] + lnb[...]
    x2_ref[...] = x2
    hn_ref[...] = (_ln(x2) * (1.0 + mn_ref[0, 1:2, :]) + mn_ref[0, 0:1, :]).astype(hn_ref.dtype)


def _combine(dest, ys, ew, x1, mods, ln_g, ln_b, mods_next, alpha, tm, nb, nctx, n_batch):
    t, d = x1.shape
    mrow = lambda i, dest: (_mod_row(i, nb, nctx, n_batch), 0, 0)
    const = lambda i, dest: (0, 0)
    return pl.pallas_call(
        functools.partial(_combine_kernel, tm=tm, t=t, alpha=alpha),
        grid_spec=pltpu.PrefetchScalarGridSpec(
            num_scalar_prefetch=1,
            grid=(t // tm,),
            in_specs=[pl.BlockSpec(memory_space=pl.ANY),
                      pl.BlockSpec((8, tm), lambda i, dest: (0, i)),
                      pl.BlockSpec((tm, d), lambda i, dest: (i, 0)),
                      pl.BlockSpec((1, 6, d), mrow),
                      pl.BlockSpec((1, d), const), pl.BlockSpec((1, d), const),
                      pl.BlockSpec((1, 6, d), mrow)],
            out_specs=[pl.BlockSpec((tm, d), lambda i, dest: (i, 0)),
                       pl.BlockSpec((tm, d), lambda i, dest: (i, 0))],
            scratch_shapes=[pltpu.VMEM((TOP_K, tm, d), F32), pltpu.SemaphoreType.DMA(())]),
        out_shape=[jax.ShapeDtypeStruct((t, d), F32), jax.ShapeDtypeStruct((t, d), BF16)],
        compiler_params=_cparams(("arbitrary",)),
        name="moe_combine",
    )(dest, ys, ew, x1, mods, ln_g, ln_b, mods_next)


def _axial_tables(seq, ctx_len):
    rows = seq // GRID_W
    row = jnp.repeat(jnp.arange(rows, dtype=F32), GRID_W)
    col = jnp.tile(jnp.arange(GRID_W, dtype=F32), rows)
    n_freq = HEAD_DIM // 4
    inv_freq = ROPE_THETA ** (-jnp.arange(n_freq, dtype=F32) / n_freq)
    ang = jnp.concatenate([row[:, None] * inv_freq, col[:, None] * inv_freq], axis=-1)
    cos, sin = jnp.cos(ang), jnp.sin(ang)
    cosf = jnp.concatenate([cos, cos], axis=-1)
    sinf = jnp.concatenate([-sin, sin], axis=-1)
    cosf = jnp.concatenate([jnp.ones((ctx_len, HEAD_DIM), F32), cosf], axis=0)
    sinf = jnp.concatenate([jnp.zeros((ctx_len, HEAD_DIM), F32), sinf], axis=0)
    return cosf, sinf


def kernel(x, c, ctx, c_ctx, w_ada, b_ada, ln_g, ln_b, w_in, conv_a, a_log, dt_bias, norm_a, qk_norm_b, ws_c, bs_c, lb_d, norm_d, w_out, w_router, b_router, w1, w3, w2):
    n_batch, seq, d = x.shape
    ctx_len = ctx.shape[1]
    n_layers = w_in.shape[0]
    n_experts = w_router.shape[1]
    mixw = d // N_MIXERS
    n_heads = mixw // HEAD_DIM
    tb = ctx_len + seq
    t = n_batch * tb
    alpha = (2.0 * n_layers) ** 0.25
    tm = _pick(math.gcd(ctx_len, seq), (256, 128))
    nb, nctx = tb // tm, ctx_len // tm

    col = {"a_qkv": 0, "a_gate": 3 * mixw, "d_f": 4 * mixw, "b_q": 6 * mixw, "b_kv": 7 * mixw, "c_u": 8 * mixw,
           "c_v": 9 * mixw, "d_q": 10 * mixw, "d_i": 11 * mixw, "d_gate": 12 * mixw, "small": 13 * mixw}
    n_proj = 13 * mixw + 256
    src = np.cumsum([0, 3 * mixw, mixw, 2 * n_heads, 2 * n_heads, mixw, mixw, mixw, mixw, mixw, mixw, mixw, 2 * mixw])
    s_qkv, s_ga, s_beta, s_dec, s_bq, s_bkv, s_cu, s_cv, s_dq, s_di, s_dg, s_df, s_end = [int(v) for v in src]

    def permute_w_in(w):
        pad = jnp.zeros((d, n_proj - 13 * mixw - 4 * n_heads), w.dtype)
        return jnp.concatenate([w[:, s_qkv:s_beta], w[:, s_df:s_end], w[:, s_bq:s_df], w[:, s_beta:s_bq], pad],
                               axis=1).astype(BF16)

    cc = jnp.zeros((8, d), F32).at[:n_batch].set(c).at[n_batch].set(c_ctx)
    mods_all = _mod_vectors(cc, w_ada, b_ada)[:, :n_batch + 1].reshape(n_layers, n_batch + 1, 6, d)

    soft = jax.nn.softmax(lb_d.astype(F32), axis=0)
    lb_all = jnp.cumsum(soft, axis=0) - soft[0]
    cosf, sinf = _axial_tables(seq, ctx_len)
    gate_lanes = jnp.zeros((1, 128), F32)
    w_router_p = jnp.zeros((d, 128), F32).at[:, :n_experts].set(w_router.astype(F32))
    b_router_c = jnp.broadcast_to(b_router.astype(F32)[:, None], (n_experts, 128))

    tmx = 256
    n_assign = t * TOP_K
    n_slots = (-(-n_assign // tmx) + n_experts) * tmx

    xu = jnp.concatenate([ctx, x], axis=1).reshape(t, d)
    h = _ln_modulate(xu, mods_all[0], tm, nb, nctx, n_batch)
    for l in range(n_layers):
        mods = mods_all[l]
        p = _matmul(h, permute_w_in(w_in[l]))
        neg_a = gate_lanes.at[0, 2 * n_heads:4 * n_heads].set(-jnp.exp(a_log[l].astype(F32)).reshape(-1))
        dtb = gate_lanes.at[0, 2 * n_heads:4 * n_heads].set(dt_bias[l].astype(F32).reshape(-1))
        prep_a = _a_prep(p, conv_a[l], neg_a, dtb, tm, nb, nctx, mixw, col["small"])
        oaf, oar = _a_scan(prep_a, n_batch, tb, ctx_len, mixw)
        qb, kb, vb = _b_prep(p, qk_norm_b[l], cosf, sinf, tm, nb, mixw, col["b_q"], col["b_kv"])
        yb = _attention(qb, kb, vb, n_batch, tb, ctx_len, mixw)
        bs_t = jnp.zeros((CHUNK_C, 128), F32).at[:, :bs_c.shape[1]].set(bs_c[l].T)
        yc = _mixer_c(p, ws_c[l].astype(BF16), bs_t, tm, mixw, col["c_u"], col["c_v"])
        odf, odr = _d_scan(p, lb_all[l][None, :], n_batch, tb, ctx_len, mixw, col["d_q"], col["d_i"], col["d_f"])
        x1, h2, ei, ew = _outproj(oaf, oar, odf, odr, p, yb, yc, norm_a[l][None, :], norm_d[l][None, :],
                                  w_out[l].astype(BF16), xu, mods, ln_g[l, 0][None, :], ln_b[l, 0][None, :],
                                  w_router_p, b_router_c, alpha, tm, nb, nctx, n_batch, mixw,
                                  col["a_gate"], col["d_gate"])
        rank, counts = _slot_ranks(ei, n_experts, tm)
        cnt = counts[:, 0].astype(jnp.int32)
        padded = (cnt + tmx - 1) // tmx * tmx
        pad_end = jnp.cumsum(padded)
        pad_start = pad_end - padded
        eids = jnp.arange(n_experts, dtype=jnp.int32)[:, None, None]
        dest = (rank[:TOP_K] + jnp.sum(jnp.where(ei[None, :TOP_K] == eids, pad_start[:, None, None], 0),
                                       axis=0)).reshape(-1)
        n_blocks = n_slots // tmx
        block_start = jnp.arange(n_blocks, dtype=jnp.int32) * tmx
        block_expert = jnp.minimum(jnp.sum((pad_end[None, :] <= block_start[:, None]).astype(jnp.int32), axis=1),
                                   n_experts - 1)
        n_active = (pad_end[-1:] // tmx).astype(jnp.int32)
        xs = _dispatch(dest, h2, n_slots, tm)
        ys = _experts(block_expert, n_active, xs, w1[l].astype(BF16), w3[l].astype(BF16), w2[l].astype(BF16), tmx)
        mods_next = mods_all[min(l + 1, n_layers - 1)]
        xu, h = _combine(dest, ys, ew, x1, mods, ln_g[l, 1][None, :], ln_b[l, 1][None, :], mods_next, alpha,
                         tm, nb, nctx, n_batch)
    return xu.reshape(n_batch, tb, d)[:, ctx_len:]
```

```python
import functools
import math

import numpy as np
import jax
import jax.numpy as jnp
from jax import lax
from jax.experimental import pallas as pl
from jax.experimental.pallas import tpu as pltpu

F32 = jnp.float32
BF16 = jnp.bfloat16
HIGHEST = lax.Precision.HIGHEST

HEAD_DIM = 128
N_MIXERS = 4
CONV_W = 5
CHUNK = 64
STEP_ROWS = 2 * CHUNK
CHUNK_C = 128
GRID_W = 64
ROPE_THETA = 10000.0
N_EXPERT_GROUPS = 4
TOP_K = 2
EPS = 1e-6
LOG2_E = math.log2(math.e)
GATE_CLIP = 30.0
N_LEVELS = 6
HALO = 8
V7X_VMEM_LIMIT = 56 * 1024 * 1024


def _pick(n, cands):
    for c in cands:
        if n % c == 0:
            return c
    raise ValueError(f"no tile in {cands} divides {n}")


def _cparams(sem, vmem=None):
    return pltpu.CompilerParams(dimension_semantics=sem, vmem_limit_bytes=vmem or V7X_VMEM_LIMIT)


def _dot(a, b, precision=None):
    return jnp.dot(a, b, preferred_element_type=F32, precision=precision)


def _dot_nt(a, b, precision=None):
    return lax.dot_general(a, b, (((1,), (1,)), ((), ())), preferred_element_type=F32, precision=precision)


def _dot_tn(a, b, precision=None):
    return lax.dot_general(a, b, (((0,), (0,)), ((), ())), preferred_element_type=F32, precision=precision)


def _sigmoid(x):
    return 1.0 / (1.0 + jnp.exp(-x))


def _silu(x):
    return x * _sigmoid(x)


def _gelu_tanh(x):
    return 0.5 * x * (1.0 + jnp.tanh(math.sqrt(2.0 / math.pi) * (x + 0.044715 * (x * x * x))))


def _ln(x):
    mu = jnp.mean(x, axis=-1, keepdims=True)
    xc = x - mu
    return xc * lax.rsqrt(jnp.mean(xc * xc, axis=-1, keepdims=True) + EPS)


def _mod_row(i, nb, nctx, n_batch):
    return jnp.where(i % nb < nctx, n_batch, i // nb)


def _mod_kernel(c_ref, w_ref, b_ref, o_ref):
    c = c_ref[...]
    o_ref[0] = _dot(_silu(c), w_ref[0], HIGHEST) + b_ref[0]


def _mod_vectors(cc, w_ada, b_ada):
    n_layers, d, n6 = w_ada.shape
    tn = _pick(n6, (1024, 512, 128))
    return pl.pallas_call(
        _mod_kernel,
        grid=(n_layers, n6 // tn),
        in_specs=[pl.BlockSpec((8, d), lambda l, j: (0, 0)),
                  pl.BlockSpec((1, d, tn), lambda l, j: (l, 0, j)),
                  pl.BlockSpec((1, 1, tn), lambda l, j: (l, 0, j))],
        out_specs=pl.BlockSpec((1, 8, tn), lambda l, j: (l, 0, j)),
        out_shape=jax.ShapeDtypeStruct((n_layers, 8, n6), F32),
        compiler_params=_cparams(("parallel", "parallel")),
        name="mod_vectors",
    )(cc, w_ada, b_ada.reshape(n_layers, 1, n6))


def _lnmod_kernel(x_ref, m_ref, h_ref):
    y = _ln(x_ref[...])
    h_ref[...] = (y * (1.0 + m_ref[0, 1:2, :]) + m_ref[0, 0:1, :]).astype(h_ref.dtype)


def _ln_modulate(x, mods, tm, nb, nctx, n_batch):
    t, d = x.shape
    return pl.pallas_call(
        _lnmod_kernel,
        grid=(t // tm,),
        in_specs=[pl.BlockSpec((tm, d), lambda i: (i, 0)),
                  pl.BlockSpec((1, 6, d), lambda i: (_mod_row(i, nb, nctx, n_batch), 0, 0))],
        out_specs=pl.BlockSpec((tm, d), lambda i: (i, 0)),
        out_shape=jax.ShapeDtypeStruct((t, d), BF16),
        compiler_params=_cparams(("parallel",)),
        name="ln_modulate",
    )(x, mods)


def _matmul_kernel(a_ref, w_ref, o_ref):
    o_ref[...] = _dot(a_ref[...], w_ref[...]).astype(o_ref.dtype)


def _matmul(a, w, out_dtype=F32):
    m, k = a.shape
    n = w.shape[1]
    tm = _pick(m, (1536, 1024, 768, 512, 256, 128))
    tn = _pick(n, (1152, 1024, 768, 512, 256, 128))
    return pl.pallas_call(
        _matmul_kernel,
        grid=(m // tm, n // tn),
        in_specs=[pl.BlockSpec((tm, k), lambda i, j: (i, 0)),
                  pl.BlockSpec((k, tn), lambda i, j: (0, j))],
        out_specs=pl.BlockSpec((tm, tn), lambda i, j: (i, j)),
        out_shape=jax.ShapeDtypeStruct((m, n), out_dtype),
        compiler_params=_cparams(("parallel", "parallel")),
        name="in_proj",
    )(a, w)


def _unit_tri_inverses(lows, eye, pair_refs):
    ts = [eye - low * pair_ref[0] for low, pair_ref in zip(lows, pair_refs)]
    for lvl in range(1, N_LEVELS):
        tbs = [t.astype(BF16) for t in ts]
        xs = [_dot(tb16, (low * pair_ref[lvl]).astype(BF16)) for tb16, low, pair_ref in zip(tbs, lows, pair_refs)]
        ts = [t - _dot(x.astype(BF16), tb16) for t, x, tb16 in zip(ts, xs, tbs)]
    return ts


def _a_prep_kernel(x_ref, xp_ref, xn_ref, sm_ref, cw_ref, na_ref, dtb_ref, trf_ref, trr_ref, pairf_ref, pairr_ref,
                   uf_ref, wqf_ref, kdf_ref, atf_ref, ur_ref, wqr_ref, kdr_ref, atr_ref, eg_ref, ext_ref,
                   *, tm, nb, nctx, mixw):
    i = pl.program_id(0)
    j = i % nb
    first = jnp.logical_or(j == 0, j == nctx)
    last = jnp.logical_or(j == nctx - 1, j == nb - 1)
    ext_ref[HALO:HALO + tm, :] = x_ref[...]
    ext_ref[0:HALO, :] = jnp.where(first, 0.0, xp_ref[...])
    ext_ref[HALO + tm:2 * HALO + tm, :] = jnp.where(last, 0.0, xn_ref[...])
    off = HALO - CONV_W // 2
    acc = cw_ref[0:1, :] * ext_ref[off:off + tm, :]
    for t in range(1, CONV_W):
        acc = acc + cw_ref[t:t + 1, :] * ext_ref[off + t:off + t + tm, :]
    y = _silu(acc)
    n_heads = mixw // HEAD_DIM
    sm = sm_ref[...]
    lane = lax.broadcasted_iota(jnp.int32, sm.shape, 1)
    zz = sm + dtb_ref[...]
    softplus = jnp.maximum(zz, 0.0) + jnp.log(1.0 + jnp.exp(-jnp.abs(zz)))
    g = jnp.where(jnp.logical_and(lane >= 2 * n_heads, lane < 4 * n_heads), na_ref[...] * softplus, 0.0)
    beta_all = _sigmoid(sm)
    tri_f = trf_ref[...]
    tri_r = trr_ref[...]
    cs_f = _dot(tri_f, g, HIGHEST)
    cs_r = _dot(tri_r, g, HIGHEST)
    gcum = jnp.where(lane < 3 * n_heads, cs_f, cs_r)
    gtot = _dot(jnp.maximum(tri_f, tri_r), g, HIGHEST)
    eg_ref[...] = jnp.exp(gtot)
    gcum_t = gcum.T
    row = lax.broadcasted_iota(jnp.int32, (tm, tm), 0)
    col = lax.broadcasted_iota(jnp.int32, (tm, tm), 1)
    eye = (row == col).astype(F32)
    n_chunks = tm // CHUNK
    outs = ((uf_ref, wqf_ref, kdf_ref, atf_ref, tri_f, pairf_ref), (ur_ref, wqr_ref, kdr_ref, atr_ref, tri_r, pairr_ref))
    items = []
    for h in range(n_heads):
        c0 = h * HEAD_DIM
        qh = y[:, c0:c0 + HEAD_DIM]
        kh = y[:, mixw + c0:mixw + c0 + HEAD_DIM]
        vh = y[:, 2 * mixw + c0:2 * mixw + c0 + HEAD_DIM]
        qh = qh * (lax.rsqrt(jnp.sum(qh * qh, axis=-1, keepdims=True) + EPS) * HEAD_DIM ** -0.5)
        kh = kh * lax.rsqrt(jnp.sum(kh * kh, axis=-1, keepdims=True) + EPS)
        kb16 = kh.astype(BF16)
        qk = _dot_nt(qh.astype(BF16), kb16)
        for d in range(2):
            tri = outs[d][4]
            lb = d * n_heads + h
            lg = (2 + d) * n_heads + h
            beta = beta_all[:, lb:lb + 1]
            gc = gcum[:, lg:lg + 1]
            incl = tri > 0.5
            decay = jnp.where(incl, jnp.exp(jnp.where(incl, gc - gcum_t[lg:lg + 1, :], 0.0)), 0.0)
            kbeta = kh * beta
            low = jnp.where(row == col, 0.0, _dot_nt(kbeta.astype(BF16), kb16) * decay)
            rhs = jnp.concatenate([vh * beta, kbeta * jnp.exp(gc)], axis=1).astype(BF16)
            items.append(dict(h=h, d=d, low=low, rhs=rhs, attn=(qk * decay).astype(BF16),
                              qd=(qh * jnp.exp(gc)).astype(BF16),
                              kd=(kh * jnp.exp(gtot[:, lg:lg + 1] - gc)).astype(BF16)))
    tinvs = _unit_tri_inverses([it["low"] for it in items], eye, [outs[it["d"]][5] for it in items])
    for it, tinv in zip(items, tinvs):
        u_ref, wq_ref, kd_ref, at_ref = outs[it["d"]][:4]
        h = it["h"]
        c0 = h * HEAD_DIM
        sol = _dot(tinv.astype(BF16), it["rhs"])
        u_ref[:, c0:c0 + HEAD_DIM] = sol[:, :HEAD_DIM]
        w = sol[:, HEAD_DIM:].astype(BF16)
        kd_ref[:, c0:c0 + HEAD_DIM] = it["kd"]
        for ci in range(n_chunks):
            r0 = ci * CHUNK
            wq_ref[2 * r0:2 * r0 + CHUNK, c0:c0 + HEAD_DIM] = w[r0:r0 + CHUNK]
            wq_ref[2 * r0 + CHUNK:2 * r0 + 2 * CHUNK, c0:c0 + HEAD_DIM] = it["qd"][r0:r0 + CHUNK]
            at_ref[r0:r0 + CHUNK, h * CHUNK:(h + 1) * CHUNK] = it["attn"][r0:r0 + CHUNK, r0:r0 + CHUNK]


def _a_prep(p, conv_w, neg_a, dtb, tm, nb, nctx, mixw, col_small):
    t = p.shape[0]
    c3 = 3 * mixw
    n_heads = mixw // HEAD_DIM
    r = np.arange(tm)
    same = (r[:, None] // CHUNK) == (r[None, :] // CHUNK)
    tri_f = jnp.asarray((same & (r[None, :] <= r[:, None])).astype(np.float32))
    tri_r = jnp.asarray((same & (r[None, :] >= r[:, None])).astype(np.float32))
    pair_f = []
    for lvl in range(N_LEVELS):
        s_blk = 1 << lvl
        joined = ((r[:, None] // (2 * s_blk)) == (r[None, :] // (2 * s_blk))) \
            & ((r[:, None] % (2 * s_blk)) >= s_blk) & ((r[None, :] % (2 * s_blk)) < s_blk)
        pair_f.append(joined.astype(np.float32))
    pair_f = np.stack(pair_f)
    pair_r = jnp.asarray(np.transpose(pair_f, (0, 2, 1)))
    pair_f = jnp.asarray(pair_f)
    hb = tm // HALO
    last_hb = t // HALO - 1
    kern = functools.partial(_a_prep_kernel, tm=tm, nb=nb, nctx=nctx, mixw=mixw)
    per_dir_specs = [pl.BlockSpec((tm, mixw), lambda i: (i, 0)),
                     pl.BlockSpec((2 * tm, mixw), lambda i: (i, 0)),
                     pl.BlockSpec((tm, mixw), lambda i: (i, 0)),
                     pl.BlockSpec((tm, n_heads * CHUNK), lambda i: (i, 0))]
    per_dir_shapes = [jax.ShapeDtypeStruct((t, mixw), F32), jax.ShapeDtypeStruct((2 * t, mixw), BF16),
                      jax.ShapeDtypeStruct((t, mixw), BF16), jax.ShapeDtypeStruct((t, n_heads * CHUNK), BF16)]
    return pl.pallas_call(
        kern,
        grid=(t // tm,),
        in_specs=[pl.BlockSpec((tm, c3), lambda i: (i, 0)),
                  pl.BlockSpec((HALO, c3), lambda i: (jnp.maximum(i * hb - 1, 0), 0)),
                  pl.BlockSpec((HALO, c3), lambda i: (jnp.minimum((i + 1) * hb, last_hb), 0)),
                  pl.BlockSpec((tm, 128), lambda i: (i, col_small // 128)),
                  pl.BlockSpec((CONV_W, c3), lambda i: (0, 0)),
                  pl.BlockSpec((1, 128), lambda i: (0, 0)),
                  pl.BlockSpec((1, 128), lambda i: (0, 0)),
                  pl.BlockSpec((tm, tm), lambda i: (0, 0)),
                  pl.BlockSpec((tm, tm), lambda i: (0, 0)),
                  pl.BlockSpec((N_LEVELS, tm, tm), lambda i: (0, 0, 0)),
                  pl.BlockSpec((N_LEVELS, tm, tm), lambda i: (0, 0, 0))],
        out_specs=per_dir_specs * 2 + [pl.BlockSpec((tm, 128), lambda i: (i, 0))],
        out_shape=per_dir_shapes * 2 + [jax.ShapeDtypeStruct((t, 128), F32)],
        scratch_shapes=[pltpu.VMEM((tm + 2 * HALO, c3), F32)],
        compiler_params=_cparams(("parallel",)),
        name="deltanet_prep",
    )(p, p, p, p, conv_w, neg_a, dtb, tri_f, tri_r, pair_f, pair_r)


def _a_scan_kernel(uf, wqf, kdf, atf, egf, ur, wqr, kdr, atr, egr, of_ref, or_ref, s_ref, sbd_ref, *, n_heads):
    n = pl.program_id(1)

    @pl.when(n == 0)
    def _():
        s_ref[...] = jnp.zeros_like(s_ref)
        sbd_ref[...] = jnp.zeros_like(sbd_ref)

    mixw = n_heads * HEAD_DIM
    lane_head = lax.broadcasted_iota(jnp.int32, (CHUNK, mixw), 1) // HEAD_DIM
    for d, (u_ref, wq_ref, kd_ref, at_ref, eg_ref, o_ref) in enumerate(
            ((uf, wqf, kdf, atf, egf, of_ref), (ur, wqr, kdr, atr, egr, or_ref))):
        for ci in ((0, 1) if d == 0 else (1, 0)):
            r0 = ci * CHUNK
            r = _dot(wq_ref[2 * r0:2 * r0 + 2 * CHUNK, :], sbd_ref[d])
            v_new = u_ref[r0:r0 + CHUNK, :] - r[:CHUNK]
            v_stack = jnp.concatenate([jnp.where(lane_head == h, v_new, 0.0) for h in range(n_heads)],
                                      axis=0).astype(BF16)
            o_ref[r0:r0 + CHUNK, :] = r[CHUNK:] + _dot(at_ref[r0:r0 + CHUNK, :], v_stack)
            vnb = v_new.astype(BF16)
            for h in range(n_heads):
                c0 = h * HEAD_DIM
                lg = (2 + d) * n_heads + h
                s_new = (s_ref[d, h] * eg_ref[r0:r0 + 1, lg:lg + 1]
                         + _dot_tn(kd_ref[r0:r0 + CHUNK, c0:c0 + HEAD_DIM], vnb[:, c0:c0 + HEAD_DIM]))
                s_ref[d, h] = s_new
                sbd_ref[d, c0:c0 + HEAD_DIM, c0:c0 + HEAD_DIM] = s_new.astype(BF16)


def _seq_block_maps(nsb, nctx_sb):
    def fwd(b, n):
        return b * nsb + n

    def rev(b, n):
        return b * nsb + jnp.where(n < nctx_sb, nctx_sb - 1 - n, nsb - 1 - (n - nctx_sb))

    return fwd, rev


def _a_scan(prep, n_batch, tb, ctx_len, mixw):
    uf, wqf, kdf, atf, ur, wqr, kdr, atr, eg = prep
    t = uf.shape[0]
    n_heads = mixw // HEAD_DIM
    nsb = tb // STEP_ROWS
    fwd, rev = _seq_block_maps(nsb, ctx_len // STEP_ROWS)
    in_specs = []
    for m in (fwd, rev):
        in_specs += [pl.BlockSpec((STEP_ROWS, mixw), lambda b, n, m=m: (m(b, n), 0)),
                     pl.BlockSpec((2 * STEP_ROWS, mixw), lambda b, n, m=m: (m(b, n), 0)),
                     pl.BlockSpec((STEP_ROWS, mixw), lambda b, n, m=m: (m(b, n), 0)),
                     pl.BlockSpec((STEP_ROWS, n_heads * CHUNK), lambda b, n, m=m: (m(b, n), 0)),
                     pl.BlockSpec((STEP_ROWS, 128), lambda b, n, m=m: (m(b, n), 0))]
    return pl.pallas_call(
        functools.partial(_a_scan_kernel, n_heads=n_heads),
        grid=(n_batch, nsb),
        in_specs=in_specs,
        out_specs=[pl.BlockSpec((STEP_ROWS, mixw), lambda b, n: (fwd(b, n), 0)),
                   pl.BlockSpec((STEP_ROWS, mixw), lambda b, n: (rev(b, n), 0))],
        out_shape=[jax.ShapeDtypeStruct((t, mixw), F32)] * 2,
        scratch_shapes=[pltpu.VMEM((2, n_heads, HEAD_DIM, HEAD_DIM), F32),
                        pltpu.VMEM((2, mixw, mixw), BF16)],
        compiler_params=_cparams(("arbitrary", "arbitrary")),
        name="deltanet_scan",
    )(uf, wqf, kdf, atf, eg, ur, wqr, kdr, atr, eg)


def _b_prep_kernel(q_ref, kv_ref, gain_ref, cos_ref, sin_ref, qo_ref, ko_ref, vo_ref, *, mixw):
    n_heads = mixw // HEAD_DIM
    n_kv = n_heads // 2
    cosf = cos_ref[...]
    sinf = sin_ref[...]

    def norm_rope(xh, gain, scale):
        yh = xh * lax.rsqrt(jnp.mean(xh * xh, axis=-1, keepdims=True) + EPS) * gain
        return (yh * cosf + pltpu.roll(yh, HEAD_DIM // 2, 1) * sinf) * scale

    for h in range(n_heads):
        c0 = h * HEAD_DIM
        qo_ref[:, c0:c0 + HEAD_DIM] = norm_rope(q_ref[:, c0:c0 + HEAD_DIM], gain_ref[0:1, :],
                                                HEAD_DIM ** -0.5 * LOG2_E).astype(qo_ref.dtype)
    for h in range(n_kv):
        c0 = h * HEAD_DIM
        ko_ref[:, c0:c0 + HEAD_DIM] = norm_rope(kv_ref[:, c0:c0 + HEAD_DIM], gain_ref[1:2, :], 1.0).astype(ko_ref.dtype)
    ones = jnp.ones((q_ref.shape[0], HEAD_DIM), vo_ref.dtype)
    for h in range(n_kv):
        c0 = (n_kv + h) * HEAD_DIM
        vo_ref[:, 2 * h * HEAD_DIM:(2 * h + 1) * HEAD_DIM] = kv_ref[:, c0:c0 + HEAD_DIM].astype(vo_ref.dtype)
        vo_ref[:, (2 * h + 1) * HEAD_DIM:(2 * h + 2) * HEAD_DIM] = ones


def _b_prep(p, qk_gain, cosf, sinf, tm, nb, mixw, col_q, col_kv):
    t = p.shape[0]
    kvw = mixw // 2
    return pl.pallas_call(
        functools.partial(_b_prep_kernel, mixw=mixw),
        grid=(t // tm,),
        in_specs=[pl.BlockSpec((tm, mixw), lambda i: (i, col_q // mixw)),
                  pl.BlockSpec((tm, mixw), lambda i: (i, col_kv // mixw)),
                  pl.BlockSpec((2, HEAD_DIM), lambda i: (0, 0)),
                  pl.BlockSpec((tm, HEAD_DIM), lambda i: (i % nb, 0)),
                  pl.BlockSpec((tm, HEAD_DIM), lambda i: (i % nb, 0))],
        out_specs=[pl.BlockSpec((tm, mixw), lambda i: (i, 0)),
                   pl.BlockSpec((tm, kvw), lambda i: (i, 0)),
                   pl.BlockSpec((tm, 2 * kvw), lambda i: (i, 0))],
        out_shape=[jax.ShapeDtypeStruct((t, mixw), BF16),
                   jax.ShapeDtypeStruct((t, kvw), BF16),
                   jax.ShapeDtypeStruct((t, 2 * kvw), BF16)],
        compiler_params=_cparams(("parallel",)),
        name="gqa_prep",
    )(p, p, qk_gain, cosf, sinf)


def _attn_kernel(q_ref, k_ref, v_ref, o_ref, m_ref, acc_ref, sa_ref, sb_ref, *, tq, tk, nk, nctx_q, ctx_len):
    qi = pl.program_id(2)
    qb = q_ref[...]
    q2 = jnp.concatenate([qb[:, :HEAD_DIM], qb[:, HEAD_DIM:]], axis=0)
    m_ref[...] = jnp.full_like(m_ref, -1e30)
    acc_ref[...] = jnp.zeros_like(acc_ref)

    def scores(i, s_ref):
        r0 = pl.multiple_of(i * tk, tk)
        s_ref[...] = _dot_nt(q2, k_ref[pl.ds(r0, tk), :])

    def absorb(s, vc):
        m_prev = m_ref[...]
        m_new = jnp.maximum(m_prev, jnp.max(s, axis=-1, keepdims=True))
        p = jnp.exp2(s - m_new)
        acc_ref[...] = jnp.exp2(m_prev - m_new) * acc_ref[...] + _dot(p.astype(BF16), vc)
        m_ref[...] = m_new

    def absorb_chunk(i, s_ref):
        r0 = pl.multiple_of(i * tk, tk)
        absorb(s_ref[...], v_ref[pl.ds(r0, tk), :])

    @pl.when(qi >= nctx_q)
    def _():
        scores(0, sa_ref)

        def pair(j, carry):
            scores(2 * j + 1, sb_ref)
            absorb_chunk(2 * j, sa_ref)
            scores(2 * j + 2, sa_ref)
            absorb_chunk(2 * j + 1, sb_ref)
            return carry

        n_pairs = (nk - 1) // 2
        lax.fori_loop(0, n_pairs, pair, 0)
        if nk % 2 == 0:
            scores(nk - 1, sb_ref)
            absorb_chunk(nk - 2, sa_ref)
            absorb_chunk(nk - 1, sb_ref)
        else:
            absorb_chunk(nk - 1, sa_ref)

    @pl.when(qi < nctx_q)
    def _():
        absorb(_dot_nt(q2, k_ref[0:ctx_len, :]), v_ref[0:ctx_len, :])

    acc = acc_ref[...]
    o = acc[:, :HEAD_DIM] / acc[:, HEAD_DIM:HEAD_DIM + 1]
    o_ref[...] = jnp.concatenate([o[:tq], o[tq:]], axis=1).astype(o_ref.dtype)


def _attention(q, k, v, n_batch, tb, ctx_len, mixw):
    t = q.shape[0]
    n_kv = mixw // HEAD_DIM // 2
    tq = _pick(math.gcd(ctx_len, tb), (256, 128))
    tk = _pick(tb, (1408, 768, 1024, 512, 640, 384, 256, 128))
    nq, nk = tb // tq, tb // tk
    return pl.pallas_call(
        functools.partial(_attn_kernel, tq=tq, tk=tk, nk=nk, nctx_q=ctx_len // tq, ctx_len=ctx_len),
        grid=(n_batch, n_kv, nq),
        in_specs=[pl.BlockSpec((tq, 2 * HEAD_DIM), lambda b, j, qi: (b * nq + qi, j)),
                  pl.BlockSpec((tb, HEAD_DIM), lambda b, j, qi: (b, j)),
                  pl.BlockSpec((tb, 2 * HEAD_DIM), lambda b, j, qi: (b, j))],
        out_specs=pl.BlockSpec((tq, 2 * HEAD_DIM), lambda b, j, qi: (b * nq + qi, j)),
        out_shape=jax.ShapeDtypeStruct((t, mixw), BF16),
        scratch_shapes=[pltpu.VMEM((2 * tq, 1), F32), pltpu.VMEM((2 * tq, 2 * HEAD_DIM), F32),
                        pltpu.VMEM((2 * tq, tk), F32), pltpu.VMEM((2 * tq, tk), F32)],
        compiler_params=_cparams(("parallel", "parallel", "parallel")),
        name="gqa_attention",
    )(q, k, v)


def _c_kernel(u_ref, v_ref, ws_ref, bs_ref, o_ref, *, tm, mixw):
    n_groups = mixw // HEAD_DIM
    for ci in range(tm // CHUNK_C):
        r0 = ci * CHUNK_C
        u = _gelu_tanh(u_ref[r0:r0 + CHUNK_C, :])
        vn = _ln(_gelu_tanh(v_ref[r0:r0 + CHUNK_C, :])).astype(BF16)
        for g in range(n_groups):
            c0 = g * HEAD_DIM
            vm = _dot(ws_ref[g], vn[:, c0:c0 + HEAD_DIM]) + bs_ref[:, g:g + 1]
            o_ref[r0:r0 + CHUNK_C, c0:c0 + HEAD_DIM] = (u[:, c0:c0 + HEAD_DIM] * vm).astype(o_ref.dtype)


def _mixer_c(p, ws, bs_t, tm, mixw, col_u, col_v):
    t = p.shape[0]
    n_groups = mixw // HEAD_DIM
    return pl.pallas_call(
        functools.partial(_c_kernel, tm=tm, mixw=mixw),
        grid=(t // tm,),
        in_specs=[pl.BlockSpec((tm, mixw), lambda i: (i, col_u // mixw)),
                  pl.BlockSpec((tm, mixw), lambda i: (i, col_v // mixw)),
                  pl.BlockSpec((n_groups, CHUNK_C, CHUNK_C), lambda i: (0, 0, 0)),
                  pl.BlockSpec((CHUNK_C, 128), lambda i: (0, 0))],
        out_specs=pl.BlockSpec((tm, mixw), lambda i: (i, 0)),
        out_shape=jax.ShapeDtypeStruct((t, mixw), BF16),
        compiler_params=_cparams(("parallel",)),
        name="gmlp",
    )(p, p, ws, bs_t)


def _hgrn_tables():
    tau = np.arange(CHUNK)
    sums_t = [tau[None, :] <= tau[:, None], tau[None, :] > tau[:, None]]
    pair_t = []
    for lvl in range(N_LEVELS):
        m = CHUNK >> (lvl + 1)
        blk = tau // (2 * m)
        upper = (tau % (2 * m)) >= m
        ref = blk * 2 * m + m - 1
        r = tau[None, :]
        w_up = upper[:, None] & (r > ref[:, None]) & (r <= tau[:, None])
        w_lo = (~upper)[:, None] & (r > tau[:, None]) & (r <= ref[:, None])
        sums_t.append(w_up | w_lo)
        pair_t.append((blk[:, None] == blk[None, :]) & upper[:, None] & (~upper)[None, :])
    sums_t = np.concatenate(sums_t, axis=0).astype(np.float32)
    pair_t = np.stack(pair_t).astype(np.float32)
    flip = tau[::-1]
    sums, pair = [], []
    for d in range(2):
        if d == 0:
            s_d, p_d = sums_t, pair_t
        else:
            s_d = sums_t.reshape(-1, CHUNK, CHUNK)[:, flip][:, :, flip].reshape(-1, CHUNK)
            p_d = pair_t[:, flip][:, :, flip]
        sums.append(np.concatenate([s_d, s_d, s_d], axis=1))
        pair.append(p_d)
    return np.stack(sums), np.stack(pair)


def _d_scan_kernel(qf, vf, ff, qr, vr, fr, lb_ref, sums_ref, pair_ref, of_ref, or_ref, st_ref, *, n_heads):
    n = pl.program_id(1)

    @pl.when(n == 0)
    def _():
        st_ref[...] = jnp.zeros_like(st_ref)

    c = CHUNK
    row = lax.broadcasted_iota(jnp.int32, (c, c), 0)
    col = lax.broadcasted_iota(jnp.int32, (c, c), 1)
    eye = (row == col).astype(F32)
    inst = []
    for d, (q_ref, v_ref, f_ref, o_ref) in enumerate(((qf, vf, ff, of_ref), (qr, vr, fr, or_ref))):
        for pos, ci in enumerate((0, 1) if d == 0 else (1, 0)):
            for h in range(n_heads):
                inst.append((d, pos, ci * c, h * HEAD_DIM, h, q_ref, v_ref, f_ref, o_ref))
    qs = [q_ref[r0:r0 + c, c0:c0 + HEAD_DIM] for (d, pos, r0, c0, h, q_ref, v_ref, f_ref, o_ref) in inst]
    vs = [v_ref[r0:r0 + c, c0:c0 + HEAD_DIM].astype(BF16) for (d, pos, r0, c0, h, q_ref, v_ref, f_ref, o_ref) in inst]
    es = [jnp.exp(-jnp.clip(f_ref[r0:r0 + c, c0:c0 + HEAD_DIM], -GATE_CLIP, GATE_CLIP))
          for (d, pos, r0, c0, h, q_ref, v_ref, f_ref, o_ref) in inst]
    sgs = [1.0 / (1.0 + e) for e in es]
    lbs = [lb_ref[0:1, it[3]:it[3] + HEAD_DIM] for it in inst]
    logfs = [jnp.log(lb + (1.0 - lb) * sg) for lb, sg in zip(lbs, sgs)]
    ks = [(1.0 - lb) * (e * sg) for lb, e, sg in zip(lbs, es, sgs)]
    his = [x.astype(BF16) for x in logfs]
    r1s = [x - hi.astype(F32) for x, hi in zip(logfs, his)]
    mids = [x.astype(BF16) for x in r1s]
    los = [(x - mid.astype(F32)).astype(BF16) for x, mid in zip(r1s, mids)]
    xs = [_dot(sums_ref[it[0]], jnp.concatenate([hi, mid, lo], axis=0)) for it, hi, mid, lo in zip(inst, his, mids, los)]
    kbs = [k.astype(BF16) for k in ks]
    attns = [eye * _dot_nt(q.astype(BF16), kb) for q, kb in zip(qs, kbs)]
    for lvl in range(N_LEVELS):
        zls = [jnp.exp(x[(2 + lvl) * c:(3 + lvl) * c]) for x in xs]
        attns = [a + pair_ref[it[0], lvl] * _dot_nt((q * zl).astype(BF16), (k * zl).astype(BF16))
                 for a, it, q, k, zl in zip(attns, inst, qs, ks, zls)]
    qds = [(q * jnp.exp(x[0:c])).astype(BF16) for q, x in zip(qs, xs)]
    kds = [(k * jnp.exp(x[c:2 * c])).astype(BF16) for k, x in zip(ks, xs)]
    intra = [_dot(a.astype(BF16), v) for a, v in zip(attns, vs)]
    kvs = [_dot_tn(v, kd) for v, kd in zip(vs, kds)]
    states = [[st_ref[d, h] for h in range(n_heads)] for d in range(2)]
    for pos in range(2):
        for i, (d, p_i, r0, c0, h, q_ref, v_ref, f_ref, o_ref) in enumerate(inst):
            if p_i != pos:
                continue
            st = states[d][h]
            o_ref[r0:r0 + c, c0:c0 + HEAD_DIM] = _dot_nt(qds[i], st.astype(BF16)) + intra[i]
            last = c - 1 if d == 0 else 0
            states[d][h] = st * jnp.exp(xs[i][last:last + 1, :]) + kvs[i]
    for d in range(2):
        for h in range(n_heads):
            st_ref[d, h] = states[d][h]


def _d_scan(p, lb, n_batch, tb, ctx_len, mixw, col_q, col_i, col_f):
    t = p.shape[0]
    n_heads = mixw // HEAD_DIM
    nsb = tb // STEP_ROWS
    fwd, rev = _seq_block_maps(nsb, ctx_len // STEP_ROWS)
    sums_np, pair_np = _hgrn_tables()
    sums = jnp.asarray(sums_np, BF16)
    pair = jnp.asarray(pair_np, F32)
    in_specs = []
    for d, m in enumerate((fwd, rev)):
        in_specs += [pl.BlockSpec((STEP_ROWS, mixw), lambda b, n, m=m: (m(b, n), col_q // mixw)),
                     pl.BlockSpec((STEP_ROWS, mixw), lambda b, n, m=m: (m(b, n), col_i // mixw)),
                     pl.BlockSpec((STEP_ROWS, mixw), lambda b, n, m=m, d=d: (m(b, n), col_f // mixw + d))]
    in_specs += [pl.BlockSpec((1, mixw), lambda b, n: (0, 0)),
                 pl.BlockSpec(sums.shape, lambda b, n: (0, 0, 0)),
                 pl.BlockSpec(pair.shape, lambda b, n: (0, 0, 0, 0))]
    return pl.pallas_call(
        functools.partial(_d_scan_kernel, n_heads=n_heads),
        grid=(n_batch, nsb),
        in_specs=in_specs,
        out_specs=[pl.BlockSpec((STEP_ROWS, mixw), lambda b, n: (fwd(b, n), 0)),
                   pl.BlockSpec((STEP_ROWS, mixw), lambda b, n: (rev(b, n), 0))],
        out_shape=[jax.ShapeDtypeStruct((t, mixw), F32)] * 2,
        scratch_shapes=[pltpu.VMEM((2, n_heads, HEAD_DIM, HEAD_DIM), F32)],
        compiler_params=_cparams(("arbitrary", "arbitrary")),
        name="hgrn2_scan",
    )(p, p, p, p, p, p, lb, sums, pair)


def _route(logits_t, bias, n_experts):
    per = n_experts // N_EXPERT_GROUPS
    scores = _sigmoid(logits_t)
    sel = scores + bias
    rows_sel = [sel[e:e + 1, :] for e in range(n_experts)]
    rows_sc = [scores[e:e + 1, :] for e in range(n_experts)]

    def top2_sum(vals):
        hi = vals[0]
        lo = jnp.full_like(hi, -jnp.inf)
        for x in vals[1:]:
            lo = jnp.maximum(lo, jnp.minimum(hi, x))
            hi = jnp.maximum(hi, x)
        return hi + lo

    best = jnp.zeros_like(rows_sel[0], dtype=jnp.int32)
    best_score = top2_sum(rows_sel[0:per])
    for g in range(1, N_EXPERT_GROUPS):
        gs = top2_sum(rows_sel[g * per:(g + 1) * per])
        better = gs > best_score
        best = jnp.where(better, g, best)
        best_score = jnp.where(better, gs, best_score)
    in_sel, in_sc = [], []
    for j in range(per):
        a = rows_sel[j]
        c = rows_sc[j]
        for g in range(1, N_EXPERT_GROUPS):
            a = jnp.where(best == g, rows_sel[g * per + j], a)
            c = jnp.where(best == g, rows_sc[g * per + j], c)
        in_sel.append(a)
        in_sc.append(c)

    def first_argmax(vals):
        idx = jnp.zeros_like(best)
        top = vals[0]
        for j in range(1, per):
            better = vals[j] > top
            idx = jnp.where(better, j, idx)
            top = jnp.where(better, vals[j], top)
        return idx

    i1 = first_argmax(in_sel)
    i2 = first_argmax([jnp.where(i1 == j, -jnp.inf, in_sel[j]) for j in range(per)])

    def pick(idx):
        w = in_sc[0]
        for j in range(1, per):
            w = jnp.where(idx == j, in_sc[j], w)
        return w

    w1, w2 = pick(i1), pick(i2)
    tot = w1 + w2
    return best * per + i1, best * per + i2, w1 / tot, w2 / tot


def _outproj_kernel(oaf, oar, ga, odf, odr, gd, yb, yc, na, nd, wo, x_ref, m_ref, lng, lnb, wr, br,
                    x1_ref, h2_ref, ei_ref, ew_ref, *, alpha, mixw, tm, n_experts):
    n_heads = mixw // HEAD_DIM

    def gated(of_ref, or_ref, g_ref, gain_ref):
        parts = []
        for h in range(n_heads):
            c0 = h * HEAD_DIM
            o = of_ref[:, c0:c0 + HEAD_DIM] + or_ref[:, c0:c0 + HEAD_DIM]
            y = o * lax.rsqrt(jnp.mean(o * o, axis=-1, keepdims=True) + EPS) * gain_ref[...]
            parts.append((y * _silu(g_ref[:, c0:c0 + HEAD_DIM])).astype(BF16))
        return jnp.concatenate(parts, axis=1)

    ya = gated(oaf, oar, ga, na)
    yd = gated(odf, odr, gd, nd)
    y = (_dot(ya, wo[0:mixw, :]) + _dot(yb[...], wo[mixw:2 * mixw, :])
         + _dot(yc[...], wo[2 * mixw:3 * mixw, :]) + _dot(yd, wo[3 * mixw:4 * mixw, :]))
    x1 = _ln(alpha * x_ref[...] + m_ref[0, 2:3, :] * y) * lng[...] + lnb[...]
    x1_ref[...] = x1
    h2 = _ln(x1) * (1.0 + m_ref[0, 4:5, :]) + m_ref[0, 3:4, :]
    h2_ref[...] = h2
    h_hi = h2.astype(BF16)
    h_r = h2 - h_hi.astype(F32)
    h_mid = h_r.astype(BF16)
    h_lo = (h_r - h_mid.astype(F32)).astype(BF16)
    pa = _dot(h_hi, wr[:, 0:2 * 128])
    pb = _dot(h_mid, wr[:, 0:2 * 128])
    pc = _dot(h_hi, wr[:, 2 * 128:4 * 128])
    pd = _dot(h_lo, wr[:, 0:2 * 128])
    logits = ((pd[:, :128] + pb[:, 128:] + pc[:, :128]) + (pb[:, :128] + pa[:, 128:])) + pa[:, :128]
    logits_t = logits.T[:n_experts]
    e1, e2, w1, w2 = _route(logits_t, br[:, 0:1], n_experts)
    zi = jnp.zeros((6, tm), jnp.int32)
    ei_ref[...] = jnp.concatenate([e1, e2, zi], axis=0)
    ew_ref[...] = jnp.concatenate([w1, w2, zi.astype(F32)], axis=0)


def _outproj(oaf, oar, odf, odr, p, yb, yc, norm_a, norm_d, w_out, x, mods, ln_g, ln_b, w_router_p, b_router,
             alpha, tm, nb, nctx, n_batch, mixw, col_ga, col_gd):
    t, d = x.shape
    n_experts = b_router.shape[0]
    row = lambda i: (i, 0)
    const = lambda i: (0, 0)
    kern = functools.partial(_outproj_kernel, alpha=alpha, mixw=mixw, tm=tm, n_experts=n_experts)
    return pl.pallas_call(
        kern,
        grid=(t // tm,),
        in_specs=[pl.BlockSpec((tm, mixw), row), pl.BlockSpec((tm, mixw), row),
                  pl.BlockSpec((tm, mixw), lambda i: (i, col_ga // mixw)),
                  pl.BlockSpec((tm, mixw), row), pl.BlockSpec((tm, mixw), row),
                  pl.BlockSpec((tm, mixw), lambda i: (i, col_gd // mixw)),
                  pl.BlockSpec((tm, mixw), row), pl.BlockSpec((tm, mixw), row),
                  pl.BlockSpec((1, HEAD_DIM), const), pl.BlockSpec((1, HEAD_DIM), const),
                  pl.BlockSpec((N_MIXERS * mixw, d), const),
                  pl.BlockSpec((tm, d), row),
                  pl.BlockSpec((1, 6, d), lambda i: (_mod_row(i, nb, nctx, n_batch), 0, 0)),
                  pl.BlockSpec((1, d), const), pl.BlockSpec((1, d), const),
                  pl.BlockSpec((d, 4 * 128), const), pl.BlockSpec((n_experts, 128), const)],
        out_specs=[pl.BlockSpec((tm, d), row), pl.BlockSpec((tm, d), row),
                   pl.BlockSpec((8, tm), lambda i: (0, i)), pl.BlockSpec((8, tm), lambda i: (0, i))],
        out_shape=[jax.ShapeDtypeStruct((t, d), F32), jax.ShapeDtypeStruct((t, d), F32),
                   jax.ShapeDtypeStruct((8, t), jnp.int32), jax.ShapeDtypeStruct((8, t), F32)],
        compiler_params=_cparams(("parallel",)),
        name="out_proj_router",
    )(oaf, oar, p, odf, odr, p, yb, yc, norm_a, norm_d, w_out, x, mods, ln_g, ln_b, w_router_p, b_router)


def _rank_kernel(ei_ref, su_ref, rank_ref, cnt_ref, base_ref, *, n_experts, tm):
    i = pl.program_id(0)

    @pl.when(i == 0)
    def _():
        base_ref[...] = jnp.zeros_like(base_ref)

    eid = lax.broadcasted_iota(jnp.int32, (n_experts, tm), 0)
    o1 = (eid == ei_ref[0:1, :]).astype(F32)
    o2 = (eid == ei_ref[1:2, :]).astype(F32)
    cnt = o1 + o2
    before = _dot(cnt.astype(BF16), su_ref[...]) + base_ref[:, 0:1]
    r1 = jnp.sum(o1 * before, axis=0, keepdims=True)
    r2 = jnp.sum(o2 * before, axis=0, keepdims=True)
    rank_ref[...] = jnp.concatenate([r1, r2, jnp.zeros((6, tm), F32)], axis=0).astype(jnp.int32)
    base_ref[...] = base_ref[...] + jnp.sum(cnt, axis=1, keepdims=True)
    cnt_ref[...] = base_ref[...]


def _slot_ranks(ei, n_experts, tm):
    t = ei.shape[1]
    r = np.arange(tm)
    su = jnp.asarray((r[:, None] < r[None, :]).astype(np.float32), BF16)
    return pl.pallas_call(
        functools.partial(_rank_kernel, n_experts=n_experts, tm=tm),
        grid=(t // tm,),
        in_specs=[pl.BlockSpec((8, tm), lambda i: (0, i)), pl.BlockSpec((tm, tm), lambda i: (0, 0))],
        out_specs=[pl.BlockSpec((8, tm), lambda i: (0, i)), pl.BlockSpec((n_experts, 128), lambda i: (0, 0))],
        out_shape=[jax.ShapeDtypeStruct((8, t), jnp.int32), jax.ShapeDtypeStruct((n_experts, 128), F32)],
        scratch_shapes=[pltpu.VMEM((n_experts, 128), F32)],
        compiler_params=_cparams(("arbitrary",)),
        name="moe_slot_ranks",
    )(ei, su)


def _row_copy(src_ref, src_row, dst_ref, dst_row, sem):
    return pltpu.make_async_copy(src_ref.at[pl.ds(src_row, 1)], dst_ref.at[pl.ds(dst_row, 1)], sem)


def _dispatch_kernel(dest_ref, h_ref, xs_in_ref, xs_ref, sem, *, tm, t):
    del xs_in_ref
    base = pl.program_id(0) * tm

    def issue(r, carry):
        for kk in range(TOP_K):
            _row_copy(h_ref, r, xs_ref, dest_ref[kk * t + base + r], sem).start()
        return carry

    lax.fori_loop(0, tm, issue, 0, unroll=4)
    for kk in range(TOP_K):
        pltpu.make_async_copy(h_ref, xs_ref.at[pl.ds(0, tm)], sem).wait()


def _dispatch(dest, h2, n_slots, tm):
    t, d = h2.shape
    xs0 = jnp.zeros((n_slots, d), h2.dtype)
    return pl.pallas_call(
        functools.partial(_dispatch_kernel, tm=tm, t=t),
        grid_spec=pltpu.PrefetchScalarGridSpec(
            num_scalar_prefetch=1,
            grid=(t // tm,),
            in_specs=[pl.BlockSpec((tm, d), lambda i, dest: (i, 0)),
                      pl.BlockSpec(memory_space=pl.ANY)],
            out_specs=pl.BlockSpec(memory_space=pl.ANY),
            scratch_shapes=[pltpu.SemaphoreType.DMA(())]),
        out_shape=jax.ShapeDtypeStruct((n_slots, d), h2.dtype),
        input_output_aliases={2: 0},
        compiler_params=_cparams(("arbitrary",)),
        name="moe_dispatch",
    )(dest, h2, xs0)


def _expert_kernel(be_ref, na_ref, x_ref, w1_ref, w3_ref, w2_ref, y_ref):
    i = pl.program_id(0)

    @pl.when(i < na_ref[0])
    def _():
        xb = x_ref[...].astype(BF16)
        h1 = _dot(xb, w1_ref[0])
        h3 = _dot(xb, w3_ref[0])
        y_ref[...] = _dot((_silu(h1) * h3).astype(BF16), w2_ref[0])

    @pl.when(i >= na_ref[0])
    def _():
        y_ref[...] = jnp.zeros_like(y_ref)


def _experts(block_expert, n_active, xs, w1, w3, w2, tmx):
    n_slots, d = xs.shape
    f = w1.shape[2]
    return pl.pallas_call(
        _expert_kernel,
        grid_spec=pltpu.PrefetchScalarGridSpec(
            num_scalar_prefetch=2,
            grid=(n_slots // tmx,),
            in_specs=[pl.BlockSpec((tmx, d), lambda i, be, na: (i, 0)),
                      pl.BlockSpec((1, d, f), lambda i, be, na: (be[i], 0, 0)),
                      pl.BlockSpec((1, d, f), lambda i, be, na: (be[i], 0, 0)),
                      pl.BlockSpec((1, f, d), lambda i, be, na: (be[i], 0, 0))],
            out_specs=pl.BlockSpec((tmx, d), lambda i, be, na: (i, 0))),
        out_shape=jax.ShapeDtypeStruct((n_slots, d), F32),
        compiler_params=_cparams(("arbitrary",)),
        name="moe_experts",
    )(block_expert, n_active, xs, w1, w3, w2)


def _combine_kernel(dest_ref, ys_ref, ew_ref, x1_ref, m_ref, lng, lnb, mn_ref, x2_ref, *rest,
                    tm, t, alpha, block_of_step, emit_next):
    hn_ref = rest[0] if emit_next else None
    g_ref, sem = rest[-2:]
    i = pl.program_id(0)
    n = pl.num_programs(0)

    def issue(step, slot):
        base = block_of_step(step) * tm

        def body(r, carry):
            for kk in range(TOP_K):
                _row_copy(ys_ref, dest_ref[kk * t + base + r], g_ref.at[slot, kk], r, sem.at[slot]).start()
            return carry

        lax.fori_loop(0, tm, body, 0, unroll=4)

    @pl.when(i == 0)
    def _():
        issue(0, 0)

    @pl.when(i + 1 < n)
    def _():
        issue(i + 1, (i + 1) % 2)

    slot = i % 2
    for kk in range(TOP_K):
        pltpu.make_async_copy(ys_ref.at[pl.ds(0, tm)], g_ref.at[slot, kk], sem.at[slot]).wait()
    row = lax.broadcasted_iota(jnp.int32, (tm, tm), 0)
    col = lax.broadcasted_iota(jnp.int32, (tm, tm), 1)
    w_t = _dot_nt((row == col).astype(F32), ew_ref[...], HIGHEST)
    f = w_t[:, 0:1] * g_ref[slot, 0] + w_t[:, 1:2] * g_ref[slot, 1]
    x2 = _ln(alpha * x1_ref[...] + m_ref[0, 5:6, :] * f) * lng[...] + lnb[...]
    x2_ref[...] = x2
    if emit_next:
        hn_ref[...] = (_ln(x2) * (1.0 + mn_ref[0, 1:2, :]) + mn_ref[0, 0:1, :]).astype(hn_ref.dtype)


def _combine(dest, ys, ew, x1, mods, ln_g, ln_b, mods_next, alpha, tm, nb, nctx, n_batch, latent_only):
    t, d = x1.shape
    nlat = nb - nctx
    if latent_only:
        n_steps = n_batch * nlat
        block_of_step = lambda i: (i // nlat) * nb + nctx + i % nlat
    else:
        n_steps = t // tm
        block_of_step = lambda i: i
    mrow = lambda i, dest: (_mod_row(block_of_step(i), nb, nctx, n_batch), 0, 0)
    const = lambda i, dest: (0, 0)
    out_specs = [pl.BlockSpec((tm, d), lambda i, dest: (i, 0))]
    out_shape = [jax.ShapeDtypeStruct((n_steps * tm, d), F32)]
    if not latent_only:
        out_specs.append(pl.BlockSpec((tm, d), lambda i, dest: (i, 0)))
        out_shape.append(jax.ShapeDtypeStruct((t, d), BF16))
    return pl.pallas_call(
        functools.partial(_combine_kernel, tm=tm, t=t, alpha=alpha, block_of_step=block_of_step,
                          emit_next=not latent_only),
        grid_spec=pltpu.PrefetchScalarGridSpec(
            num_scalar_prefetch=1,
            grid=(n_steps,),
            in_specs=[pl.BlockSpec(memory_space=pl.ANY),
                      pl.BlockSpec((8, tm), lambda i, dest: (0, block_of_step(i))),
                      pl.BlockSpec((tm, d), lambda i, dest: (block_of_step(i), 0)),
                      pl.BlockSpec((1, 6, d), mrow),
                      pl.BlockSpec((1, d), const), pl.BlockSpec((1, d), const),
                      pl.BlockSpec((1, 6, d), mrow)],
            out_specs=out_specs,
            scratch_shapes=[pltpu.VMEM((2, TOP_K, tm, d), F32), pltpu.SemaphoreType.DMA((2,))]),
        out_shape=out_shape,
        compiler_params=_cparams(("arbitrary",)),
        name="moe_combine",
    )(dest, ys, ew, x1, mods, ln_g, ln_b, mods_next)


def _axial_tables(seq, ctx_len):
    rows = seq // GRID_W
    row = jnp.repeat(jnp.arange(rows, dtype=F32), GRID_W)
    col = jnp.tile(jnp.arange(GRID_W, dtype=F32), rows)
    n_freq = HEAD_DIM // 4
    inv_freq = ROPE_THETA ** (-jnp.arange(n_freq, dtype=F32) / n_freq)
    ang = jnp.concatenate([row[:, None] * inv_freq, col[:, None] * inv_freq], axis=-1)
    cos, sin = jnp.cos(ang), jnp.sin(ang)
    cosf = jnp.concatenate([cos, cos], axis=-1)
    sinf = jnp.concatenate([-sin, sin], axis=-1)
    cosf = jnp.concatenate([jnp.ones((ctx_len, HEAD_DIM), F32), cosf], axis=0)
    sinf = jnp.concatenate([jnp.zeros((ctx_len, HEAD_DIM), F32), sinf], axis=0)
    return cosf, sinf


def kernel(x, c, ctx, c_ctx, w_ada, b_ada, ln_g, ln_b, w_in, conv_a, a_log, dt_bias, norm_a, qk_norm_b, ws_c, bs_c, lb_d, norm_d, w_out, w_router, b_router, w1, w3, w2):
    n_batch, seq, d = x.shape
    ctx_len = ctx.shape[1]
    n_layers = w_in.shape[0]
    n_experts = w_router.shape[1]
    mixw = d // N_MIXERS
    n_heads = mixw // HEAD_DIM
    tb = ctx_len + seq
    t = n_batch * tb
    alpha = (2.0 * n_layers) ** 0.25
    tm = _pick(math.gcd(ctx_len, seq), (256, 128))
    nb, nctx = tb // tm, ctx_len // tm

    col = {"a_qkv": 0, "a_gate": 3 * mixw, "d_f": 4 * mixw, "b_q": 6 * mixw, "b_kv": 7 * mixw, "c_u": 8 * mixw,
           "c_v": 9 * mixw, "d_q": 10 * mixw, "d_i": 11 * mixw, "d_gate": 12 * mixw, "small": 13 * mixw}
    n_proj = 13 * mixw + 256
    src = np.cumsum([0, 3 * mixw, mixw, 2 * n_heads, 2 * n_heads, mixw, mixw, mixw, mixw, mixw, mixw, mixw, 2 * mixw])
    s_qkv, s_ga, s_beta, s_dec, s_bq, s_bkv, s_cu, s_cv, s_dq, s_di, s_dg, s_df, s_end = [int(v) for v in src]

    def permute_w_in(w):
        pad = jnp.zeros((d, n_proj - 13 * mixw - 4 * n_heads), w.dtype)
        return jnp.concatenate([w[:, s_qkv:s_beta], w[:, s_df:s_end], w[:, s_bq:s_df], w[:, s_beta:s_bq], pad],
                               axis=1).astype(BF16)

    cc = jnp.zeros((8, d), F32).at[:n_batch].set(c).at[n_batch].set(c_ctx)
    mods_all = _mod_vectors(cc, w_ada, b_ada)[:, :n_batch + 1].reshape(n_layers, n_batch + 1, 6, d)

    soft = jax.nn.softmax(lb_d.astype(F32), axis=0)
    lb_all = jnp.cumsum(soft, axis=0) - soft[0]
    cosf, sinf = _axial_tables(seq, ctx_len)
    gate_lanes = jnp.zeros((1, 128), F32)
    wr_f = w_router.astype(F32)
    wr_hi = wr_f.astype(BF16)
    wr_r = wr_f - wr_hi.astype(F32)
    wr_mid = wr_r.astype(BF16)
    wr_lo = (wr_r - wr_mid.astype(F32)).astype(BF16)
    w_router_p = jnp.zeros((d, 4 * 128), BF16)
    for blk, part in enumerate((wr_hi, wr_mid, wr_lo)):
        w_router_p = w_router_p.at[:, blk * 128:blk * 128 + n_experts].set(part)
    b_router_c = jnp.broadcast_to(b_router.astype(F32)[:, None], (n_experts, 128))

    tmx = 256
    n_assign = t * TOP_K
    n_slots = (-(-n_assign // tmx) + n_experts) * tmx

    xu = jnp.concatenate([ctx, x], axis=1).reshape(t, d)
    h = _ln_modulate(xu, mods_all[0], tm, nb, nctx, n_batch)
    for l in range(n_layers):
        mods = mods_all[l]
        p = _matmul(h, permute_w_in(w_in[l]))
        neg_a = gate_lanes.at[0, 2 * n_heads:4 * n_heads].set(-jnp.exp(a_log[l].astype(F32)).reshape(-1))
        dtb = gate_lanes.at[0, 2 * n_heads:4 * n_heads].set(dt_bias[l].astype(F32).reshape(-1))
        prep_a = _a_prep(p, conv_a[l], neg_a, dtb, tm, nb, nctx, mixw, col["small"])
        oaf, oar = _a_scan(prep_a, n_batch, tb, ctx_len, mixw)
        qb, kb, vb = _b_prep(p, qk_norm_b[l], cosf, sinf, tm, nb, mixw, col["b_q"], col["b_kv"])
        yb = _attention(qb, kb, vb, n_batch, tb, ctx_len, mixw)
        bs_t = jnp.zeros((CHUNK_C, 128), F32).at[:, :bs_c.shape[1]].set(bs_c[l].T)
        yc = _mixer_c(p, ws_c[l].astype(BF16), bs_t, tm, mixw, col["c_u"], col["c_v"])
        odf, odr = _d_scan(p, lb_all[l][None, :], n_batch, tb, ctx_len, mixw, col["d_q"], col["d_i"], col["d_f"])
        x1, h2, ei, ew = _outproj(oaf, oar, odf, odr, p, yb, yc, norm_a[l][None, :], norm_d[l][None, :],
                                  w_out[l].astype(BF16), xu, mods, ln_g[l, 0][None, :], ln_b[l, 0][None, :],
                                  w_router_p, b_router_c, alpha, tm, nb, nctx, n_batch, mixw,
                                  col["a_gate"], col["d_gate"])
        rank, counts = _slot_ranks(ei, n_experts, tm)
        cnt = counts[:, 0].astype(jnp.int32)
        padded = (cnt + tmx - 1) // tmx * tmx
        pad_end = jnp.cumsum(padded)
        pad_start = pad_end - padded
        eids = jnp.arange(n_experts, dtype=jnp.int32)[:, None, None]
        dest = (rank[:TOP_K] + jnp.sum(jnp.where(ei[None, :TOP_K] == eids, pad_start[:, None, None], 0),
                                       axis=0)).reshape(-1)
        n_blocks = n_slots // tmx
        block_start = jnp.arange(n_blocks, dtype=jnp.int32) * tmx
        block_expert = jnp.minimum(jnp.sum((pad_end[None, :] <= block_start[:, None]).astype(jnp.int32), axis=1),
                                   n_experts - 1)
        n_active = (pad_end[-1:] // tmx).astype(jnp.int32)
        xs = _dispatch(dest, h2, n_slots, tm)
        ys = _experts(block_expert, n_active, xs, w1[l].astype(BF16), w3[l].astype(BF16), w2[l].astype(BF16), tmx)
        last = l == n_layers - 1
        res = _combine(dest, ys, ew, x1, mods, ln_g[l, 1][None, :], ln_b[l, 1][None, :],
                       mods_all[min(l + 1, n_layers - 1)], alpha, tm, nb, nctx, n_batch, latent_only=last)
        if last:
            return res[0].reshape(n_batch, seq, d)
        xu, h = res
```

```python
import functools
import math

import numpy as np
import jax
import jax.numpy as jnp
from jax import lax
from jax.experimental import pallas as pl
from jax.experimental.pallas import tpu as pltpu

F32 = jnp.float32
BF16 = jnp.bfloat16
HIGHEST = lax.Precision.HIGHEST

HEAD_DIM = 128
N_MIXERS = 4
CONV_W = 5
CHUNK = 64
STEP_ROWS = 2 * CHUNK
SOLVE_ROWS = 2 * CHUNK
CHUNK_C = 128
GRID_W = 64
ROPE_THETA = 10000.0
N_EXPERT_GROUPS = 4
TOP_K = 2
EPS = 1e-6
LOG2_E = math.log2(math.e)
GATE_CLIP = 30.0
N_LEVELS = 6
HALO = 8
FILL_ALIGN = 8
V7X_VMEM_LIMIT = 56 * 1024 * 1024


def _pick(n, cands):
    for c in cands:
        if n % c == 0:
            return c
    raise ValueError(f"no tile in {cands} divides {n}")


def _cparams(sem, vmem=None):
    return pltpu.CompilerParams(dimension_semantics=sem, vmem_limit_bytes=vmem or V7X_VMEM_LIMIT)


def _dot(a, b, precision=None):
    return jnp.dot(a, b, preferred_element_type=F32, precision=precision)


def _dot_nt(a, b, precision=None):
    return lax.dot_general(a, b, (((1,), (1,)), ((), ())), preferred_element_type=F32, precision=precision)


def _dot_tn(a, b, precision=None):
    return lax.dot_general(a, b, (((0,), (0,)), ((), ())), preferred_element_type=F32, precision=precision)


def _sigmoid(x):
    return 1.0 / (1.0 + jnp.exp(-x))


def _silu(x):
    return x * _sigmoid(x)


def _gelu_tanh(x):
    return 0.5 * x * (1.0 + jnp.tanh(math.sqrt(2.0 / math.pi) * (x + 0.044715 * (x * x * x))))


def _ln(x):
    mu = jnp.mean(x, axis=-1, keepdims=True)
    xc = x - mu
    return xc * lax.rsqrt(jnp.mean(xc * xc, axis=-1, keepdims=True) + EPS)


def _mod_row(i, nb, nctx, n_batch):
    return jnp.where(i % nb < nctx, n_batch, i // nb)


def _mod_kernel(c_ref, w_ref, b_ref, o_ref):
    c = c_ref[...]
    o_ref[0] = _dot(_silu(c), w_ref[0], HIGHEST) + b_ref[0]


def _mod_vectors(cc, w_ada, b_ada):
    n_layers, d, n6 = w_ada.shape
    tn = _pick(n6, (1024, 512, 128))
    return pl.pallas_call(
        _mod_kernel,
        grid=(n_layers, n6 // tn),
        in_specs=[pl.BlockSpec((8, d), lambda l, j: (0, 0)),
                  pl.BlockSpec((1, d, tn), lambda l, j: (l, 0, j)),
                  pl.BlockSpec((1, 1, tn), lambda l, j: (l, 0, j))],
        out_specs=pl.BlockSpec((1, 8, tn), lambda l, j: (l, 0, j)),
        out_shape=jax.ShapeDtypeStruct((n_layers, 8, n6), F32),
        compiler_params=_cparams(("parallel", "parallel")),
        name="mod_vectors",
    )(cc, w_ada, b_ada.reshape(n_layers, 1, n6))


def _lnmod_kernel(x_ref, m_ref, h_ref):
    y = _ln(x_ref[...])
    h_ref[...] = (y * (1.0 + m_ref[0, 1:2, :]) + m_ref[0, 0:1, :]).astype(h_ref.dtype)


def _ln_modulate(x, mods, tm, nb, nctx, n_batch):
    t, d = x.shape
    return pl.pallas_call(
        _lnmod_kernel,
        grid=(t // tm,),
        in_specs=[pl.BlockSpec((tm, d), lambda i: (i, 0)),
                  pl.BlockSpec((1, 6, d), lambda i: (_mod_row(i, nb, nctx, n_batch), 0, 0))],
        out_specs=pl.BlockSpec((tm, d), lambda i: (i, 0)),
        out_shape=jax.ShapeDtypeStruct((t, d), BF16),
        compiler_params=_cparams(("parallel",)),
        name="ln_modulate",
    )(x, mods)


def _matmul_kernel(a_ref, w_ref, o_ref):
    o_ref[...] = _dot(a_ref[...], w_ref[...]).astype(o_ref.dtype)


def _matmul(a, w, out_dtype=F32):
    m, k = a.shape
    n = w.shape[1]
    tm = _pick(m, (1536, 1024, 768, 512, 256, 128))
    tn = _pick(n, (1152, 1024, 768, 512, 256, 128))
    return pl.pallas_call(
        _matmul_kernel,
        grid=(m // tm, n // tn),
        in_specs=[pl.BlockSpec((tm, k), lambda i, j: (i, 0)),
                  pl.BlockSpec((k, tn), lambda i, j: (0, j))],
        out_specs=pl.BlockSpec((tm, tn), lambda i, j: (i, j)),
        out_shape=jax.ShapeDtypeStruct((m, n), out_dtype),
        compiler_params=_cparams(("parallel", "parallel")),
        name="in_proj",
    )(a, w)


def _unit_tri_inverses(lows, eye, pair_refs):
    ts = [eye - low * pair_ref[0] for low, pair_ref in zip(lows, pair_refs)]
    for lvl in range(1, N_LEVELS):
        tbs = [t.astype(BF16) for t in ts]
        xs = [_dot(tb16, (low * pair_ref[lvl]).astype(BF16)) for tb16, low, pair_ref in zip(tbs, lows, pair_refs)]
        ts = [t - _dot(x.astype(BF16), tb16) for t, x, tb16 in zip(ts, xs, tbs)]
    return ts


def _a_prep_kernel(x_ref, xp_ref, xn_ref, sm_ref, cw_ref, na_ref, dtb_ref, trf_ref, trr_ref, pairf_ref, pairr_ref,
                   uf_ref, wqf_ref, kdf_ref, atf_ref, ur_ref, wqr_ref, kdr_ref, atr_ref, eg_ref, ext_ref,
                   *, tm, nb, nctx, mixw):
    i = pl.program_id(0)
    j = i % nb
    first = jnp.logical_or(j == 0, j == nctx)
    last = jnp.logical_or(j == nctx - 1, j == nb - 1)
    ext_ref[HALO:HALO + tm, :] = x_ref[...]
    ext_ref[0:HALO, :] = jnp.where(first, 0.0, xp_ref[...])
    ext_ref[HALO + tm:2 * HALO + tm, :] = jnp.where(last, 0.0, xn_ref[...])
    off = HALO - CONV_W // 2
    acc = cw_ref[0:1, :] * ext_ref[off:off + tm, :]
    for t in range(1, CONV_W):
        acc = acc + cw_ref[t:t + 1, :] * ext_ref[off + t:off + t + tm, :]
    y = _silu(acc)
    n_heads = mixw // HEAD_DIM
    sm = sm_ref[...]
    lane = lax.broadcasted_iota(jnp.int32, sm.shape, 1)
    zz = sm + dtb_ref[...]
    softplus = jnp.maximum(zz, 0.0) + jnp.log(1.0 + jnp.exp(-jnp.abs(zz)))
    g = jnp.where(jnp.logical_and(lane >= 2 * n_heads, lane < 4 * n_heads), na_ref[...] * softplus, 0.0)
    beta_all = _sigmoid(sm)
    tri_f = trf_ref[...]
    tri_r = trr_ref[...]
    cs_f = _dot(tri_f, g, HIGHEST)
    cs_r = _dot(tri_r, g, HIGHEST)
    gcum = jnp.where(lane < 3 * n_heads, cs_f, cs_r)
    gtot = _dot(jnp.maximum(tri_f, tri_r), g, HIGHEST)
    eg_ref[...] = jnp.exp(gtot)
    gcum_t = gcum.T
    sr = SOLVE_ROWS
    row = lax.broadcasted_iota(jnp.int32, (sr, sr), 0)
    col = lax.broadcasted_iota(jnp.int32, (sr, sr), 1)
    eye = (row == col).astype(F32)
    incls = (tri_f[0:sr, 0:sr] > 0.5, tri_r[0:sr, 0:sr] > 0.5)
    pairs = ([pairf_ref[lvl, 0:sr, 0:sr] for lvl in range(N_LEVELS)],
             [pairr_ref[lvl, 0:sr, 0:sr] for lvl in range(N_LEVELS)])
    outs = ((uf_ref, wqf_ref, kdf_ref, atf_ref), (ur_ref, wqr_ref, kdr_ref, atr_ref))
    items = []
    for sb in range(tm // sr):
        rs = sb * sr
        for h in range(n_heads):
            c0 = h * HEAD_DIM
            qh = y[rs:rs + sr, c0:c0 + HEAD_DIM]
            kh = y[rs:rs + sr, mixw + c0:mixw + c0 + HEAD_DIM]
            vh = y[rs:rs + sr, 2 * mixw + c0:2 * mixw + c0 + HEAD_DIM]
            qh = qh * (lax.rsqrt(jnp.sum(qh * qh, axis=-1, keepdims=True) + EPS) * HEAD_DIM ** -0.5)
            kh = kh * lax.rsqrt(jnp.sum(kh * kh, axis=-1, keepdims=True) + EPS)
            kb16 = kh.astype(BF16)
            qk = _dot_nt(qh.astype(BF16), kb16)
            for d in range(2):
                lb = d * n_heads + h
                lg = (2 + d) * n_heads + h
                beta = beta_all[rs:rs + sr, lb:lb + 1]
                gc = gcum[rs:rs + sr, lg:lg + 1]
                incl = incls[d]
                decay = jnp.where(incl, jnp.exp(jnp.where(incl, gc - gcum_t[lg:lg + 1, rs:rs + sr], 0.0)), 0.0)
                kbeta = kh * beta
                low = jnp.where(row == col, 0.0, _dot_nt(kbeta.astype(BF16), kb16) * decay)
                rhs = jnp.concatenate([vh * beta, kbeta * jnp.exp(gc)], axis=1).astype(BF16)
                items.append(dict(h=h, d=d, rs=rs, low=low, rhs=rhs, attn=(qk * decay).astype(BF16),
                                  qd=(qh * jnp.exp(gc)).astype(BF16),
                                  kd=(kh * jnp.exp(gtot[rs:rs + sr, lg:lg + 1] - gc)).astype(BF16)))
    tinvs = _unit_tri_inverses([it["low"] for it in items], eye, [pairs[it["d"]] for it in items])
    for it, tinv in zip(items, tinvs):
        u_ref, wq_ref, kd_ref, at_ref = outs[it["d"]]
        h, rs = it["h"], it["rs"]
        c0 = h * HEAD_DIM
        sol = _dot(tinv.astype(BF16), it["rhs"])
        u_ref[rs:rs + sr, c0:c0 + HEAD_DIM] = sol[:, :HEAD_DIM]
        w = sol[:, HEAD_DIM:].astype(BF16)
        kd_ref[rs:rs + sr, c0:c0 + HEAD_DIM] = it["kd"]
        for ci in range(sr // CHUNK):
            r0 = ci * CHUNK
            g0 = rs + r0
            wq_ref[2 * g0:2 * g0 + CHUNK, c0:c0 + HEAD_DIM] = w[r0:r0 + CHUNK]
            wq_ref[2 * g0 + CHUNK:2 * g0 + 2 * CHUNK, c0:c0 + HEAD_DIM] = it["qd"][r0:r0 + CHUNK]
            at_ref[g0:g0 + CHUNK, h * CHUNK:(h + 1) * CHUNK] = it["attn"][r0:r0 + CHUNK, r0:r0 + CHUNK]


def _a_prep(p, conv_w, neg_a, dtb, tm, nb, nctx, mixw, col_small):
    t = p.shape[0]
    c3 = 3 * mixw
    n_heads = mixw // HEAD_DIM
    r = np.arange(tm)
    same = (r[:, None] // CHUNK) == (r[None, :] // CHUNK)
    tri_f = jnp.asarray((same & (r[None, :] <= r[:, None])).astype(np.float32))
    tri_r = jnp.asarray((same & (r[None, :] >= r[:, None])).astype(np.float32))
    pair_f = []
    for lvl in range(N_LEVELS):
        s_blk = 1 << lvl
        joined = ((r[:, None] // (2 * s_blk)) == (r[None, :] // (2 * s_blk))) \
            & ((r[:, None] % (2 * s_blk)) >= s_blk) & ((r[None, :] % (2 * s_blk)) < s_blk)
        pair_f.append(joined.astype(np.float32))
    pair_f = np.stack(pair_f)
    pair_r = jnp.asarray(np.transpose(pair_f, (0, 2, 1)))
    pair_f = jnp.asarray(pair_f)
    hb = tm // HALO
    last_hb = t // HALO - 1
    kern = functools.partial(_a_prep_kernel, tm=tm, nb=nb, nctx=nctx, mixw=mixw)
    per_dir_specs = [pl.BlockSpec((tm, mixw), lambda i: (i, 0)),
                     pl.BlockSpec((2 * tm, mixw), lambda i: (i, 0)),
                     pl.BlockSpec((tm, mixw), lambda i: (i, 0)),
                     pl.BlockSpec((tm, n_heads * CHUNK), lambda i: (i, 0))]
    per_dir_shapes = [jax.ShapeDtypeStruct((t, mixw), F32), jax.ShapeDtypeStruct((2 * t, mixw), BF16),
                      jax.ShapeDtypeStruct((t, mixw), BF16), jax.ShapeDtypeStruct((t, n_heads * CHUNK), BF16)]
    return pl.pallas_call(
        kern,
        grid=(t // tm,),
        in_specs=[pl.BlockSpec((tm, c3), lambda i: (i, 0)),
                  pl.BlockSpec((HALO, c3), lambda i: (jnp.maximum(i * hb - 1, 0), 0)),
                  pl.BlockSpec((HALO, c3), lambda i: (jnp.minimum((i + 1) * hb, last_hb), 0)),
                  pl.BlockSpec((tm, 128), lambda i: (i, col_small // 128)),
                  pl.BlockSpec((CONV_W, c3), lambda i: (0, 0)),
                  pl.BlockSpec((1, 128), lambda i: (0, 0)),
                  pl.BlockSpec((1, 128), lambda i: (0, 0)),
                  pl.BlockSpec((tm, tm), lambda i: (0, 0)),
                  pl.BlockSpec((tm, tm), lambda i: (0, 0)),
                  pl.BlockSpec((N_LEVELS, tm, tm), lambda i: (0, 0, 0)),
                  pl.BlockSpec((N_LEVELS, tm, tm), lambda i: (0, 0, 0))],
        out_specs=per_dir_specs * 2 + [pl.BlockSpec((tm, 128), lambda i: (i, 0))],
        out_shape=per_dir_shapes * 2 + [jax.ShapeDtypeStruct((t, 128), F32)],
        scratch_shapes=[pltpu.VMEM((tm + 2 * HALO, c3), F32)],
        compiler_params=_cparams(("parallel",)),
        name="deltanet_prep",
    )(p, p, p, p, conv_w, neg_a, dtb, tri_f, tri_r, pair_f, pair_r)


def _a_scan_kernel(uf, wqf, kdf, atf, egf, ur, wqr, kdr, atr, egr, of_ref, or_ref, s_ref, sbd_ref, *, n_heads):
    n = pl.program_id(1)

    @pl.when(n == 0)
    def _():
        s_ref[...] = jnp.zeros_like(s_ref)
        sbd_ref[...] = jnp.zeros_like(sbd_ref)

    mixw = n_heads * HEAD_DIM
    lane_head = lax.broadcasted_iota(jnp.int32, (CHUNK, mixw), 1) // HEAD_DIM
    for d, (u_ref, wq_ref, kd_ref, at_ref, eg_ref, o_ref) in enumerate(
            ((uf, wqf, kdf, atf, egf, of_ref), (ur, wqr, kdr, atr, egr, or_ref))):
        for ci in ((0, 1) if d == 0 else (1, 0)):
            r0 = ci * CHUNK
            r = _dot(wq_ref[2 * r0:2 * r0 + 2 * CHUNK, :], sbd_ref[d])
            v_new = u_ref[r0:r0 + CHUNK, :] - r[:CHUNK]
            v_stack = jnp.concatenate([jnp.where(lane_head == h, v_new, 0.0) for h in range(n_heads)],
                                      axis=0).astype(BF16)
            o_ref[r0:r0 + CHUNK, :] = r[CHUNK:] + _dot(at_ref[r0:r0 + CHUNK, :], v_stack)
            vnb = v_new.astype(BF16)
            for h in range(n_heads):
                c0 = h * HEAD_DIM
                lg = (2 + d) * n_heads + h
                s_new = (s_ref[d, h] * eg_ref[r0:r0 + 1, lg:lg + 1]
                         + _dot_tn(kd_ref[r0:r0 + CHUNK, c0:c0 + HEAD_DIM], vnb[:, c0:c0 + HEAD_DIM]))
                s_ref[d, h] = s_new
                sbd_ref[d, c0:c0 + HEAD_DIM, c0:c0 + HEAD_DIM] = s_new.astype(BF16)


def _seq_block_maps(nsb, nctx_sb):
    def fwd(b, n):
        return b * nsb + n

    def rev(b, n):
        return b * nsb + jnp.where(n < nctx_sb, nctx_sb - 1 - n, nsb - 1 - (n - nctx_sb))

    return fwd, rev


def _a_scan(prep, n_batch, tb, ctx_len, mixw):
    uf, wqf, kdf, atf, ur, wqr, kdr, atr, eg = prep
    t = uf.shape[0]
    n_heads = mixw // HEAD_DIM
    nsb = tb // STEP_ROWS
    fwd, rev = _seq_block_maps(nsb, ctx_len // STEP_ROWS)
    in_specs = []
    for m in (fwd, rev):
        in_specs += [pl.BlockSpec((STEP_ROWS, mixw), lambda b, n, m=m: (m(b, n), 0)),
                     pl.BlockSpec((2 * STEP_ROWS, mixw), lambda b, n, m=m: (m(b, n), 0)),
                     pl.BlockSpec((STEP_ROWS, mixw), lambda b, n, m=m: (m(b, n), 0)),
                     pl.BlockSpec((STEP_ROWS, n_heads * CHUNK), lambda b, n, m=m: (m(b, n), 0)),
                     pl.BlockSpec((STEP_ROWS, 128), lambda b, n, m=m: (m(b, n), 0))]
    return pl.pallas_call(
        functools.partial(_a_scan_kernel, n_heads=n_heads),
        grid=(n_batch, nsb),
        in_specs=in_specs,
        out_specs=[pl.BlockSpec((STEP_ROWS, mixw), lambda b, n: (fwd(b, n), 0)),
                   pl.BlockSpec((STEP_ROWS, mixw), lambda b, n: (rev(b, n), 0))],
        out_shape=[jax.ShapeDtypeStruct((t, mixw), F32)] * 2,
        scratch_shapes=[pltpu.VMEM((2, n_heads, HEAD_DIM, HEAD_DIM), F32),
                        pltpu.VMEM((2, mixw, mixw), BF16)],
        compiler_params=_cparams(("arbitrary", "arbitrary")),
        name="deltanet_scan",
    )(uf, wqf, kdf, atf, eg, ur, wqr, kdr, atr, eg)


def _b_prep_kernel(q_ref, kv_ref, gain_ref, cos_ref, sin_ref, qo_ref, ko_ref, vo_ref, *, mixw):
    n_heads = mixw // HEAD_DIM
    n_kv = n_heads // 2
    cosf = cos_ref[...]
    sinf = sin_ref[...]

    def norm_rope(xh, gain, scale):
        yh = xh * lax.rsqrt(jnp.mean(xh * xh, axis=-1, keepdims=True) + EPS) * gain
        return (yh * cosf + pltpu.roll(yh, HEAD_DIM // 2, 1) * sinf) * scale

    for h in range(n_heads):
        c0 = h * HEAD_DIM
        qo_ref[:, c0:c0 + HEAD_DIM] = norm_rope(q_ref[:, c0:c0 + HEAD_DIM], gain_ref[0:1, :],
                                                HEAD_DIM ** -0.5 * LOG2_E).astype(qo_ref.dtype)
    for h in range(n_kv):
        c0 = h * HEAD_DIM
        ko_ref[:, c0:c0 + HEAD_DIM] = norm_rope(kv_ref[:, c0:c0 + HEAD_DIM], gain_ref[1:2, :], 1.0).astype(ko_ref.dtype)
    ones = jnp.ones((q_ref.shape[0], HEAD_DIM), vo_ref.dtype)
    for h in range(n_kv):
        c0 = (n_kv + h) * HEAD_DIM
        vo_ref[:, 2 * h * HEAD_DIM:(2 * h + 1) * HEAD_DIM] = kv_ref[:, c0:c0 + HEAD_DIM].astype(vo_ref.dtype)
        vo_ref[:, (2 * h + 1) * HEAD_DIM:(2 * h + 2) * HEAD_DIM] = ones


def _b_prep(p, qk_gain, cosf, sinf, tm, nb, mixw, col_q, col_kv):
    t = p.shape[0]
    kvw = mixw // 2
    return pl.pallas_call(
        functools.partial(_b_prep_kernel, mixw=mixw),
        grid=(t // tm,),
        in_specs=[pl.BlockSpec((tm, mixw), lambda i: (i, col_q // mixw)),
                  pl.BlockSpec((tm, mixw), lambda i: (i, col_kv // mixw)),
                  pl.BlockSpec((2, HEAD_DIM), lambda i: (0, 0)),
                  pl.BlockSpec((tm, HEAD_DIM), lambda i: (i % nb, 0)),
                  pl.BlockSpec((tm, HEAD_DIM), lambda i: (i % nb, 0))],
        out_specs=[pl.BlockSpec((tm, mixw), lambda i: (i, 0)),
                   pl.BlockSpec((tm, kvw), lambda i: (i, 0)),
                   pl.BlockSpec((tm, 2 * kvw), lambda i: (i, 0))],
        out_shape=[jax.ShapeDtypeStruct((t, mixw), BF16),
                   jax.ShapeDtypeStruct((t, kvw), BF16),
                   jax.ShapeDtypeStruct((t, 2 * kvw), BF16)],
        compiler_params=_cparams(("parallel",)),
        name="gqa_prep",
    )(p, p, qk_gain, cosf, sinf)


def _attn_kernel(q_ref, k_ref, v_ref, o_ref, m_ref, acc_ref, sa_ref, sb_ref, *, tq, tk, nk, nctx_q, ctx_len):
    qi = pl.program_id(2)
    qb = q_ref[...]
    q2 = jnp.concatenate([qb[:, :HEAD_DIM], qb[:, HEAD_DIM:]], axis=0)
    m_ref[...] = jnp.full_like(m_ref, -1e30)
    acc_ref[...] = jnp.zeros_like(acc_ref)

    def scores(i, s_ref):
        r0 = pl.multiple_of(i * tk, tk)
        s_ref[...] = _dot_nt(q2, k_ref[pl.ds(r0, tk), :])

    def absorb(s, vc):
        m_prev = m_ref[...]
        m_new = jnp.maximum(m_prev, jnp.max(s, axis=-1, keepdims=True))
        p = jnp.exp2(s - m_new)
        acc_ref[...] = jnp.exp2(m_prev - m_new) * acc_ref[...] + _dot(p.astype(BF16), vc)
        m_ref[...] = m_new

    def absorb_chunk(i, s_ref):
        r0 = pl.multiple_of(i * tk, tk)
        absorb(s_ref[...], v_ref[pl.ds(r0, tk), :])

    @pl.when(qi >= nctx_q)
    def _():
        scores(0, sa_ref)

        def pair(j, carry):
            scores(2 * j + 1, sb_ref)
            absorb_chunk(2 * j, sa_ref)
            scores(2 * j + 2, sa_ref)
            absorb_chunk(2 * j + 1, sb_ref)
            return carry

        n_pairs = (nk - 1) // 2
        lax.fori_loop(0, n_pairs, pair, 0)
        if nk % 2 == 0:
            scores(nk - 1, sb_ref)
            absorb_chunk(nk - 2, sa_ref)
            absorb_chunk(nk - 1, sb_ref)
        else:
            absorb_chunk(nk - 1, sa_ref)

    @pl.when(qi < nctx_q)
    def _():
        absorb(_dot_nt(q2, k_ref[0:ctx_len, :]), v_ref[0:ctx_len, :])

    acc = acc_ref[...]
    o = acc[:, :HEAD_DIM] / acc[:, HEAD_DIM:HEAD_DIM + 1]
    o_ref[...] = jnp.concatenate([o[:tq], o[tq:]], axis=1).astype(o_ref.dtype)


def _attention(q, k, v, n_batch, tb, ctx_len, mixw):
    t = q.shape[0]
    n_kv = mixw // HEAD_DIM // 2
    tq = _pick(math.gcd(ctx_len, tb), (256, 128))
    tk = _pick(tb, (1408, 768, 1024, 512, 640, 384, 256, 128))
    nq, nk = tb // tq, tb // tk
    return pl.pallas_call(
        functools.partial(_attn_kernel, tq=tq, tk=tk, nk=nk, nctx_q=ctx_len // tq, ctx_len=ctx_len),
        grid=(n_batch, n_kv, nq),
        in_specs=[pl.BlockSpec((tq, 2 * HEAD_DIM), lambda b, j, qi: (b * nq + qi, j)),
                  pl.BlockSpec((tb, HEAD_DIM), lambda b, j, qi: (b, j)),
                  pl.BlockSpec((tb, 2 * HEAD_DIM), lambda b, j, qi: (b, j))],
        out_specs=pl.BlockSpec((tq, 2 * HEAD_DIM), lambda b, j, qi: (b * nq + qi, j)),
        out_shape=jax.ShapeDtypeStruct((t, mixw), BF16),
        scratch_shapes=[pltpu.VMEM((2 * tq, 1), F32), pltpu.VMEM((2 * tq, 2 * HEAD_DIM), F32),
                        pltpu.VMEM((2 * tq, tk), F32), pltpu.VMEM((2 * tq, tk), F32)],
        compiler_params=_cparams(("parallel", "parallel", "parallel")),
        name="gqa_attention",
    )(q, k, v)


def _c_kernel(u_ref, v_ref, ws_ref, bs_ref, o_ref, *, tm, mixw):
    n_groups = mixw // HEAD_DIM
    for ci in range(tm // CHUNK_C):
        r0 = ci * CHUNK_C
        u = _gelu_tanh(u_ref[r0:r0 + CHUNK_C, :])
        vn = _ln(_gelu_tanh(v_ref[r0:r0 + CHUNK_C, :])).astype(BF16)
        for g in range(n_groups):
            c0 = g * HEAD_DIM
            vm = _dot(ws_ref[g], vn[:, c0:c0 + HEAD_DIM]) + bs_ref[:, g:g + 1]
            o_ref[r0:r0 + CHUNK_C, c0:c0 + HEAD_DIM] = (u[:, c0:c0 + HEAD_DIM] * vm).astype(o_ref.dtype)


def _mixer_c(p, ws, bs_t, tm, mixw, col_u, col_v):
    t = p.shape[0]
    n_groups = mixw // HEAD_DIM
    return pl.pallas_call(
        functools.partial(_c_kernel, tm=tm, mixw=mixw),
        grid=(t // tm,),
        in_specs=[pl.BlockSpec((tm, mixw), lambda i: (i, col_u // mixw)),
                  pl.BlockSpec((tm, mixw), lambda i: (i, col_v // mixw)),
                  pl.BlockSpec((n_groups, CHUNK_C, CHUNK_C), lambda i: (0, 0, 0)),
                  pl.BlockSpec((CHUNK_C, 128), lambda i: (0, 0))],
        out_specs=pl.BlockSpec((tm, mixw), lambda i: (i, 0)),
        out_shape=jax.ShapeDtypeStruct((t, mixw), BF16),
        compiler_params=_cparams(("parallel",)),
        name="gmlp",
    )(p, p, ws, bs_t)


def _hgrn_tables():
    tau = np.arange(CHUNK)
    sums_t = [tau[None, :] <= tau[:, None], tau[None, :] > tau[:, None]]
    pair_t = []
    for lvl in range(N_LEVELS):
        m = CHUNK >> (lvl + 1)
        blk = tau // (2 * m)
        upper = (tau % (2 * m)) >= m
        ref = blk * 2 * m + m - 1
        r = tau[None, :]
        w_up = upper[:, None] & (r > ref[:, None]) & (r <= tau[:, None])
        w_lo = (~upper)[:, None] & (r > tau[:, None]) & (r <= ref[:, None])
        sums_t.append(w_up | w_lo)
        pair_t.append((blk[:, None] == blk[None, :]) & upper[:, None] & (~upper)[None, :])
    sums_t = np.concatenate(sums_t, axis=0).astype(np.float32)
    pair_t = np.stack(pair_t).astype(np.float32)
    flip = tau[::-1]
    sums, pair = [], []
    for d in range(2):
        if d == 0:
            s_d, p_d = sums_t, pair_t
        else:
            s_d = sums_t.reshape(-1, CHUNK, CHUNK)[:, flip][:, :, flip].reshape(-1, CHUNK)
            p_d = pair_t[:, flip][:, :, flip]
        sums.append(np.concatenate([s_d, s_d, s_d], axis=1))
        pair.append(p_d)
    return np.stack(sums), np.stack(pair)


def _d_scan_kernel(qf, vf, ff, qr, vr, fr, lb_ref, sums_ref, pair_ref, of_ref, or_ref, st_ref, *, n_heads):
    n = pl.program_id(1)

    @pl.when(n == 0)
    def _():
        st_ref[...] = jnp.zeros_like(st_ref)

    c = CHUNK
    row = lax.broadcasted_iota(jnp.int32, (c, c), 0)
    col = lax.broadcasted_iota(jnp.int32, (c, c), 1)
    eye = (row == col).astype(F32)
    inst = []
    for d, (q_ref, v_ref, f_ref, o_ref) in enumerate(((qf, vf, ff, of_ref), (qr, vr, fr, or_ref))):
        for pos, ci in enumerate((0, 1) if d == 0 else (1, 0)):
            for h in range(n_heads):
                inst.append((d, pos, ci * c, h * HEAD_DIM, h, q_ref, v_ref, f_ref, o_ref))
    qs = [q_ref[r0:r0 + c, c0:c0 + HEAD_DIM] for (d, pos, r0, c0, h, q_ref, v_ref, f_ref, o_ref) in inst]
    vs = [v_ref[r0:r0 + c, c0:c0 + HEAD_DIM].astype(BF16) for (d, pos, r0, c0, h, q_ref, v_ref, f_ref, o_ref) in inst]
    es = [jnp.exp(-jnp.clip(f_ref[r0:r0 + c, c0:c0 + HEAD_DIM], -GATE_CLIP, GATE_CLIP))
          for (d, pos, r0, c0, h, q_ref, v_ref, f_ref, o_ref) in inst]
    sgs = [1.0 / (1.0 + e) for e in es]
    lbs = [lb_ref[0:1, it[3]:it[3] + HEAD_DIM] for it in inst]
    logfs = [jnp.log(lb + (1.0 - lb) * sg) for lb, sg in zip(lbs, sgs)]
    ks = [(1.0 - lb) * (e * sg) for lb, e, sg in zip(lbs, es, sgs)]
    his = [x.astype(BF16) for x in logfs]
    r1s = [x - hi.astype(F32) for x, hi in zip(logfs, his)]
    mids = [x.astype(BF16) for x in r1s]
    los = [(x - mid.astype(F32)).astype(BF16) for x, mid in zip(r1s, mids)]
    cats = [jnp.concatenate([hi, mid, lo], axis=0) for hi, mid, lo in zip(his, mids, los)]
    xs = []
    for i0 in range(0, len(inst), 2):
        x2 = _dot(sums_ref[inst[i0][0]], jnp.concatenate([cats[i0], cats[i0 + 1]], axis=1))
        xs += [x2[:, :HEAD_DIM], x2[:, HEAD_DIM:]]
    kbs = [k.astype(BF16) for k in ks]
    attns = [eye * _dot_nt(q.astype(BF16), kb) for q, kb in zip(qs, kbs)]
    for lvl in range(N_LEVELS):
        zls = [jnp.exp(x[(2 + lvl) * c:(3 + lvl) * c]) for x in xs]
        attns = [a + pair_ref[it[0], lvl] * _dot_nt((q * zl).astype(BF16), (k * zl).astype(BF16))
                 for a, it, q, k, zl in zip(attns, inst, qs, ks, zls)]
    qds = [(q * jnp.exp(x[0:c])).astype(BF16) for q, x in zip(qs, xs)]
    kds = [(k * jnp.exp(x[c:2 * c])).astype(BF16) for k, x in zip(ks, xs)]
    intra = [_dot(a.astype(BF16), v) for a, v in zip(attns, vs)]
    kvs = [_dot_tn(v, kd) for v, kd in zip(vs, kds)]
    states = [[st_ref[d, h] for h in range(n_heads)] for d in range(2)]
    for pos in range(2):
        for i, (d, p_i, r0, c0, h, q_ref, v_ref, f_ref, o_ref) in enumerate(inst):
            if p_i != pos:
                continue
            st = states[d][h]
            o_ref[r0:r0 + c, c0:c0 + HEAD_DIM] = _dot_nt(qds[i], st.astype(BF16)) + intra[i]
            last = c - 1 if d == 0 else 0
            states[d][h] = st * jnp.exp(xs[i][last:last + 1, :]) + kvs[i]
    for d in range(2):
        for h in range(n_heads):
            st_ref[d, h] = states[d][h]


def _d_scan(p, lb, n_batch, tb, ctx_len, mixw, col_q, col_i, col_f):
    t = p.shape[0]
    n_heads = mixw // HEAD_DIM
    nsb = tb // STEP_ROWS
    fwd, rev = _seq_block_maps(nsb, ctx_len // STEP_ROWS)
    sums_np, pair_np = _hgrn_tables()
    sums = jnp.asarray(sums_np, BF16)
    pair = jnp.asarray(pair_np, F32)
    in_specs = []
    for d, m in enumerate((fwd, rev)):
        in_specs += [pl.BlockSpec((STEP_ROWS, mixw), lambda b, n, m=m: (m(b, n), col_q // mixw)),
                     pl.BlockSpec((STEP_ROWS, mixw), lambda b, n, m=m: (m(b, n), col_i // mixw)),
                     pl.BlockSpec((STEP_ROWS, mixw), lambda b, n, m=m, d=d: (m(b, n), col_f // mixw + d))]
    in_specs += [pl.BlockSpec((1, mixw), lambda b, n: (0, 0)),
                 pl.BlockSpec(sums.shape, lambda b, n: (0, 0, 0)),
                 pl.BlockSpec(pair.shape, lambda b, n: (0, 0, 0, 0))]
    return pl.pallas_call(
        functools.partial(_d_scan_kernel, n_heads=n_heads),
        grid=(n_batch, nsb),
        in_specs=in_specs,
        out_specs=[pl.BlockSpec((STEP_ROWS, mixw), lambda b, n: (fwd(b, n), 0)),
                   pl.BlockSpec((STEP_ROWS, mixw), lambda b, n: (rev(b, n), 0))],
        out_shape=[jax.ShapeDtypeStruct((t, mixw), F32)] * 2,
        scratch_shapes=[pltpu.VMEM((2, n_heads, HEAD_DIM, HEAD_DIM), F32)],
        compiler_params=_cparams(("arbitrary", "arbitrary")),
        name="hgrn2_scan",
    )(p, p, p, p, p, p, lb, sums, pair)


def _route(logits_t, bias, n_experts):
    per = n_experts // N_EXPERT_GROUPS
    scores = _sigmoid(logits_t)
    sel = scores + bias
    rows_sel = [sel[e:e + 1, :] for e in range(n_experts)]
    rows_sc = [scores[e:e + 1, :] for e in range(n_experts)]

    def top2_sum(vals):
        hi = vals[0]
        lo = jnp.full_like(hi, -jnp.inf)
        for x in vals[1:]:
            lo = jnp.maximum(lo, jnp.minimum(hi, x))
            hi = jnp.maximum(hi, x)
        return hi + lo

    best = jnp.zeros_like(rows_sel[0], dtype=jnp.int32)
    best_score = top2_sum(rows_sel[0:per])
    for g in range(1, N_EXPERT_GROUPS):
        gs = top2_sum(rows_sel[g * per:(g + 1) * per])
        better = gs > best_score
        best = jnp.where(better, g, best)
        best_score = jnp.where(better, gs, best_score)
    in_sel, in_sc = [], []
    for j in range(per):
        a = rows_sel[j]
        c = rows_sc[j]
        for g in range(1, N_EXPERT_GROUPS):
            a = jnp.where(best == g, rows_sel[g * per + j], a)
            c = jnp.where(best == g, rows_sc[g * per + j], c)
        in_sel.append(a)
        in_sc.append(c)

    def first_argmax(vals):
        idx = jnp.zeros_like(best)
        top = vals[0]
        for j in range(1, per):
            better = vals[j] > top
            idx = jnp.where(better, j, idx)
            top = jnp.where(better, vals[j], top)
        return idx

    i1 = first_argmax(in_sel)
    i2 = first_argmax([jnp.where(i1 == j, -jnp.inf, in_sel[j]) for j in range(per)])

    def pick(idx):
        w = in_sc[0]
        for j in range(1, per):
            w = jnp.where(idx == j, in_sc[j], w)
        return w

    w1, w2 = pick(i1), pick(i2)
    tot = w1 + w2
    return best * per + i1, best * per + i2, w1 / tot, w2 / tot


def _outproj_kernel(oaf, oar, ga, odf, odr, gd, yb, yc, na, nd, wo, x_ref, m_ref, lng, lnb, wr, br,
                    x1_ref, h2_ref, ei_ref, ew_ref, *, alpha, mixw, tm, n_experts):
    n_heads = mixw // HEAD_DIM

    def gated(of_ref, or_ref, g_ref, gain_ref):
        parts = []
        for h in range(n_heads):
            c0 = h * HEAD_DIM
            o = of_ref[:, c0:c0 + HEAD_DIM] + or_ref[:, c0:c0 + HEAD_DIM]
            y = o * lax.rsqrt(jnp.mean(o * o, axis=-1, keepdims=True) + EPS) * gain_ref[...]
            parts.append((y * _silu(g_ref[:, c0:c0 + HEAD_DIM])).astype(BF16))
        return jnp.concatenate(parts, axis=1)

    ya = gated(oaf, oar, ga, na)
    yd = gated(odf, odr, gd, nd)
    y = (_dot(ya, wo[0:mixw, :]) + _dot(yb[...], wo[mixw:2 * mixw, :])
         + _dot(yc[...], wo[2 * mixw:3 * mixw, :]) + _dot(yd, wo[3 * mixw:4 * mixw, :]))
    x1 = _ln(alpha * x_ref[...] + m_ref[0, 2:3, :] * y) * lng[...] + lnb[...]
    x1_ref[...] = x1
    h2 = _ln(x1) * (1.0 + m_ref[0, 4:5, :]) + m_ref[0, 3:4, :]
    h2_ref[...] = h2
    h_hi = h2.astype(BF16)
    h_r = h2 - h_hi.astype(F32)
    h_mid = h_r.astype(BF16)
    h_lo = (h_r - h_mid.astype(F32)).astype(BF16)
    pa = _dot(h_hi, wr[:, 0:2 * 128])
    pb = _dot(h_mid, wr[:, 0:2 * 128])
    pc = _dot(h_hi, wr[:, 2 * 128:4 * 128])
    pd = _dot(h_lo, wr[:, 0:2 * 128])
    logits = ((pd[:, :128] + pb[:, 128:] + pc[:, :128]) + (pb[:, :128] + pa[:, 128:])) + pa[:, :128]
    logits_t = logits.T[:n_experts]
    e1, e2, w1, w2 = _route(logits_t, br[:, 0:1], n_experts)
    zi = jnp.zeros((6, tm), jnp.int32)
    ei_ref[...] = jnp.concatenate([e1, e2, zi], axis=0)
    ew_ref[...] = jnp.concatenate([w1, w2, zi.astype(F32)], axis=0)


def _outproj(oaf, oar, odf, odr, p, yb, yc, norm_a, norm_d, w_out, x, mods, ln_g, ln_b, w_router_p, b_router,
             alpha, tm, nb, nctx, n_batch, mixw, col_ga, col_gd):
    t, d = x.shape
    n_experts = b_router.shape[0]
    row = lambda i: (i, 0)
    const = lambda i: (0, 0)
    kern = functools.partial(_outproj_kernel, alpha=alpha, mixw=mixw, tm=tm, n_experts=n_experts)
    return pl.pallas_call(
        kern,
        grid=(t // tm,),
        in_specs=[pl.BlockSpec((tm, mixw), row), pl.BlockSpec((tm, mixw), row),
                  pl.BlockSpec((tm, mixw), lambda i: (i, col_ga // mixw)),
                  pl.BlockSpec((tm, mixw), row), pl.BlockSpec((tm, mixw), row),
                  pl.BlockSpec((tm, mixw), lambda i: (i, col_gd // mixw)),
                  pl.BlockSpec((tm, mixw), row), pl.BlockSpec((tm, mixw), row),
                  pl.BlockSpec((1, HEAD_DIM), const), pl.BlockSpec((1, HEAD_DIM), const),
                  pl.BlockSpec((N_MIXERS * mixw, d), const),
                  pl.BlockSpec((tm, d), row),
                  pl.BlockSpec((1, 6, d), lambda i: (_mod_row(i, nb, nctx, n_batch), 0, 0)),
                  pl.BlockSpec((1, d), const), pl.BlockSpec((1, d), const),
                  pl.BlockSpec((d, 4 * 128), const), pl.BlockSpec((n_experts, 128), const)],
        out_specs=[pl.BlockSpec((tm, d), row), pl.BlockSpec((tm, d), row),
                   pl.BlockSpec((8, tm), lambda i: (0, i)), pl.BlockSpec((8, tm), lambda i: (0, i))],
        out_shape=[jax.ShapeDtypeStruct((t, d), F32), jax.ShapeDtypeStruct((t, d), F32),
                   jax.ShapeDtypeStruct((8, t), jnp.int32), jax.ShapeDtypeStruct((8, t), F32)],
        compiler_params=_cparams(("parallel",)),
        name="out_proj_router",
    )(oaf, oar, p, odf, odr, p, yb, yc, norm_a, norm_d, w_out, x, mods, ln_g, ln_b, w_router_p, b_router)


def _rank_kernel(ei_ref, su_ref, rank_ref, cnt_ref, base_ref, *, n_experts, tm):
    i = pl.program_id(0)

    @pl.when(i == 0)
    def _():
        base_ref[...] = jnp.zeros_like(base_ref)

    eid = lax.broadcasted_iota(jnp.int32, (n_experts, tm), 0)
    o1 = (eid == ei_ref[0:1, :]).astype(F32)
    o2 = (eid == ei_ref[1:2, :]).astype(F32)
    cnt = o1 + o2
    before = _dot(cnt.astype(BF16), su_ref[...]) + base_ref[:, 0:1]
    r1 = jnp.sum(o1 * before, axis=0, keepdims=True)
    r2 = jnp.sum(o2 * before, axis=0, keepdims=True)
    rank_ref[...] = jnp.concatenate([r1, r2, jnp.zeros((6, tm), F32)], axis=0).astype(jnp.int32)
    base_ref[...] = base_ref[...] + jnp.sum(cnt, axis=1, keepdims=True)
    cnt_ref[...] = base_ref[...]


def _slot_ranks(ei, n_experts, tm):
    t = ei.shape[1]
    r = np.arange(tm)
    su = jnp.asarray((r[:, None] < r[None, :]).astype(np.float32), BF16)
    return pl.pallas_call(
        functools.partial(_rank_kernel, n_experts=n_experts, tm=tm),
        grid=(t // tm,),
        in_specs=[pl.BlockSpec((8, tm), lambda i: (0, i)), pl.BlockSpec((tm, tm), lambda i: (0, 0))],
        out_specs=[pl.BlockSpec((8, tm), lambda i: (0, i)), pl.BlockSpec((n_experts, 128), lambda i: (0, 0))],
        out_shape=[jax.ShapeDtypeStruct((8, t), jnp.int32), jax.ShapeDtypeStruct((n_experts, 128), F32)],
        scratch_shapes=[pltpu.VMEM((n_experts, 128), F32)],
        compiler_params=_cparams(("arbitrary",)),
        name="moe_slot_ranks",
    )(ei, su)


def _row_copy(src_ref, src_row, dst_ref, dst_row, sem):
    return pltpu.make_async_copy(src_ref.at[pl.ds(src_row, 1)], dst_ref.at[pl.ds(dst_row, 1)], sem)


def _dispatch_kernel(dest_ref, zs_ref, na_ref, h_ref, xs_ref, zero_ref, sem, zsem, *, tm, t, n_experts, tmx, n_blocks):
    base = pl.program_id(0) * tm

    @pl.when(pl.program_id(0) == 0)
    def _():
        zero_ref[...] = jnp.zeros_like(zero_ref)
        for e in range(n_experts):
            fill = pltpu.make_async_copy(zero_ref, xs_ref.at[pl.ds(pl.multiple_of(zs_ref[e], FILL_ALIGN),
                                                                  tmx + FILL_ALIGN)], zsem)
            fill.start()
            fill.wait()

        def tail(b, carry):
            fill = pltpu.make_async_copy(zero_ref.at[pl.ds(0, tmx)],
                                         xs_ref.at[pl.ds(pl.multiple_of(b * tmx, tmx), tmx)], zsem)
            fill.start()
            fill.wait()
            return carry

        lax.fori_loop(na_ref[0], n_blocks, tail, 0)

    def issue(r, carry):
        for kk in range(TOP_K):
            _row_copy(h_ref, r, xs_ref, dest_ref[kk * t + base + r], sem).start()
        return carry

    lax.fori_loop(0, tm, issue, 0, unroll=4)
    for kk in range(TOP_K):
        pltpu.make_async_copy(h_ref, xs_ref.at[pl.ds(0, tm)], sem).wait()


def _dispatch(dest, zero_start, n_active, h2, n_slots, tm, tmx):
    t, d = h2.shape
    n_experts = zero_start.shape[0]
    return pl.pallas_call(
        functools.partial(_dispatch_kernel, tm=tm, t=t, n_experts=n_experts, tmx=tmx, n_blocks=n_slots // tmx),
        grid_spec=pltpu.PrefetchScalarGridSpec(
            num_scalar_prefetch=3,
            grid=(t // tm,),
            in_specs=[pl.BlockSpec((tm, d), lambda i, dest, zs, na: (i, 0))],
            out_specs=pl.BlockSpec(memory_space=pl.ANY),
            scratch_shapes=[pltpu.VMEM((tmx + FILL_ALIGN, d), h2.dtype), pltpu.SemaphoreType.DMA(()),
                            pltpu.SemaphoreType.DMA(())]),
        out_shape=jax.ShapeDtypeStruct((n_slots, d), h2.dtype),
        compiler_params=_cparams(("arbitrary",)),
        name="moe_dispatch",
    )(dest, zero_start, n_active, h2)


def _expert_kernel(be_ref, na_ref, x_ref, w1_ref, w3_ref, w2_ref, y_ref):
    i = pl.program_id(0)

    @pl.when(i < na_ref[0])
    def _():
        xb = x_ref[...].astype(BF16)
        h1 = _dot(xb, w1_ref[0, 0])
        h3 = _dot(xb, w3_ref[0, 0])
        y_ref[...] = _dot((_silu(h1) * h3).astype(BF16), w2_ref[0, 0])

    @pl.when(i >= na_ref[0])
    def _():
        y_ref[...] = jnp.zeros_like(y_ref)


def _experts(block_expert, n_active, xs, w1, w3, w2, layer, tmx):
    n_slots, d = xs.shape
    f = w1.shape[3]
    return pl.pallas_call(
        _expert_kernel,
        grid_spec=pltpu.PrefetchScalarGridSpec(
            num_scalar_prefetch=2,
            grid=(n_slots // tmx,),
            in_specs=[pl.BlockSpec((tmx, d), lambda i, be, na: (jnp.minimum(i, na[0] - 1), 0)),
                      pl.BlockSpec((1, 1, d, f), lambda i, be, na: (layer, be[i], 0, 0)),
                      pl.BlockSpec((1, 1, d, f), lambda i, be, na: (layer, be[i], 0, 0)),
                      pl.BlockSpec((1, 1, f, d), lambda i, be, na: (layer, be[i], 0, 0))],
            out_specs=pl.BlockSpec((tmx, d), lambda i, be, na: (i, 0))),
        out_shape=jax.ShapeDtypeStruct((n_slots, d), F32),
        compiler_params=_cparams(("arbitrary",)),
        name="moe_experts",
    )(block_expert, n_active, xs, w1, w3, w2)


def _combine_kernel(dest_ref, ys_ref, ew_ref, x1_ref, m_ref, lng, lnb, mn_ref, x2_ref, *rest,
                    tm, t, alpha, block_of_step, emit_next):
    hn_ref = rest[0] if emit_next else None
    g_ref, sem = rest[-2:]
    i = pl.program_id(0)
    n = pl.num_programs(0)

    def issue(step, slot):
        base = block_of_step(step) * tm

        def body(r, carry):
            for kk in range(TOP_K):
                _row_copy(ys_ref, dest_ref[kk * t + base + r], g_ref.at[slot, kk], r, sem.at[slot]).start()
            return carry

        lax.fori_loop(0, tm, body, 0, unroll=4)

    @pl.when(i == 0)
    def _():
        issue(0, 0)

    @pl.when(i + 1 < n)
    def _():
        issue(i + 1, (i + 1) % 2)

    slot = i % 2
    for kk in range(TOP_K):
        pltpu.make_async_copy(ys_ref.at[pl.ds(0, tm)], g_ref.at[slot, kk], sem.at[slot]).wait()
    row = lax.broadcasted_iota(jnp.int32, (tm, tm), 0)
    col = lax.broadcasted_iota(jnp.int32, (tm, tm), 1)
    w_t = _dot_nt((row == col).astype(F32), ew_ref[...], HIGHEST)
    f = w_t[:, 0:1] * g_ref[slot, 0] + w_t[:, 1:2] * g_ref[slot, 1]
    x2 = _ln(alpha * x1_ref[...] + m_ref[0, 5:6, :] * f) * lng[...] + lnb[...]
    x2_ref[...] = x2
    if emit_next:
        hn_ref[...] = (_ln(x2) * (1.0 + mn_ref[0, 1:2, :]) + mn_ref[0, 0:1, :]).astype(hn_ref.dtype)


def _combine(dest, ys, ew, x1, mods, ln_g, ln_b, mods_next, alpha, tm, nb, nctx, n_batch, latent_only):
    t, d = x1.shape
    nlat = nb - nctx
    if latent_only:
        n_steps = n_batch * nlat
        block_of_step = lambda i: (i // nlat) * nb + nctx + i % nlat
    else:
        n_steps = t // tm
        block_of_step = lambda i: i
    mrow = lambda i, dest: (_mod_row(block_of_step(i), nb, nctx, n_batch), 0, 0)
    const = lambda i, dest: (0, 0)
    out_specs = [pl.BlockSpec((tm, d), lambda i, dest: (i, 0))]
    out_shape = [jax.ShapeDtypeStruct((n_steps * tm, d), F32)]
    if not latent_only:
        out_specs.append(pl.BlockSpec((tm, d), lambda i, dest: (i, 0)))
        out_shape.append(jax.ShapeDtypeStruct((t, d), BF16))
    return pl.pallas_call(
        functools.partial(_combine_kernel, tm=tm, t=t, alpha=alpha, block_of_step=block_of_step,
                          emit_next=not latent_only),
        grid_spec=pltpu.PrefetchScalarGridSpec(
            num_scalar_prefetch=1,
            grid=(n_steps,),
            in_specs=[pl.BlockSpec(memory_space=pl.ANY),
                      pl.BlockSpec((8, tm), lambda i, dest: (0, block_of_step(i))),
                      pl.BlockSpec((tm, d), lambda i, dest: (block_of_step(i), 0)),
                      pl.BlockSpec((1, 6, d), mrow),
                      pl.BlockSpec((1, d), const), pl.BlockSpec((1, d), const),
                      pl.BlockSpec((1, 6, d), mrow)],
            out_specs=out_specs,
            scratch_shapes=[pltpu.VMEM((2, TOP_K, tm, d), F32), pltpu.SemaphoreType.DMA((2,))]),
        out_shape=out_shape,
        compiler_params=_cparams(("arbitrary",)),
        name="moe_combine",
    )(dest, ys, ew, x1, mods, ln_g, ln_b, mods_next)


def _axial_tables(seq, ctx_len):
    rows = seq // GRID_W
    row = jnp.repeat(jnp.arange(rows, dtype=F32), GRID_W)
    col = jnp.tile(jnp.arange(GRID_W, dtype=F32), rows)
    n_freq = HEAD_DIM // 4
    inv_freq = ROPE_THETA ** (-jnp.arange(n_freq, dtype=F32) / n_freq)
    ang = jnp.concatenate([row[:, None] * inv_freq, col[:, None] * inv_freq], axis=-1)
    cos, sin = jnp.cos(ang), jnp.sin(ang)
    cosf = jnp.concatenate([cos, cos], axis=-1)
    sinf = jnp.concatenate([-sin, sin], axis=-1)
    cosf = jnp.concatenate([jnp.ones((ctx_len, HEAD_DIM), F32), cosf], axis=0)
    sinf = jnp.concatenate([jnp.zeros((ctx_len, HEAD_DIM), F32), sinf], axis=0)
    return cosf, sinf


def kernel(x, c, ctx, c_ctx, w_ada, b_ada, ln_g, ln_b, w_in, conv_a, a_log, dt_bias, norm_a, qk_norm_b, ws_c, bs_c, lb_d, norm_d, w_out, w_router, b_router, w1, w3, w2):
    n_batch, seq, d = x.shape
    ctx_len = ctx.shape[1]
    n_layers = w_in.shape[0]
    n_experts = w_router.shape[1]
    mixw = d // N_MIXERS
    n_heads = mixw // HEAD_DIM
    tb = ctx_len + seq
    t = n_batch * tb
    alpha = (2.0 * n_layers) ** 0.25
    tm = _pick(math.gcd(ctx_len, seq), (256, 128))
    nb, nctx = tb // tm, ctx_len // tm

    col = {"a_qkv": 0, "a_gate": 3 * mixw, "d_f": 4 * mixw, "b_q": 6 * mixw, "b_kv": 7 * mixw, "c_u": 8 * mixw,
           "c_v": 9 * mixw, "d_q": 10 * mixw, "d_i": 11 * mixw, "d_gate": 12 * mixw, "small": 13 * mixw}
    n_proj = 13 * mixw + 256
    src = np.cumsum([0, 3 * mixw, mixw, 2 * n_heads, 2 * n_heads, mixw, mixw, mixw, mixw, mixw, mixw, mixw, 2 * mixw])
    s_qkv, s_ga, s_beta, s_dec, s_bq, s_bkv, s_cu, s_cv, s_dq, s_di, s_dg, s_df, s_end = [int(v) for v in src]

    def permute_w_in(w):
        pad = jnp.zeros((d, n_proj - 13 * mixw - 4 * n_heads), w.dtype)
        return jnp.concatenate([w[:, s_qkv:s_beta], w[:, s_df:s_end], w[:, s_bq:s_df], w[:, s_beta:s_bq], pad],
                               axis=1).astype(BF16)

    cc = jnp.zeros((8, d), F32).at[:n_batch].set(c).at[n_batch].set(c_ctx)
    mods_all = _mod_vectors(cc, w_ada, b_ada)[:, :n_batch + 1].reshape(n_layers, n_batch + 1, 6, d)

    soft = jax.nn.softmax(lb_d.astype(F32), axis=0)
    lb_all = jnp.cumsum(soft, axis=0) - soft[0]
    cosf, sinf = _axial_tables(seq, ctx_len)
    gate_lanes = jnp.zeros((1, 128), F32)
    wr_f = w_router.astype(F32)
    wr_hi = wr_f.astype(BF16)
    wr_r = wr_f - wr_hi.astype(F32)
    wr_mid = wr_r.astype(BF16)
    wr_lo = (wr_r - wr_mid.astype(F32)).astype(BF16)
    w_router_p = jnp.zeros((d, 4 * 128), BF16)
    for blk, part in enumerate((wr_hi, wr_mid, wr_lo)):
        w_router_p = w_router_p.at[:, blk * 128:blk * 128 + n_experts].set(part)
    b_router_c = jnp.broadcast_to(b_router.astype(F32)[:, None], (n_experts, 128))

    tmx = 256
    n_assign = t * TOP_K
    n_slots = (-(-n_assign // tmx) + n_experts + 2) * tmx

    w1_b, w3_b, w2_b = w1.astype(BF16), w3.astype(BF16), w2.astype(BF16)
    xu = jnp.concatenate([ctx, x], axis=1).reshape(t, d)
    h = _ln_modulate(xu, mods_all[0], tm, nb, nctx, n_batch)
    for l in range(n_layers):
        mods = mods_all[l]
        p = _matmul(h, permute_w_in(w_in[l]))
        neg_a = gate_lanes.at[0, 2 * n_heads:4 * n_heads].set(-jnp.exp(a_log[l].astype(F32)).reshape(-1))
        dtb = gate_lanes.at[0, 2 * n_heads:4 * n_heads].set(dt_bias[l].astype(F32).reshape(-1))
        prep_a = _a_prep(p, conv_a[l], neg_a, dtb, tm, nb, nctx, mixw, col["small"])
        oaf, oar = _a_scan(prep_a, n_batch, tb, ctx_len, mixw)
        qb, kb, vb = _b_prep(p, qk_norm_b[l], cosf, sinf, tm, nb, mixw, col["b_q"], col["b_kv"])
        yb = _attention(qb, kb, vb, n_batch, tb, ctx_len, mixw)
        bs_t = jnp.zeros((CHUNK_C, 128), F32).at[:, :bs_c.shape[1]].set(bs_c[l].T)
        yc = _mixer_c(p, ws_c[l].astype(BF16), bs_t, tm, mixw, col["c_u"], col["c_v"])
        odf, odr = _d_scan(p, lb_all[l][None, :], n_batch, tb, ctx_len, mixw, col["d_q"], col["d_i"], col["d_f"])
        x1, h2, ei, ew = _outproj(oaf, oar, odf, odr, p, yb, yc, norm_a[l][None, :], norm_d[l][None, :],
                                  w_out[l].astype(BF16), xu, mods, ln_g[l, 0][None, :], ln_b[l, 0][None, :],
                                  w_router_p, b_router_c, alpha, tm, nb, nctx, n_batch, mixw,
                                  col["a_gate"], col["d_gate"])
        rank, counts = _slot_ranks(ei, n_experts, tm)
        cnt = counts[:, 0].astype(jnp.int32)
        padded = (cnt + tmx - 1) // tmx * tmx
        pad_end = jnp.cumsum(padded)
        pad_start = pad_end - padded
        eids = jnp.arange(n_experts, dtype=jnp.int32)[:, None, None]
        dest = (rank[:TOP_K] + jnp.sum(jnp.where(ei[None, :TOP_K] == eids, pad_start[:, None, None], 0),
                                       axis=0)).reshape(-1)
        n_blocks = n_slots // tmx
        block_start = jnp.arange(n_blocks, dtype=jnp.int32) * tmx
        block_expert = jnp.minimum(jnp.sum((pad_end[None, :] <= block_start[:, None]).astype(jnp.int32), axis=1),
                                   n_experts - 1)
        n_active = (pad_end[-1:] // tmx).astype(jnp.int32)
        fill_start = (pad_start + cnt) // FILL_ALIGN * FILL_ALIGN
        xs = _dispatch(dest, fill_start, n_active, h2, n_slots, tm, tmx)
        ys = _experts(block_expert, n_active, xs, w1_b, w3_b, w2_b, l, tmx)
        last = l == n_layers - 1
        res = _combine(dest, ys, ew, x1, mods, ln_g[l, 1][None, :], ln_b[l, 1][None, :],
                       mods_all[min(l + 1, n_layers - 1)], alpha, tm, nb, nctx, n_batch, latent_only=last)
        if last:
            return res[0].reshape(n_batch, seq, d)
        xu, h = res
```

```python
import functools
import math

import numpy as np
import jax
import jax.numpy as jnp
from jax import lax
from jax.experimental import pallas as pl
from jax.experimental.pallas import tpu as pltpu

F32 = jnp.float32
BF16 = jnp.bfloat16
HIGHEST = lax.Precision.HIGHEST

HEAD_DIM = 128
N_MIXERS = 4
CONV_W = 5
CHUNK = 64
STEP_ROWS = 2 * CHUNK
SOLVE_ROWS = 2 * CHUNK
CHUNK_C = 128
GRID_W = 64
ROPE_THETA = 10000.0
N_EXPERT_GROUPS = 4
TOP_K = 2
EPS = 1e-6
LOG2_E = math.log2(math.e)
GATE_CLIP = 30.0
N_LEVELS = 6
HALO = 8
FILL_ALIGN = 8
V7X_VMEM_LIMIT = 56 * 1024 * 1024


def _pick(n, cands):
    for c in cands:
        if n % c == 0:
            return c
    raise ValueError(f"no tile in {cands} divides {n}")


def _cparams(sem, vmem=None):
    return pltpu.CompilerParams(dimension_semantics=sem, vmem_limit_bytes=vmem or V7X_VMEM_LIMIT)


def _dot(a, b, precision=None):
    return jnp.dot(a, b, preferred_element_type=F32, precision=precision)


def _dot_nt(a, b, precision=None):
    return lax.dot_general(a, b, (((1,), (1,)), ((), ())), preferred_element_type=F32, precision=precision)


def _dot_tn(a, b, precision=None):
    return lax.dot_general(a, b, (((0,), (0,)), ((), ())), preferred_element_type=F32, precision=precision)


def _sigmoid(x):
    return 1.0 / (1.0 + jnp.exp(-x))


def _silu(x):
    return x * _sigmoid(x)


def _gelu_tanh(x):
    return 0.5 * x * (1.0 + jnp.tanh(math.sqrt(2.0 / math.pi) * (x + 0.044715 * (x * x * x))))


def _ln(x):
    mu = jnp.mean(x, axis=-1, keepdims=True)
    xc = x - mu
    return xc * lax.rsqrt(jnp.mean(xc * xc, axis=-1, keepdims=True) + EPS)


def _mod_row(i, nb, nctx, n_batch):
    return jnp.where(i % nb < nctx, n_batch, i // nb)


def _mod_kernel(c_ref, w_ref, b_ref, o_ref):
    c = c_ref[...]
    o_ref[0] = _dot(_silu(c), w_ref[0], HIGHEST) + b_ref[0]


def _mod_vectors(cc, w_ada, b_ada):
    n_layers, d, n6 = w_ada.shape
    tn = _pick(n6, (1024, 512, 128))
    return pl.pallas_call(
        _mod_kernel,
        grid=(n_layers, n6 // tn),
        in_specs=[pl.BlockSpec((8, d), lambda l, j: (0, 0)),
                  pl.BlockSpec((1, d, tn), lambda l, j: (l, 0, j)),
                  pl.BlockSpec((1, 1, tn), lambda l, j: (l, 0, j))],
        out_specs=pl.BlockSpec((1, 8, tn), lambda l, j: (l, 0, j)),
        out_shape=jax.ShapeDtypeStruct((n_layers, 8, n6), F32),
        compiler_params=_cparams(("parallel", "parallel")),
        name="mod_vectors",
    )(cc, w_ada, b_ada.reshape(n_layers, 1, n6))


def _lnmod_kernel(x_ref, m_ref, h_ref):
    y = _ln(x_ref[...])
    h_ref[...] = (y * (1.0 + m_ref[0, 1:2, :]) + m_ref[0, 0:1, :]).astype(h_ref.dtype)


def _ln_modulate(x, mods, tm, nb, nctx, n_batch):
    t, d = x.shape
    return pl.pallas_call(
        _lnmod_kernel,
        grid=(t // tm,),
        in_specs=[pl.BlockSpec((tm, d), lambda i: (i, 0)),
                  pl.BlockSpec((1, 6, d), lambda i: (_mod_row(i, nb, nctx, n_batch), 0, 0))],
        out_specs=pl.BlockSpec((tm, d), lambda i: (i, 0)),
        out_shape=jax.ShapeDtypeStruct((t, d), BF16),
        compiler_params=_cparams(("parallel",)),
        name="ln_modulate",
    )(x, mods)


def _matmul_kernel(a_ref, w_ref, o_ref):
    o_ref[...] = _dot(a_ref[...], w_ref[...]).astype(o_ref.dtype)


def _matmul(a, w, out_dtype=F32):
    m, k = a.shape
    n = w.shape[1]
    tm = _pick(m, (1536, 1024, 768, 512, 256, 128))
    tn = _pick(n, (1152, 1024, 768, 512, 256, 128))
    return pl.pallas_call(
        _matmul_kernel,
        grid=(m // tm, n // tn),
        in_specs=[pl.BlockSpec((tm, k), lambda i, j: (i, 0)),
                  pl.BlockSpec((k, tn), lambda i, j: (0, j))],
        out_specs=pl.BlockSpec((tm, tn), lambda i, j: (i, j)),
        out_shape=jax.ShapeDtypeStruct((m, n), out_dtype),
        compiler_params=_cparams(("parallel", "parallel")),
        name="in_proj",
    )(a, w)


def _unit_tri_inverses(lows, eye, pair_refs):
    ts = [eye - low * pair_ref[0] for low, pair_ref in zip(lows, pair_refs)]
    for lvl in range(1, N_LEVELS):
        tbs = [t.astype(BF16) for t in ts]
        xs = [_dot(tb16, (low * pair_ref[lvl]).astype(BF16)) for tb16, low, pair_ref in zip(tbs, lows, pair_refs)]
        ts = [t - _dot(x.astype(BF16), tb16) for t, x, tb16 in zip(ts, xs, tbs)]
    return ts


def _a_prep_kernel(x_ref, xp_ref, xn_ref, sm_ref, cw_ref, na_ref, dtb_ref, trf_ref, trr_ref, pairf_ref, pairr_ref,
                   uf_ref, wqf_ref, kdf_ref, atf_ref, ur_ref, wqr_ref, kdr_ref, atr_ref, eg_ref, ext_ref,
                   *, tm, nb, nctx, mixw):
    i = pl.program_id(0)
    j = i % nb
    first = jnp.logical_or(j == 0, j == nctx)
    last = jnp.logical_or(j == nctx - 1, j == nb - 1)
    ext_ref[HALO:HALO + tm, :] = x_ref[...]
    ext_ref[0:HALO, :] = jnp.where(first, 0.0, xp_ref[...])
    ext_ref[HALO + tm:2 * HALO + tm, :] = jnp.where(last, 0.0, xn_ref[...])
    off = HALO - CONV_W // 2
    acc = cw_ref[0:1, :] * ext_ref[off:off + tm, :]
    for t in range(1, CONV_W):
        acc = acc + cw_ref[t:t + 1, :] * ext_ref[off + t:off + t + tm, :]
    y = _silu(acc)
    n_heads = mixw // HEAD_DIM
    sm = sm_ref[...]
    lane = lax.broadcasted_iota(jnp.int32, sm.shape, 1)
    zz = sm + dtb_ref[...]
    softplus = jnp.maximum(zz, 0.0) + jnp.log(1.0 + jnp.exp(-jnp.abs(zz)))
    g = jnp.where(jnp.logical_and(lane >= 2 * n_heads, lane < 4 * n_heads), na_ref[...] * softplus, 0.0)
    beta_all = _sigmoid(sm)
    tri_f = trf_ref[...]
    tri_r = trr_ref[...]
    cs_f = _dot(tri_f, g, HIGHEST)
    cs_r = _dot(tri_r, g, HIGHEST)
    gcum = jnp.where(lane < 3 * n_heads, cs_f, cs_r)
    gtot = _dot(jnp.maximum(tri_f, tri_r), g, HIGHEST)
    eg_ref[...] = jnp.exp(gtot)
    gcum_t = gcum.T
    sr = SOLVE_ROWS
    row = lax.broadcasted_iota(jnp.int32, (sr, sr), 0)
    col = lax.broadcasted_iota(jnp.int32, (sr, sr), 1)
    eye = (row == col).astype(F32)
    incls = (tri_f[0:sr, 0:sr] > 0.5, tri_r[0:sr, 0:sr] > 0.5)
    pairs = ([pairf_ref[lvl, 0:sr, 0:sr] for lvl in range(N_LEVELS)],
             [pairr_ref[lvl, 0:sr, 0:sr] for lvl in range(N_LEVELS)])
    outs = ((uf_ref, wqf_ref, kdf_ref, atf_ref), (ur_ref, wqr_ref, kdr_ref, atr_ref))
    items = []
    for sb in range(tm // sr):
        rs = sb * sr
        for h in range(n_heads):
            c0 = h * HEAD_DIM
            qh = y[rs:rs + sr, c0:c0 + HEAD_DIM]
            kh = y[rs:rs + sr, mixw + c0:mixw + c0 + HEAD_DIM]
            vh = y[rs:rs + sr, 2 * mixw + c0:2 * mixw + c0 + HEAD_DIM]
            qh = qh * (lax.rsqrt(jnp.sum(qh * qh, axis=-1, keepdims=True) + EPS) * HEAD_DIM ** -0.5)
            kh = kh * lax.rsqrt(jnp.sum(kh * kh, axis=-1, keepdims=True) + EPS)
            kb16 = kh.astype(BF16)
            qk = _dot_nt(qh.astype(BF16), kb16)
            for d in range(2):
                lb = d * n_heads + h
                lg = (2 + d) * n_heads + h
                beta = beta_all[rs:rs + sr, lb:lb + 1]
                gc = gcum[rs:rs + sr, lg:lg + 1]
                incl = incls[d]
                decay = jnp.where(incl, jnp.exp(jnp.where(incl, gc - gcum_t[lg:lg + 1, rs:rs + sr], 0.0)), 0.0)
                kbeta = kh * beta
                low = jnp.where(row == col, 0.0, _dot_nt(kbeta.astype(BF16), kb16) * decay)
                rhs = jnp.concatenate([vh * beta, kbeta * jnp.exp(gc)], axis=1).astype(BF16)
                items.append(dict(h=h, d=d, rs=rs, low=low, rhs=rhs, attn=(qk * decay).astype(BF16),
                                  qd=(qh * jnp.exp(gc)).astype(BF16),
                                  kd=(kh * jnp.exp(gtot[rs:rs + sr, lg:lg + 1] - gc)).astype(BF16)))
    tinvs = _unit_tri_inverses([it["low"] for it in items], eye, [pairs[it["d"]] for it in items])
    for it, tinv in zip(items, tinvs):
        u_ref, wq_ref, kd_ref, at_ref = outs[it["d"]]
        h, rs = it["h"], it["rs"]
        c0 = h * HEAD_DIM
        sol = _dot(tinv.astype(BF16), it["rhs"])
        u_ref[rs:rs + sr, c0:c0 + HEAD_DIM] = sol[:, :HEAD_DIM]
        w = sol[:, HEAD_DIM:].astype(BF16)
        kd_ref[rs:rs + sr, c0:c0 + HEAD_DIM] = it["kd"]
        for ci in range(sr // CHUNK):
            r0 = ci * CHUNK
            g0 = rs + r0
            wq_ref[2 * g0:2 * g0 + CHUNK, c0:c0 + HEAD_DIM] = w[r0:r0 + CHUNK]
            wq_ref[2 * g0 + CHUNK:2 * g0 + 2 * CHUNK, c0:c0 + HEAD_DIM] = it["qd"][r0:r0 + CHUNK]
            at_ref[g0:g0 + CHUNK, h * CHUNK:(h + 1) * CHUNK] = it["attn"][r0:r0 + CHUNK, r0:r0 + CHUNK]


def _a_prep(p, conv_w, neg_a, dtb, tm, nb, nctx, mixw, col_small):
    t = p.shape[0]
    c3 = 3 * mixw
    n_heads = mixw // HEAD_DIM
    r = np.arange(tm)
    same = (r[:, None] // CHUNK) == (r[None, :] // CHUNK)
    tri_f = jnp.asarray((same & (r[None, :] <= r[:, None])).astype(np.float32))
    tri_r = jnp.asarray((same & (r[None, :] >= r[:, None])).astype(np.float32))
    pair_f = []
    for lvl in range(N_LEVELS):
        s_blk = 1 << lvl
        joined = ((r[:, None] // (2 * s_blk)) == (r[None, :] // (2 * s_blk))) \
            & ((r[:, None] % (2 * s_blk)) >= s_blk) & ((r[None, :] % (2 * s_blk)) < s_blk)
        pair_f.append(joined.astype(np.float32))
    pair_f = np.stack(pair_f)
    pair_r = jnp.asarray(np.transpose(pair_f, (0, 2, 1)))
    pair_f = jnp.asarray(pair_f)
    hb = tm // HALO
    last_hb = t // HALO - 1
    kern = functools.partial(_a_prep_kernel, tm=tm, nb=nb, nctx=nctx, mixw=mixw)
    per_dir_specs = [pl.BlockSpec((tm, mixw), lambda i: (i, 0)),
                     pl.BlockSpec((2 * tm, mixw), lambda i: (i, 0)),
                     pl.BlockSpec((tm, mixw), lambda i: (i, 0)),
                     pl.BlockSpec((tm, n_heads * CHUNK), lambda i: (i, 0))]
    per_dir_shapes = [jax.ShapeDtypeStruct((t, mixw), F32), jax.ShapeDtypeStruct((2 * t, mixw), BF16),
                      jax.ShapeDtypeStruct((t, mixw), BF16), jax.ShapeDtypeStruct((t, n_heads * CHUNK), BF16)]
    return pl.pallas_call(
        kern,
        grid=(t // tm,),
        in_specs=[pl.BlockSpec((tm, c3), lambda i: (i, 0)),
                  pl.BlockSpec((HALO, c3), lambda i: (jnp.maximum(i * hb - 1, 0), 0)),
                  pl.BlockSpec((HALO, c3), lambda i: (jnp.minimum((i + 1) * hb, last_hb), 0)),
                  pl.BlockSpec((tm, 128), lambda i: (i, col_small // 128)),
                  pl.BlockSpec((CONV_W, c3), lambda i: (0, 0)),
                  pl.BlockSpec((1, 128), lambda i: (0, 0)),
                  pl.BlockSpec((1, 128), lambda i: (0, 0)),
                  pl.BlockSpec((tm, tm), lambda i: (0, 0)),
                  pl.BlockSpec((tm, tm), lambda i: (0, 0)),
                  pl.BlockSpec((N_LEVELS, tm, tm), lambda i: (0, 0, 0)),
                  pl.BlockSpec((N_LEVELS, tm, tm), lambda i: (0, 0, 0))],
        out_specs=per_dir_specs * 2 + [pl.BlockSpec((tm, 128), lambda i: (i, 0))],
        out_shape=per_dir_shapes * 2 + [jax.ShapeDtypeStruct((t, 128), F32)],
        scratch_shapes=[pltpu.VMEM((tm + 2 * HALO, c3), F32)],
        compiler_params=_cparams(("parallel",)),
        name="deltanet_prep",
    )(p, p, p, p, conv_w, neg_a, dtb, tri_f, tri_r, pair_f, pair_r)


def _a_scan_kernel(uf, wqf, kdf, atf, egf, ur, wqr, kdr, atr, egr, of_ref, or_ref, s_ref, sbd_ref, *, n_heads):
    n = pl.program_id(1)

    @pl.when(n == 0)
    def _():
        s_ref[...] = jnp.zeros_like(s_ref)
        sbd_ref[...] = jnp.zeros_like(sbd_ref)

    mixw = n_heads * HEAD_DIM
    lane_head = lax.broadcasted_iota(jnp.int32, (CHUNK, mixw), 1) // HEAD_DIM
    dirs = ((uf, wqf, kdf, atf, egf, of_ref), (ur, wqr, kdr, atr, egr, or_ref))
    for pos in range(2):
        r0s = [(pos if d == 0 else 1 - pos) * CHUNK for d in range(2)]
        rs = [_dot(dirs[d][1][2 * r0s[d]:2 * r0s[d] + 2 * CHUNK, :], sbd_ref[d]) for d in range(2)]
        v_news = [dirs[d][0][r0s[d]:r0s[d] + CHUNK, :] - rs[d][:CHUNK] for d in range(2)]
        v_stacks = [jnp.concatenate([jnp.where(lane_head == h, v_news[d], 0.0) for h in range(n_heads)],
                                    axis=0).astype(BF16) for d in range(2)]
        for d in range(2):
            dirs[d][5][r0s[d]:r0s[d] + CHUNK, :] = (rs[d][CHUNK:]
                                                    + _dot(dirs[d][3][r0s[d]:r0s[d] + CHUNK, :], v_stacks[d]))
        vnbs = [v.astype(BF16) for v in v_news]
        for h in range(n_heads):
            c0 = h * HEAD_DIM
            for d in range(2):
                lg = (2 + d) * n_heads + h
                r0 = r0s[d]
                s_new = (s_ref[d, h] * dirs[d][4][r0:r0 + 1, lg:lg + 1]
                         + _dot_tn(dirs[d][2][r0:r0 + CHUNK, c0:c0 + HEAD_DIM], vnbs[d][:, c0:c0 + HEAD_DIM]))
                s_ref[d, h] = s_new
                sbd_ref[d, c0:c0 + HEAD_DIM, c0:c0 + HEAD_DIM] = s_new.astype(BF16)


def _seq_block_maps(nsb, nctx_sb):
    def fwd(b, n):
        return b * nsb + n

    def rev(b, n):
        return b * nsb + jnp.where(n < nctx_sb, nctx_sb - 1 - n, nsb - 1 - (n - nctx_sb))

    return fwd, rev


def _a_scan(prep, n_batch, tb, ctx_len, mixw):
    uf, wqf, kdf, atf, ur, wqr, kdr, atr, eg = prep
    t = uf.shape[0]
    n_heads = mixw // HEAD_DIM
    nsb = tb // STEP_ROWS
    fwd, rev = _seq_block_maps(nsb, ctx_len // STEP_ROWS)
    in_specs = []
    for m in (fwd, rev):
        in_specs += [pl.BlockSpec((STEP_ROWS, mixw), lambda b, n, m=m: (m(b, n), 0)),
                     pl.BlockSpec((2 * STEP_ROWS, mixw), lambda b, n, m=m: (m(b, n), 0)),
                     pl.BlockSpec((STEP_ROWS, mixw), lambda b, n, m=m: (m(b, n), 0)),
                     pl.BlockSpec((STEP_ROWS, n_heads * CHUNK), lambda b, n, m=m: (m(b, n), 0)),
                     pl.BlockSpec((STEP_ROWS, 128), lambda b, n, m=m: (m(b, n), 0))]
    return pl.pallas_call(
        functools.partial(_a_scan_kernel, n_heads=n_heads),
        grid=(n_batch, nsb),
        in_specs=in_specs,
        out_specs=[pl.BlockSpec((STEP_ROWS, mixw), lambda b, n: (fwd(b, n), 0)),
                   pl.BlockSpec((STEP_ROWS, mixw), lambda b, n: (rev(b, n), 0))],
        out_shape=[jax.ShapeDtypeStruct((t, mixw), F32)] * 2,
        scratch_shapes=[pltpu.VMEM((2, n_heads, HEAD_DIM, HEAD_DIM), F32),
                        pltpu.VMEM((2, mixw, mixw), BF16)],
        compiler_params=_cparams(("arbitrary", "arbitrary")),
        name="deltanet_scan",
    )(uf, wqf, kdf, atf, eg, ur, wqr, kdr, atr, eg)


def _b_prep_kernel(q_ref, kv_ref, gain_ref, cos_ref, sin_ref, qo_ref, ko_ref, vo_ref, *, mixw):
    n_heads = mixw // HEAD_DIM
    n_kv = n_heads // 2
    cosf = cos_ref[...]
    sinf = sin_ref[...]

    def norm_rope(xh, gain, scale):
        yh = xh * lax.rsqrt(jnp.mean(xh * xh, axis=-1, keepdims=True) + EPS) * gain
        return (yh * cosf + pltpu.roll(yh, HEAD_DIM // 2, 1) * sinf) * scale

    for h in range(n_heads):
        c0 = h * HEAD_DIM
        qo_ref[:, c0:c0 + HEAD_DIM] = norm_rope(q_ref[:, c0:c0 + HEAD_DIM], gain_ref[0:1, :],
                                                HEAD_DIM ** -0.5 * LOG2_E).astype(qo_ref.dtype)
    for h in range(n_kv):
        c0 = h * HEAD_DIM
        ko_ref[:, c0:c0 + HEAD_DIM] = norm_rope(kv_ref[:, c0:c0 + HEAD_DIM], gain_ref[1:2, :], 1.0).astype(ko_ref.dtype)
    ones = jnp.ones((q_ref.shape[0], HEAD_DIM), vo_ref.dtype)
    for h in range(n_kv):
        c0 = (n_kv + h) * HEAD_DIM
        vo_ref[:, 2 * h * HEAD_DIM:(2 * h + 1) * HEAD_DIM] = kv_ref[:, c0:c0 + HEAD_DIM].astype(vo_ref.dtype)
        vo_ref[:, (2 * h + 1) * HEAD_DIM:(2 * h + 2) * HEAD_DIM] = ones


def _b_prep(p, qk_gain, cosf, sinf, tm, nb, mixw, col_q, col_kv):
    t = p.shape[0]
    kvw = mixw // 2
    return pl.pallas_call(
        functools.partial(_b_prep_kernel, mixw=mixw),
        grid=(t // tm,),
        in_specs=[pl.BlockSpec((tm, mixw), lambda i: (i, col_q // mixw)),
                  pl.BlockSpec((tm, mixw), lambda i: (i, col_kv // mixw)),
                  pl.BlockSpec((2, HEAD_DIM), lambda i: (0, 0)),
                  pl.BlockSpec((tm, HEAD_DIM), lambda i: (i % nb, 0)),
                  pl.BlockSpec((tm, HEAD_DIM), lambda i: (i % nb, 0))],
        out_specs=[pl.BlockSpec((tm, mixw), lambda i: (i, 0)),
                   pl.BlockSpec((tm, kvw), lambda i: (i, 0)),
                   pl.BlockSpec((tm, 2 * kvw), lambda i: (i, 0))],
        out_shape=[jax.ShapeDtypeStruct((t, mixw), BF16),
                   jax.ShapeDtypeStruct((t, kvw), BF16),
                   jax.ShapeDtypeStruct((t, 2 * kvw), BF16)],
        compiler_params=_cparams(("parallel",)),
        name="gqa_prep",
    )(p, p, qk_gain, cosf, sinf)


def _attn_kernel(q_ref, k_ref, v_ref, o_ref, m_ref, acc_ref, sa_ref, sb_ref, *, tq, tk, nk, nctx_q, ctx_len):
    qi = pl.program_id(2)
    qb = q_ref[...]
    q2 = jnp.concatenate([qb[:, :HEAD_DIM], qb[:, HEAD_DIM:]], axis=0)
    m_ref[...] = jnp.full_like(m_ref, -1e30)
    acc_ref[...] = jnp.zeros_like(acc_ref)

    def scores(i, s_ref):
        s_ref[...] = _dot_nt(q2, k_ref[i * tk:(i + 1) * tk, :])

    def absorb(s_of, vc):
        for g in range(2):
            sl = slice(g * tq, (g + 1) * tq)
            s = s_of(sl)
            m_prev = m_ref[sl, :]
            m_new = jnp.maximum(m_prev, jnp.max(s, axis=-1, keepdims=True))
            p = jnp.exp2(s - m_new)
            acc_ref[sl, :] = jnp.exp2(m_prev - m_new) * acc_ref[sl, :] + _dot(p.astype(BF16), vc)
            m_ref[sl, :] = m_new

    def absorb_chunk(i, s_ref):
        absorb(lambda sl: s_ref[sl, :], v_ref[i * tk:(i + 1) * tk, :])

    @pl.when(qi >= nctx_q)
    def _():
        bufs = (sa_ref, sb_ref)
        scores(0, bufs[0])
        for i in range(nk):
            if i + 1 < nk:
                scores(i + 1, bufs[(i + 1) % 2])
            absorb_chunk(i, bufs[i % 2])

    @pl.when(qi < nctx_q)
    def _():
        s_ctx = _dot_nt(q2, k_ref[0:ctx_len, :])
        absorb(lambda sl: s_ctx[sl, :], v_ref[0:ctx_len, :])

    acc = acc_ref[...]
    o = acc[:, :HEAD_DIM] / acc[:, HEAD_DIM:HEAD_DIM + 1]
    o_ref[...] = jnp.concatenate([o[:tq], o[tq:]], axis=1).astype(o_ref.dtype)


def _attention(q, k, v, n_batch, tb, ctx_len, mixw):
    t = q.shape[0]
    n_kv = mixw // HEAD_DIM // 2
    tq = _pick(math.gcd(ctx_len, tb), (256, 128))
    tk = _pick(tb, (1408, 768, 1024, 512, 640, 384, 256, 128))
    nq, nk = tb // tq, tb // tk
    return pl.pallas_call(
        functools.partial(_attn_kernel, tq=tq, tk=tk, nk=nk, nctx_q=ctx_len // tq, ctx_len=ctx_len),
        grid=(n_batch, n_kv, nq),
        in_specs=[pl.BlockSpec((tq, 2 * HEAD_DIM), lambda b, j, qi: (b * nq + qi, j)),
                  pl.BlockSpec((tb, HEAD_DIM), lambda b, j, qi: (b, j)),
                  pl.BlockSpec((tb, 2 * HEAD_DIM), lambda b, j, qi: (b, j))],
        out_specs=pl.BlockSpec((tq, 2 * HEAD_DIM), lambda b, j, qi: (b * nq + qi, j)),
        out_shape=jax.ShapeDtypeStruct((t, mixw), BF16),
        scratch_shapes=[pltpu.VMEM((2 * tq, 1), F32), pltpu.VMEM((2 * tq, 2 * HEAD_DIM), F32),
                        pltpu.VMEM((2 * tq, tk), F32), pltpu.VMEM((2 * tq, tk), F32)],
        compiler_params=_cparams(("parallel", "parallel", "parallel")),
        name="gqa_attention",
    )(q, k, v)


def _c_kernel(u_ref, v_ref, ws_ref, bs_ref, o_ref, *, tm, mixw):
    n_groups = mixw // HEAD_DIM
    for ci in range(tm // CHUNK_C):
        r0 = ci * CHUNK_C
        u = _gelu_tanh(u_ref[r0:r0 + CHUNK_C, :])
        vn = _ln(_gelu_tanh(v_ref[r0:r0 + CHUNK_C, :])).astype(BF16)
        for g in range(n_groups):
            c0 = g * HEAD_DIM
            vm = _dot(ws_ref[g], vn[:, c0:c0 + HEAD_DIM]) + bs_ref[:, g:g + 1]
            o_ref[r0:r0 + CHUNK_C, c0:c0 + HEAD_DIM] = (u[:, c0:c0 + HEAD_DIM] * vm).astype(o_ref.dtype)


def _mixer_c(p, ws, bs_t, tm, mixw, col_u, col_v):
    t = p.shape[0]
    n_groups = mixw // HEAD_DIM
    return pl.pallas_call(
        functools.partial(_c_kernel, tm=tm, mixw=mixw),
        grid=(t // tm,),
        in_specs=[pl.BlockSpec((tm, mixw), lambda i: (i, col_u // mixw)),
                  pl.BlockSpec((tm, mixw), lambda i: (i, col_v // mixw)),
                  pl.BlockSpec((n_groups, CHUNK_C, CHUNK_C), lambda i: (0, 0, 0)),
                  pl.BlockSpec((CHUNK_C, 128), lambda i: (0, 0))],
        out_specs=pl.BlockSpec((tm, mixw), lambda i: (i, 0)),
        out_shape=jax.ShapeDtypeStruct((t, mixw), BF16),
        compiler_params=_cparams(("parallel",)),
        name="gmlp",
    )(p, p, ws, bs_t)


def _hgrn_tables():
    tau = np.arange(CHUNK)
    sums_t = [tau[None, :] <= tau[:, None], tau[None, :] > tau[:, None]]
    pair_t = []
    for lvl in range(N_LEVELS):
        m = CHUNK >> (lvl + 1)
        blk = tau // (2 * m)
        upper = (tau % (2 * m)) >= m
        ref = blk * 2 * m + m - 1
        r = tau[None, :]
        w_up = upper[:, None] & (r > ref[:, None]) & (r <= tau[:, None])
        w_lo = (~upper)[:, None] & (r > tau[:, None]) & (r <= ref[:, None])
        sums_t.append(w_up | w_lo)
        pair_t.append((blk[:, None] == blk[None, :]) & upper[:, None] & (~upper)[None, :])
    sums_t = np.concatenate(sums_t, axis=0).astype(np.float32)
    pair_t = np.stack(pair_t).astype(np.float32)
    flip = tau[::-1]
    sums, pair = [], []
    for d in range(2):
        if d == 0:
            s_d, p_d = sums_t, pair_t
        else:
            s_d = sums_t.reshape(-1, CHUNK, CHUNK)[:, flip][:, :, flip].reshape(-1, CHUNK)
            p_d = pair_t[:, flip][:, :, flip]
        sums.append(np.concatenate([s_d, s_d, s_d], axis=1))
        pair.append(p_d)
    return np.stack(sums), np.stack(pair)


def _d_scan_kernel(qf, vf, ff, qr, vr, fr, lb_ref, sums_ref, pair_ref, of_ref, or_ref, st_ref, *, n_heads):
    n = pl.program_id(1)

    @pl.when(n == 0)
    def _():
        st_ref[...] = jnp.zeros_like(st_ref)

    c = CHUNK
    row = lax.broadcasted_iota(jnp.int32, (c, c), 0)
    col = lax.broadcasted_iota(jnp.int32, (c, c), 1)
    eye = (row == col).astype(F32)
    inst = []
    for d, (q_ref, v_ref, f_ref, o_ref) in enumerate(((qf, vf, ff, of_ref), (qr, vr, fr, or_ref))):
        for pos, ci in enumerate((0, 1) if d == 0 else (1, 0)):
            for h in range(n_heads):
                inst.append((d, pos, ci * c, h * HEAD_DIM, h, q_ref, v_ref, f_ref, o_ref))
    qs = [q_ref[r0:r0 + c, c0:c0 + HEAD_DIM] for (d, pos, r0, c0, h, q_ref, v_ref, f_ref, o_ref) in inst]
    vs = [v_ref[r0:r0 + c, c0:c0 + HEAD_DIM].astype(BF16) for (d, pos, r0, c0, h, q_ref, v_ref, f_ref, o_ref) in inst]
    es = [jnp.exp(-jnp.clip(f_ref[r0:r0 + c, c0:c0 + HEAD_DIM], -GATE_CLIP, GATE_CLIP))
          for (d, pos, r0, c0, h, q_ref, v_ref, f_ref, o_ref) in inst]
    sgs = [1.0 / (1.0 + e) for e in es]
    lbs = [lb_ref[0:1, it[3]:it[3] + HEAD_DIM] for it in inst]
    logfs = [jnp.log(lb + (1.0 - lb) * sg) for lb, sg in zip(lbs, sgs)]
    ks = [(1.0 - lb) * (e * sg) for lb, e, sg in zip(lbs, es, sgs)]
    his = [x.astype(BF16) for x in logfs]
    r1s = [x - hi.astype(F32) for x, hi in zip(logfs, his)]
    mids = [x.astype(BF16) for x in r1s]
    los = [(x - mid.astype(F32)).astype(BF16) for x, mid in zip(r1s, mids)]
    cats = [jnp.concatenate([hi, mid, lo], axis=0) for hi, mid, lo in zip(his, mids, los)]
    xs = []
    for i0 in range(0, len(inst), 2):
        x2 = _dot(sums_ref[inst[i0][0]], jnp.concatenate([cats[i0], cats[i0 + 1]], axis=1))
        xs += [x2[:, :HEAD_DIM], x2[:, HEAD_DIM:]]
    kbs = [k.astype(BF16) for k in ks]
    attns = [eye * _dot_nt(q.astype(BF16), kb) for q, kb in zip(qs, kbs)]
    for lvl in range(N_LEVELS):
        zls = [jnp.exp(x[(2 + lvl) * c:(3 + lvl) * c]) for x in xs]
        attns = [a + pair_ref[it[0], lvl] * _dot_nt((q * zl).astype(BF16), (k * zl).astype(BF16))
                 for a, it, q, k, zl in zip(attns, inst, qs, ks, zls)]
    qds = [(q * jnp.exp(x[0:c])).astype(BF16) for q, x in zip(qs, xs)]
    kds = [(k * jnp.exp(x[c:2 * c])).astype(BF16) for k, x in zip(ks, xs)]
    intra = [_dot(a.astype(BF16), v) for a, v in zip(attns, vs)]
    kvs = [_dot_tn(v, kd) for v, kd in zip(vs, kds)]
    states = [[st_ref[d, h] for h in range(n_heads)] for d in range(2)]
    for pos in range(2):
        for i, (d, p_i, r0, c0, h, q_ref, v_ref, f_ref, o_ref) in enumerate(inst):
            if p_i != pos:
                continue
            st = states[d][h]
            o_ref[r0:r0 + c, c0:c0 + HEAD_DIM] = _dot_nt(qds[i], st.astype(BF16)) + intra[i]
            last = c - 1 if d == 0 else 0
            states[d][h] = st * jnp.exp(xs[i][last:last + 1, :]) + kvs[i]
    for d in range(2):
        for h in range(n_heads):
            st_ref[d, h] = states[d][h]


def _d_scan(p, lb, n_batch, tb, ctx_len, mixw, col_q, col_i, col_f):
    t = p.shape[0]
    n_heads = mixw // HEAD_DIM
    nsb = tb // STEP_ROWS
    fwd, rev = _seq_block_maps(nsb, ctx_len // STEP_ROWS)
    sums_np, pair_np = _hgrn_tables()
    sums = jnp.asarray(sums_np, BF16)
    pair = jnp.asarray(pair_np, F32)
    in_specs = []
    for d, m in enumerate((fwd, rev)):
        in_specs += [pl.BlockSpec((STEP_ROWS, mixw), lambda b, n, m=m: (m(b, n), col_q // mixw)),
                     pl.BlockSpec((STEP_ROWS, mixw), lambda b, n, m=m: (m(b, n), col_i // mixw)),
                     pl.BlockSpec((STEP_ROWS, mixw), lambda b, n, m=m, d=d: (m(b, n), col_f // mixw + d))]
    in_specs += [pl.BlockSpec((1, mixw), lambda b, n: (0, 0)),
                 pl.BlockSpec(sums.shape, lambda b, n: (0, 0, 0)),
                 pl.BlockSpec(pair.shape, lambda b, n: (0, 0, 0, 0))]
    return pl.pallas_call(
        functools.partial(_d_scan_kernel, n_heads=n_heads),
        grid=(n_batch, nsb),
        in_specs=in_specs,
        out_specs=[pl.BlockSpec((STEP_ROWS, mixw), lambda b, n: (fwd(b, n), 0)),
                   pl.BlockSpec((STEP_ROWS, mixw), lambda b, n: (rev(b, n), 0))],
        out_shape=[jax.ShapeDtypeStruct((t, mixw), F32)] * 2,
        scratch_shapes=[pltpu.VMEM((2, n_heads, HEAD_DIM, HEAD_DIM), F32)],
        compiler_params=_cparams(("arbitrary", "arbitrary")),
        name="hgrn2_scan",
    )(p, p, p, p, p, p, lb, sums, pair)


def _route(logits_t, bias, n_experts):
    per = n_experts // N_EXPERT_GROUPS
    scores = _sigmoid(logits_t)
    sel = scores + bias
    rows_sel = [sel[e:e + 1, :] for e in range(n_experts)]
    rows_sc = [scores[e:e + 1, :] for e in range(n_experts)]

    def top2_sum(vals):
        hi = vals[0]
        lo = jnp.full_like(hi, -jnp.inf)
        for x in vals[1:]:
            lo = jnp.maximum(lo, jnp.minimum(hi, x))
            hi = jnp.maximum(hi, x)
        return hi + lo

    best = jnp.zeros_like(rows_sel[0], dtype=jnp.int32)
    best_score = top2_sum(rows_sel[0:per])
    for g in range(1, N_EXPERT_GROUPS):
        gs = top2_sum(rows_sel[g * per:(g + 1) * per])
        better = gs > best_score
        best = jnp.where(better, g, best)
        best_score = jnp.where(better, gs, best_score)
    in_sel, in_sc = [], []
    for j in range(per):
        a = rows_sel[j]
        c = rows_sc[j]
        for g in range(1, N_EXPERT_GROUPS):
            a = jnp.where(best == g, rows_sel[g * per + j], a)
            c = jnp.where(best == g, rows_sc[g * per + j], c)
        in_sel.append(a)
        in_sc.append(c)

    def first_argmax(vals):
        idx = jnp.zeros_like(best)
        top = vals[0]
        for j in range(1, per):
            better = vals[j] > top
            idx = jnp.where(better, j, idx)
            top = jnp.where(better, vals[j], top)
        return idx

    i1 = first_argmax(in_sel)
    i2 = first_argmax([jnp.where(i1 == j, -jnp.inf, in_sel[j]) for j in range(per)])

    def pick(idx):
        w = in_sc[0]
        for j in range(1, per):
            w = jnp.where(idx == j, in_sc[j], w)
        return w

    w1, w2 = pick(i1), pick(i2)
    tot = w1 + w2
    return best * per + i1, best * per + i2, w1 / tot, w2 / tot


def _outproj_kernel(oaf, oar, ga, odf, odr, gd, yb, yc, na, nd, wo, x_ref, m_ref, lng, lnb, wr, br,
                    x1_ref, h2_ref, ei_ref, ew_ref, *, alpha, mixw, tm, n_experts):
    n_heads = mixw // HEAD_DIM

    def gated(of_ref, or_ref, g_ref, gain_ref):
        parts = []
        for h in range(n_heads):
            c0 = h * HEAD_DIM
            o = of_ref[:, c0:c0 + HEAD_DIM] + or_ref[:, c0:c0 + HEAD_DIM]
            y = o * lax.rsqrt(jnp.mean(o * o, axis=-1, keepdims=True) + EPS) * gain_ref[...]
            parts.append((y * _silu(g_ref[:, c0:c0 + HEAD_DIM])).astype(BF16))
        return jnp.concatenate(parts, axis=1)

    ya = gated(oaf, oar, ga, na)
    yd = gated(odf, odr, gd, nd)
    y = (_dot(ya, wo[0:mixw, :]) + _dot(yb[...], wo[mixw:2 * mixw, :])
         + _dot(yc[...], wo[2 * mixw:3 * mixw, :]) + _dot(yd, wo[3 * mixw:4 * mixw, :]))
    x1 = _ln(alpha * x_ref[...] + m_ref[0, 2:3, :] * y) * lng[...] + lnb[...]
    x1_ref[...] = x1
    h2 = _ln(x1) * (1.0 + m_ref[0, 4:5, :]) + m_ref[0, 3:4, :]
    h2_ref[...] = h2
    h_hi = h2.astype(BF16)
    h_r = h2 - h_hi.astype(F32)
    h_mid = h_r.astype(BF16)
    h_lo = (h_r - h_mid.astype(F32)).astype(BF16)
    pa = _dot(h_hi, wr[:, 0:2 * 128])
    pb = _dot(h_mid, wr[:, 0:2 * 128])
    pc = _dot(h_hi, wr[:, 2 * 128:4 * 128])
    pd = _dot(h_lo, wr[:, 0:2 * 128])
    logits = ((pd[:, :128] + pb[:, 128:] + pc[:, :128]) + (pb[:, :128] + pa[:, 128:])) + pa[:, :128]
    logits_t = logits.T[:n_experts]
    e1, e2, w1, w2 = _route(logits_t, br[:, 0:1], n_experts)
    zi = jnp.zeros((6, tm), jnp.int32)
    ei_ref[...] = jnp.concatenate([e1, e2, zi], axis=0)
    ew_ref[...] = jnp.concatenate([w1, w2, zi.astype(F32)], axis=0)


def _outproj(oaf, oar, odf, odr, p, yb, yc, norm_a, norm_d, w_out, x, mods, ln_g, ln_b, w_router_p, b_router,
             alpha, tm, nb, nctx, n_batch, mixw, col_ga, col_gd):
    t, d = x.shape
    n_experts = b_router.shape[0]
    row = lambda i: (i, 0)
    const = lambda i: (0, 0)
    kern = functools.partial(_outproj_kernel, alpha=alpha, mixw=mixw, tm=tm, n_experts=n_experts)
    return pl.pallas_call(
        kern,
        grid=(t // tm,),
        in_specs=[pl.BlockSpec((tm, mixw), row), pl.BlockSpec((tm, mixw), row),
                  pl.BlockSpec((tm, mixw), lambda i: (i, col_ga // mixw)),
                  pl.BlockSpec((tm, mixw), row), pl.BlockSpec((tm, mixw), row),
                  pl.BlockSpec((tm, mixw), lambda i: (i, col_gd // mixw)),
                  pl.BlockSpec((tm, mixw), row), pl.BlockSpec((tm, mixw), row),
                  pl.BlockSpec((1, HEAD_DIM), const), pl.BlockSpec((1, HEAD_DIM), const),
                  pl.BlockSpec((N_MIXERS * mixw, d), const),
                  pl.BlockSpec((tm, d), row),
                  pl.BlockSpec((1, 6, d), lambda i: (_mod_row(i, nb, nctx, n_batch), 0, 0)),
                  pl.BlockSpec((1, d), const), pl.BlockSpec((1, d), const),
                  pl.BlockSpec((d, 4 * 128), const), pl.BlockSpec((n_experts, 128), const)],
        out_specs=[pl.BlockSpec((tm, d), row), pl.BlockSpec((tm, d), row),
                   pl.BlockSpec((8, tm), lambda i: (0, i)), pl.BlockSpec((8, tm), lambda i: (0, i))],
        out_shape=[jax.ShapeDtypeStruct((t, d), F32), jax.ShapeDtypeStruct((t, d), F32),
                   jax.ShapeDtypeStruct((8, t), jnp.int32), jax.ShapeDtypeStruct((8, t), F32)],
        compiler_params=_cparams(("parallel",)),
        name="out_proj_router",
    )(oaf, oar, p, odf, odr, p, yb, yc, norm_a, norm_d, w_out, x, mods, ln_g, ln_b, w_router_p, b_router)


def _rank_kernel(ei_ref, su_ref, rank_ref, cnt_ref, base_ref, *, n_experts, tm):
    i = pl.program_id(0)

    @pl.when(i == 0)
    def _():
        base_ref[...] = jnp.zeros_like(base_ref)

    eid = lax.broadcasted_iota(jnp.int32, (n_experts, tm), 0)
    o1 = (eid == ei_ref[0:1, :]).astype(F32)
    o2 = (eid == ei_ref[1:2, :]).astype(F32)
    cnt = o1 + o2
    before = _dot(cnt.astype(BF16), su_ref[...]) + base_ref[:, 0:1]
    r1 = jnp.sum(o1 * before, axis=0, keepdims=True)
    r2 = jnp.sum(o2 * before, axis=0, keepdims=True)
    rank_ref[...] = jnp.concatenate([r1, r2, jnp.zeros((6, tm), F32)], axis=0).astype(jnp.int32)
    base_ref[...] = base_ref[...] + jnp.sum(cnt, axis=1, keepdims=True)
    cnt_ref[...] = base_ref[...]


def _slot_ranks(ei, n_experts, tm):
    t = ei.shape[1]
    r = np.arange(tm)
    su = jnp.asarray((r[:, None] < r[None, :]).astype(np.float32), BF16)
    return pl.pallas_call(
        functools.partial(_rank_kernel, n_experts=n_experts, tm=tm),
        grid=(t // tm,),
        in_specs=[pl.BlockSpec((8, tm), lambda i: (0, i)), pl.BlockSpec((tm, tm), lambda i: (0, 0))],
        out_specs=[pl.BlockSpec((8, tm), lambda i: (0, i)), pl.BlockSpec((n_experts, 128), lambda i: (0, 0))],
        out_shape=[jax.ShapeDtypeStruct((8, t), jnp.int32), jax.ShapeDtypeStruct((n_experts, 128), F32)],
        scratch_shapes=[pltpu.VMEM((n_experts, 128), F32)],
        compiler_params=_cparams(("arbitrary",)),
        name="moe_slot_ranks",
    )(ei, su)


def _row_copy(src_ref, src_row, dst_ref, dst_row, sem):
    return pltpu.make_async_copy(src_ref.at[pl.ds(src_row, 1)], dst_ref.at[pl.ds(dst_row, 1)], sem)


def _dispatch_kernel(dest_ref, zs_ref, na_ref, h_ref, xs_ref, zero_ref, sem, zsem, *, tm, t, n_experts, tmx, n_blocks):
    base = pl.program_id(0) * tm

    @pl.when(pl.program_id(0) == 0)
    def _():
        zero_ref[...] = jnp.zeros_like(zero_ref)
        for e in range(n_experts):
            fill = pltpu.make_async_copy(zero_ref, xs_ref.at[pl.ds(pl.multiple_of(zs_ref[e], FILL_ALIGN),
                                                                  tmx + FILL_ALIGN)], zsem)
            fill.start()
            fill.wait()

        def tail(b, carry):
            fill = pltpu.make_async_copy(zero_ref.at[pl.ds(0, tmx)],
                                         xs_ref.at[pl.ds(pl.multiple_of(b * tmx, tmx), tmx)], zsem)
            fill.start()
            fill.wait()
            return carry

        lax.fori_loop(na_ref[0], n_blocks, tail, 0)

    def issue(r, carry):
        for kk in range(TOP_K):
            _row_copy(h_ref, r, xs_ref, dest_ref[kk * t + base + r], sem).start()
        return carry

    lax.fori_loop(0, tm, issue, 0, unroll=4)
    for kk in range(TOP_K):
        pltpu.make_async_copy(h_ref, xs_ref.at[pl.ds(0, tm)], sem).wait()


def _dispatch(dest, zero_start, n_active, h2, n_slots, tm, tmx):
    t, d = h2.shape
    n_experts = zero_start.shape[0]
    return pl.pallas_call(
        functools.partial(_dispatch_kernel, tm=tm, t=t, n_experts=n_experts, tmx=tmx, n_blocks=n_slots // tmx),
        grid_spec=pltpu.PrefetchScalarGridSpec(
            num_scalar_prefetch=3,
            grid=(t // tm,),
            in_specs=[pl.BlockSpec((tm, d), lambda i, dest, zs, na: (i, 0))],
            out_specs=pl.BlockSpec(memory_space=pl.ANY),
            scratch_shapes=[pltpu.VMEM((tmx + FILL_ALIGN, d), h2.dtype), pltpu.SemaphoreType.DMA(()),
                            pltpu.SemaphoreType.DMA(())]),
        out_shape=jax.ShapeDtypeStruct((n_slots, d), h2.dtype),
        compiler_params=_cparams(("arbitrary",)),
        name="moe_dispatch",
    )(dest, zero_start, n_active, h2)


def _expert_kernel(be_ref, na_ref, x_ref, w1_ref, w3_ref, w2_ref, y_ref):
    i = pl.program_id(0)

    @pl.when(i < na_ref[0])
    def _():
        xb = x_ref[...].astype(BF16)
        h1 = _dot(xb, w1_ref[0, 0])
        h3 = _dot(xb, w3_ref[0, 0])
        y_ref[...] = _dot((_silu(h1) * h3).astype(BF16), w2_ref[0, 0])

    @pl.when(i >= na_ref[0])
    def _():
        y_ref[...] = jnp.zeros_like(y_ref)


def _experts(block_expert, n_active, xs, w1, w3, w2, layer, tmx):
    n_slots, d = xs.shape
    f = w1.shape[3]
    return pl.pallas_call(
        _expert_kernel,
        grid_spec=pltpu.PrefetchScalarGridSpec(
            num_scalar_prefetch=2,
            grid=(n_slots // tmx,),
            in_specs=[pl.BlockSpec((tmx, d), lambda i, be, na: (jnp.minimum(i, na[0] - 1), 0)),
                      pl.BlockSpec((1, 1, d, f), lambda i, be, na: (layer, be[i], 0, 0)),
                      pl.BlockSpec((1, 1, d, f), lambda i, be, na: (layer, be[i], 0, 0)),
                      pl.BlockSpec((1, 1, f, d), lambda i, be, na: (layer, be[i], 0, 0))],
            out_specs=pl.BlockSpec((tmx, d), lambda i, be, na: (i, 0))),
        out_shape=jax.ShapeDtypeStruct((n_slots, d), F32),
        compiler_params=_cparams(("arbitrary",)),
        name="moe_experts",
    )(block_expert, n_active, xs, w1, w3, w2)


def _combine_kernel(dest_ref, ys_ref, ew_ref, x1_ref, m_ref, lng, lnb, mn_ref, x2_ref, *rest,
                    tm, t, alpha, block_of_step, emit_next):
    hn_ref = rest[0] if emit_next else None
    g_ref, sem = rest[-2:]
    i = pl.program_id(0)
    n = pl.num_programs(0)

    def issue(step, slot):
        base = block_of_step(step) * tm

        def body(r, carry):
            for kk in range(TOP_K):
                _row_copy(ys_ref, dest_ref[kk * t + base + r], g_ref.at[slot, kk], r, sem.at[slot]).start()
            return carry

        lax.fori_loop(0, tm, body, 0, unroll=4)

    @pl.when(i == 0)
    def _():
        issue(0, 0)

    @pl.when(i + 1 < n)
    def _():
        issue(i + 1, (i + 1) % 2)

    slot = i % 2
    for kk in range(TOP_K):
        pltpu.make_async_copy(ys_ref.at[pl.ds(0, tm)], g_ref.at[slot, kk], sem.at[slot]).wait()
    row = lax.broadcasted_iota(jnp.int32, (tm, tm), 0)
    col = lax.broadcasted_iota(jnp.int32, (tm, tm), 1)
    w_t = _dot_nt((row == col).astype(F32), ew_ref[...], HIGHEST)
    f = w_t[:, 0:1] * g_ref[slot, 0] + w_t[:, 1:2] * g_ref[slot, 1]
    x2 = _ln(alpha * x1_ref[...] + m_ref[0, 5:6, :] * f) * lng[...] + lnb[...]
    x2_ref[...] = x2
    if emit_next:
        hn_ref[...] = (_ln(x2) * (1.0 + mn_ref[0, 1:2, :]) + mn_ref[0, 0:1, :]).astype(hn_ref.dtype)


def _combine(dest, ys, ew, x1, mods, ln_g, ln_b, mods_next, alpha, tm, nb, nctx, n_batch, latent_only):
    t, d = x1.shape
    nlat = nb - nctx
    if latent_only:
        n_steps = n_batch * nlat
        block_of_step = lambda i: (i // nlat) * nb + nctx + i % nlat
    else:
        n_steps = t // tm
        block_of_step = lambda i: i
    mrow = lambda i, dest: (_mod_row(block_of_step(i), nb, nctx, n_batch), 0, 0)
    const = lambda i, dest: (0, 0)
    out_specs = [pl.BlockSpec((tm, d), lambda i, dest: (i, 0))]
    out_shape = [jax.ShapeDtypeStruct((n_steps * tm, d), F32)]
    if not latent_only:
        out_specs.append(pl.BlockSpec((tm, d), lambda i, dest: (i, 0)))
        out_shape.append(jax.ShapeDtypeStruct((t, d), BF16))
    return pl.pallas_call(
        functools.partial(_combine_kernel, tm=tm, t=t, alpha=alpha, block_of_step=block_of_step,
                          emit_next=not latent_only),
        grid_spec=pltpu.PrefetchScalarGridSpec(
            num_scalar_prefetch=1,
            grid=(n_steps,),
            in_specs=[pl.BlockSpec(memory_space=pl.ANY),
                      pl.BlockSpec((8, tm), lambda i, dest: (0, block_of_step(i))),
                      pl.BlockSpec((tm, d), lambda i, dest: (block_of_step(i), 0)),
                      pl.BlockSpec((1, 6, d), mrow),
                      pl.BlockSpec((1, d), const), pl.BlockSpec((1, d), const),
                      pl.BlockSpec((1, 6, d), mrow)],
            out_specs=out_specs,
            scratch_shapes=[pltpu.VMEM((2, TOP_K, tm, d), F32), pltpu.SemaphoreType.DMA((2,))]),
        out_shape=out_shape,
        compiler_params=_cparams(("arbitrary",)),
        name="moe_combine",
    )(dest, ys, ew, x1, mods, ln_g, ln_b, mods_next)


def _axial_tables(seq, ctx_len):
    rows = seq // GRID_W
    row = jnp.repeat(jnp.arange(rows, dtype=F32), GRID_W)
    col = jnp.tile(jnp.arange(GRID_W, dtype=F32), rows)
    n_freq = HEAD_DIM // 4
    inv_freq = ROPE_THETA ** (-jnp.arange(n_freq, dtype=F32) / n_freq)
    ang = jnp.concatenate([row[:, None] * inv_freq, col[:, None] * inv_freq], axis=-1)
    cos, sin = jnp.cos(ang), jnp.sin(ang)
    cosf = jnp.concatenate([cos, cos], axis=-1)
    sinf = jnp.concatenate([-sin, sin], axis=-1)
    cosf = jnp.concatenate([jnp.ones((ctx_len, HEAD_DIM), F32), cosf], axis=0)
    sinf = jnp.concatenate([jnp.zeros((ctx_len, HEAD_DIM), F32), sinf], axis=0)
    return cosf, sinf


def kernel(x, c, ctx, c_ctx, w_ada, b_ada, ln_g, ln_b, w_in, conv_a, a_log, dt_bias, norm_a, qk_norm_b, ws_c, bs_c, lb_d, norm_d, w_out, w_router, b_router, w1, w3, w2):
    n_batch, seq, d = x.shape
    ctx_len = ctx.shape[1]
    n_layers = w_in.shape[0]
    n_experts = w_router.shape[1]
    mixw = d // N_MIXERS
    n_heads = mixw // HEAD_DIM
    tb = ctx_len + seq
    t = n_batch * tb
    alpha = (2.0 * n_layers) ** 0.25
    tm = _pick(math.gcd(ctx_len, seq), (256, 128))
    nb, nctx = tb // tm, ctx_len // tm

    col = {"a_qkv": 0, "a_gate": 3 * mixw, "d_f": 4 * mixw, "b_q": 6 * mixw, "b_kv": 7 * mixw, "c_u": 8 * mixw,
           "c_v": 9 * mixw, "d_q": 10 * mixw, "d_i": 11 * mixw, "d_gate": 12 * mixw, "small": 13 * mixw}
    n_proj = 13 * mixw + 256
    src = np.cumsum([0, 3 * mixw, mixw, 2 * n_heads, 2 * n_heads, mixw, mixw, mixw, mixw, mixw, mixw, mixw, 2 * mixw])
    s_qkv, s_ga, s_beta, s_dec, s_bq, s_bkv, s_cu, s_cv, s_dq, s_di, s_dg, s_df, s_end = [int(v) for v in src]

    def permute_w_in(w):
        pad = jnp.zeros((d, n_proj - 13 * mixw - 4 * n_heads), w.dtype)
        return jnp.concatenate([w[:, s_qkv:s_beta], w[:, s_df:s_end], w[:, s_bq:s_df], w[:, s_beta:s_bq], pad],
                               axis=1).astype(BF16)

    cc = jnp.zeros((8, d), F32).at[:n_batch].set(c).at[n_batch].set(c_ctx)
    mods_all = _mod_vectors(cc, w_ada, b_ada)[:, :n_batch + 1].reshape(n_layers, n_batch + 1, 6, d)

    soft = jax.nn.softmax(lb_d.astype(F32), axis=0)
    lb_all = jnp.cumsum(soft, axis=0) - soft[0]
    cosf, sinf = _axial_tables(seq, ctx_len)
    gate_lanes = jnp.zeros((1, 128), F32)
    wr_f = w_router.astype(F32)
    wr_hi = wr_f.astype(BF16)
    wr_r = wr_f - wr_hi.astype(F32)
    wr_mid = wr_r.astype(BF16)
    wr_lo = (wr_r - wr_mid.astype(F32)).astype(BF16)
    w_router_p = jnp.zeros((d, 4 * 128), BF16)
    for blk, part in enumerate((wr_hi, wr_mid, wr_lo)):
        w_router_p = w_router_p.at[:, blk * 128:blk * 128 + n_experts].set(part)
    b_router_c = jnp.broadcast_to(b_router.astype(F32)[:, None], (n_experts, 128))

    tmx = 256
    n_assign = t * TOP_K
    n_slots = (-(-n_assign // tmx) + n_experts + 2) * tmx

    w1_b, w3_b, w2_b = w1.astype(BF16), w3.astype(BF16), w2.astype(BF16)
    xu = jnp.concatenate([ctx, x], axis=1).reshape(t, d)
    h = _ln_modulate(xu, mods_all[0], tm, nb, nctx, n_batch)
    for l in range(n_layers):
        mods = mods_all[l]
        p = _matmul(h, permute_w_in(w_in[l]))
        neg_a = gate_lanes.at[0, 2 * n_heads:4 * n_heads].set(-jnp.exp(a_log[l].astype(F32)).reshape(-1))
        dtb = gate_lanes.at[0, 2 * n_heads:4 * n_heads].set(dt_bias[l].astype(F32).reshape(-1))
        prep_a = _a_prep(p, conv_a[l], neg_a, dtb, tm, nb, nctx, mixw, col["small"])
        oaf, oar = _a_scan(prep_a, n_batch, tb, ctx_len, mixw)
        qb, kb, vb = _b_prep(p, qk_norm_b[l], cosf, sinf, tm, nb, mixw, col["b_q"], col["b_kv"])
        yb = _attention(qb, kb, vb, n_batch, tb, ctx_len, mixw)
        bs_t = jnp.zeros((CHUNK_C, 128), F32).at[:, :bs_c.shape[1]].set(bs_c[l].T)
        yc = _mixer_c(p, ws_c[l].astype(BF16), bs_t, tm, mixw, col["c_u"], col["c_v"])
        odf, odr = _d_scan(p, lb_all[l][None, :], n_batch, tb, ctx_len, mixw, col["d_q"], col["d_i"], col["d_f"])
        x1, h2, ei, ew = _outproj(oaf, oar, odf, odr, p, yb, yc, norm_a[l][None, :], norm_d[l][None, :],
                                  w_out[l].astype(BF16), xu, mods, ln_g[l, 0][None, :], ln_b[l, 0][None, :],
                                  w_router_p, b_router_c, alpha, tm, nb, nctx, n_batch, mixw,
                                  col["a_gate"], col["d_gate"])
        rank, counts = _slot_ranks(ei, n_experts, tm)
        cnt = counts[:, 0].astype(jnp.int32)
        padded = (cnt + tmx - 1) // tmx * tmx
        pad_end = jnp.cumsum(padded)
        pad_start = pad_end - padded
        eids = jnp.arange(n_experts, dtype=jnp.int32)[:, None, None]
        dest = (rank[:TOP_K] + jnp.sum(jnp.where(ei[None, :TOP_K] == eids, pad_start[:, None, None], 0),
                                       axis=0)).reshape(-1)
        n_blocks = n_slots // tmx
        block_start = jnp.arange(n_blocks, dtype=jnp.int32) * tmx
        block_expert = jnp.minimum(jnp.sum((pad_end[None, :] <= block_start[:, None]).astype(jnp.int32), axis=1),
                                   n_experts - 1)
        n_active = (pad_end[-1:] // tmx).astype(jnp.int32)
        fill_start = (pad_start + cnt) // FILL_ALIGN * FILL_ALIGN
        xs = _dispatch(dest, fill_start, n_active, h2, n_slots, tm, tmx)
        ys = _experts(block_expert, n_active, xs, w1_b, w3_b, w2_b, l, tmx)
        last = l == n_layers - 1
        res = _combine(dest, ys, ew, x1, mods, ln_g[l, 1][None, :], ln_b[l, 1][None, :],
                       mods_all[min(l + 1, n_layers - 1)], alpha, tm, nb, nctx, n_batch, latent_only=last)
        if last:
            return res[0].reshape(n_batch, seq, d)
        xu, h = res
```

```python
import functools
import math

import numpy as np
import jax
import jax.numpy as jnp
from jax import lax
from jax.experimental import pallas as pl
from jax.experimental.pallas import tpu as pltpu

F32 = jnp.float32
BF16 = jnp.bfloat16
HIGHEST = lax.Precision.HIGHEST

HEAD_DIM = 128
N_MIXERS = 4
CONV_W = 5
CHUNK = 64
STEP_ROWS = 2 * CHUNK
SOLVE_ROWS = 2 * CHUNK
CHUNK_C = 128
GRID_W = 64
ROPE_THETA = 10000.0
N_EXPERT_GROUPS = 4
TOP_K = 2
EPS = 1e-6
LOG2_E = math.log2(math.e)
GATE_CLIP = 30.0
N_LEVELS = 6
LANES = 128
SUBLANES = 8
HALO = SUBLANES
FILL_ALIGN = SUBLANES
EXPERT_BLOCK_ROWS = 256
V7X_VMEM_LIMIT = 56 * 1024 * 1024


def _pick(n, cands):
    for c in cands:
        if n % c == 0:
            return c
    raise ValueError(f"no tile in {cands} divides {n}")


def _cparams(sem, vmem=None):
    return pltpu.CompilerParams(dimension_semantics=sem, vmem_limit_bytes=vmem or V7X_VMEM_LIMIT)


def _dot(a, b, precision=None):
    return jnp.dot(a, b, preferred_element_type=F32, precision=precision)


def _dot_nt(a, b, precision=None):
    return lax.dot_general(a, b, (((1,), (1,)), ((), ())), preferred_element_type=F32, precision=precision)


def _dot_tn(a, b, precision=None):
    return lax.dot_general(a, b, (((0,), (0,)), ((), ())), preferred_element_type=F32, precision=precision)


def _sigmoid(x):
    return 1.0 / (1.0 + jnp.exp(-x))


def _silu(x):
    return x * _sigmoid(x)


def _gelu_tanh(x):
    return 0.5 * x * (1.0 + jnp.tanh(math.sqrt(2.0 / math.pi) * (x + 0.044715 * (x * x * x))))


def _ln(x):
    mu = jnp.mean(x, axis=-1, keepdims=True)
    xc = x - mu
    return xc * lax.rsqrt(jnp.mean(xc * xc, axis=-1, keepdims=True) + EPS)


def _mod_row(i, nb, nctx, n_batch):
    return jnp.where(i % nb < nctx, n_batch, i // nb)


def _mod_kernel(c_ref, w_ref, b_ref, o_ref):
    o_ref[0] = _dot(_silu(c_ref[...]), w_ref[0], HIGHEST) + b_ref[0]


def _mod_vectors(cc, w_ada, b_ada):
    n_layers, d, n6 = w_ada.shape
    tn = _pick(n6, (1024, 512, 128))
    return pl.pallas_call(
        _mod_kernel,
        grid=(n_layers, n6 // tn),
        in_specs=[pl.BlockSpec((SUBLANES, d), lambda l, j: (0, 0)),
                  pl.BlockSpec((1, d, tn), lambda l, j: (l, 0, j)),
                  pl.BlockSpec((1, 1, tn), lambda l, j: (l, 0, j))],
        out_specs=pl.BlockSpec((1, SUBLANES, tn), lambda l, j: (l, 0, j)),
        out_shape=jax.ShapeDtypeStruct((n_layers, SUBLANES, n6), F32),
        compiler_params=_cparams(("parallel", "parallel")),
        name="mod_vectors",
    )(cc, w_ada, b_ada.reshape(n_layers, 1, n6))


def _lnmod_kernel(x_ref, m_ref, h_ref):
    y = _ln(x_ref[...])
    h_ref[...] = (y * (1.0 + m_ref[0, 1:2, :]) + m_ref[0, 0:1, :]).astype(h_ref.dtype)


def _ln_modulate(x, mods, tm, nb, nctx, n_batch):
    t, d = x.shape
    return pl.pallas_call(
        _lnmod_kernel,
        grid=(t // tm,),
        in_specs=[pl.BlockSpec((tm, d), lambda i: (i, 0)),
                  pl.BlockSpec((1, 6, d), lambda i: (_mod_row(i, nb, nctx, n_batch), 0, 0))],
        out_specs=pl.BlockSpec((tm, d), lambda i: (i, 0)),
        out_shape=jax.ShapeDtypeStruct((t, d), BF16),
        compiler_params=_cparams(("parallel",)),
        name="ln_modulate",
    )(x, mods)


def _matmul_kernel(a_ref, w_ref, o_ref):
    o_ref[...] = _dot(a_ref[...], w_ref[...]).astype(o_ref.dtype)


def _matmul(a, w, out_dtype=F32):
    m, k = a.shape
    n = w.shape[1]
    tm = _pick(m, (1536, 1024, 768, 512, 256, 128))
    tn = _pick(n, (1152, 1024, 768, 512, 256, 128))
    return pl.pallas_call(
        _matmul_kernel,
        grid=(m // tm, n // tn),
        in_specs=[pl.BlockSpec((tm, k), lambda i, j: (i, 0)),
                  pl.BlockSpec((k, tn), lambda i, j: (0, j))],
        out_specs=pl.BlockSpec((tm, tn), lambda i, j: (i, j)),
        out_shape=jax.ShapeDtypeStruct((m, n), out_dtype),
        compiler_params=_cparams(("parallel", "parallel")),
        name="in_proj",
    )(a, w)


def _unit_tri_inverses(lows, eye, pair_refs):
    ts = [eye - low * pair_ref[0] for low, pair_ref in zip(lows, pair_refs)]
    for lvl in range(1, N_LEVELS):
        tbs = [t.astype(BF16) for t in ts]
        xs = [_dot(tb16, (low * pair_ref[lvl]).astype(BF16)) for tb16, low, pair_ref in zip(tbs, lows, pair_refs)]
        ts = [t - _dot(x.astype(BF16), tb16) for t, x, tb16 in zip(ts, xs, tbs)]
    return ts


def _a_prep_kernel(x_ref, xp_ref, xn_ref, sm_ref, cw_ref, na_ref, dtb_ref, trf_ref, trr_ref, pairf_ref, pairr_ref,
                   uf_ref, wqf_ref, kdf_ref, atf_ref, ur_ref, wqr_ref, kdr_ref, atr_ref, eg_ref, ext_ref,
                   *, tm, nb, nctx, mixw):
    i = pl.program_id(0)
    j = i % nb
    first = jnp.logical_or(j == 0, j == nctx)
    last = jnp.logical_or(j == nctx - 1, j == nb - 1)
    ext_ref[HALO:HALO + tm, :] = x_ref[...]
    ext_ref[0:HALO, :] = jnp.where(first, 0.0, xp_ref[...])
    ext_ref[HALO + tm:2 * HALO + tm, :] = jnp.where(last, 0.0, xn_ref[...])
    off = HALO - CONV_W // 2
    acc = cw_ref[0:1, :] * ext_ref[off:off + tm, :]
    for t in range(1, CONV_W):
        acc = acc + cw_ref[t:t + 1, :] * ext_ref[off + t:off + t + tm, :]
    y = _silu(acc)
    n_heads = mixw // HEAD_DIM
    sm = sm_ref[...]
    lane = lax.broadcasted_iota(jnp.int32, sm.shape, 1)
    zz = sm + dtb_ref[...]
    softplus = jnp.maximum(zz, 0.0) + jnp.log(1.0 + jnp.exp(-jnp.abs(zz)))
    g = jnp.where(jnp.logical_and(lane >= 2 * n_heads, lane < 4 * n_heads), na_ref[...] * softplus, 0.0)
    beta_all = _sigmoid(sm)
    tri_f = trf_ref[...]
    tri_r = trr_ref[...]
    cs_f = _dot(tri_f, g, HIGHEST)
    cs_r = _dot(tri_r, g, HIGHEST)
    gcum = jnp.where(lane < 3 * n_heads, cs_f, cs_r)
    gtot = _dot(jnp.maximum(tri_f, tri_r), g, HIGHEST)
    eg_ref[...] = jnp.exp(gtot)
    gcum_t = gcum.T
    sr = SOLVE_ROWS
    row = lax.broadcasted_iota(jnp.int32, (sr, sr), 0)
    col = lax.broadcasted_iota(jnp.int32, (sr, sr), 1)
    eye = (row == col).astype(F32)
    incls = (tri_f[0:sr, 0:sr] > 0.5, tri_r[0:sr, 0:sr] > 0.5)
    pairs = ([pairf_ref[lvl, 0:sr, 0:sr] for lvl in range(N_LEVELS)],
             [pairr_ref[lvl, 0:sr, 0:sr] for lvl in range(N_LEVELS)])
    outs = ((uf_ref, wqf_ref, kdf_ref, atf_ref), (ur_ref, wqr_ref, kdr_ref, atr_ref))
    items = []
    for sb in range(tm // sr):
        rs = sb * sr
        for h in range(n_heads):
            c0 = h * HEAD_DIM
            qh = y[rs:rs + sr, c0:c0 + HEAD_DIM]
            kh = y[rs:rs + sr, mixw + c0:mixw + c0 + HEAD_DIM]
            vh = y[rs:rs + sr, 2 * mixw + c0:2 * mixw + c0 + HEAD_DIM]
            qh = qh * (lax.rsqrt(jnp.sum(qh * qh, axis=-1, keepdims=True) + EPS) * HEAD_DIM ** -0.5)
            kh = kh * lax.rsqrt(jnp.sum(kh * kh, axis=-1, keepdims=True) + EPS)
            kb16 = kh.astype(BF16)
            qk = _dot_nt(qh.astype(BF16), kb16)
            for d in range(2):
                lb = d * n_heads + h
                lg = (2 + d) * n_heads + h
                beta = beta_all[rs:rs + sr, lb:lb + 1]
                gc = gcum[rs:rs + sr, lg:lg + 1]
                incl = incls[d]
                decay = jnp.where(incl, jnp.exp(jnp.where(incl, gc - gcum_t[lg:lg + 1, rs:rs + sr], 0.0)), 0.0)
                kbeta = kh * beta
                low = jnp.where(row == col, 0.0, _dot_nt(kbeta.astype(BF16), kb16) * decay)
                rhs = jnp.concatenate([vh * beta, kbeta * jnp.exp(gc)], axis=1).astype(BF16)
                items.append(dict(h=h, d=d, rs=rs, low=low, rhs=rhs, attn=(qk * decay).astype(BF16),
                                  qd=(qh * jnp.exp(gc)).astype(BF16),
                                  kd=(kh * jnp.exp(gtot[rs:rs + sr, lg:lg + 1] - gc)).astype(BF16)))
    tinvs = _unit_tri_inverses([it["low"] for it in items], eye, [pairs[it["d"]] for it in items])
    for it, tinv in zip(items, tinvs):
        u_ref, wq_ref, kd_ref, at_ref = outs[it["d"]]
        h, rs = it["h"], it["rs"]
        c0 = h * HEAD_DIM
        sol = _dot(tinv.astype(BF16), it["rhs"])
        u_ref[rs:rs + sr, c0:c0 + HEAD_DIM] = sol[:, :HEAD_DIM]
        w = sol[:, HEAD_DIM:].astype(BF16)
        kd_ref[rs:rs + sr, c0:c0 + HEAD_DIM] = it["kd"]
        for ci in range(sr // CHUNK):
            r0 = ci * CHUNK
            g0 = rs + r0
            wq_ref[2 * g0:2 * g0 + CHUNK, c0:c0 + HEAD_DIM] = w[r0:r0 + CHUNK]
            wq_ref[2 * g0 + CHUNK:2 * g0 + 2 * CHUNK, c0:c0 + HEAD_DIM] = it["qd"][r0:r0 + CHUNK]
            at_ref[g0:g0 + CHUNK, h * CHUNK:(h + 1) * CHUNK] = it["attn"][r0:r0 + CHUNK, r0:r0 + CHUNK]


def _a_prep(p, conv_w, neg_a, dtb, tm, nb, nctx, mixw, col_small):
    t = p.shape[0]
    c3 = 3 * mixw
    n_heads = mixw // HEAD_DIM
    r = np.arange(tm)
    same = (r[:, None] // CHUNK) == (r[None, :] // CHUNK)
    tri_f = jnp.asarray((same & (r[None, :] <= r[:, None])).astype(np.float32))
    tri_r = jnp.asarray((same & (r[None, :] >= r[:, None])).astype(np.float32))
    pair_f = []
    for lvl in range(N_LEVELS):
        s_blk = 1 << lvl
        joined = ((r[:, None] // (2 * s_blk)) == (r[None, :] // (2 * s_blk))) \
            & ((r[:, None] % (2 * s_blk)) >= s_blk) & ((r[None, :] % (2 * s_blk)) < s_blk)
        pair_f.append(joined.astype(np.float32))
    pair_f = np.stack(pair_f)
    pair_r = jnp.asarray(np.transpose(pair_f, (0, 2, 1)))
    pair_f = jnp.asarray(pair_f)
    hb = tm // HALO
    last_hb = t // HALO - 1
    kern = functools.partial(_a_prep_kernel, tm=tm, nb=nb, nctx=nctx, mixw=mixw)
    per_dir_specs = [pl.BlockSpec((tm, mixw), lambda i: (i, 0)),
                     pl.BlockSpec((2 * tm, mixw), lambda i: (i, 0)),
                     pl.BlockSpec((tm, mixw), lambda i: (i, 0)),
                     pl.BlockSpec((tm, n_heads * CHUNK), lambda i: (i, 0))]
    per_dir_shapes = [jax.ShapeDtypeStruct((t, mixw), F32), jax.ShapeDtypeStruct((2 * t, mixw), BF16),
                      jax.ShapeDtypeStruct((t, mixw), BF16), jax.ShapeDtypeStruct((t, n_heads * CHUNK), BF16)]
    return pl.pallas_call(
        kern,
        grid=(t // tm,),
        in_specs=[pl.BlockSpec((tm, c3), lambda i: (i, 0)),
                  pl.BlockSpec((HALO, c3), lambda i: (jnp.maximum(i * hb - 1, 0), 0)),
                  pl.BlockSpec((HALO, c3), lambda i: (jnp.minimum((i + 1) * hb, last_hb), 0)),
                  pl.BlockSpec((tm, LANES), lambda i: (i, col_small // LANES)),
                  pl.BlockSpec((CONV_W, c3), lambda i: (0, 0)),
                  pl.BlockSpec((1, LANES), lambda i: (0, 0)),
                  pl.BlockSpec((1, LANES), lambda i: (0, 0)),
                  pl.BlockSpec((tm, tm), lambda i: (0, 0)),
                  pl.BlockSpec((tm, tm), lambda i: (0, 0)),
                  pl.BlockSpec((N_LEVELS, tm, tm), lambda i: (0, 0, 0)),
                  pl.BlockSpec((N_LEVELS, tm, tm), lambda i: (0, 0, 0))],
        out_specs=per_dir_specs * 2 + [pl.BlockSpec((tm, LANES), lambda i: (i, 0))],
        out_shape=per_dir_shapes * 2 + [jax.ShapeDtypeStruct((t, LANES), F32)],
        scratch_shapes=[pltpu.VMEM((tm + 2 * HALO, c3), F32)],
        compiler_params=_cparams(("parallel",)),
        name="deltanet_prep",
    )(p, p, p, p, conv_w, neg_a, dtb, tri_f, tri_r, pair_f, pair_r)


def _a_scan_kernel(uf, wqf, kdf, atf, egf, ur, wqr, kdr, atr, egr, of_ref, or_ref, s_ref, sbd_ref, *, n_heads):
    n = pl.program_id(1)

    @pl.when(n == 0)
    def _():
        s_ref[...] = jnp.zeros_like(s_ref)
        sbd_ref[...] = jnp.zeros_like(sbd_ref)

    mixw = n_heads * HEAD_DIM
    lane_head = lax.broadcasted_iota(jnp.int32, (CHUNK, mixw), 1) // HEAD_DIM
    dirs = ((uf, wqf, kdf, atf, egf, of_ref), (ur, wqr, kdr, atr, egr, or_ref))
    for pos in range(2):
        r0s = [(pos if d == 0 else 1 - pos) * CHUNK for d in range(2)]
        rs = [_dot(dirs[d][1][2 * r0s[d]:2 * r0s[d] + 2 * CHUNK, :], sbd_ref[d]) for d in range(2)]
        v_news = [dirs[d][0][r0s[d]:r0s[d] + CHUNK, :] - rs[d][:CHUNK] for d in range(2)]
        v_stacks = [jnp.concatenate([jnp.where(lane_head == h, v_news[d], 0.0) for h in range(n_heads)],
                                    axis=0).astype(BF16) for d in range(2)]
        for d in range(2):
            dirs[d][5][r0s[d]:r0s[d] + CHUNK, :] = (rs[d][CHUNK:]
                                                    + _dot(dirs[d][3][r0s[d]:r0s[d] + CHUNK, :], v_stacks[d]))
        vnbs = [v.astype(BF16) for v in v_news]
        for h in range(n_heads):
            c0 = h * HEAD_DIM
            for d in range(2):
                lg = (2 + d) * n_heads + h
                r0 = r0s[d]
                s_new = (s_ref[d, h] * dirs[d][4][r0:r0 + 1, lg:lg + 1]
                         + _dot_tn(dirs[d][2][r0:r0 + CHUNK, c0:c0 + HEAD_DIM], vnbs[d][:, c0:c0 + HEAD_DIM]))
                s_ref[d, h] = s_new
                sbd_ref[d, c0:c0 + HEAD_DIM, c0:c0 + HEAD_DIM] = s_new.astype(BF16)


def _seq_block_maps(nsb, nctx_sb):
    def fwd(b, n):
        return b * nsb + n

    def rev(b, n):
        return b * nsb + jnp.where(n < nctx_sb, nctx_sb - 1 - n, nsb - 1 - (n - nctx_sb))

    return fwd, rev


def _a_scan(prep, n_batch, tb, ctx_len, mixw):
    uf, wqf, kdf, atf, ur, wqr, kdr, atr, eg = prep
    t = uf.shape[0]
    n_heads = mixw // HEAD_DIM
    nsb = tb // STEP_ROWS
    fwd, rev = _seq_block_maps(nsb, ctx_len // STEP_ROWS)
    in_specs = []
    for m in (fwd, rev):
        in_specs += [pl.BlockSpec((STEP_ROWS, mixw), lambda b, n, m=m: (m(b, n), 0)),
                     pl.BlockSpec((2 * STEP_ROWS, mixw), lambda b, n, m=m: (m(b, n), 0)),
                     pl.BlockSpec((STEP_ROWS, mixw), lambda b, n, m=m: (m(b, n), 0)),
                     pl.BlockSpec((STEP_ROWS, n_heads * CHUNK), lambda b, n, m=m: (m(b, n), 0)),
                     pl.BlockSpec((STEP_ROWS, LANES), lambda b, n, m=m: (m(b, n), 0))]
    return pl.pallas_call(
        functools.partial(_a_scan_kernel, n_heads=n_heads),
        grid=(n_batch, nsb),
        in_specs=in_specs,
        out_specs=[pl.BlockSpec((STEP_ROWS, mixw), lambda b, n: (fwd(b, n), 0)),
                   pl.BlockSpec((STEP_ROWS, mixw), lambda b, n: (rev(b, n), 0))],
        out_shape=[jax.ShapeDtypeStruct((t, mixw), F32)] * 2,
        scratch_shapes=[pltpu.VMEM((2, n_heads, HEAD_DIM, HEAD_DIM), F32),
                        pltpu.VMEM((2, mixw, mixw), BF16)],
        compiler_params=_cparams(("arbitrary", "arbitrary")),
        name="deltanet_scan",
    )(uf, wqf, kdf, atf, eg, ur, wqr, kdr, atr, eg)


def _b_prep_kernel(q_ref, kv_ref, gain_ref, cos_ref, sin_ref, qo_ref, ko_ref, vo_ref, *, mixw):
    n_heads = mixw // HEAD_DIM
    n_kv = n_heads // 2
    cosf = cos_ref[...]
    sinf = sin_ref[...]

    def norm_rope(xh, gain, scale):
        yh = xh * lax.rsqrt(jnp.mean(xh * xh, axis=-1, keepdims=True) + EPS) * gain
        return (yh * cosf + pltpu.roll(yh, HEAD_DIM // 2, 1) * sinf) * scale

    for h in range(n_heads):
        c0 = h * HEAD_DIM
        qo_ref[:, c0:c0 + HEAD_DIM] = norm_rope(q_ref[:, c0:c0 + HEAD_DIM], gain_ref[0:1, :],
                                                HEAD_DIM ** -0.5 * LOG2_E).astype(qo_ref.dtype)
    for h in range(n_kv):
        c0 = h * HEAD_DIM
        ko_ref[:, c0:c0 + HEAD_DIM] = norm_rope(kv_ref[:, c0:c0 + HEAD_DIM], gain_ref[1:2, :], 1.0).astype(ko_ref.dtype)
    ones = jnp.ones((q_ref.shape[0], HEAD_DIM), vo_ref.dtype)
    for h in range(n_kv):
        c0 = (n_kv + h) * HEAD_DIM
        vo_ref[:, 2 * h * HEAD_DIM:(2 * h + 1) * HEAD_DIM] = kv_ref[:, c0:c0 + HEAD_DIM].astype(vo_ref.dtype)
        vo_ref[:, (2 * h + 1) * HEAD_DIM:(2 * h + 2) * HEAD_DIM] = ones


def _b_prep(p, qk_gain, cosf, sinf, tm, nb, mixw, col_q, col_kv):
    t = p.shape[0]
    kvw = mixw // 2
    return pl.pallas_call(
        functools.partial(_b_prep_kernel, mixw=mixw),
        grid=(t // tm,),
        in_specs=[pl.BlockSpec((tm, mixw), lambda i: (i, col_q // mixw)),
                  pl.BlockSpec((tm, mixw), lambda i: (i, col_kv // mixw)),
                  pl.BlockSpec((2, HEAD_DIM), lambda i: (0, 0)),
                  pl.BlockSpec((tm, HEAD_DIM), lambda i: (i % nb, 0)),
                  pl.BlockSpec((tm, HEAD_DIM), lambda i: (i % nb, 0))],
        out_specs=[pl.BlockSpec((tm, mixw), lambda i: (i, 0)),
                   pl.BlockSpec((tm, kvw), lambda i: (i, 0)),
                   pl.BlockSpec((tm, 2 * kvw), lambda i: (i, 0))],
        out_shape=[jax.ShapeDtypeStruct((t, mixw), BF16),
                   jax.ShapeDtypeStruct((t, kvw), BF16),
                   jax.ShapeDtypeStruct((t, 2 * kvw), BF16)],
        compiler_params=_cparams(("parallel",)),
        name="gqa_prep",
    )(p, p, qk_gain, cosf, sinf)


def _attn_kernel(q_ref, k_ref, v_ref, o_ref, m_ref, acc_ref, sa_ref, sb_ref, *, tq, tk, nk, nctx_q, ctx_len):
    qi = pl.program_id(2)
    qb = q_ref[...]
    q2 = jnp.concatenate([qb[:, :HEAD_DIM], qb[:, HEAD_DIM:]], axis=0)
    m_ref[...] = jnp.full_like(m_ref, -1e30)
    acc_ref[...] = jnp.zeros_like(acc_ref)

    def scores(i, s_ref):
        s_ref[...] = _dot_nt(q2, k_ref[i * tk:(i + 1) * tk, :])

    def absorb(s_of, vc):
        for g in range(2):
            sl = slice(g * tq, (g + 1) * tq)
            s = s_of(sl)
            m_prev = m_ref[sl, :]
            m_new = jnp.maximum(m_prev, jnp.max(s, axis=-1, keepdims=True))
            p = jnp.exp2(s - m_new)
            acc_ref[sl, :] = jnp.exp2(m_prev - m_new) * acc_ref[sl, :] + _dot(p.astype(BF16), vc)
            m_ref[sl, :] = m_new

    def absorb_chunk(i, s_ref):
        absorb(lambda sl: s_ref[sl, :], v_ref[i * tk:(i + 1) * tk, :])

    @pl.when(qi >= nctx_q)
    def _():
        bufs = (sa_ref, sb_ref)
        scores(0, bufs[0])
        for i in range(nk):
            if i + 1 < nk:
                scores(i + 1, bufs[(i + 1) % 2])
            absorb_chunk(i, bufs[i % 2])

    @pl.when(qi < nctx_q)
    def _():
        s_ctx = _dot_nt(q2, k_ref[0:ctx_len, :])
        absorb(lambda sl: s_ctx[sl, :], v_ref[0:ctx_len, :])

    acc = acc_ref[...]
    o = acc[:, :HEAD_DIM] / acc[:, HEAD_DIM:HEAD_DIM + 1]
    o_ref[...] = jnp.concatenate([o[:tq], o[tq:]], axis=1).astype(o_ref.dtype)


def _attention(q, k, v, n_batch, tb, ctx_len, mixw):
    t = q.shape[0]
    n_kv = mixw // HEAD_DIM // 2
    tq = _pick(math.gcd(ctx_len, tb), (256, 128))
    tk = _pick(tb, (1408, 768, 1024, 512, 640, 384, 256, 128))
    nq, nk = tb // tq, tb // tk
    return pl.pallas_call(
        functools.partial(_attn_kernel, tq=tq, tk=tk, nk=nk, nctx_q=ctx_len // tq, ctx_len=ctx_len),
        grid=(n_batch, n_kv, nq),
        in_specs=[pl.BlockSpec((tq, 2 * HEAD_DIM), lambda b, j, qi: (b * nq + qi, j)),
                  pl.BlockSpec((tb, HEAD_DIM), lambda b, j, qi: (b, j)),
                  pl.BlockSpec((tb, 2 * HEAD_DIM), lambda b, j, qi: (b, j))],
        out_specs=pl.BlockSpec((tq, 2 * HEAD_DIM), lambda b, j, qi: (b * nq + qi, j)),
        out_shape=jax.ShapeDtypeStruct((t, mixw), BF16),
        scratch_shapes=[pltpu.VMEM((2 * tq, 1), F32), pltpu.VMEM((2 * tq, 2 * HEAD_DIM), F32),
                        pltpu.VMEM((2 * tq, tk), F32), pltpu.VMEM((2 * tq, tk), F32)],
        compiler_params=_cparams(("parallel", "parallel", "parallel")),
        name="gqa_attention",
    )(q, k, v)


def _c_kernel(u_ref, v_ref, ws_ref, bs_ref, o_ref, *, tm, mixw):
    n_groups = mixw // HEAD_DIM
    for ci in range(tm // CHUNK_C):
        r0 = ci * CHUNK_C
        u = _gelu_tanh(u_ref[r0:r0 + CHUNK_C, :])
        vn = _ln(_gelu_tanh(v_ref[r0:r0 + CHUNK_C, :])).astype(BF16)
        for g in range(n_groups):
            c0 = g * HEAD_DIM
            vm = _dot(ws_ref[g], vn[:, c0:c0 + HEAD_DIM]) + bs_ref[:, g:g + 1]
            o_ref[r0:r0 + CHUNK_C, c0:c0 + HEAD_DIM] = (u[:, c0:c0 + HEAD_DIM] * vm).astype(o_ref.dtype)


def _mixer_c(p, ws, bs_t, tm, mixw, col_u, col_v):
    t = p.shape[0]
    n_groups = mixw // HEAD_DIM
    return pl.pallas_call(
        functools.partial(_c_kernel, tm=tm, mixw=mixw),
        grid=(t // tm,),
        in_specs=[pl.BlockSpec((tm, mixw), lambda i: (i, col_u // mixw)),
                  pl.BlockSpec((tm, mixw), lambda i: (i, col_v // mixw)),
                  pl.BlockSpec((n_groups, CHUNK_C, CHUNK_C), lambda i: (0, 0, 0)),
                  pl.BlockSpec((CHUNK_C, LANES), lambda i: (0, 0))],
        out_specs=pl.BlockSpec((tm, mixw), lambda i: (i, 0)),
        out_shape=jax.ShapeDtypeStruct((t, mixw), BF16),
        compiler_params=_cparams(("parallel",)),
        name="gmlp",
    )(p, p, ws, bs_t)


def _hgrn_tables():
    tau = np.arange(CHUNK)
    sums_t = [tau[None, :] <= tau[:, None], tau[None, :] > tau[:, None]]
    pair_t = []
    for lvl in range(N_LEVELS):
        m = CHUNK >> (lvl + 1)
        blk = tau // (2 * m)
        upper = (tau % (2 * m)) >= m
        ref = blk * 2 * m + m - 1
        r = tau[None, :]
        w_up = upper[:, None] & (r > ref[:, None]) & (r <= tau[:, None])
        w_lo = (~upper)[:, None] & (r > tau[:, None]) & (r <= ref[:, None])
        sums_t.append(w_up | w_lo)
        pair_t.append((blk[:, None] == blk[None, :]) & upper[:, None] & (~upper)[None, :])
    sums_t = np.concatenate(sums_t, axis=0).astype(np.float32)
    pair_t = np.stack(pair_t).astype(np.float32)
    flip = tau[::-1]
    sums, pair = [], []
    for d in range(2):
        if d == 0:
            s_d, p_d = sums_t, pair_t
        else:
            s_d = sums_t.reshape(-1, CHUNK, CHUNK)[:, flip][:, :, flip].reshape(-1, CHUNK)
            p_d = pair_t[:, flip][:, :, flip]
        sums.append(np.concatenate([s_d, s_d, s_d], axis=1))
        pair.append(p_d)
    return np.stack(sums), np.stack(pair)


def _d_scan_kernel(qf, vf, ff, qr, vr, fr, lb_ref, sums_ref, pair_ref, of_ref, or_ref, st_ref, *, n_heads):
    n = pl.program_id(1)

    @pl.when(n == 0)
    def _():
        st_ref[...] = jnp.zeros_like(st_ref)

    c = CHUNK
    row = lax.broadcasted_iota(jnp.int32, (c, c), 0)
    col = lax.broadcasted_iota(jnp.int32, (c, c), 1)
    eye = (row == col).astype(F32)
    inst = []
    for d, (q_ref, v_ref, f_ref, o_ref) in enumerate(((qf, vf, ff, of_ref), (qr, vr, fr, or_ref))):
        for pos, ci in enumerate((0, 1) if d == 0 else (1, 0)):
            for h in range(n_heads):
                inst.append((d, pos, ci * c, h * HEAD_DIM, h, q_ref, v_ref, f_ref, o_ref))
    qs = [q_ref[r0:r0 + c, c0:c0 + HEAD_DIM] for (d, pos, r0, c0, h, q_ref, v_ref, f_ref, o_ref) in inst]
    vs = [v_ref[r0:r0 + c, c0:c0 + HEAD_DIM].astype(BF16) for (d, pos, r0, c0, h, q_ref, v_ref, f_ref, o_ref) in inst]
    es = [jnp.exp(-jnp.clip(f_ref[r0:r0 + c, c0:c0 + HEAD_DIM], -GATE_CLIP, GATE_CLIP))
          for (d, pos, r0, c0, h, q_ref, v_ref, f_ref, o_ref) in inst]
    sgs = [1.0 / (1.0 + e) for e in es]
    lbs = [lb_ref[0:1, it[3]:it[3] + HEAD_DIM] for it in inst]
    logfs = [jnp.log(lb + (1.0 - lb) * sg) for lb, sg in zip(lbs, sgs)]
    ks = [(1.0 - lb) * (e * sg) for lb, e, sg in zip(lbs, es, sgs)]
    his = [x.astype(BF16) for x in logfs]
    r1s = [x - hi.astype(F32) for x, hi in zip(logfs, his)]
    mids = [x.astype(BF16) for x in r1s]
    los = [(x - mid.astype(F32)).astype(BF16) for x, mid in zip(r1s, mids)]
    cats = [jnp.concatenate([hi, mid, lo], axis=0) for hi, mid, lo in zip(his, mids, los)]
    xs = []
    for i0 in range(0, len(inst), 2):
        x2 = _dot(sums_ref[inst[i0][0]], jnp.concatenate([cats[i0], cats[i0 + 1]], axis=1))
        xs += [x2[:, :HEAD_DIM], x2[:, HEAD_DIM:]]
    kbs = [k.astype(BF16) for k in ks]
    attns = [eye * _dot_nt(q.astype(BF16), kb) for q, kb in zip(qs, kbs)]
    for lvl in range(N_LEVELS):
        zls = [jnp.exp(x[(2 + lvl) * c:(3 + lvl) * c]) for x in xs]
        attns = [a + pair_ref[it[0], lvl] * _dot_nt((q * zl).astype(BF16), (k * zl).astype(BF16))
                 for a, it, q, k, zl in zip(attns, inst, qs, ks, zls)]
    qds = [(q * jnp.exp(x[0:c])).astype(BF16) for q, x in zip(qs, xs)]
    kds = [(k * jnp.exp(x[c:2 * c])).astype(BF16) for k, x in zip(ks, xs)]
    intra = [_dot(a.astype(BF16), v) for a, v in zip(attns, vs)]
    kvs = [_dot_tn(v, kd) for v, kd in zip(vs, kds)]
    states = [[st_ref[d, h] for h in range(n_heads)] for d in range(2)]
    for pos in range(2):
        for i, (d, p_i, r0, c0, h, q_ref, v_ref, f_ref, o_ref) in enumerate(inst):
            if p_i != pos:
                continue
            st = states[d][h]
            o_ref[r0:r0 + c, c0:c0 + HEAD_DIM] = _dot_nt(qds[i], st.astype(BF16)) + intra[i]
            last = c - 1 if d == 0 else 0
            states[d][h] = st * jnp.exp(xs[i][last:last + 1, :]) + kvs[i]
    for d in range(2):
        for h in range(n_heads):
            st_ref[d, h] = states[d][h]


def _d_scan(p, lb, n_batch, tb, ctx_len, mixw, col_q, col_i, col_f):
    t = p.shape[0]
    n_heads = mixw // HEAD_DIM
    nsb = tb // STEP_ROWS
    fwd, rev = _seq_block_maps(nsb, ctx_len // STEP_ROWS)
    sums_np, pair_np = _hgrn_tables()
    sums = jnp.asarray(sums_np, BF16)
    pair = jnp.asarray(pair_np, F32)
    in_specs = []
    for d, m in enumerate((fwd, rev)):
        in_specs += [pl.BlockSpec((STEP_ROWS, mixw), lambda b, n, m=m: (m(b, n), col_q // mixw)),
                     pl.BlockSpec((STEP_ROWS, mixw), lambda b, n, m=m: (m(b, n), col_i // mixw)),
                     pl.BlockSpec((STEP_ROWS, mixw), lambda b, n, m=m, d=d: (m(b, n), col_f // mixw + d))]
    in_specs += [pl.BlockSpec((1, mixw), lambda b, n: (0, 0)),
                 pl.BlockSpec(sums.shape, lambda b, n: (0, 0, 0)),
                 pl.BlockSpec(pair.shape, lambda b, n: (0, 0, 0, 0))]
    return pl.pallas_call(
        functools.partial(_d_scan_kernel, n_heads=n_heads),
        grid=(n_batch, nsb),
        in_specs=in_specs,
        out_specs=[pl.BlockSpec((STEP_ROWS, mixw), lambda b, n: (fwd(b, n), 0)),
                   pl.BlockSpec((STEP_ROWS, mixw), lambda b, n: (rev(b, n), 0))],
        out_shape=[jax.ShapeDtypeStruct((t, mixw), F32)] * 2,
        scratch_shapes=[pltpu.VMEM((2, n_heads, HEAD_DIM, HEAD_DIM), F32)],
        compiler_params=_cparams(("arbitrary", "arbitrary")),
        name="hgrn2_scan",
    )(p, p, p, p, p, p, lb, sums, pair)


def _route(logits_t, bias, n_experts):
    per = n_experts // N_EXPERT_GROUPS
    scores = _sigmoid(logits_t)
    sel = scores + bias
    rows_sel = [sel[e:e + 1, :] for e in range(n_experts)]
    rows_sc = [scores[e:e + 1, :] for e in range(n_experts)]

    def top2_sum(vals):
        hi = vals[0]
        lo = jnp.full_like(hi, -jnp.inf)
        for x in vals[1:]:
            lo = jnp.maximum(lo, jnp.minimum(hi, x))
            hi = jnp.maximum(hi, x)
        return hi + lo

    best = jnp.zeros_like(rows_sel[0], dtype=jnp.int32)
    best_score = top2_sum(rows_sel[0:per])
    for g in range(1, N_EXPERT_GROUPS):
        gs = top2_sum(rows_sel[g * per:(g + 1) * per])
        better = gs > best_score
        best = jnp.where(better, g, best)
        best_score = jnp.where(better, gs, best_score)
    in_sel, in_sc = [], []
    for j in range(per):
        a = rows_sel[j]
        c = rows_sc[j]
        for g in range(1, N_EXPERT_GROUPS):
            a = jnp.where(best == g, rows_sel[g * per + j], a)
            c = jnp.where(best == g, rows_sc[g * per + j], c)
        in_sel.append(a)
        in_sc.append(c)

    def first_argmax(vals):
        idx = jnp.zeros_like(best)
        top = vals[0]
        for j in range(1, per):
            better = vals[j] > top
            idx = jnp.where(better, j, idx)
            top = jnp.where(better, vals[j], top)
        return idx

    i1 = first_argmax(in_sel)
    i2 = first_argmax([jnp.where(i1 == j, -jnp.inf, in_sel[j]) for j in range(per)])

    def pick(idx):
        w = in_sc[0]
        for j in range(1, per):
            w = jnp.where(idx == j, in_sc[j], w)
        return w

    w1, w2 = pick(i1), pick(i2)
    tot = w1 + w2
    return best * per + i1, best * per + i2, w1 / tot, w2 / tot


def _outproj_kernel(oaf, oar, ga, odf, odr, gd, yb, yc, na, nd, wo, x_ref, m_ref, lng, lnb, wr, br,
                    x1_ref, h2_ref, ei_ref, ew_ref, *, alpha, mixw, tm, n_experts):
    n_heads = mixw // HEAD_DIM

    def gated(of_ref, or_ref, g_ref, gain_ref):
        parts = []
        for h in range(n_heads):
            c0 = h * HEAD_DIM
            o = of_ref[:, c0:c0 + HEAD_DIM] + or_ref[:, c0:c0 + HEAD_DIM]
            y = o * lax.rsqrt(jnp.mean(o * o, axis=-1, keepdims=True) + EPS) * gain_ref[...]
            parts.append((y * _silu(g_ref[:, c0:c0 + HEAD_DIM])).astype(BF16))
        return jnp.concatenate(parts, axis=1)

    ya = gated(oaf, oar, ga, na)
    yd = gated(odf, odr, gd, nd)
    y = (_dot(ya, wo[0:mixw, :]) + _dot(yb[...], wo[mixw:2 * mixw, :])
         + _dot(yc[...], wo[2 * mixw:3 * mixw, :]) + _dot(yd, wo[3 * mixw:4 * mixw, :]))
    x1 = _ln(alpha * x_ref[...] + m_ref[0, 2:3, :] * y) * lng[...] + lnb[...]
    x1_ref[...] = x1
    h2 = _ln(x1) * (1.0 + m_ref[0, 4:5, :]) + m_ref[0, 3:4, :]
    h2_ref[...] = h2
    h_hi = h2.astype(BF16)
    h_r = h2 - h_hi.astype(F32)
    h_mid = h_r.astype(BF16)
    h_lo = (h_r - h_mid.astype(F32)).astype(BF16)
    pa = _dot(h_hi, wr[:, 0:2 * LANES])
    pb = _dot(h_mid, wr[:, 0:2 * LANES])
    pc = _dot(h_hi, wr[:, 2 * LANES:4 * LANES])
    pd = _dot(h_lo, wr[:, 0:2 * LANES])
    logits = (((pd[:, :LANES] + pb[:, LANES:] + pc[:, :LANES]) + (pb[:, :LANES] + pa[:, LANES:]))
              + pa[:, :LANES])
    logits_t = logits.T[:n_experts]
    e1, e2, w1, w2 = _route(logits_t, br[:, 0:1], n_experts)
    zi = jnp.zeros((SUBLANES - TOP_K, tm), jnp.int32)
    ei_ref[...] = jnp.concatenate([e1, e2, zi], axis=0)
    ew_ref[...] = jnp.concatenate([w1, w2, zi.astype(F32)], axis=0)


def _outproj(oaf, oar, odf, odr, p, yb, yc, norm_a, norm_d, w_out, x, mods, ln_g, ln_b, w_router_p, b_router,
             alpha, tm, nb, nctx, n_batch, mixw, col_ga, col_gd):
    t, d = x.shape
    n_experts = b_router.shape[0]
    row = lambda i: (i, 0)
    const = lambda i: (0, 0)
    kern = functools.partial(_outproj_kernel, alpha=alpha, mixw=mixw, tm=tm, n_experts=n_experts)
    return pl.pallas_call(
        kern,
        grid=(t // tm,),
        in_specs=[pl.BlockSpec((tm, mixw), row), pl.BlockSpec((tm, mixw), row),
                  pl.BlockSpec((tm, mixw), lambda i: (i, col_ga // mixw)),
                  pl.BlockSpec((tm, mixw), row), pl.BlockSpec((tm, mixw), row),
                  pl.BlockSpec((tm, mixw), lambda i: (i, col_gd // mixw)),
                  pl.BlockSpec((tm, mixw), row), pl.BlockSpec((tm, mixw), row),
                  pl.BlockSpec((1, HEAD_DIM), const), pl.BlockSpec((1, HEAD_DIM), const),
                  pl.BlockSpec((N_MIXERS * mixw, d), const),
                  pl.BlockSpec((tm, d), row),
                  pl.BlockSpec((1, 6, d), lambda i: (_mod_row(i, nb, nctx, n_batch), 0, 0)),
                  pl.BlockSpec((1, d), const), pl.BlockSpec((1, d), const),
                  pl.BlockSpec((d, 4 * LANES), const), pl.BlockSpec((n_experts, LANES), const)],
        out_specs=[pl.BlockSpec((tm, d), row), pl.BlockSpec((tm, d), row),
                   pl.BlockSpec((SUBLANES, tm), lambda i: (0, i)), pl.BlockSpec((SUBLANES, tm), lambda i: (0, i))],
        out_shape=[jax.ShapeDtypeStruct((t, d), F32), jax.ShapeDtypeStruct((t, d), F32),
                   jax.ShapeDtypeStruct((SUBLANES, t), jnp.int32), jax.ShapeDtypeStruct((SUBLANES, t), F32)],
        compiler_params=_cparams(("parallel",)),
        name="out_proj_router",
    )(oaf, oar, p, odf, odr, p, yb, yc, norm_a, norm_d, w_out, x, mods, ln_g, ln_b, w_router_p, b_router)


def _rank_kernel(ei_ref, su_ref, rank_ref, cnt_ref, base_ref, *, n_experts, tm):
    i = pl.program_id(0)

    @pl.when(i == 0)
    def _():
        base_ref[...] = jnp.zeros_like(base_ref)

    eid = lax.broadcasted_iota(jnp.int32, (n_experts, tm), 0)
    o1 = (eid == ei_ref[0:1, :]).astype(F32)
    o2 = (eid == ei_ref[1:2, :]).astype(F32)
    cnt = o1 + o2
    before = _dot(cnt.astype(BF16), su_ref[...]) + base_ref[:, 0:1]
    r1 = jnp.sum(o1 * before, axis=0, keepdims=True)
    r2 = jnp.sum(o2 * before, axis=0, keepdims=True)
    rank_ref[...] = jnp.concatenate([r1, r2, jnp.zeros((SUBLANES - TOP_K, tm), F32)], axis=0).astype(jnp.int32)
    base_ref[...] = base_ref[...] + jnp.sum(cnt, axis=1, keepdims=True)
    cnt_ref[...] = base_ref[...]


def _slot_ranks(ei, n_experts, tm):
    t = ei.shape[1]
    r = np.arange(tm)
    su = jnp.asarray((r[:, None] < r[None, :]).astype(np.float32), BF16)
    return pl.pallas_call(
        functools.partial(_rank_kernel, n_experts=n_experts, tm=tm),
        grid=(t // tm,),
        in_specs=[pl.BlockSpec((SUBLANES, tm), lambda i: (0, i)), pl.BlockSpec((tm, tm), lambda i: (0, 0))],
        out_specs=[pl.BlockSpec((SUBLANES, tm), lambda i: (0, i)), pl.BlockSpec((n_experts, LANES), lambda i: (0, 0))],
        out_shape=[jax.ShapeDtypeStruct((SUBLANES, t), jnp.int32), jax.ShapeDtypeStruct((n_experts, LANES), F32)],
        scratch_shapes=[pltpu.VMEM((n_experts, LANES), F32)],
        compiler_params=_cparams(("arbitrary",)),
        name="moe_slot_ranks",
    )(ei, su)


def _row_copy(src_ref, src_row, dst_ref, dst_row, sem):
    return pltpu.make_async_copy(src_ref.at[pl.ds(src_row, 1)], dst_ref.at[pl.ds(dst_row, 1)], sem)


def _dispatch_kernel(dest_ref, zs_ref, na_ref, h_ref, xs_ref, zero_ref, sem, zsem, *, tm, t, n_experts, tmx, n_blocks):
    base = pl.program_id(0) * tm

    @pl.when(pl.program_id(0) == 0)
    def _():
        zero_ref[...] = jnp.zeros_like(zero_ref)
        for e in range(n_experts):
            fill = pltpu.make_async_copy(zero_ref, xs_ref.at[pl.ds(pl.multiple_of(zs_ref[e], FILL_ALIGN),
                                                                  tmx + FILL_ALIGN)], zsem)
            fill.start()
            fill.wait()

        def tail(b, carry):
            fill = pltpu.make_async_copy(zero_ref.at[pl.ds(0, tmx)],
                                         xs_ref.at[pl.ds(pl.multiple_of(b * tmx, tmx), tmx)], zsem)
            fill.start()
            fill.wait()
            return carry

        lax.fori_loop(na_ref[0], n_blocks, tail, 0)

    for r in range(tm):
        for kk in range(TOP_K):
            _row_copy(h_ref, r, xs_ref, dest_ref[kk * t + base + r], sem).start()
    for kk in range(TOP_K):
        pltpu.make_async_copy(h_ref, xs_ref.at[pl.ds(0, tm)], sem).wait()


def _dispatch(dest, zero_start, n_active, h2, n_slots, tm, tmx):
    t, d = h2.shape
    n_experts = zero_start.shape[0]
    return pl.pallas_call(
        functools.partial(_dispatch_kernel, tm=tm, t=t, n_experts=n_experts, tmx=tmx, n_blocks=n_slots // tmx),
        grid_spec=pltpu.PrefetchScalarGridSpec(
            num_scalar_prefetch=3,
            grid=(t // tm,),
            in_specs=[pl.BlockSpec((tm, d), lambda i, dest, zs, na: (i, 0))],
            out_specs=pl.BlockSpec(memory_space=pl.ANY),
            scratch_shapes=[pltpu.VMEM((tmx + FILL_ALIGN, d), h2.dtype), pltpu.SemaphoreType.DMA(()),
                            pltpu.SemaphoreType.DMA(())]),
        out_shape=jax.ShapeDtypeStruct((n_slots, d), h2.dtype),
        compiler_params=_cparams(("arbitrary",)),
        name="moe_dispatch",
    )(dest, zero_start, n_active, h2)


def _expert_kernel(be_ref, na_ref, x_ref, w1_ref, w3_ref, w2_ref, y_ref):
    i = pl.program_id(0)

    @pl.when(i < na_ref[0])
    def _():
        xb = x_ref[...].astype(BF16)
        h1 = _dot(xb, w1_ref[0, 0])
        h3 = _dot(xb, w3_ref[0, 0])
        y_ref[...] = _dot((_silu(h1) * h3).astype(BF16), w2_ref[0, 0])

    @pl.when(i >= na_ref[0])
    def _():
        y_ref[...] = jnp.zeros_like(y_ref)


def _experts(block_expert, n_active, xs, w1, w3, w2, layer, tmx):
    n_slots, d = xs.shape
    f = w1.shape[3]
    return pl.pallas_call(
        _expert_kernel,
        grid_spec=pltpu.PrefetchScalarGridSpec(
            num_scalar_prefetch=2,
            grid=(n_slots // tmx,),
            in_specs=[pl.BlockSpec((tmx, d), lambda i, be, na: (jnp.minimum(i, na[0] - 1), 0)),
                      pl.BlockSpec((1, 1, d, f), lambda i, be, na: (layer, be[i], 0, 0)),
                      pl.BlockSpec((1, 1, d, f), lambda i, be, na: (layer, be[i], 0, 0)),
                      pl.BlockSpec((1, 1, f, d), lambda i, be, na: (layer, be[i], 0, 0))],
            out_specs=pl.BlockSpec((tmx, d), lambda i, be, na: (i, 0))),
        out_shape=jax.ShapeDtypeStruct((n_slots, d), F32),
        compiler_params=_cparams(("arbitrary",)),
        name="moe_experts",
    )(block_expert, n_active, xs, w1, w3, w2)


def _combine_kernel(dest_ref, ys_ref, ew_ref, x1_ref, m_ref, lng, lnb, mn_ref, x2_ref, *rest,
                    tm, t, alpha, block_of_step, emit_next):
    hn_ref = rest[0] if emit_next else None
    g_ref, sem = rest[-2:]
    i = pl.program_id(0)
    n = pl.num_programs(0)

    def issue(step, slot):
        base = block_of_step(step) * tm

        for r in range(tm):
            for kk in range(TOP_K):
                _row_copy(ys_ref, dest_ref[kk * t + base + r], g_ref.at[slot, kk], r, sem.at[slot]).start()

    @pl.when(i == 0)
    def _():
        issue(0, 0)

    @pl.when(i + 1 < n)
    def _():
        issue(i + 1, (i + 1) % 2)

    slot = i % 2
    for kk in range(TOP_K):
        pltpu.make_async_copy(ys_ref.at[pl.ds(0, tm)], g_ref.at[slot, kk], sem.at[slot]).wait()
    row = lax.broadcasted_iota(jnp.int32, (tm, tm), 0)
    col = lax.broadcasted_iota(jnp.int32, (tm, tm), 1)
    w_t = _dot_nt((row == col).astype(F32), ew_ref[...], HIGHEST)
    f = w_t[:, 0:1] * g_ref[slot, 0] + w_t[:, 1:2] * g_ref[slot, 1]
    x2 = _ln(alpha * x1_ref[...] + m_ref[0, 5:6, :] * f) * lng[...] + lnb[...]
    x2_ref[...] = x2
    if emit_next:
        hn_ref[...] = (_ln(x2) * (1.0 + mn_ref[0, 1:2, :]) + mn_ref[0, 0:1, :]).astype(hn_ref.dtype)


def _combine(dest, ys, ew, x1, mods, ln_g, ln_b, mods_next, alpha, tm, nb, nctx, n_batch, latent_only):
    t, d = x1.shape
    nlat = nb - nctx
    if latent_only:
        n_steps = n_batch * nlat
        block_of_step = lambda i: (i // nlat) * nb + nctx + i % nlat
    else:
        n_steps = t // tm
        block_of_step = lambda i: i
    mrow = lambda i, dest: (_mod_row(block_of_step(i), nb, nctx, n_batch), 0, 0)
    const = lambda i, dest: (0, 0)
    out_specs = [pl.BlockSpec((tm, d), lambda i, dest: (i, 0))]
    out_shape = [jax.ShapeDtypeStruct((n_steps * tm, d), F32)]
    if not latent_only:
        out_specs.append(pl.BlockSpec((tm, d), lambda i, dest: (i, 0)))
        out_shape.append(jax.ShapeDtypeStruct((t, d), BF16))
    return pl.pallas_call(
        functools.partial(_combine_kernel, tm=tm, t=t, alpha=alpha, block_of_step=block_of_step,
                          emit_next=not latent_only),
        grid_spec=pltpu.PrefetchScalarGridSpec(
            num_scalar_prefetch=1,
            grid=(n_steps,),
            in_specs=[pl.BlockSpec(memory_space=pl.ANY),
                      pl.BlockSpec((SUBLANES, tm), lambda i, dest: (0, block_of_step(i))),
                      pl.BlockSpec((tm, d), lambda i, dest: (block_of_step(i), 0)),
                      pl.BlockSpec((1, 6, d), mrow),
                      pl.BlockSpec((1, d), const), pl.BlockSpec((1, d), const),
                      pl.BlockSpec((1, 6, d), mrow)],
            out_specs=out_specs,
            scratch_shapes=[pltpu.VMEM((2, TOP_K, tm, d), F32), pltpu.SemaphoreType.DMA((2,))]),
        out_shape=out_shape,
        compiler_params=_cparams(("arbitrary",)),
        name="moe_combine",
    )(dest, ys, ew, x1, mods, ln_g, ln_b, mods_next)


def _axial_tables(seq, ctx_len):
    rows = seq // GRID_W
    row = jnp.repeat(jnp.arange(rows, dtype=F32), GRID_W)
    col = jnp.tile(jnp.arange(GRID_W, dtype=F32), rows)
    n_freq = HEAD_DIM // 4
    inv_freq = ROPE_THETA ** (-jnp.arange(n_freq, dtype=F32) / n_freq)
    ang = jnp.concatenate([row[:, None] * inv_freq, col[:, None] * inv_freq], axis=-1)
    cos, sin = jnp.cos(ang), jnp.sin(ang)
    cosf = jnp.concatenate([cos, cos], axis=-1)
    sinf = jnp.concatenate([-sin, sin], axis=-1)
    cosf = jnp.concatenate([jnp.ones((ctx_len, HEAD_DIM), F32), cosf], axis=0)
    sinf = jnp.concatenate([jnp.zeros((ctx_len, HEAD_DIM), F32), sinf], axis=0)
    return cosf, sinf


def kernel(x, c, ctx, c_ctx, w_ada, b_ada, ln_g, ln_b, w_in, conv_a, a_log, dt_bias, norm_a, qk_norm_b, ws_c, bs_c, lb_d, norm_d, w_out, w_router, b_router, w1, w3, w2):
    n_batch, seq, d = x.shape
    ctx_len = ctx.shape[1]
    n_layers = w_in.shape[0]
    n_experts = w_router.shape[1]
    mixw = d // N_MIXERS
    n_heads = mixw // HEAD_DIM
    tb = ctx_len + seq
    t = n_batch * tb
    alpha = (2.0 * n_layers) ** 0.25
    tm = _pick(math.gcd(ctx_len, seq), (256, 128))
    nb, nctx = tb // tm, ctx_len // tm

    col = {"a_qkv": 0, "a_gate": 3 * mixw, "d_f": 4 * mixw, "b_q": 6 * mixw, "b_kv": 7 * mixw, "c_u": 8 * mixw,
           "c_v": 9 * mixw, "d_q": 10 * mixw, "d_i": 11 * mixw, "d_gate": 12 * mixw, "small": 13 * mixw}
    n_proj = 13 * mixw + 256
    src = np.cumsum([0, 3 * mixw, mixw, 2 * n_heads, 2 * n_heads, mixw, mixw, mixw, mixw, mixw, mixw, mixw, 2 * mixw])
    s_qkv, s_ga, s_beta, s_dec, s_bq, s_bkv, s_cu, s_cv, s_dq, s_di, s_dg, s_df, s_end = [int(v) for v in src]

    def permute_w_in(w):
        pad = jnp.zeros((d, n_proj - 13 * mixw - 4 * n_heads), w.dtype)
        return jnp.concatenate([w[:, s_qkv:s_beta], w[:, s_df:s_end], w[:, s_bq:s_df], w[:, s_beta:s_bq], pad],
                               axis=1).astype(BF16)

    assert n_batch + 1 <= SUBLANES
    cc = jnp.zeros((SUBLANES, d), F32).at[:n_batch].set(c).at[n_batch].set(c_ctx)
    mods_all = _mod_vectors(cc, w_ada, b_ada)[:, :n_batch + 1].reshape(n_layers, n_batch + 1, 6, d)

    soft = jax.nn.softmax(lb_d.astype(F32), axis=0)
    lb_all = jnp.cumsum(soft, axis=0) - soft[0]
    cosf, sinf = _axial_tables(seq, ctx_len)
    gate_lanes = jnp.zeros((1, LANES), F32)
    wr_f = w_router.astype(F32)
    wr_hi = wr_f.astype(BF16)
    wr_r = wr_f - wr_hi.astype(F32)
    wr_mid = wr_r.astype(BF16)
    wr_lo = (wr_r - wr_mid.astype(F32)).astype(BF16)
    w_router_p = jnp.zeros((d, 4 * LANES), BF16)
    for blk, part in enumerate((wr_hi, wr_mid, wr_lo)):
        w_router_p = w_router_p.at[:, blk * LANES:blk * LANES + n_experts].set(part)
    b_router_c = jnp.broadcast_to(b_router.astype(F32)[:, None], (n_experts, LANES))

    tmx = EXPERT_BLOCK_ROWS
    n_assign = t * TOP_K
    n_slots = (-(-n_assign // tmx) + n_experts + 2) * tmx

    w1_b, w3_b, w2_b = w1.astype(BF16), w3.astype(BF16), w2.astype(BF16)
    xu = jnp.concatenate([ctx, x], axis=1).reshape(t, d)
    h = _ln_modulate(xu, mods_all[0], tm, nb, nctx, n_batch)
    for l in range(n_layers):
        mods = mods_all[l]
        p = _matmul(h, permute_w_in(w_in[l]))
        neg_a = gate_lanes.at[0, 2 * n_heads:4 * n_heads].set(-jnp.exp(a_log[l].astype(F32)).reshape(-1))
        dtb = gate_lanes.at[0, 2 * n_heads:4 * n_heads].set(dt_bias[l].astype(F32).reshape(-1))
        prep_a = _a_prep(p, conv_a[l], neg_a, dtb, tm, nb, nctx, mixw, col["small"])
        oaf, oar = _a_scan(prep_a, n_batch, tb, ctx_len, mixw)
        qb, kb, vb = _b_prep(p, qk_norm_b[l], cosf, sinf, tm, nb, mixw, col["b_q"], col["b_kv"])
        yb = _attention(qb, kb, vb, n_batch, tb, ctx_len, mixw)
        bs_t = jnp.zeros((CHUNK_C, LANES), F32).at[:, :bs_c.shape[1]].set(bs_c[l].T)
        yc = _mixer_c(p, ws_c[l].astype(BF16), bs_t, tm, mixw, col["c_u"], col["c_v"])
        odf, odr = _d_scan(p, lb_all[l][None, :], n_batch, tb, ctx_len, mixw, col["d_q"], col["d_i"], col["d_f"])
        x1, h2, ei, ew = _outproj(oaf, oar, odf, odr, p, yb, yc, norm_a[l][None, :], norm_d[l][None, :],
                                  w_out[l].astype(BF16), xu, mods, ln_g[l, 0][None, :], ln_b[l, 0][None, :],
                                  w_router_p, b_router_c, alpha, tm, nb, nctx, n_batch, mixw,
                                  col["a_gate"], col["d_gate"])
        rank, counts = _slot_ranks(ei, n_experts, tm)
        cnt = counts[:, 0].astype(jnp.int32)
        padded = (cnt + tmx - 1) // tmx * tmx
        pad_end = jnp.cumsum(padded)
        pad_start = pad_end - padded
        eids = jnp.arange(n_experts, dtype=jnp.int32)[:, None, None]
        dest = (rank[:TOP_K] + jnp.sum(jnp.where(ei[None, :TOP_K] == eids, pad_start[:, None, None], 0),
                                       axis=0)).reshape(-1)
        n_blocks = n_slots // tmx
        block_start = jnp.arange(n_blocks, dtype=jnp.int32) * tmx
        block_expert = jnp.minimum(jnp.sum((pad_end[None, :] <= block_start[:, None]).astype(jnp.int32), axis=1),
                                   n_experts - 1)
        n_active = (pad_end[-1:] // tmx).astype(jnp.int32)
        fill_start = (pad_start + cnt) // FILL_ALIGN * FILL_ALIGN
        xs = _dispatch(dest, fill_start, n_active, h2, n_slots, tm, tmx)
        ys = _experts(block_expert, n_active, xs, w1_b, w3_b, w2_b, l, tmx)
        last = l == n_layers - 1
        res = _combine(dest, ys, ew, x1, mods, ln_g[l, 1][None, :], ln_b[l, 1][None, :],
                       mods_all[min(l + 1, n_layers - 1)], alpha, tm, nb, nctx, n_batch, latent_only=last)
        if last:
            return res[0].reshape(n_batch, seq, d)
        xu, h = res
```

```python
import functools
import math

import numpy as np
import jax
import jax.numpy as jnp
from jax import lax
from jax.experimental import pallas as pl
from jax.experimental.pallas import tpu as pltpu

F32 = jnp.float32
BF16 = jnp.bfloat16
HIGHEST = lax.Precision.HIGHEST

HEAD_DIM = 128
N_MIXERS = 4
CONV_W = 5
CHUNK = 64
STEP_ROWS = 2 * CHUNK
SOLVE_ROWS = 2 * CHUNK
CHUNK_C = 128
GRID_W = 64
ROPE_THETA = 10000.0
N_EXPERT_GROUPS = 4
TOP_K = 2
EPS = 1e-6
LOG2_E = math.log2(math.e)
GATE_CLIP = 30.0
N_LEVELS = 6
LANES = 128
SUBLANES = 8
HALO = SUBLANES
FILL_ALIGN = SUBLANES
EXPERT_BLOCK_ROWS = 256
V7X_VMEM_LIMIT = 56 * 1024 * 1024


def _pick(n, cands):
    for c in cands:
        if n % c == 0:
            return c
    raise ValueError(f"no tile in {cands} divides {n}")


def _cparams(sem, vmem=None):
    return pltpu.CompilerParams(dimension_semantics=sem, vmem_limit_bytes=vmem or V7X_VMEM_LIMIT)


def _dot(a, b, precision=None):
    return jnp.dot(a, b, preferred_element_type=F32, precision=precision)


def _dot_nt(a, b, precision=None):
    return lax.dot_general(a, b, (((1,), (1,)), ((), ())), preferred_element_type=F32, precision=precision)


def _dot_tn(a, b, precision=None):
    return lax.dot_general(a, b, (((0,), (0,)), ((), ())), preferred_element_type=F32, precision=precision)


def _sigmoid(x):
    return 1.0 / (1.0 + jnp.exp(-x))


def _silu(x):
    return x * _sigmoid(x)


def _gelu_tanh(x):
    return 0.5 * x * (1.0 + jnp.tanh(math.sqrt(2.0 / math.pi) * (x + 0.044715 * (x * x * x))))


def _ln(x):
    mu = jnp.mean(x, axis=-1, keepdims=True)
    xc = x - mu
    return xc * lax.rsqrt(jnp.mean(xc * xc, axis=-1, keepdims=True) + EPS)


def _mod_row(i, nb, nctx, n_batch):
    return jnp.where(i % nb < nctx, n_batch, i // nb)


def _mod_kernel(c_ref, w_ref, b_ref, o_ref):
    o_ref[0] = _dot(_silu(c_ref[...]), w_ref[0], HIGHEST) + b_ref[0]


def _mod_vectors(cc, w_ada, b_ada):
    n_layers, d, n6 = w_ada.shape
    tn = _pick(n6, (1024, 512, 128))
    return pl.pallas_call(
        _mod_kernel,
        grid=(n_layers, n6 // tn),
        in_specs=[pl.BlockSpec((SUBLANES, d), lambda l, j: (0, 0)),
                  pl.BlockSpec((1, d, tn), lambda l, j: (l, 0, j)),
                  pl.BlockSpec((1, 1, tn), lambda l, j: (l, 0, j))],
        out_specs=pl.BlockSpec((1, SUBLANES, tn), lambda l, j: (l, 0, j)),
        out_shape=jax.ShapeDtypeStruct((n_layers, SUBLANES, n6), F32),
        compiler_params=_cparams(("parallel", "parallel")),
        name="mod_vectors",
    )(cc, w_ada, b_ada.reshape(n_layers, 1, n6))


def _lnmod_kernel(x_ref, m_ref, h_ref):
    y = _ln(x_ref[...])
    h_ref[...] = (y * (1.0 + m_ref[0, 1:2, :]) + m_ref[0, 0:1, :]).astype(h_ref.dtype)


def _ln_modulate(x, mods, tm, nb, nctx, n_batch):
    t, d = x.shape
    return pl.pallas_call(
        _lnmod_kernel,
        grid=(t // tm,),
        in_specs=[pl.BlockSpec((tm, d), lambda i: (i, 0)),
                  pl.BlockSpec((1, 6, d), lambda i: (_mod_row(i, nb, nctx, n_batch), 0, 0))],
        out_specs=pl.BlockSpec((tm, d), lambda i: (i, 0)),
        out_shape=jax.ShapeDtypeStruct((t, d), BF16),
        compiler_params=_cparams(("parallel",)),
        name="ln_modulate",
    )(x, mods)


def _matmul_kernel(a_ref, w_ref, o_ref):
    o_ref[...] = _dot(a_ref[...], w_ref[...]).astype(o_ref.dtype)


def _matmul(a, w, out_dtype=F32):
    m, k = a.shape
    n = w.shape[1]
    tm = _pick(m, (1536, 1024, 768, 512, 256, 128))
    tn = _pick(n, (1152, 1024, 768, 512, 256, 128))
    return pl.pallas_call(
        _matmul_kernel,
        grid=(m // tm, n // tn),
        in_specs=[pl.BlockSpec((tm, k), lambda i, j: (i, 0)),
                  pl.BlockSpec((k, tn), lambda i, j: (0, j))],
        out_specs=pl.BlockSpec((tm, tn), lambda i, j: (i, j)),
        out_shape=jax.ShapeDtypeStruct((m, n), out_dtype),
        compiler_params=_cparams(("parallel", "parallel")),
        name="in_proj",
    )(a, w)


def _unit_tri_inverses(lows, eye, pair_refs):
    ts = [eye - low * pair_ref[0] for low, pair_ref in zip(lows, pair_refs)]
    for lvl in range(1, N_LEVELS):
        tbs = [t.astype(BF16) for t in ts]
        xs = [_dot(tb16, (low * pair_ref[lvl]).astype(BF16)) for tb16, low, pair_ref in zip(tbs, lows, pair_refs)]
        ts = [t - _dot(x.astype(BF16), tb16) for t, x, tb16 in zip(ts, xs, tbs)]
    return ts


def _a_prep_kernel(x_ref, xp_ref, xn_ref, sm_ref, cw_ref, na_ref, dtb_ref, trf_ref, trr_ref, pairf_ref, pairr_ref,
                   uf_ref, wqf_ref, kdf_ref, atf_ref, ur_ref, wqr_ref, kdr_ref, atr_ref, eg_ref, ext_ref,
                   *, tm, nb, nctx, mixw):
    i = pl.program_id(0)
    j = i % nb
    first = jnp.logical_or(j == 0, j == nctx)
    last = jnp.logical_or(j == nctx - 1, j == nb - 1)
    ext_ref[HALO:HALO + tm, :] = x_ref[...]
    ext_ref[0:HALO, :] = jnp.where(first, 0.0, xp_ref[...])
    ext_ref[HALO + tm:2 * HALO + tm, :] = jnp.where(last, 0.0, xn_ref[...])
    off = HALO - CONV_W // 2
    acc = cw_ref[0:1, :] * ext_ref[off:off + tm, :]
    for t in range(1, CONV_W):
        acc = acc + cw_ref[t:t + 1, :] * ext_ref[off + t:off + t + tm, :]
    y = _silu(acc)
    n_heads = mixw // HEAD_DIM
    sm = sm_ref[...]
    lane = lax.broadcasted_iota(jnp.int32, sm.shape, 1)
    zz = sm + dtb_ref[...]
    softplus = jnp.maximum(zz, 0.0) + jnp.log(1.0 + jnp.exp(-jnp.abs(zz)))
    g = jnp.where(jnp.logical_and(lane >= 2 * n_heads, lane < 4 * n_heads), na_ref[...] * softplus, 0.0)
    beta_all = _sigmoid(sm)
    tri_f = trf_ref[...]
    tri_r = trr_ref[...]
    cs_f = _dot(tri_f, g, HIGHEST)
    cs_r = _dot(tri_r, g, HIGHEST)
    gcum = jnp.where(lane < 3 * n_heads, cs_f, cs_r)
    gtot = _dot(jnp.maximum(tri_f, tri_r), g, HIGHEST)
    eg_ref[...] = jnp.exp(gtot)
    gcum_t = gcum.T
    sr = SOLVE_ROWS
    row = lax.broadcasted_iota(jnp.int32, (sr, sr), 0)
    col = lax.broadcasted_iota(jnp.int32, (sr, sr), 1)
    eye = (row == col).astype(F32)
    incls = (tri_f[0:sr, 0:sr] > 0.5, tri_r[0:sr, 0:sr] > 0.5)
    pairs = ([pairf_ref[lvl, 0:sr, 0:sr] for lvl in range(N_LEVELS)],
             [pairr_ref[lvl, 0:sr, 0:sr] for lvl in range(N_LEVELS)])
    outs = ((uf_ref, wqf_ref, kdf_ref, atf_ref), (ur_ref, wqr_ref, kdr_ref, atr_ref))
    items = []
    for sb in range(tm // sr):
        rs = sb * sr
        for h in range(n_heads):
            c0 = h * HEAD_DIM
            qh = y[rs:rs + sr, c0:c0 + HEAD_DIM]
            kh = y[rs:rs + sr, mixw + c0:mixw + c0 + HEAD_DIM]
            vh = y[rs:rs + sr, 2 * mixw + c0:2 * mixw + c0 + HEAD_DIM]
            qh = qh * (lax.rsqrt(jnp.sum(qh * qh, axis=-1, keepdims=True) + EPS) * HEAD_DIM ** -0.5)
            kh = kh * lax.rsqrt(jnp.sum(kh * kh, axis=-1, keepdims=True) + EPS)
            kb16 = kh.astype(BF16)
            qk = _dot_nt(qh.astype(BF16), kb16)
            for d in range(2):
                lb = d * n_heads + h
                lg = (2 + d) * n_heads + h
                beta = beta_all[rs:rs + sr, lb:lb + 1]
                gc = gcum[rs:rs + sr, lg:lg + 1]
                incl = incls[d]
                decay = jnp.where(incl, jnp.exp(jnp.where(incl, gc - gcum_t[lg:lg + 1, rs:rs + sr], 0.0)), 0.0)
                kbeta = kh * beta
                low = jnp.where(row == col, 0.0, _dot_nt(kbeta.astype(BF16), kb16) * decay)
                rhs = jnp.concatenate([vh * beta, kbeta * jnp.exp(gc)], axis=1).astype(BF16)
                items.append(dict(h=h, d=d, rs=rs, low=low, rhs=rhs, attn=(qk * decay).astype(BF16),
                                  qd=(qh * jnp.exp(gc)).astype(BF16),
                                  kd=(kh * jnp.exp(gtot[rs:rs + sr, lg:lg + 1] - gc)).astype(BF16)))
    tinvs = _unit_tri_inverses([it["low"] for it in items], eye, [pairs[it["d"]] for it in items])
    for it, tinv in zip(items, tinvs):
        u_ref, wq_ref, kd_ref, at_ref = outs[it["d"]]
        h, rs = it["h"], it["rs"]
        c0 = h * HEAD_DIM
        sol = _dot(tinv.astype(BF16), it["rhs"])
        u_ref[rs:rs + sr, c0:c0 + HEAD_DIM] = sol[:, :HEAD_DIM]
        w = sol[:, HEAD_DIM:].astype(BF16)
        kd_ref[rs:rs + sr, c0:c0 + HEAD_DIM] = it["kd"]
        for ci in range(sr // CHUNK):
            r0 = ci * CHUNK
            g0 = rs + r0
            wq_ref[2 * g0:2 * g0 + CHUNK, c0:c0 + HEAD_DIM] = w[r0:r0 + CHUNK]
            wq_ref[2 * g0 + CHUNK:2 * g0 + 2 * CHUNK, c0:c0 + HEAD_DIM] = it["qd"][r0:r0 + CHUNK]
            at_ref[g0:g0 + CHUNK, h * CHUNK:(h + 1) * CHUNK] = it["attn"][r0:r0 + CHUNK, r0:r0 + CHUNK]


def _a_prep(p, conv_w, neg_a, dtb, tm, nb, nctx, mixw, col_small):
    t = p.shape[0]
    c3 = 3 * mixw
    n_heads = mixw // HEAD_DIM
    r = np.arange(tm)
    same = (r[:, None] // CHUNK) == (r[None, :] // CHUNK)
    tri_f = jnp.asarray((same & (r[None, :] <= r[:, None])).astype(np.float32))
    tri_r = jnp.asarray((same & (r[None, :] >= r[:, None])).astype(np.float32))
    pair_f = []
    for lvl in range(N_LEVELS):
        s_blk = 1 << lvl
        joined = ((r[:, None] // (2 * s_blk)) == (r[None, :] // (2 * s_blk))) \
            & ((r[:, None] % (2 * s_blk)) >= s_blk) & ((r[None, :] % (2 * s_blk)) < s_blk)
        pair_f.append(joined.astype(np.float32))
    pair_f = np.stack(pair_f)
    pair_r = jnp.asarray(np.transpose(pair_f, (0, 2, 1)))
    pair_f = jnp.asarray(pair_f)
    hb = tm // HALO
    last_hb = t // HALO - 1
    kern = functools.partial(_a_prep_kernel, tm=tm, nb=nb, nctx=nctx, mixw=mixw)
    per_dir_specs = [pl.BlockSpec((tm, mixw), lambda i: (i, 0)),
                     pl.BlockSpec((2 * tm, mixw), lambda i: (i, 0)),
                     pl.BlockSpec((tm, mixw), lambda i: (i, 0)),
                     pl.BlockSpec((tm, n_heads * CHUNK), lambda i: (i, 0))]
    per_dir_shapes = [jax.ShapeDtypeStruct((t, mixw), F32), jax.ShapeDtypeStruct((2 * t, mixw), BF16),
                      jax.ShapeDtypeStruct((t, mixw), BF16), jax.ShapeDtypeStruct((t, n_heads * CHUNK), BF16)]
    return pl.pallas_call(
        kern,
        grid=(t // tm,),
        in_specs=[pl.BlockSpec((tm, c3), lambda i: (i, 0)),
                  pl.BlockSpec((HALO, c3), lambda i: (jnp.maximum(i * hb - 1, 0), 0)),
                  pl.BlockSpec((HALO, c3), lambda i: (jnp.minimum((i + 1) * hb, last_hb), 0)),
                  pl.BlockSpec((tm, LANES), lambda i: (i, col_small // LANES)),
                  pl.BlockSpec((CONV_W, c3), lambda i: (0, 0)),
                  pl.BlockSpec((1, LANES), lambda i: (0, 0)),
                  pl.BlockSpec((1, LANES), lambda i: (0, 0)),
                  pl.BlockSpec((tm, tm), lambda i: (0, 0)),
                  pl.BlockSpec((tm, tm), lambda i: (0, 0)),
                  pl.BlockSpec((N_LEVELS, tm, tm), lambda i: (0, 0, 0)),
                  pl.BlockSpec((N_LEVELS, tm, tm), lambda i: (0, 0, 0))],
        out_specs=per_dir_specs * 2 + [pl.BlockSpec((tm, LANES), lambda i: (i, 0))],
        out_shape=per_dir_shapes * 2 + [jax.ShapeDtypeStruct((t, LANES), F32)],
        scratch_shapes=[pltpu.VMEM((tm + 2 * HALO, c3), F32)],
        compiler_params=_cparams(("parallel",)),
        name="deltanet_prep",
    )(p, p, p, p, conv_w, neg_a, dtb, tri_f, tri_r, pair_f, pair_r)


def _delta_steps(dirs, s_ref, sbd_ref, n_heads):
    mixw = n_heads * HEAD_DIM
    lane_head = lax.broadcasted_iota(jnp.int32, (CHUNK, mixw), 1) // HEAD_DIM
    for pos in range(2):
        r0s = [(pos if d == 0 else 1 - pos) * CHUNK for d in range(2)]
        rs = [_dot(dirs[d][1][2 * r0s[d]:2 * r0s[d] + 2 * CHUNK, :], sbd_ref[d]) for d in range(2)]
        v_news = [dirs[d][0][r0s[d]:r0s[d] + CHUNK, :] - rs[d][:CHUNK] for d in range(2)]
        v_stacks = [jnp.concatenate([jnp.where(lane_head == h, v_news[d], 0.0) for h in range(n_heads)],
                                    axis=0).astype(BF16) for d in range(2)]
        for d in range(2):
            dirs[d][5][r0s[d]:r0s[d] + CHUNK, :] = (rs[d][CHUNK:]
                                                    + _dot(dirs[d][3][r0s[d]:r0s[d] + CHUNK, :], v_stacks[d]))
        vnbs = [v.astype(BF16) for v in v_news]
        for h in range(n_heads):
            c0 = h * HEAD_DIM
            for d in range(2):
                lg = (2 + d) * n_heads + h
                r0 = r0s[d]
                s_new = (s_ref[d, h] * dirs[d][4][r0:r0 + 1, lg:lg + 1]
                         + _dot_tn(dirs[d][2][r0:r0 + CHUNK, c0:c0 + HEAD_DIM], vnbs[d][:, c0:c0 + HEAD_DIM]))
                s_ref[d, h] = s_new
                sbd_ref[d, c0:c0 + HEAD_DIM, c0:c0 + HEAD_DIM] = s_new.astype(BF16)


def _seq_block_maps(nsb, nctx_sb):
    def fwd(b, n):
        return b * nsb + n

    def rev(b, n):
        return b * nsb + jnp.where(n < nctx_sb, nctx_sb - 1 - n, nsb - 1 - (n - nctx_sb))

    return fwd, rev


def _b_prep_kernel(q_ref, kv_ref, gain_ref, cos_ref, sin_ref, qo_ref, ko_ref, vo_ref, *, mixw):
    n_heads = mixw // HEAD_DIM
    n_kv = n_heads // 2
    cosf = cos_ref[...]
    sinf = sin_ref[...]

    def norm_rope(xh, gain, scale):
        yh = xh * lax.rsqrt(jnp.mean(xh * xh, axis=-1, keepdims=True) + EPS) * gain
        return (yh * cosf + pltpu.roll(yh, HEAD_DIM // 2, 1) * sinf) * scale

    for h in range(n_heads):
        c0 = h * HEAD_DIM
        qo_ref[:, c0:c0 + HEAD_DIM] = norm_rope(q_ref[:, c0:c0 + HEAD_DIM], gain_ref[0:1, :],
                                                HEAD_DIM ** -0.5 * LOG2_E).astype(qo_ref.dtype)
    for h in range(n_kv):
        c0 = h * HEAD_DIM
        ko_ref[:, c0:c0 + HEAD_DIM] = norm_rope(kv_ref[:, c0:c0 + HEAD_DIM], gain_ref[1:2, :], 1.0).astype(ko_ref.dtype)
    ones = jnp.ones((q_ref.shape[0], HEAD_DIM), vo_ref.dtype)
    for h in range(n_kv):
        c0 = (n_kv + h) * HEAD_DIM
        vo_ref[:, 2 * h * HEAD_DIM:(2 * h + 1) * HEAD_DIM] = kv_ref[:, c0:c0 + HEAD_DIM].astype(vo_ref.dtype)
        vo_ref[:, (2 * h + 1) * HEAD_DIM:(2 * h + 2) * HEAD_DIM] = ones


def _b_prep(p, qk_gain, cosf, sinf, tm, nb, mixw, col_q, col_kv):
    t = p.shape[0]
    kvw = mixw // 2
    return pl.pallas_call(
        functools.partial(_b_prep_kernel, mixw=mixw),
        grid=(t // tm,),
        in_specs=[pl.BlockSpec((tm, mixw), lambda i: (i, col_q // mixw)),
                  pl.BlockSpec((tm, mixw), lambda i: (i, col_kv // mixw)),
                  pl.BlockSpec((2, HEAD_DIM), lambda i: (0, 0)),
                  pl.BlockSpec((tm, HEAD_DIM), lambda i: (i % nb, 0)),
                  pl.BlockSpec((tm, HEAD_DIM), lambda i: (i % nb, 0))],
        out_specs=[pl.BlockSpec((tm, mixw), lambda i: (i, 0)),
                   pl.BlockSpec((tm, kvw), lambda i: (i, 0)),
                   pl.BlockSpec((tm, 2 * kvw), lambda i: (i, 0))],
        out_shape=[jax.ShapeDtypeStruct((t, mixw), BF16),
                   jax.ShapeDtypeStruct((t, kvw), BF16),
                   jax.ShapeDtypeStruct((t, 2 * kvw), BF16)],
        compiler_params=_cparams(("parallel",)),
        name="gqa_prep",
    )(p, p, qk_gain, cosf, sinf)


def _attn_kernel(q_ref, k_ref, v_ref, o_ref, m_ref, acc_ref, sa_ref, sb_ref, *, tq, tk, nk, nctx_q, ctx_len):
    qi = pl.program_id(2)
    qb = q_ref[...]
    q2 = jnp.concatenate([qb[:, :HEAD_DIM], qb[:, HEAD_DIM:]], axis=0)
    m_ref[...] = jnp.full_like(m_ref, -1e30)
    acc_ref[...] = jnp.zeros_like(acc_ref)

    def scores(i, s_ref):
        s_ref[...] = _dot_nt(q2, k_ref[i * tk:(i + 1) * tk, :])

    def absorb(s_of, vc):
        for g in range(2):
            sl = slice(g * tq, (g + 1) * tq)
            s = s_of(sl)
            m_prev = m_ref[sl, :]
            m_new = jnp.maximum(m_prev, jnp.max(s, axis=-1, keepdims=True))
            p = jnp.exp2(s - m_new)
            acc_ref[sl, :] = jnp.exp2(m_prev - m_new) * acc_ref[sl, :] + _dot(p.astype(BF16), vc)
            m_ref[sl, :] = m_new

    def absorb_chunk(i, s_ref):
        absorb(lambda sl: s_ref[sl, :], v_ref[i * tk:(i + 1) * tk, :])

    @pl.when(qi >= nctx_q)
    def _():
        bufs = (sa_ref, sb_ref)
        scores(0, bufs[0])
        for i in range(nk):
            if i + 1 < nk:
                scores(i + 1, bufs[(i + 1) % 2])
            absorb_chunk(i, bufs[i % 2])

    @pl.when(qi < nctx_q)
    def _():
        s_ctx = _dot_nt(q2, k_ref[0:ctx_len, :])
        absorb(lambda sl: s_ctx[sl, :], v_ref[0:ctx_len, :])

    acc = acc_ref[...]
    o = acc[:, :HEAD_DIM] / acc[:, HEAD_DIM:HEAD_DIM + 1]
    o_ref[...] = jnp.concatenate([o[:tq], o[tq:]], axis=1).astype(o_ref.dtype)


def _attention(q, k, v, n_batch, tb, ctx_len, mixw):
    t = q.shape[0]
    n_kv = mixw // HEAD_DIM // 2
    tq = _pick(math.gcd(ctx_len, tb), (256, 128))
    tk = _pick(tb, (1408, 768, 1024, 512, 640, 384, 256, 128))
    nq, nk = tb // tq, tb // tk
    return pl.pallas_call(
        functools.partial(_attn_kernel, tq=tq, tk=tk, nk=nk, nctx_q=ctx_len // tq, ctx_len=ctx_len),
        grid=(n_batch, n_kv, nq),
        in_specs=[pl.BlockSpec((tq, 2 * HEAD_DIM), lambda b, j, qi: (b * nq + qi, j)),
                  pl.BlockSpec((tb, HEAD_DIM), lambda b, j, qi: (b, j)),
                  pl.BlockSpec((tb, 2 * HEAD_DIM), lambda b, j, qi: (b, j))],
        out_specs=pl.BlockSpec((tq, 2 * HEAD_DIM), lambda b, j, qi: (b * nq + qi, j)),
        out_shape=jax.ShapeDtypeStruct((t, mixw), BF16),
        scratch_shapes=[pltpu.VMEM((2 * tq, 1), F32), pltpu.VMEM((2 * tq, 2 * HEAD_DIM), F32),
                        pltpu.VMEM((2 * tq, tk), F32), pltpu.VMEM((2 * tq, tk), F32)],
        compiler_params=_cparams(("parallel", "parallel", "parallel")),
        name="gqa_attention",
    )(q, k, v)


def _c_kernel(u_ref, v_ref, ws_ref, bs_ref, o_ref, *, tm, mixw):
    n_groups = mixw // HEAD_DIM
    for ci in range(tm // CHUNK_C):
        r0 = ci * CHUNK_C
        u = _gelu_tanh(u_ref[r0:r0 + CHUNK_C, :])
        vn = _ln(_gelu_tanh(v_ref[r0:r0 + CHUNK_C, :])).astype(BF16)
        for g in range(n_groups):
            c0 = g * HEAD_DIM
            vm = _dot(ws_ref[g], vn[:, c0:c0 + HEAD_DIM]) + bs_ref[:, g:g + 1]
            o_ref[r0:r0 + CHUNK_C, c0:c0 + HEAD_DIM] = (u[:, c0:c0 + HEAD_DIM] * vm).astype(o_ref.dtype)


def _mixer_c(p, ws, bs_t, tm, mixw, col_u, col_v):
    t = p.shape[0]
    n_groups = mixw // HEAD_DIM
    return pl.pallas_call(
        functools.partial(_c_kernel, tm=tm, mixw=mixw),
        grid=(t // tm,),
        in_specs=[pl.BlockSpec((tm, mixw), lambda i: (i, col_u // mixw)),
                  pl.BlockSpec((tm, mixw), lambda i: (i, col_v // mixw)),
                  pl.BlockSpec((n_groups, CHUNK_C, CHUNK_C), lambda i: (0, 0, 0)),
                  pl.BlockSpec((CHUNK_C, LANES), lambda i: (0, 0))],
        out_specs=pl.BlockSpec((tm, mixw), lambda i: (i, 0)),
        out_shape=jax.ShapeDtypeStruct((t, mixw), BF16),
        compiler_params=_cparams(("parallel",)),
        name="gmlp",
    )(p, p, ws, bs_t)


def _hgrn_tables():
    tau = np.arange(CHUNK)
    sums_t = [tau[None, :] <= tau[:, None], tau[None, :] > tau[:, None]]
    pair_t = []
    for lvl in range(N_LEVELS):
        m = CHUNK >> (lvl + 1)
        blk = tau // (2 * m)
        upper = (tau % (2 * m)) >= m
        ref = blk * 2 * m + m - 1
        r = tau[None, :]
        w_up = upper[:, None] & (r > ref[:, None]) & (r <= tau[:, None])
        w_lo = (~upper)[:, None] & (r > tau[:, None]) & (r <= ref[:, None])
        sums_t.append(w_up | w_lo)
        pair_t.append((blk[:, None] == blk[None, :]) & upper[:, None] & (~upper)[None, :])
    sums_t = np.concatenate(sums_t, axis=0).astype(np.float32)
    pair_t = np.stack(pair_t).astype(np.float32)
    flip = tau[::-1]
    sums, pair = [], []
    for d in range(2):
        if d == 0:
            s_d, p_d = sums_t, pair_t
        else:
            s_d = sums_t.reshape(-1, CHUNK, CHUNK)[:, flip][:, :, flip].reshape(-1, CHUNK)
            p_d = pair_t[:, flip][:, :, flip]
        sums.append(np.concatenate([s_d, s_d, s_d], axis=1))
        pair.append(p_d)
    return np.stack(sums), np.stack(pair)


def _scan_kernel(uf, wqf, kdf, atf, egf, ur, wqr, kdr, atr, egr, qf, vf, ff, qr, vr, fr, lb_ref, sums_ref, pair_ref,
                 oaf_ref, oar_ref, of_ref, or_ref, s_ref, sbd_ref, st_ref, *, n_heads):
    n = pl.program_id(1)

    @pl.when(n == 0)
    def _():
        st_ref[...] = jnp.zeros_like(st_ref)
        s_ref[...] = jnp.zeros_like(s_ref)
        sbd_ref[...] = jnp.zeros_like(sbd_ref)

    c = CHUNK
    row = lax.broadcasted_iota(jnp.int32, (c, c), 0)
    col = lax.broadcasted_iota(jnp.int32, (c, c), 1)
    eye = (row == col).astype(F32)
    inst = []
    for d, (q_ref, v_ref, f_ref, o_ref) in enumerate(((qf, vf, ff, of_ref), (qr, vr, fr, or_ref))):
        for pos, ci in enumerate((0, 1) if d == 0 else (1, 0)):
            for h in range(n_heads):
                inst.append((d, pos, ci * c, h * HEAD_DIM, h, q_ref, v_ref, f_ref, o_ref))
    qs = [q_ref[r0:r0 + c, c0:c0 + HEAD_DIM] for (d, pos, r0, c0, h, q_ref, v_ref, f_ref, o_ref) in inst]
    vs = [v_ref[r0:r0 + c, c0:c0 + HEAD_DIM].astype(BF16) for (d, pos, r0, c0, h, q_ref, v_ref, f_ref, o_ref) in inst]
    es = [jnp.exp(-jnp.clip(f_ref[r0:r0 + c, c0:c0 + HEAD_DIM], -GATE_CLIP, GATE_CLIP))
          for (d, pos, r0, c0, h, q_ref, v_ref, f_ref, o_ref) in inst]
    sgs = [1.0 / (1.0 + e) for e in es]
    lbs = [lb_ref[0:1, it[3]:it[3] + HEAD_DIM] for it in inst]
    logfs = [jnp.log(lb + (1.0 - lb) * sg) for lb, sg in zip(lbs, sgs)]
    ks = [(1.0 - lb) * (e * sg) for lb, e, sg in zip(lbs, es, sgs)]
    his = [x.astype(BF16) for x in logfs]
    r1s = [x - hi.astype(F32) for x, hi in zip(logfs, his)]
    mids = [x.astype(BF16) for x in r1s]
    los = [(x - mid.astype(F32)).astype(BF16) for x, mid in zip(r1s, mids)]
    cats = [jnp.concatenate([hi, mid, lo], axis=0) for hi, mid, lo in zip(his, mids, los)]
    xs = []
    for i0 in range(0, len(inst), 2):
        x2 = _dot(sums_ref[inst[i0][0]], jnp.concatenate([cats[i0], cats[i0 + 1]], axis=1))
        xs += [x2[:, :HEAD_DIM], x2[:, HEAD_DIM:]]
    kbs = [k.astype(BF16) for k in ks]
    attns = [eye * _dot_nt(q.astype(BF16), kb) for q, kb in zip(qs, kbs)]
    for lvl in range(N_LEVELS):
        zls = [jnp.exp(x[(2 + lvl) * c:(3 + lvl) * c]) for x in xs]
        attns = [a + pair_ref[it[0], lvl] * _dot_nt((q * zl).astype(BF16), (k * zl).astype(BF16))
                 for a, it, q, k, zl in zip(attns, inst, qs, ks, zls)]
    qds = [(q * jnp.exp(x[0:c])).astype(BF16) for q, x in zip(qs, xs)]
    kds = [(k * jnp.exp(x[c:2 * c])).astype(BF16) for k, x in zip(ks, xs)]
    intra = [_dot(a.astype(BF16), v) for a, v in zip(attns, vs)]
    kvs = [_dot_tn(v, kd) for v, kd in zip(vs, kds)]
    _delta_steps(((uf, wqf, kdf, atf, egf, oaf_ref), (ur, wqr, kdr, atr, egr, oar_ref)), s_ref, sbd_ref, n_heads)
    states = [[st_ref[d, h] for h in range(n_heads)] for d in range(2)]
    for pos in range(2):
        for i, (d, p_i, r0, c0, h, q_ref, v_ref, f_ref, o_ref) in enumerate(inst):
            if p_i != pos:
                continue
            st = states[d][h]
            o_ref[r0:r0 + c, c0:c0 + HEAD_DIM] = _dot_nt(qds[i], st.astype(BF16)) + intra[i]
            last = c - 1 if d == 0 else 0
            states[d][h] = st * jnp.exp(xs[i][last:last + 1, :]) + kvs[i]
    for d in range(2):
        for h in range(n_heads):
            st_ref[d, h] = states[d][h]


def _recurrent_scans(prep_a, p, lb, n_batch, tb, ctx_len, mixw, col_q, col_i, col_f):
    uf, wqf, kdf, atf, ur, wqr, kdr, atr, eg = prep_a
    t = p.shape[0]
    n_heads = mixw // HEAD_DIM
    nsb = tb // STEP_ROWS
    fwd, rev = _seq_block_maps(nsb, ctx_len // STEP_ROWS)
    sums_np, pair_np = _hgrn_tables()
    sums = jnp.asarray(sums_np, BF16)
    pair = jnp.asarray(pair_np, F32)
    in_specs = []
    for m in (fwd, rev):
        in_specs += [pl.BlockSpec((STEP_ROWS, mixw), lambda b, n, m=m: (m(b, n), 0)),
                     pl.BlockSpec((2 * STEP_ROWS, mixw), lambda b, n, m=m: (m(b, n), 0)),
                     pl.BlockSpec((STEP_ROWS, mixw), lambda b, n, m=m: (m(b, n), 0)),
                     pl.BlockSpec((STEP_ROWS, n_heads * CHUNK), lambda b, n, m=m: (m(b, n), 0)),
                     pl.BlockSpec((STEP_ROWS, LANES), lambda b, n, m=m: (m(b, n), 0))]
    for d, m in enumerate((fwd, rev)):
        in_specs += [pl.BlockSpec((STEP_ROWS, mixw), lambda b, n, m=m: (m(b, n), col_q // mixw)),
                     pl.BlockSpec((STEP_ROWS, mixw), lambda b, n, m=m: (m(b, n), col_i // mixw)),
                     pl.BlockSpec((STEP_ROWS, mixw), lambda b, n, m=m, d=d: (m(b, n), col_f // mixw + d))]
    in_specs += [pl.BlockSpec((1, mixw), lambda b, n: (0, 0)),
                 pl.BlockSpec(sums.shape, lambda b, n: (0, 0, 0)),
                 pl.BlockSpec(pair.shape, lambda b, n: (0, 0, 0, 0))]
    return pl.pallas_call(
        functools.partial(_scan_kernel, n_heads=n_heads),
        grid=(n_batch, nsb),
        in_specs=in_specs,
        out_specs=[pl.BlockSpec((STEP_ROWS, mixw), lambda b, n: (fwd(b, n), 0)),
                   pl.BlockSpec((STEP_ROWS, mixw), lambda b, n: (rev(b, n), 0))] * 2,
        out_shape=[jax.ShapeDtypeStruct((t, mixw), F32)] * 4,
        scratch_shapes=[pltpu.VMEM((2, n_heads, HEAD_DIM, HEAD_DIM), F32),
                        pltpu.VMEM((2, mixw, mixw), BF16),
                        pltpu.VMEM((2, n_heads, HEAD_DIM, HEAD_DIM), F32)],
        compiler_params=_cparams(("arbitrary", "arbitrary")),
        name="recurrent_scans",
    )(uf, wqf, kdf, atf, eg, ur, wqr, kdr, atr, eg, p, p, p, p, p, p, lb, sums, pair)


def _route(logits_t, bias, n_experts):
    per = n_experts // N_EXPERT_GROUPS
    scores = _sigmoid(logits_t)
    sel = scores + bias
    rows_sel = [sel[e:e + 1, :] for e in range(n_experts)]
    rows_sc = [scores[e:e + 1, :] for e in range(n_experts)]

    def top2_sum(vals):
        hi = vals[0]
        lo = jnp.full_like(hi, -jnp.inf)
        for x in vals[1:]:
            lo = jnp.maximum(lo, jnp.minimum(hi, x))
            hi = jnp.maximum(hi, x)
        return hi + lo

    best = jnp.zeros_like(rows_sel[0], dtype=jnp.int32)
    best_score = top2_sum(rows_sel[0:per])
    for g in range(1, N_EXPERT_GROUPS):
        gs = top2_sum(rows_sel[g * per:(g + 1) * per])
        better = gs > best_score
        best = jnp.where(better, g, best)
        best_score = jnp.where(better, gs, best_score)
    in_sel, in_sc = [], []
    for j in range(per):
        a = rows_sel[j]
        c = rows_sc[j]
        for g in range(1, N_EXPERT_GROUPS):
            a = jnp.where(best == g, rows_sel[g * per + j], a)
            c = jnp.where(best == g, rows_sc[g * per + j], c)
        in_sel.append(a)
        in_sc.append(c)

    def first_argmax(vals):
        idx = jnp.zeros_like(best)
        top = vals[0]
        for j in range(1, per):
            better = vals[j] > top
            idx = jnp.where(better, j, idx)
            top = jnp.where(better, vals[j], top)
        return idx

    i1 = first_argmax(in_sel)
    i2 = first_argmax([jnp.where(i1 == j, -jnp.inf, in_sel[j]) for j in range(per)])

    def pick(idx):
        w = in_sc[0]
        for j in range(1, per):
            w = jnp.where(idx == j, in_sc[j], w)
        return w

    w1, w2 = pick(i1), pick(i2)
    tot = w1 + w2
    return best * per + i1, best * per + i2, w1 / tot, w2 / tot


def _outproj_kernel(oaf, oar, ga, odf, odr, gd, yb, yc, na, nd, wo, x_ref, m_ref, lng, lnb, wr, br,
                    x1_ref, h2_ref, ei_ref, ew_ref, *, alpha, mixw, tm, n_experts):
    n_heads = mixw // HEAD_DIM

    def gated(of_ref, or_ref, g_ref, gain_ref):
        parts = []
        for h in range(n_heads):
            c0 = h * HEAD_DIM
            o = of_ref[:, c0:c0 + HEAD_DIM] + or_ref[:, c0:c0 + HEAD_DIM]
            y = o * lax.rsqrt(jnp.mean(o * o, axis=-1, keepdims=True) + EPS) * gain_ref[...]
            parts.append((y * _silu(g_ref[:, c0:c0 + HEAD_DIM])).astype(BF16))
        return jnp.concatenate(parts, axis=1)

    ya = gated(oaf, oar, ga, na)
    yd = gated(odf, odr, gd, nd)
    y = (_dot(ya, wo[0:mixw, :]) + _dot(yb[...], wo[mixw:2 * mixw, :])
         + _dot(yc[...], wo[2 * mixw:3 * mixw, :]) + _dot(yd, wo[3 * mixw:4 * mixw, :]))
    x1 = _ln(alpha * x_ref[...] + m_ref[0, 2:3, :] * y) * lng[...] + lnb[...]
    x1_ref[...] = x1
    h2 = _ln(x1) * (1.0 + m_ref[0, 4:5, :]) + m_ref[0, 3:4, :]
    h2_ref[...] = h2
    h_hi = h2.astype(BF16)
    h_r = h2 - h_hi.astype(F32)
    h_mid = h_r.astype(BF16)
    h_lo = (h_r - h_mid.astype(F32)).astype(BF16)
    pa = _dot(h_hi, wr[:, 0:2 * LANES])
    pb = _dot(h_mid, wr[:, 0:2 * LANES])
    pc = _dot(h_hi, wr[:, 2 * LANES:4 * LANES])
    pd = _dot(h_lo, wr[:, 0:2 * LANES])
    logits = (((pd[:, :LANES] + pb[:, LANES:] + pc[:, :LANES]) + (pb[:, :LANES] + pa[:, LANES:]))
              + pa[:, :LANES])
    logits_t = logits.T[:n_experts]
    e1, e2, w1, w2 = _route(logits_t, br[:, 0:1], n_experts)
    zi = jnp.zeros((SUBLANES - TOP_K, tm), jnp.int32)
    ei_ref[...] = jnp.concatenate([e1, e2, zi], axis=0)
    ew_ref[...] = jnp.concatenate([w1, w2, zi.astype(F32)], axis=0)


def _outproj(oaf, oar, odf, odr, p, yb, yc, norm_a, norm_d, w_out, x, mods, ln_g, ln_b, w_router_p, b_router,
             alpha, tm, nb, nctx, n_batch, mixw, col_ga, col_gd):
    t, d = x.shape
    n_experts = b_router.shape[0]
    row = lambda i: (i, 0)
    const = lambda i: (0, 0)
    kern = functools.partial(_outproj_kernel, alpha=alpha, mixw=mixw, tm=tm, n_experts=n_experts)
    return pl.pallas_call(
        kern,
        grid=(t // tm,),
        in_specs=[pl.BlockSpec((tm, mixw), row), pl.BlockSpec((tm, mixw), row),
                  pl.BlockSpec((tm, mixw), lambda i: (i, col_ga // mixw)),
                  pl.BlockSpec((tm, mixw), row), pl.BlockSpec((tm, mixw), row),
                  pl.BlockSpec((tm, mixw), lambda i: (i, col_gd // mixw)),
                  pl.BlockSpec((tm, mixw), row), pl.BlockSpec((tm, mixw), row),
                  pl.BlockSpec((1, HEAD_DIM), const), pl.BlockSpec((1, HEAD_DIM), const),
                  pl.BlockSpec((N_MIXERS * mixw, d), const),
                  pl.BlockSpec((tm, d), row),
                  pl.BlockSpec((1, 6, d), lambda i: (_mod_row(i, nb, nctx, n_batch), 0, 0)),
                  pl.BlockSpec((1, d), const), pl.BlockSpec((1, d), const),
                  pl.BlockSpec((d, 4 * LANES), const), pl.BlockSpec((n_experts, LANES), const)],
        out_specs=[pl.BlockSpec((tm, d), row), pl.BlockSpec((tm, d), row),
                   pl.BlockSpec((SUBLANES, tm), lambda i: (0, i)), pl.BlockSpec((SUBLANES, tm), lambda i: (0, i))],
        out_shape=[jax.ShapeDtypeStruct((t, d), F32), jax.ShapeDtypeStruct((t, d), F32),
                   jax.ShapeDtypeStruct((SUBLANES, t), jnp.int32), jax.ShapeDtypeStruct((SUBLANES, t), F32)],
        compiler_params=_cparams(("parallel",)),
        name="out_proj_router",
    )(oaf, oar, p, odf, odr, p, yb, yc, norm_a, norm_d, w_out, x, mods, ln_g, ln_b, w_router_p, b_router)


def _rank_kernel(ei_ref, su_ref, rank_ref, cnt_ref, base_ref, *, n_experts, tm):
    i = pl.program_id(0)

    @pl.when(i == 0)
    def _():
        base_ref[...] = jnp.zeros_like(base_ref)

    eid = lax.broadcasted_iota(jnp.int32, (n_experts, tm), 0)
    o1 = (eid == ei_ref[0:1, :]).astype(F32)
    o2 = (eid == ei_ref[1:2, :]).astype(F32)
    cnt = o1 + o2
    before = _dot(cnt.astype(BF16), su_ref[...]) + base_ref[:, 0:1]
    r1 = jnp.sum(o1 * before, axis=0, keepdims=True)
    r2 = jnp.sum(o2 * before, axis=0, keepdims=True)
    rank_ref[...] = jnp.concatenate([r1, r2, jnp.zeros((SUBLANES - TOP_K, tm), F32)], axis=0).astype(jnp.int32)
    base_ref[...] = base_ref[...] + jnp.sum(cnt, axis=1, keepdims=True)
    cnt_ref[...] = base_ref[...]


def _slot_ranks(ei, n_experts, tm):
    t = ei.shape[1]
    r = np.arange(tm)
    su = jnp.asarray((r[:, None] < r[None, :]).astype(np.float32), BF16)
    return pl.pallas_call(
        functools.partial(_rank_kernel, n_experts=n_experts, tm=tm),
        grid=(t // tm,),
        in_specs=[pl.BlockSpec((SUBLANES, tm), lambda i: (0, i)), pl.BlockSpec((tm, tm), lambda i: (0, 0))],
        out_specs=[pl.BlockSpec((SUBLANES, tm), lambda i: (0, i)), pl.BlockSpec((n_experts, LANES), lambda i: (0, 0))],
        out_shape=[jax.ShapeDtypeStruct((SUBLANES, t), jnp.int32), jax.ShapeDtypeStruct((n_experts, LANES), F32)],
        scratch_shapes=[pltpu.VMEM((n_experts, LANES), F32)],
        compiler_params=_cparams(("arbitrary",)),
        name="moe_slot_ranks",
    )(ei, su)


def _row_copy(src_ref, src_row, dst_ref, dst_row, sem):
    return pltpu.make_async_copy(src_ref.at[pl.ds(src_row, 1)], dst_ref.at[pl.ds(dst_row, 1)], sem)


def _dispatch_kernel(dest_ref, zs_ref, na_ref, h_ref, xs_ref, zero_ref, sem, zsem, *, tm, t, n_experts, tmx, n_blocks):
    base = pl.program_id(0) * tm

    @pl.when(pl.program_id(0) == 0)
    def _():
        zero_ref[...] = jnp.zeros_like(zero_ref)
        for e in range(n_experts):
            fill = pltpu.make_async_copy(zero_ref, xs_ref.at[pl.ds(pl.multiple_of(zs_ref[e], FILL_ALIGN),
                                                                  tmx + FILL_ALIGN)], zsem)
            fill.start()
            fill.wait()

        def tail(b, carry):
            fill = pltpu.make_async_copy(zero_ref.at[pl.ds(0, tmx)],
                                         xs_ref.at[pl.ds(pl.multiple_of(b * tmx, tmx), tmx)], zsem)
            fill.start()
            fill.wait()
            return carry

        lax.fori_loop(na_ref[0], n_blocks, tail, 0)

    for r in range(tm):
        for kk in range(TOP_K):
            _row_copy(h_ref, r, xs_ref, dest_ref[kk * t + base + r], sem).start()
    for kk in range(TOP_K):
        pltpu.make_async_copy(h_ref, xs_ref.at[pl.ds(0, tm)], sem).wait()


def _dispatch(dest, zero_start, n_active, h2, n_slots, tm, tmx):
    t, d = h2.shape
    n_experts = zero_start.shape[0]
    return pl.pallas_call(
        functools.partial(_dispatch_kernel, tm=tm, t=t, n_experts=n_experts, tmx=tmx, n_blocks=n_slots // tmx),
        grid_spec=pltpu.PrefetchScalarGridSpec(
            num_scalar_prefetch=3,
            grid=(t // tm,),
            in_specs=[pl.BlockSpec((tm, d), lambda i, dest, zs, na: (i, 0))],
            out_specs=pl.BlockSpec(memory_space=pl.ANY),
            scratch_shapes=[pltpu.VMEM((tmx + FILL_ALIGN, d), h2.dtype), pltpu.SemaphoreType.DMA(()),
                            pltpu.SemaphoreType.DMA(())]),
        out_shape=jax.ShapeDtypeStruct((n_slots, d), h2.dtype),
        compiler_params=_cparams(("arbitrary",)),
        name="moe_dispatch",
    )(dest, zero_start, n_active, h2)


def _expert_kernel(be_ref, na_ref, x_ref, w1_ref, w3_ref, w2_ref, y_ref):
    i = pl.program_id(0)

    @pl.when(i < na_ref[0])
    def _():
        xb = x_ref[...].astype(BF16)
        h1 = _dot(xb, w1_ref[0, 0])
        h3 = _dot(xb, w3_ref[0, 0])
        y_ref[...] = _dot((_silu(h1) * h3).astype(BF16), w2_ref[0, 0])

    @pl.when(i >= na_ref[0])
    def _():
        y_ref[...] = jnp.zeros_like(y_ref)


def _experts(block_expert, n_active, xs, w1, w3, w2, layer, tmx):
    n_slots, d = xs.shape
    f = w1.shape[3]
    return pl.pallas_call(
        _expert_kernel,
        grid_spec=pltpu.PrefetchScalarGridSpec(
            num_scalar_prefetch=2,
            grid=(n_slots // tmx,),
            in_specs=[pl.BlockSpec((tmx, d), lambda i, be, na: (jnp.minimum(i, na[0] - 1), 0)),
                      pl.BlockSpec((1, 1, d, f), lambda i, be, na: (layer, be[i], 0, 0)),
                      pl.BlockSpec((1, 1, d, f), lambda i, be, na: (layer, be[i], 0, 0)),
                      pl.BlockSpec((1, 1, f, d), lambda i, be, na: (layer, be[i], 0, 0))],
            out_specs=pl.BlockSpec((tmx, d), lambda i, be, na: (i, 0))),
        out_shape=jax.ShapeDtypeStruct((n_slots, d), F32),
        compiler_params=_cparams(("arbitrary",)),
        name="moe_experts",
    )(block_expert, n_active, xs, w1, w3, w2)


def _combine_kernel(dest_ref, ys_ref, ew_ref, x1_ref, m_ref, lng, lnb, mn_ref, x2_ref, *rest,
                    tm, t, alpha, block_of_step, emit_next):
    hn_ref = rest[0] if emit_next else None
    g_ref, sem = rest[-2:]
    i = pl.program_id(0)
    n = pl.num_programs(0)

    def issue(step, slot):
        base = block_of_step(step) * tm

        for r in range(tm):
            for kk in range(TOP_K):
                _row_copy(ys_ref, dest_ref[kk * t + base + r], g_ref.at[slot, kk], r, sem.at[slot]).start()

    @pl.when(i == 0)
    def _():
        issue(0, 0)

    @pl.when(i + 1 < n)
    def _():
        issue(i + 1, (i + 1) % 2)

    slot = i % 2
    for kk in range(TOP_K):
        pltpu.make_async_copy(ys_ref.at[pl.ds(0, tm)], g_ref.at[slot, kk], sem.at[slot]).wait()
    row = lax.broadcasted_iota(jnp.int32, (tm, tm), 0)
    col = lax.broadcasted_iota(jnp.int32, (tm, tm), 1)
    w_t = _dot_nt((row == col).astype(F32), ew_ref[...], HIGHEST)
    f = w_t[:, 0:1] * g_ref[slot, 0] + w_t[:, 1:2] * g_ref[slot, 1]
    x2 = _ln(alpha * x1_ref[...] + m_ref[0, 5:6, :] * f) * lng[...] + lnb[...]
    x2_ref[...] = x2
    if emit_next:
        hn_ref[...] = (_ln(x2) * (1.0 + mn_ref[0, 1:2, :]) + mn_ref[0, 0:1, :]).astype(hn_ref.dtype)


def _combine(dest, ys, ew, x1, mods, ln_g, ln_b, mods_next, alpha, tm, nb, nctx, n_batch, latent_only):
    t, d = x1.shape
    nlat = nb - nctx
    if latent_only:
        n_steps = n_batch * nlat
        block_of_step = lambda i: (i // nlat) * nb + nctx + i % nlat
    else:
        n_steps = t // tm
        block_of_step = lambda i: i
    mrow = lambda i, dest: (_mod_row(block_of_step(i), nb, nctx, n_batch), 0, 0)
    const = lambda i, dest: (0, 0)
    out_specs = [pl.BlockSpec((tm, d), lambda i, dest: (i, 0))]
    out_shape = [jax.ShapeDtypeStruct((n_steps * tm, d), F32)]
    if not latent_only:
        out_specs.append(pl.BlockSpec((tm, d), lambda i, dest: (i, 0)))
        out_shape.append(jax.ShapeDtypeStruct((t, d), BF16))
    return pl.pallas_call(
        functools.partial(_combine_kernel, tm=tm, t=t, alpha=alpha, block_of_step=block_of_step,
                          emit_next=not latent_only),
        grid_spec=pltpu.PrefetchScalarGridSpec(
            num_scalar_prefetch=1,
            grid=(n_steps,),
            in_specs=[pl.BlockSpec(memory_space=pl.ANY),
                      pl.BlockSpec((SUBLANES, tm), lambda i, dest: (0, block_of_step(i))),
                      pl.BlockSpec((tm, d), lambda i, dest: (block_of_step(i), 0)),
                      pl.BlockSpec((1, 6, d), mrow),
                      pl.BlockSpec((1, d), const), pl.BlockSpec((1, d), const),
                      pl.BlockSpec((1, 6, d), mrow)],
            out_specs=out_specs,
            scratch_shapes=[pltpu.VMEM((2, TOP_K, tm, d), F32), pltpu.SemaphoreType.DMA((2,))]),
        out_shape=out_shape,
        compiler_params=_cparams(("arbitrary",)),
        name="moe_combine",
    )(dest, ys, ew, x1, mods, ln_g, ln_b, mods_next)


def _axial_tables(seq, ctx_len):
    rows = seq // GRID_W
    row = jnp.repeat(jnp.arange(rows, dtype=F32), GRID_W)
    col = jnp.tile(jnp.arange(GRID_W, dtype=F32), rows)
    n_freq = HEAD_DIM // 4
    inv_freq = ROPE_THETA ** (-jnp.arange(n_freq, dtype=F32) / n_freq)
    ang = jnp.concatenate([row[:, None] * inv_freq, col[:, None] * inv_freq], axis=-1)
    cos, sin = jnp.cos(ang), jnp.sin(ang)
    cosf = jnp.concatenate([cos, cos], axis=-1)
    sinf = jnp.concatenate([-sin, sin], axis=-1)
    cosf = jnp.concatenate([jnp.ones((ctx_len, HEAD_DIM), F32), cosf], axis=0)
    sinf = jnp.concatenate([jnp.zeros((ctx_len, HEAD_DIM), F32), sinf], axis=0)
    return cosf, sinf


def kernel(x, c, ctx, c_ctx, w_ada, b_ada, ln_g, ln_b, w_in, conv_a, a_log, dt_bias, norm_a, qk_norm_b, ws_c, bs_c, lb_d, norm_d, w_out, w_router, b_router, w1, w3, w2):
    n_batch, seq, d = x.shape
    ctx_len = ctx.shape[1]
    n_layers = w_in.shape[0]
    n_experts = w_router.shape[1]
    mixw = d // N_MIXERS
    n_heads = mixw // HEAD_DIM
    tb = ctx_len + seq
    t = n_batch * tb
    alpha = (2.0 * n_layers) ** 0.25
    tm = _pick(math.gcd(ctx_len, seq), (256, 128))
    nb, nctx = tb // tm, ctx_len // tm

    col = {"a_qkv": 0, "a_gate": 3 * mixw, "d_f": 4 * mixw, "b_q": 6 * mixw, "b_kv": 7 * mixw, "c_u": 8 * mixw,
           "c_v": 9 * mixw, "d_q": 10 * mixw, "d_i": 11 * mixw, "d_gate": 12 * mixw, "small": 13 * mixw}
    n_proj = 13 * mixw + 256
    src = np.cumsum([0, 3 * mixw, mixw, 2 * n_heads, 2 * n_heads, mixw, mixw, mixw, mixw, mixw, mixw, mixw, 2 * mixw])
    s_qkv, s_ga, s_beta, s_dec, s_bq, s_bkv, s_cu, s_cv, s_dq, s_di, s_dg, s_df, s_end = [int(v) for v in src]

    def permute_w_in(w):
        pad = jnp.zeros((d, n_proj - 13 * mixw - 4 * n_heads), w.dtype)
        return jnp.concatenate([w[:, s_qkv:s_beta], w[:, s_df:s_end], w[:, s_bq:s_df], w[:, s_beta:s_bq], pad],
                               axis=1).astype(BF16)

    assert n_batch + 1 <= SUBLANES
    cc = jnp.zeros((SUBLANES, d), F32).at[:n_batch].set(c).at[n_batch].set(c_ctx)
    mods_all = _mod_vectors(cc, w_ada, b_ada)[:, :n_batch + 1].reshape(n_layers, n_batch + 1, 6, d)

    soft = jax.nn.softmax(lb_d.astype(F32), axis=0)
    lb_all = jnp.cumsum(soft, axis=0) - soft[0]
    cosf, sinf = _axial_tables(seq, ctx_len)
    gate_lanes = jnp.zeros((1, LANES), F32)
    wr_f = w_router.astype(F32)
    wr_hi = wr_f.astype(BF16)
    wr_r = wr_f - wr_hi.astype(F32)
    wr_mid = wr_r.astype(BF16)
    wr_lo = (wr_r - wr_mid.astype(F32)).astype(BF16)
    w_router_p = jnp.zeros((d, 4 * LANES), BF16)
    for blk, part in enumerate((wr_hi, wr_mid, wr_lo)):
        w_router_p = w_router_p.at[:, blk * LANES:blk * LANES + n_experts].set(part)
    b_router_c = jnp.broadcast_to(b_router.astype(F32)[:, None], (n_experts, LANES))

    tmx = EXPERT_BLOCK_ROWS
    n_assign = t * TOP_K
    n_slots = (-(-n_assign // tmx) + n_experts + 2) * tmx

    w1_b, w3_b, w2_b = w1.astype(BF16), w3.astype(BF16), w2.astype(BF16)
    xu = jnp.concatenate([ctx, x], axis=1).reshape(t, d)
    h = _ln_modulate(xu, mods_all[0], tm, nb, nctx, n_batch)
    for l in range(n_layers):
        mods = mods_all[l]
        p = _matmul(h, permute_w_in(w_in[l]))
        neg_a = gate_lanes.at[0, 2 * n_heads:4 * n_heads].set(-jnp.exp(a_log[l].astype(F32)).reshape(-1))
        dtb = gate_lanes.at[0, 2 * n_heads:4 * n_heads].set(dt_bias[l].astype(F32).reshape(-1))
        prep_a = _a_prep(p, conv_a[l], neg_a, dtb, tm, nb, nctx, mixw, col["small"])
        oaf, oar, odf, odr = _recurrent_scans(prep_a, p, lb_all[l][None, :], n_batch, tb, ctx_len, mixw,
                                              col["d_q"], col["d_i"], col["d_f"])
        qb, kb, vb = _b_prep(p, qk_norm_b[l], cosf, sinf, tm, nb, mixw, col["b_q"], col["b_kv"])
        yb = _attention(qb, kb, vb, n_batch, tb, ctx_len, mixw)
        bs_t = jnp.zeros((CHUNK_C, LANES), F32).at[:, :bs_c.shape[1]].set(bs_c[l].T)
        yc = _mixer_c(p, ws_c[l].astype(BF16), bs_t, tm, mixw, col["c_u"], col["c_v"])
        x1, h2, ei, ew = _outproj(oaf, oar, odf, odr, p, yb, yc, norm_a[l][None, :], norm_d[l][None, :],
                                  w_out[l].astype(BF16), xu, mods, ln_g[l, 0][None, :], ln_b[l, 0][None, :],
                                  w_router_p, b_router_c, alpha, tm, nb, nctx, n_batch, mixw,
                                  col["a_gate"], col["d_gate"])
        rank, counts = _slot_ranks(ei, n_experts, tm)
        cnt = counts[:, 0].astype(jnp.int32)
        padded = (cnt + tmx - 1) // tmx * tmx
        pad_end = jnp.cumsum(padded)
        pad_start = pad_end - padded
        eids = jnp.arange(n_experts, dtype=jnp.int32)[:, None, None]
        dest = (rank[:TOP_K] + jnp.sum(jnp.where(ei[None, :TOP_K] == eids, pad_start[:, None, None], 0),
                                       axis=0)).reshape(-1)
        n_blocks = n_slots // tmx
        block_start = jnp.arange(n_blocks, dtype=jnp.int32) * tmx
        block_expert = jnp.minimum(jnp.sum((pad_end[None, :] <= block_start[:, None]).astype(jnp.int32), axis=1),
                                   n_experts - 1)
        n_active = (pad_end[-1:] // tmx).astype(jnp.int32)
        fill_start = (pad_start + cnt) // FILL_ALIGN * FILL_ALIGN
        xs = _dispatch(dest, fill_start, n_active, h2, n_slots, tm, tmx)
        ys = _experts(block_expert, n_active, xs, w1_b, w3_b, w2_b, l, tmx)
        last = l == n_layers - 1
        res = _combine(dest, ys, ew, x1, mods, ln_g[l, 1][None, :], ln_b[l, 1][None, :],
                       mods_all[min(l + 1, n_layers - 1)], alpha, tm, nb, nctx, n_batch, latent_only=last)
        if last:
            return res[0].reshape(n_batch, seq, d)
        xu, h = res
```

```python
import functools
import math

import numpy as np
import jax
import jax.numpy as jnp
from jax import lax
from jax.experimental import pallas as pl
from jax.experimental.pallas import tpu as pltpu

F32 = jnp.float32
BF16 = jnp.bfloat16
HIGHEST = lax.Precision.HIGHEST

HEAD_DIM = 128
N_MIXERS = 4
CONV_W = 5
CHUNK = 64
STEP_ROWS = 2 * CHUNK
SOLVE_ROWS = 2 * CHUNK
CHUNK_C = 128
GRID_W = 64
ROPE_THETA = 10000.0
N_EXPERT_GROUPS = 4
TOP_K = 2
EPS = 1e-6
LOG2_E = math.log2(math.e)
GATE_CLIP = 30.0
N_LEVELS = 6
LANES = 128
SUBLANES = 8
HALO = SUBLANES
FILL_ALIGN = SUBLANES
EXPERT_BLOCK_ROWS = 256
V7X_VMEM_LIMIT = 56 * 1024 * 1024


def _pick(n, cands):
    for c in cands:
        if n % c == 0:
            return c
    raise ValueError(f"no tile in {cands} divides {n}")


def _cparams(sem, vmem=None):
    return pltpu.CompilerParams(dimension_semantics=sem, vmem_limit_bytes=vmem or V7X_VMEM_LIMIT)


def _dot(a, b, precision=None):
    return jnp.dot(a, b, preferred_element_type=F32, precision=precision)


def _dot_nt(a, b, precision=None):
    return lax.dot_general(a, b, (((1,), (1,)), ((), ())), preferred_element_type=F32, precision=precision)


def _dot_tn(a, b, precision=None):
    return lax.dot_general(a, b, (((0,), (0,)), ((), ())), preferred_element_type=F32, precision=precision)


def _sigmoid(x):
    return 1.0 / (1.0 + jnp.exp(-x))


def _silu(x):
    return x * _sigmoid(x)


def _gelu_tanh(x):
    return 0.5 * x * (1.0 + jnp.tanh(math.sqrt(2.0 / math.pi) * (x + 0.044715 * (x * x * x))))


def _ln(x):
    mu = jnp.mean(x, axis=-1, keepdims=True)
    xc = x - mu
    return xc * lax.rsqrt(jnp.mean(xc * xc, axis=-1, keepdims=True) + EPS)


def _mod_row(i, nb, nctx, n_batch):
    return jnp.where(i % nb < nctx, n_batch, i // nb)


def _mod_kernel(c_ref, w_ref, b_ref, o_ref):
    o_ref[0] = _dot(_silu(c_ref[...]), w_ref[0], HIGHEST) + b_ref[0]


def _mod_vectors(cc, w_ada, b_ada):
    n_layers, d, n6 = w_ada.shape
    tn = _pick(n6, (1024, 512, 128))
    return pl.pallas_call(
        _mod_kernel,
        grid=(n_layers, n6 // tn),
        in_specs=[pl.BlockSpec((SUBLANES, d), lambda l, j: (0, 0)),
                  pl.BlockSpec((1, d, tn), lambda l, j: (l, 0, j)),
                  pl.BlockSpec((1, 1, tn), lambda l, j: (l, 0, j))],
        out_specs=pl.BlockSpec((1, SUBLANES, tn), lambda l, j: (l, 0, j)),
        out_shape=jax.ShapeDtypeStruct((n_layers, SUBLANES, n6), F32),
        compiler_params=_cparams(("parallel", "parallel")),
        name="mod_vectors",
    )(cc, w_ada, b_ada.reshape(n_layers, 1, n6))


def _lnmod_kernel(x_ref, m_ref, h_ref):
    y = _ln(x_ref[...])
    h_ref[...] = (y * (1.0 + m_ref[0, 1:2, :]) + m_ref[0, 0:1, :]).astype(h_ref.dtype)


def _ln_modulate(x, mods, tm, nb, nctx, n_batch):
    t, d = x.shape
    return pl.pallas_call(
        _lnmod_kernel,
        grid=(t // tm,),
        in_specs=[pl.BlockSpec((tm, d), lambda i: (i, 0)),
                  pl.BlockSpec((1, 6, d), lambda i: (_mod_row(i, nb, nctx, n_batch), 0, 0))],
        out_specs=pl.BlockSpec((tm, d), lambda i: (i, 0)),
        out_shape=jax.ShapeDtypeStruct((t, d), BF16),
        compiler_params=_cparams(("parallel",)),
        name="ln_modulate",
    )(x, mods)


def _matmul_kernel(a_ref, w_ref, o_ref):
    o_ref[...] = _dot(a_ref[...], w_ref[...]).astype(o_ref.dtype)


def _matmul(a, w, out_dtype=F32):
    m, k = a.shape
    n = w.shape[1]
    tm = _pick(m, (1536, 1024, 768, 512, 256, 128))
    tn = _pick(n, (1152, 1024, 768, 512, 256, 128))
    return pl.pallas_call(
        _matmul_kernel,
        grid=(m // tm, n // tn),
        in_specs=[pl.BlockSpec((tm, k), lambda i, j: (i, 0)),
                  pl.BlockSpec((k, tn), lambda i, j: (0, j))],
        out_specs=pl.BlockSpec((tm, tn), lambda i, j: (i, j)),
        out_shape=jax.ShapeDtypeStruct((m, n), out_dtype),
        compiler_params=_cparams(("parallel", "parallel")),
        name="in_proj",
    )(a, w)


def _unit_tri_inverses(lows, eye, pair_refs):
    ts = [eye - low * pair_ref[0] for low, pair_ref in zip(lows, pair_refs)]
    for lvl in range(1, N_LEVELS):
        tbs = [t.astype(BF16) for t in ts]
        xs = [_dot(tb16, (low * pair_ref[lvl]).astype(BF16)) for tb16, low, pair_ref in zip(tbs, lows, pair_refs)]
        ts = [t - _dot(x.astype(BF16), tb16) for t, x, tb16 in zip(ts, xs, tbs)]
    return ts


def _a_prep_kernel(x_ref, xp_ref, xn_ref, sm_ref, cw_ref, na_ref, dtb_ref, trf_ref, trr_ref, pairf_ref, pairr_ref,
                   uf_ref, wqf_ref, kdf_ref, atf_ref, ur_ref, wqr_ref, kdr_ref, atr_ref, eg_ref, ext_ref,
                   *, tm, nb, nctx, mixw):
    i = pl.program_id(0)
    j = i % nb
    first = jnp.logical_or(j == 0, j == nctx)
    last = jnp.logical_or(j == nctx - 1, j == nb - 1)
    ext_ref[HALO:HALO + tm, :] = x_ref[...]
    ext_ref[0:HALO, :] = jnp.where(first, 0.0, xp_ref[...])
    ext_ref[HALO + tm:2 * HALO + tm, :] = jnp.where(last, 0.0, xn_ref[...])
    off = HALO - CONV_W // 2
    acc = cw_ref[0:1, :] * ext_ref[off:off + tm, :]
    for t in range(1, CONV_W):
        acc = acc + cw_ref[t:t + 1, :] * ext_ref[off + t:off + t + tm, :]
    y = _silu(acc)
    n_heads = mixw // HEAD_DIM
    sm = sm_ref[...]
    lane = lax.broadcasted_iota(jnp.int32, sm.shape, 1)
    zz = sm + dtb_ref[...]
    softplus = jnp.maximum(zz, 0.0) + jnp.log(1.0 + jnp.exp(-jnp.abs(zz)))
    g = jnp.where(jnp.logical_and(lane >= 2 * n_heads, lane < 4 * n_heads), na_ref[...] * softplus, 0.0)
    beta_all = _sigmoid(sm)
    tri_f = trf_ref[...]
    tri_r = trr_ref[...]
    cs_f = _dot(tri_f, g, HIGHEST)
    cs_r = _dot(tri_r, g, HIGHEST)
    gcum = jnp.where(lane < 3 * n_heads, cs_f, cs_r)
    gtot = _dot(jnp.maximum(tri_f, tri_r), g, HIGHEST)
    eg_ref[...] = jnp.exp(gtot)
    gcum_t = gcum.T
    sr = SOLVE_ROWS
    row = lax.broadcasted_iota(jnp.int32, (sr, sr), 0)
    col = lax.broadcasted_iota(jnp.int32, (sr, sr), 1)
    eye = (row == col).astype(F32)
    incls = (tri_f[0:sr, 0:sr] > 0.5, tri_r[0:sr, 0:sr] > 0.5)
    pairs = ([pairf_ref[lvl, 0:sr, 0:sr] for lvl in range(N_LEVELS)],
             [pairr_ref[lvl, 0:sr, 0:sr] for lvl in range(N_LEVELS)])
    outs = ((uf_ref, wqf_ref, kdf_ref, atf_ref), (ur_ref, wqr_ref, kdr_ref, atr_ref))
    items = []
    for sb in range(tm // sr):
        rs = sb * sr
        for h in range(n_heads):
            c0 = h * HEAD_DIM
            qh = y[rs:rs + sr, c0:c0 + HEAD_DIM]
            kh = y[rs:rs + sr, mixw + c0:mixw + c0 + HEAD_DIM]
            vh = y[rs:rs + sr, 2 * mixw + c0:2 * mixw + c0 + HEAD_DIM]
            qh = qh * (lax.rsqrt(jnp.sum(qh * qh, axis=-1, keepdims=True) + EPS) * HEAD_DIM ** -0.5)
            kh = kh * lax.rsqrt(jnp.sum(kh * kh, axis=-1, keepdims=True) + EPS)
            kb16 = kh.astype(BF16)
            qk = _dot_nt(qh.astype(BF16), kb16)
            for d in range(2):
                lb = d * n_heads + h
                lg = (2 + d) * n_heads + h
                beta = beta_all[rs:rs + sr, lb:lb + 1]
                gc = gcum[rs:rs + sr, lg:lg + 1]
                incl = incls[d]
                decay = jnp.where(incl, jnp.exp(jnp.where(incl, gc - gcum_t[lg:lg + 1, rs:rs + sr], 0.0)), 0.0)
                kbeta = kh * beta
                low = jnp.where(row == col, 0.0, _dot_nt(kbeta.astype(BF16), kb16) * decay)
                rhs = jnp.concatenate([vh * beta, kbeta * jnp.exp(gc)], axis=1).astype(BF16)
                items.append(dict(h=h, d=d, rs=rs, low=low, rhs=rhs, attn=(qk * decay).astype(BF16),
                                  qd=(qh * jnp.exp(gc)).astype(BF16),
                                  kd=(kh * jnp.exp(gtot[rs:rs + sr, lg:lg + 1] - gc)).astype(BF16)))
    tinvs = _unit_tri_inverses([it["low"] for it in items], eye, [pairs[it["d"]] for it in items])
    for it, tinv in zip(items, tinvs):
        u_ref, wq_ref, kd_ref, at_ref = outs[it["d"]]
        h, rs = it["h"], it["rs"]
        c0 = h * HEAD_DIM
        sol = _dot(tinv.astype(BF16), it["rhs"])
        u_ref[rs:rs + sr, c0:c0 + HEAD_DIM] = sol[:, :HEAD_DIM]
        w = sol[:, HEAD_DIM:].astype(BF16)
        kd_ref[rs:rs + sr, c0:c0 + HEAD_DIM] = it["kd"]
        for ci in range(sr // CHUNK):
            r0 = ci * CHUNK
            g0 = rs + r0
            wq_ref[2 * g0:2 * g0 + CHUNK, c0:c0 + HEAD_DIM] = w[r0:r0 + CHUNK]
            wq_ref[2 * g0 + CHUNK:2 * g0 + 2 * CHUNK, c0:c0 + HEAD_DIM] = it["qd"][r0:r0 + CHUNK]
            at_ref[g0:g0 + CHUNK, h * CHUNK:(h + 1) * CHUNK] = it["attn"][r0:r0 + CHUNK, r0:r0 + CHUNK]


def _a_prep(p, conv_w, neg_a, dtb, tm, nb, nctx, mixw, col_small):
    t = p.shape[0]
    c3 = 3 * mixw
    n_heads = mixw // HEAD_DIM
    r = np.arange(tm)
    same = (r[:, None] // CHUNK) == (r[None, :] // CHUNK)
    tri_f = jnp.asarray((same & (r[None, :] <= r[:, None])).astype(np.float32))
    tri_r = jnp.asarray((same & (r[None, :] >= r[:, None])).astype(np.float32))
    pair_f = []
    for lvl in range(N_LEVELS):
        s_blk = 1 << lvl
        joined = ((r[:, None] // (2 * s_blk)) == (r[None, :] // (2 * s_blk))) \
            & ((r[:, None] % (2 * s_blk)) >= s_blk) & ((r[None, :] % (2 * s_blk)) < s_blk)
        pair_f.append(joined.astype(np.float32))
    pair_f = np.stack(pair_f)
    pair_r = jnp.asarray(np.transpose(pair_f, (0, 2, 1)))
    pair_f = jnp.asarray(pair_f)
    hb = tm // HALO
    last_hb = t // HALO - 1
    kern = functools.partial(_a_prep_kernel, tm=tm, nb=nb, nctx=nctx, mixw=mixw)
    per_dir_specs = [pl.BlockSpec((tm, mixw), lambda i: (i, 0)),
                     pl.BlockSpec((2 * tm, mixw), lambda i: (i, 0)),
                     pl.BlockSpec((tm, mixw), lambda i: (i, 0)),
                     pl.BlockSpec((tm, n_heads * CHUNK), lambda i: (i, 0))]
    per_dir_shapes = [jax.ShapeDtypeStruct((t, mixw), F32), jax.ShapeDtypeStruct((2 * t, mixw), BF16),
                      jax.ShapeDtypeStruct((t, mixw), BF16), jax.ShapeDtypeStruct((t, n_heads * CHUNK), BF16)]
    return pl.pallas_call(
        kern,
        grid=(t // tm,),
        in_specs=[pl.BlockSpec((tm, c3), lambda i: (i, 0)),
                  pl.BlockSpec((HALO, c3), lambda i: (jnp.maximum(i * hb - 1, 0), 0)),
                  pl.BlockSpec((HALO, c3), lambda i: (jnp.minimum((i + 1) * hb, last_hb), 0)),
                  pl.BlockSpec((tm, LANES), lambda i: (i, col_small // LANES)),
                  pl.BlockSpec((CONV_W, c3), lambda i: (0, 0)),
                  pl.BlockSpec((1, LANES), lambda i: (0, 0)),
                  pl.BlockSpec((1, LANES), lambda i: (0, 0)),
                  pl.BlockSpec((tm, tm), lambda i: (0, 0)),
                  pl.BlockSpec((tm, tm), lambda i: (0, 0)),
                  pl.BlockSpec((N_LEVELS, tm, tm), lambda i: (0, 0, 0)),
                  pl.BlockSpec((N_LEVELS, tm, tm), lambda i: (0, 0, 0))],
        out_specs=per_dir_specs * 2 + [pl.BlockSpec((tm, LANES), lambda i: (i, 0))],
        out_shape=per_dir_shapes * 2 + [jax.ShapeDtypeStruct((t, LANES), F32)],
        scratch_shapes=[pltpu.VMEM((tm + 2 * HALO, c3), F32)],
        compiler_params=_cparams(("parallel",)),
        name="deltanet_prep",
    )(p, p, p, p, conv_w, neg_a, dtb, tri_f, tri_r, pair_f, pair_r)


def _delta_steps(dirs, s_ref, sbd_ref, n_heads):
    mixw = n_heads * HEAD_DIM
    lane_head = lax.broadcasted_iota(jnp.int32, (CHUNK, mixw), 1) // HEAD_DIM
    for pos in range(2):
        r0s = [(pos if d == 0 else 1 - pos) * CHUNK for d in range(2)]
        rs = [_dot(dirs[d][1][2 * r0s[d]:2 * r0s[d] + 2 * CHUNK, :], sbd_ref[d]) for d in range(2)]
        v_news = [dirs[d][0][r0s[d]:r0s[d] + CHUNK, :] - rs[d][:CHUNK] for d in range(2)]
        v_stacks = [jnp.concatenate([jnp.where(lane_head == h, v_news[d], 0.0) for h in range(n_heads)],
                                    axis=0).astype(BF16) for d in range(2)]
        for d in range(2):
            dirs[d][5][r0s[d]:r0s[d] + CHUNK, :] = (rs[d][CHUNK:]
                                                    + _dot(dirs[d][3][r0s[d]:r0s[d] + CHUNK, :], v_stacks[d]))
        vnbs = [v.astype(BF16) for v in v_news]
        for h in range(n_heads):
            c0 = h * HEAD_DIM
            for d in range(2):
                lg = (2 + d) * n_heads + h
                r0 = r0s[d]
                s_new = (s_ref[d, h] * dirs[d][4][r0:r0 + 1, lg:lg + 1]
                         + _dot_tn(dirs[d][2][r0:r0 + CHUNK, c0:c0 + HEAD_DIM], vnbs[d][:, c0:c0 + HEAD_DIM]))
                s_ref[d, h] = s_new
                sbd_ref[d, c0:c0 + HEAD_DIM, c0:c0 + HEAD_DIM] = s_new.astype(BF16)


def _seq_block_maps(nsb, nctx_sb):
    def fwd(b, n):
        return b * nsb + n

    def rev(b, n):
        return b * nsb + jnp.where(n < nctx_sb, nctx_sb - 1 - n, nsb - 1 - (n - nctx_sb))

    return fwd, rev


def _b_prep_kernel(q_ref, kv_ref, gain_ref, cos_ref, sin_ref, qo_ref, ko_ref, vo_ref, *, mixw):
    n_heads = mixw // HEAD_DIM
    n_kv = n_heads // 2
    cosf = cos_ref[...]
    sinf = sin_ref[...]

    def norm_rope(xh, gain, scale):
        yh = xh * lax.rsqrt(jnp.mean(xh * xh, axis=-1, keepdims=True) + EPS) * gain
        return (yh * cosf + pltpu.roll(yh, HEAD_DIM // 2, 1) * sinf) * scale

    for h in range(n_heads):
        c0 = h * HEAD_DIM
        qo_ref[:, c0:c0 + HEAD_DIM] = norm_rope(q_ref[:, c0:c0 + HEAD_DIM], gain_ref[0:1, :],
                                                HEAD_DIM ** -0.5 * LOG2_E).astype(qo_ref.dtype)
    for h in range(n_kv):
        c0 = h * HEAD_DIM
        ko_ref[:, c0:c0 + HEAD_DIM] = norm_rope(kv_ref[:, c0:c0 + HEAD_DIM], gain_ref[1:2, :], 1.0).astype(ko_ref.dtype)
    ones = jnp.ones((q_ref.shape[0], HEAD_DIM), vo_ref.dtype)
    for h in range(n_kv):
        c0 = (n_kv + h) * HEAD_DIM
        vo_ref[:, 2 * h * HEAD_DIM:(2 * h + 1) * HEAD_DIM] = kv_ref[:, c0:c0 + HEAD_DIM].astype(vo_ref.dtype)
        vo_ref[:, (2 * h + 1) * HEAD_DIM:(2 * h + 2) * HEAD_DIM] = ones


def _b_prep(p, qk_gain, cosf, sinf, tm, nb, mixw, col_q, col_kv):
    t = p.shape[0]
    kvw = mixw // 2
    return pl.pallas_call(
        functools.partial(_b_prep_kernel, mixw=mixw),
        grid=(t // tm,),
        in_specs=[pl.BlockSpec((tm, mixw), lambda i: (i, col_q // mixw)),
                  pl.BlockSpec((tm, mixw), lambda i: (i, col_kv // mixw)),
                  pl.BlockSpec((2, HEAD_DIM), lambda i: (0, 0)),
                  pl.BlockSpec((tm, HEAD_DIM), lambda i: (i % nb, 0)),
                  pl.BlockSpec((tm, HEAD_DIM), lambda i: (i % nb, 0))],
        out_specs=[pl.BlockSpec((tm, mixw), lambda i: (i, 0)),
                   pl.BlockSpec((tm, kvw), lambda i: (i, 0)),
                   pl.BlockSpec((tm, 2 * kvw), lambda i: (i, 0))],
        out_shape=[jax.ShapeDtypeStruct((t, mixw), BF16),
                   jax.ShapeDtypeStruct((t, kvw), BF16),
                   jax.ShapeDtypeStruct((t, 2 * kvw), BF16)],
        compiler_params=_cparams(("parallel",)),
        name="gqa_prep",
    )(p, p, qk_gain, cosf, sinf)


def _attn_kernel(q_ref, k_ref, v_ref, o_ref, m_ref, acc_ref, sa_ref, sb_ref, *, tq, tk, nk, nctx_q, ctx_len):
    qi = pl.program_id(2)
    qb = q_ref[...]
    q2 = jnp.concatenate([qb[:, :HEAD_DIM], qb[:, HEAD_DIM:]], axis=0)
    m_ref[...] = jnp.full_like(m_ref, -1e30)
    acc_ref[...] = jnp.zeros_like(acc_ref)

    def scores(i, s_ref):
        s_ref[...] = _dot_nt(q2, k_ref[i * tk:(i + 1) * tk, :])

    def absorb(s_of, vc):
        for g in range(2):
            sl = slice(g * tq, (g + 1) * tq)
            s = s_of(sl)
            m_prev = m_ref[sl, :]
            m_new = jnp.maximum(m_prev, jnp.max(s, axis=-1, keepdims=True))
            p = jnp.exp2(s - m_new)
            acc_ref[sl, :] = jnp.exp2(m_prev - m_new) * acc_ref[sl, :] + _dot(p.astype(BF16), vc)
            m_ref[sl, :] = m_new

    def absorb_chunk(i, s_ref):
        absorb(lambda sl: s_ref[sl, :], v_ref[i * tk:(i + 1) * tk, :])

    @pl.when(qi >= nctx_q)
    def _():
        bufs = (sa_ref, sb_ref)
        scores(0, bufs[0])
        for i in range(nk):
            if i + 1 < nk:
                scores(i + 1, bufs[(i + 1) % 2])
            absorb_chunk(i, bufs[i % 2])

    @pl.when(qi < nctx_q)
    def _():
        s_ctx = _dot_nt(q2, k_ref[0:ctx_len, :])
        absorb(lambda sl: s_ctx[sl, :], v_ref[0:ctx_len, :])

    acc = acc_ref[...]
    o = acc[:, :HEAD_DIM] / acc[:, HEAD_DIM:HEAD_DIM + 1]
    o_ref[...] = jnp.concatenate([o[:tq], o[tq:]], axis=1).astype(o_ref.dtype)


def _attention(q, k, v, n_batch, tb, ctx_len, mixw):
    t = q.shape[0]
    n_kv = mixw // HEAD_DIM // 2
    tq = _pick(math.gcd(ctx_len, tb), (256, 128))
    tk = _pick(tb, (1408, 768, 1024, 512, 640, 384, 256, 128))
    nq, nk = tb // tq, tb // tk
    return pl.pallas_call(
        functools.partial(_attn_kernel, tq=tq, tk=tk, nk=nk, nctx_q=ctx_len // tq, ctx_len=ctx_len),
        grid=(n_batch, n_kv, nq),
        in_specs=[pl.BlockSpec((tq, 2 * HEAD_DIM), lambda b, j, qi: (b * nq + qi, j)),
                  pl.BlockSpec((tb, HEAD_DIM), lambda b, j, qi: (b, j)),
                  pl.BlockSpec((tb, 2 * HEAD_DIM), lambda b, j, qi: (b, j))],
        out_specs=pl.BlockSpec((tq, 2 * HEAD_DIM), lambda b, j, qi: (b * nq + qi, j)),
        out_shape=jax.ShapeDtypeStruct((t, mixw), BF16),
        scratch_shapes=[pltpu.VMEM((2 * tq, 1), F32), pltpu.VMEM((2 * tq, 2 * HEAD_DIM), F32),
                        pltpu.VMEM((2 * tq, tk), F32), pltpu.VMEM((2 * tq, tk), F32)],
        compiler_params=_cparams(("parallel", "parallel", "parallel")),
        name="gqa_attention",
    )(q, k, v)


def _c_kernel(u_ref, v_ref, ws_ref, bs_ref, o_ref, *, tm, mixw):
    n_groups = mixw // HEAD_DIM
    for ci in range(tm // CHUNK_C):
        r0 = ci * CHUNK_C
        u = _gelu_tanh(u_ref[r0:r0 + CHUNK_C, :])
        vn = _ln(_gelu_tanh(v_ref[r0:r0 + CHUNK_C, :])).astype(BF16)
        for g in range(n_groups):
            c0 = g * HEAD_DIM
            vm = _dot(ws_ref[g], vn[:, c0:c0 + HEAD_DIM]) + bs_ref[:, g:g + 1]
            o_ref[r0:r0 + CHUNK_C, c0:c0 + HEAD_DIM] = (u[:, c0:c0 + HEAD_DIM] * vm).astype(o_ref.dtype)


def _mixer_c(p, ws, bs_t, tm, mixw, col_u, col_v):
    t = p.shape[0]
    n_groups = mixw // HEAD_DIM
    return pl.pallas_call(
        functools.partial(_c_kernel, tm=tm, mixw=mixw),
        grid=(t // tm,),
        in_specs=[pl.BlockSpec((tm, mixw), lambda i: (i, col_u // mixw)),
                  pl.BlockSpec((tm, mixw), lambda i: (i, col_v // mixw)),
                  pl.BlockSpec((n_groups, CHUNK_C, CHUNK_C), lambda i: (0, 0, 0)),
                  pl.BlockSpec((CHUNK_C, LANES), lambda i: (0, 0))],
        out_specs=pl.BlockSpec((tm, mixw), lambda i: (i, 0)),
        out_shape=jax.ShapeDtypeStruct((t, mixw), BF16),
        compiler_params=_cparams(("parallel",)),
        name="gmlp",
    )(p, p, ws, bs_t)


def _hgrn_tables():
    tau = np.arange(CHUNK)
    sums_t = [tau[None, :] <= tau[:, None], tau[None, :] > tau[:, None]]
    pair_t = []
    for lvl in range(N_LEVELS):
        m = CHUNK >> (lvl + 1)
        blk = tau // (2 * m)
        upper = (tau % (2 * m)) >= m
        ref = blk * 2 * m + m - 1
        r = tau[None, :]
        w_up = upper[:, None] & (r > ref[:, None]) & (r <= tau[:, None])
        w_lo = (~upper)[:, None] & (r > tau[:, None]) & (r <= ref[:, None])
        sums_t.append(w_up | w_lo)
        pair_t.append((blk[:, None] == blk[None, :]) & upper[:, None] & (~upper)[None, :])
    sums_t = np.concatenate(sums_t, axis=0).astype(np.float32)
    pair_t = np.stack(pair_t).astype(np.float32)
    flip = tau[::-1]
    sums, pair = [], []
    for d in range(2):
        if d == 0:
            s_d, p_d = sums_t, pair_t
        else:
            s_d = sums_t.reshape(-1, CHUNK, CHUNK)[:, flip][:, :, flip].reshape(-1, CHUNK)
            p_d = pair_t[:, flip][:, :, flip]
        sums.append(np.concatenate([s_d, s_d, s_d], axis=1))
        pair.append(p_d)
    return np.stack(sums), np.stack(pair)


def _scan_kernel(uf, wqf, kdf, atf, egf, ur, wqr, kdr, atr, egr, qf, vf, ff, qr, vr, fr, lb_ref, sums_ref, pair_ref,
                 oaf_ref, oar_ref, of_ref, or_ref, s_ref, sbd_ref, st_ref, *, n_heads):
    n = pl.program_id(1)

    @pl.when(n == 0)
    def _():
        st_ref[...] = jnp.zeros_like(st_ref)
        s_ref[...] = jnp.zeros_like(s_ref)
        sbd_ref[...] = jnp.zeros_like(sbd_ref)

    c = CHUNK
    row = lax.broadcasted_iota(jnp.int32, (c, c), 0)
    col = lax.broadcasted_iota(jnp.int32, (c, c), 1)
    eye = (row == col).astype(F32)
    inst = []
    for d, (q_ref, v_ref, f_ref, o_ref) in enumerate(((qf, vf, ff, of_ref), (qr, vr, fr, or_ref))):
        for pos, ci in enumerate((0, 1) if d == 0 else (1, 0)):
            for h in range(n_heads):
                inst.append((d, pos, ci * c, h * HEAD_DIM, h, q_ref, v_ref, f_ref, o_ref))
    qs = [q_ref[r0:r0 + c, c0:c0 + HEAD_DIM] for (d, pos, r0, c0, h, q_ref, v_ref, f_ref, o_ref) in inst]
    vs = [v_ref[r0:r0 + c, c0:c0 + HEAD_DIM].astype(BF16) for (d, pos, r0, c0, h, q_ref, v_ref, f_ref, o_ref) in inst]
    es = [jnp.exp(-jnp.clip(f_ref[r0:r0 + c, c0:c0 + HEAD_DIM], -GATE_CLIP, GATE_CLIP))
          for (d, pos, r0, c0, h, q_ref, v_ref, f_ref, o_ref) in inst]
    sgs = [1.0 / (1.0 + e) for e in es]
    lbs = [lb_ref[0:1, it[3]:it[3] + HEAD_DIM] for it in inst]
    logfs = [jnp.log(lb + (1.0 - lb) * sg) for lb, sg in zip(lbs, sgs)]
    ks = [(1.0 - lb) * (e * sg) for lb, e, sg in zip(lbs, es, sgs)]
    his = [x.astype(BF16) for x in logfs]
    r1s = [x - hi.astype(F32) for x, hi in zip(logfs, his)]
    mids = [x.astype(BF16) for x in r1s]
    los = [(x - mid.astype(F32)).astype(BF16) for x, mid in zip(r1s, mids)]
    cats = [jnp.concatenate([hi, mid, lo], axis=0) for hi, mid, lo in zip(his, mids, los)]
    xs = []
    for i0 in range(0, len(inst), 2):
        x2 = _dot(sums_ref[inst[i0][0]], jnp.concatenate([cats[i0], cats[i0 + 1]], axis=1))
        xs += [x2[:, :HEAD_DIM], x2[:, HEAD_DIM:]]
    kbs = [k.astype(BF16) for k in ks]
    attns = [eye * _dot_nt(q.astype(BF16), kb) for q, kb in zip(qs, kbs)]
    for lvl in range(N_LEVELS):
        zls = [jnp.exp(x[(2 + lvl) * c:(3 + lvl) * c]) for x in xs]
        attns = [a + pair_ref[it[0], lvl] * _dot_nt((q * zl).astype(BF16), (k * zl).astype(BF16))
                 for a, it, q, k, zl in zip(attns, inst, qs, ks, zls)]
    qds = [(q * jnp.exp(x[0:c])).astype(BF16) for q, x in zip(qs, xs)]
    kds = [(k * jnp.exp(x[c:2 * c])).astype(BF16) for k, x in zip(ks, xs)]
    intra = [_dot(a.astype(BF16), v) for a, v in zip(attns, vs)]
    kvs = [_dot_tn(v, kd) for v, kd in zip(vs, kds)]
    _delta_steps(((uf, wqf, kdf, atf, egf, oaf_ref), (ur, wqr, kdr, atr, egr, oar_ref)), s_ref, sbd_ref, n_heads)
    states = [[st_ref[d, h] for h in range(n_heads)] for d in range(2)]
    for pos in range(2):
        for i, (d, p_i, r0, c0, h, q_ref, v_ref, f_ref, o_ref) in enumerate(inst):
            if p_i != pos:
                continue
            st = states[d][h]
            o_ref[r0:r0 + c, c0:c0 + HEAD_DIM] = _dot_nt(qds[i], st.astype(BF16)) + intra[i]
            last = c - 1 if d == 0 else 0
            states[d][h] = st * jnp.exp(xs[i][last:last + 1, :]) + kvs[i]
    for d in range(2):
        for h in range(n_heads):
            st_ref[d, h] = states[d][h]


def _recurrent_scans(prep_a, p, lb, n_batch, tb, ctx_len, mixw, col_q, col_i, col_f):
    uf, wqf, kdf, atf, ur, wqr, kdr, atr, eg = prep_a
    t = p.shape[0]
    n_heads = mixw // HEAD_DIM
    nsb = tb // STEP_ROWS
    fwd, rev = _seq_block_maps(nsb, ctx_len // STEP_ROWS)
    sums_np, pair_np = _hgrn_tables()
    sums = jnp.asarray(sums_np, BF16)
    pair = jnp.asarray(pair_np, F32)
    in_specs = []
    for m in (fwd, rev):
        in_specs += [pl.BlockSpec((STEP_ROWS, mixw), lambda b, n, m=m: (m(b, n), 0)),
                     pl.BlockSpec((2 * STEP_ROWS, mixw), lambda b, n, m=m: (m(b, n), 0)),
                     pl.BlockSpec((STEP_ROWS, mixw), lambda b, n, m=m: (m(b, n), 0)),
                     pl.BlockSpec((STEP_ROWS, n_heads * CHUNK), lambda b, n, m=m: (m(b, n), 0)),
                     pl.BlockSpec((STEP_ROWS, LANES), lambda b, n, m=m: (m(b, n), 0))]
    for d, m in enumerate((fwd, rev)):
        in_specs += [pl.BlockSpec((STEP_ROWS, mixw), lambda b, n, m=m: (m(b, n), col_q // mixw)),
                     pl.BlockSpec((STEP_ROWS, mixw), lambda b, n, m=m: (m(b, n), col_i // mixw)),
                     pl.BlockSpec((STEP_ROWS, mixw), lambda b, n, m=m, d=d: (m(b, n), col_f // mixw + d))]
    in_specs += [pl.BlockSpec((1, mixw), lambda b, n: (0, 0)),
                 pl.BlockSpec(sums.shape, lambda b, n: (0, 0, 0)),
                 pl.BlockSpec(pair.shape, lambda b, n: (0, 0, 0, 0))]
    return pl.pallas_call(
        functools.partial(_scan_kernel, n_heads=n_heads),
        grid=(n_batch, nsb),
        in_specs=in_specs,
        out_specs=[pl.BlockSpec((STEP_ROWS, mixw), lambda b, n: (fwd(b, n), 0)),
                   pl.BlockSpec((STEP_ROWS, mixw), lambda b, n: (rev(b, n), 0))] * 2,
        out_shape=[jax.ShapeDtypeStruct((t, mixw), F32)] * 4,
        scratch_shapes=[pltpu.VMEM((2, n_heads, HEAD_DIM, HEAD_DIM), F32),
                        pltpu.VMEM((2, mixw, mixw), BF16),
                        pltpu.VMEM((2, n_heads, HEAD_DIM, HEAD_DIM), F32)],
        compiler_params=_cparams(("arbitrary", "arbitrary")),
        name="recurrent_scans",
    )(uf, wqf, kdf, atf, eg, ur, wqr, kdr, atr, eg, p, p, p, p, p, p, lb, sums, pair)


def _route(logits_t, bias, n_experts):
    per = n_experts // N_EXPERT_GROUPS
    scores = _sigmoid(logits_t)
    sel = scores + bias
    rows_sel = [sel[e:e + 1, :] for e in range(n_experts)]
    rows_sc = [scores[e:e + 1, :] for e in range(n_experts)]

    def top2_sum(vals):
        hi = vals[0]
        lo = jnp.full_like(hi, -jnp.inf)
        for x in vals[1:]:
            lo = jnp.maximum(lo, jnp.minimum(hi, x))
            hi = jnp.maximum(hi, x)
        return hi + lo

    best = jnp.zeros_like(rows_sel[0], dtype=jnp.int32)
    best_score = top2_sum(rows_sel[0:per])
    for g in range(1, N_EXPERT_GROUPS):
        gs = top2_sum(rows_sel[g * per:(g + 1) * per])
        better = gs > best_score
        best = jnp.where(better, g, best)
        best_score = jnp.where(better, gs, best_score)
    in_sel, in_sc = [], []
    for j in range(per):
        a = rows_sel[j]
        c = rows_sc[j]
        for g in range(1, N_EXPERT_GROUPS):
            a = jnp.where(best == g, rows_sel[g * per + j], a)
            c = jnp.where(best == g, rows_sc[g * per + j], c)
        in_sel.append(a)
        in_sc.append(c)

    def first_argmax(vals):
        idx = jnp.zeros_like(best)
        top = vals[0]
        for j in range(1, per):
            better = vals[j] > top
            idx = jnp.where(better, j, idx)
            top = jnp.where(better, vals[j], top)
        return idx

    i1 = first_argmax(in_sel)
    i2 = first_argmax([jnp.where(i1 == j, -jnp.inf, in_sel[j]) for j in range(per)])

    def pick(idx):
        w = in_sc[0]
        for j in range(1, per):
            w = jnp.where(idx == j, in_sc[j], w)
        return w

    w1, w2 = pick(i1), pick(i2)
    tot = w1 + w2
    return best * per + i1, best * per + i2, w1 / tot, w2 / tot


def _outproj_kernel(oaf, oar, ga, odf, odr, gd, yb, yc, na, nd, wo, x_ref, m_ref, lng, lnb, wr, br,
                    x1_ref, h2_ref, ei_ref, ew_ref, *, alpha, mixw, tm, n_experts):
    n_heads = mixw // HEAD_DIM

    def gated(of_ref, or_ref, g_ref, gain_ref):
        parts = []
        for h in range(n_heads):
            c0 = h * HEAD_DIM
            o = of_ref[:, c0:c0 + HEAD_DIM] + or_ref[:, c0:c0 + HEAD_DIM]
            y = o * lax.rsqrt(jnp.mean(o * o, axis=-1, keepdims=True) + EPS) * gain_ref[...]
            parts.append((y * _silu(g_ref[:, c0:c0 + HEAD_DIM])).astype(BF16))
        return jnp.concatenate(parts, axis=1)

    ya = gated(oaf, oar, ga, na)
    yd = gated(odf, odr, gd, nd)
    y = (_dot(ya, wo[0:mixw, :]) + _dot(yb[...], wo[mixw:2 * mixw, :])
         + _dot(yc[...], wo[2 * mixw:3 * mixw, :]) + _dot(yd, wo[3 * mixw:4 * mixw, :]))
    x1 = _ln(alpha * x_ref[...] + m_ref[0, 2:3, :] * y) * lng[...] + lnb[...]
    x1_ref[...] = x1
    h2 = _ln(x1) * (1.0 + m_ref[0, 4:5, :]) + m_ref[0, 3:4, :]
    h2_ref[...] = h2
    h_hi = h2.astype(BF16)
    h_r = h2 - h_hi.astype(F32)
    h_mid = h_r.astype(BF16)
    h_lo = (h_r - h_mid.astype(F32)).astype(BF16)
    pa = _dot(h_hi, wr[:, 0:2 * LANES])
    pb = _dot(h_mid, wr[:, 0:2 * LANES])
    pc = _dot(h_hi, wr[:, 2 * LANES:4 * LANES])
    pd = _dot(h_lo, wr[:, 0:2 * LANES])
    logits = (((pd[:, :LANES] + pb[:, LANES:] + pc[:, :LANES]) + (pb[:, :LANES] + pa[:, LANES:]))
              + pa[:, :LANES])
    logits_t = logits.T[:n_experts]
    e1, e2, w1, w2 = _route(logits_t, br[:, 0:1], n_experts)
    zi = jnp.zeros((SUBLANES - TOP_K, tm), jnp.int32)
    ei_ref[...] = jnp.concatenate([e1, e2, zi], axis=0)
    ew_ref[...] = jnp.concatenate([w1, w2, zi.astype(F32)], axis=0)


def _outproj(oaf, oar, odf, odr, p, yb, yc, norm_a, norm_d, w_out, x, mods, ln_g, ln_b, w_router_p, b_router,
             alpha, tm, nb, nctx, n_batch, mixw, col_ga, col_gd):
    t, d = x.shape
    n_experts = b_router.shape[0]
    row = lambda i: (i, 0)
    const = lambda i: (0, 0)
    kern = functools.partial(_outproj_kernel, alpha=alpha, mixw=mixw, tm=tm, n_experts=n_experts)
    return pl.pallas_call(
        kern,
        grid=(t // tm,),
        in_specs=[pl.BlockSpec((tm, mixw), row), pl.BlockSpec((tm, mixw), row),
                  pl.BlockSpec((tm, mixw), lambda i: (i, col_ga // mixw)),
                  pl.BlockSpec((tm, mixw), row), pl.BlockSpec((tm, mixw), row),
                  pl.BlockSpec((tm, mixw), lambda i: (i, col_gd // mixw)),
                  pl.BlockSpec((tm, mixw), row), pl.BlockSpec((tm, mixw), row),
                  pl.BlockSpec((1, HEAD_DIM), const), pl.BlockSpec((1, HEAD_DIM), const),
                  pl.BlockSpec((N_MIXERS * mixw, d), const),
                  pl.BlockSpec((tm, d), row),
                  pl.BlockSpec((1, 6, d), lambda i: (_mod_row(i, nb, nctx, n_batch), 0, 0)),
                  pl.BlockSpec((1, d), const), pl.BlockSpec((1, d), const),
                  pl.BlockSpec((d, 4 * LANES), const), pl.BlockSpec((n_experts, LANES), const)],
        out_specs=[pl.BlockSpec((tm, d), row), pl.BlockSpec((tm, d), row),
                   pl.BlockSpec((SUBLANES, tm), lambda i: (0, i)), pl.BlockSpec((SUBLANES, tm), lambda i: (0, i))],
        out_shape=[jax.ShapeDtypeStruct((t, d), F32), jax.ShapeDtypeStruct((t, d), F32),
                   jax.ShapeDtypeStruct((SUBLANES, t), jnp.int32), jax.ShapeDtypeStruct((SUBLANES, t), F32)],
        compiler_params=_cparams(("parallel",)),
        name="out_proj_router",
    )(oaf, oar, p, odf, odr, p, yb, yc, norm_a, norm_d, w_out, x, mods, ln_g, ln_b, w_router_p, b_router)


def _rank_kernel(ei_ref, su_ref, rank_ref, cnt_ref, base_ref, *, n_experts, tm):
    i = pl.program_id(0)

    @pl.when(i == 0)
    def _():
        base_ref[...] = jnp.zeros_like(base_ref)

    eid = lax.broadcasted_iota(jnp.int32, (n_experts, tm), 0)
    o1 = (eid == ei_ref[0:1, :]).astype(F32)
    o2 = (eid == ei_ref[1:2, :]).astype(F32)
    cnt = o1 + o2
    before = _dot(cnt.astype(BF16), su_ref[...]) + base_ref[:, 0:1]
    r1 = jnp.sum(o1 * before, axis=0, keepdims=True)
    r2 = jnp.sum(o2 * before, axis=0, keepdims=True)
    rank_ref[...] = jnp.concatenate([r1, r2, jnp.zeros((SUBLANES - TOP_K, tm), F32)], axis=0).astype(jnp.int32)
    base_ref[...] = base_ref[...] + jnp.sum(cnt, axis=1, keepdims=True)
    cnt_ref[...] = base_ref[...]


def _slot_ranks(ei, n_experts, tm):
    t = ei.shape[1]
    r = np.arange(tm)
    su = jnp.asarray((r[:, None] < r[None, :]).astype(np.float32), BF16)
    return pl.pallas_call(
        functools.partial(_rank_kernel, n_experts=n_experts, tm=tm),
        grid=(t // tm,),
        in_specs=[pl.BlockSpec((SUBLANES, tm), lambda i: (0, i)), pl.BlockSpec((tm, tm), lambda i: (0, 0))],
        out_specs=[pl.BlockSpec((SUBLANES, tm), lambda i: (0, i)), pl.BlockSpec((n_experts, LANES), lambda i: (0, 0))],
        out_shape=[jax.ShapeDtypeStruct((SUBLANES, t), jnp.int32), jax.ShapeDtypeStruct((n_experts, LANES), F32)],
        scratch_shapes=[pltpu.VMEM((n_experts, LANES), F32)],
        compiler_params=_cparams(("arbitrary",)),
        name="moe_slot_ranks",
    )(ei, su)


def _row_copy(src_ref, src_row, dst_ref, dst_row, sem):
    return pltpu.make_async_copy(src_ref.at[pl.ds(src_row, 1)], dst_ref.at[pl.ds(dst_row, 1)], sem)


def _dispatch_kernel(dest_ref, zs_ref, na_ref, h_ref, xs_ref, zero_ref, sem, zsem, *, tm, t, n_experts, tmx, n_blocks):
    base = pl.program_id(0) * tm

    @pl.when(pl.program_id(0) == 0)
    def _():
        zero_ref[...] = jnp.zeros_like(zero_ref)
        for e in range(n_experts):
            fill = pltpu.make_async_copy(zero_ref, xs_ref.at[pl.ds(pl.multiple_of(zs_ref[e], FILL_ALIGN),
                                                                  tmx + FILL_ALIGN)], zsem)
            fill.start()
            fill.wait()

        def tail(b, carry):
            fill = pltpu.make_async_copy(zero_ref.at[pl.ds(0, tmx)],
                                         xs_ref.at[pl.ds(pl.multiple_of(b * tmx, tmx), tmx)], zsem)
            fill.start()
            fill.wait()
            return carry

        lax.fori_loop(na_ref[0], n_blocks, tail, 0)

    for r in range(tm):
        for kk in range(TOP_K):
            _row_copy(h_ref, r, xs_ref, dest_ref[kk * t + base + r], sem).start()
    for kk in range(TOP_K):
        pltpu.make_async_copy(h_ref, xs_ref.at[pl.ds(0, tm)], sem).wait()


def _dispatch(dest, zero_start, n_active, h2, n_slots, tm, tmx):
    t, d = h2.shape
    n_experts = zero_start.shape[0]
    return pl.pallas_call(
        functools.partial(_dispatch_kernel, tm=tm, t=t, n_experts=n_experts, tmx=tmx, n_blocks=n_slots // tmx),
        grid_spec=pltpu.PrefetchScalarGridSpec(
            num_scalar_prefetch=3,
            grid=(t // tm,),
            in_specs=[pl.BlockSpec((tm, d), lambda i, dest, zs, na: (i, 0))],
            out_specs=pl.BlockSpec(memory_space=pl.ANY),
            scratch_shapes=[pltpu.VMEM((tmx + FILL_ALIGN, d), h2.dtype), pltpu.SemaphoreType.DMA(()),
                            pltpu.SemaphoreType.DMA(())]),
        out_shape=jax.ShapeDtypeStruct((n_slots, d), h2.dtype),
        compiler_params=_cparams(("arbitrary",)),
        name="moe_dispatch",
    )(dest, zero_start, n_active, h2)


def _expert_kernel(be_ref, na_ref, x_ref, w1_ref, w3_ref, w2_ref, y_ref):
    i = pl.program_id(0)

    @pl.when(i < na_ref[0])
    def _():
        xb = x_ref[...].astype(BF16)
        h1 = _dot(xb, w1_ref[0, 0])
        h3 = _dot(xb, w3_ref[0, 0])
        y_ref[...] = _dot((_silu(h1) * h3).astype(BF16), w2_ref[0, 0])

    @pl.when(i >= na_ref[0])
    def _():
        y_ref[...] = jnp.zeros_like(y_ref)


def _experts(block_expert, n_active, xs, w1, w3, w2, layer, tmx):
    n_slots, d = xs.shape
    f = w1.shape[3]
    return pl.pallas_call(
        _expert_kernel,
        grid_spec=pltpu.PrefetchScalarGridSpec(
            num_scalar_prefetch=2,
            grid=(n_slots // tmx,),
            in_specs=[pl.BlockSpec((tmx, d), lambda i, be, na: (jnp.minimum(i, na[0] - 1), 0)),
                      pl.BlockSpec((1, 1, d, f), lambda i, be, na: (layer, be[i], 0, 0)),
                      pl.BlockSpec((1, 1, d, f), lambda i, be, na: (layer, be[i], 0, 0)),
                      pl.BlockSpec((1, 1, f, d), lambda i, be, na: (layer, be[i], 0, 0))],
            out_specs=pl.BlockSpec((tmx, d), lambda i, be, na: (i, 0))),
        out_shape=jax.ShapeDtypeStruct((n_slots, d), F32),
        compiler_params=_cparams(("arbitrary",)),
        name="moe_experts",
    )(block_expert, n_active, xs, w1, w3, w2)


def _combine_kernel(dest_ref, ys_ref, ew_ref, x1_ref, m_ref, lng, lnb, mn_ref, x2_ref, *rest,
                    tm, t, alpha, block_of_step, emit_next):
    hn_ref = rest[0] if emit_next else None
    g_ref, sem = rest[-2:]
    i = pl.program_id(0)
    n = pl.num_programs(0)

    def issue(step, slot):
        base = block_of_step(step) * tm

        for r in range(tm):
            for kk in range(TOP_K):
                _row_copy(ys_ref, dest_ref[kk * t + base + r], g_ref.at[slot, kk], r, sem.at[slot]).start()

    @pl.when(i == 0)
    def _():
        issue(0, 0)

    @pl.when(i + 1 < n)
    def _():
        issue(i + 1, (i + 1) % 2)

    slot = i % 2
    for kk in range(TOP_K):
        pltpu.make_async_copy(ys_ref.at[pl.ds(0, tm)], g_ref.at[slot, kk], sem.at[slot]).wait()
    row = lax.broadcasted_iota(jnp.int32, (tm, tm), 0)
    col = lax.broadcasted_iota(jnp.int32, (tm, tm), 1)
    w_t = _dot_nt((row == col).astype(F32), ew_ref[...], HIGHEST)
    f = w_t[:, 0:1] * g_ref[slot, 0] + w_t[:, 1:2] * g_ref[slot, 1]
    x2 = _ln(alpha * x1_ref[...] + m_ref[0, 5:6, :] * f) * lng[...] + lnb[...]
    x2_ref[...] = x2
    if emit_next:
        hn_ref[...] = (_ln(x2) * (1.0 + mn_ref[0, 1:2, :]) + mn_ref[0, 0:1, :]).astype(hn_ref.dtype)


def _combine(dest, ys, ew, x1, mods, ln_g, ln_b, mods_next, alpha, tm, nb, nctx, n_batch, latent_only):
    t, d = x1.shape
    nlat = nb - nctx
    if latent_only:
        n_steps = n_batch * nlat
        block_of_step = lambda i: (i // nlat) * nb + nctx + i % nlat
    else:
        n_steps = t // tm
        block_of_step = lambda i: i
    mrow = lambda i, dest: (_mod_row(block_of_step(i), nb, nctx, n_batch), 0, 0)
    const = lambda i, dest: (0, 0)
    out_specs = [pl.BlockSpec((tm, d), lambda i, dest: (i, 0))]
    out_shape = [jax.ShapeDtypeStruct((n_steps * tm, d), F32)]
    if not latent_only:
        out_specs.append(pl.BlockSpec((tm, d), lambda i, dest: (i, 0)))
        out_shape.append(jax.ShapeDtypeStruct((t, d), BF16))
    return pl.pallas_call(
        functools.partial(_combine_kernel, tm=tm, t=t, alpha=alpha, block_of_step=block_of_step,
                          emit_next=not latent_only),
        grid_spec=pltpu.PrefetchScalarGridSpec(
            num_scalar_prefetch=1,
            grid=(n_steps,),
            in_specs=[pl.BlockSpec(memory_space=pl.ANY),
                      pl.BlockSpec((SUBLANES, tm), lambda i, dest: (0, block_of_step(i))),
                      pl.BlockSpec((tm, d), lambda i, dest: (block_of_step(i), 0)),
                      pl.BlockSpec((1, 6, d), mrow),
                      pl.BlockSpec((1, d), const), pl.BlockSpec((1, d), const),
                      pl.BlockSpec((1, 6, d), mrow)],
            out_specs=out_specs,
            scratch_shapes=[pltpu.VMEM((2, TOP_K, tm, d), F32), pltpu.SemaphoreType.DMA((2,))]),
        out_shape=out_shape,
        compiler_params=_cparams(("arbitrary",)),
        name="moe_combine",
    )(dest, ys, ew, x1, mods, ln_g, ln_b, mods_next)


def _axial_tables(seq, ctx_len):
    rows = seq // GRID_W
    row = jnp.repeat(jnp.arange(rows, dtype=F32), GRID_W)
    col = jnp.tile(jnp.arange(GRID_W, dtype=F32), rows)
    n_freq = HEAD_DIM // 4
    inv_freq = ROPE_THETA ** (-jnp.arange(n_freq, dtype=F32) / n_freq)
    ang = jnp.concatenate([row[:, None] * inv_freq, col[:, None] * inv_freq], axis=-1)
    cos, sin = jnp.cos(ang), jnp.sin(ang)
    cosf = jnp.concatenate([cos, cos], axis=-1)
    sinf = jnp.concatenate([-sin, sin], axis=-1)
    cosf = jnp.concatenate([jnp.ones((ctx_len, HEAD_DIM), F32), cosf], axis=0)
    sinf = jnp.concatenate([jnp.zeros((ctx_len, HEAD_DIM), F32), sinf], axis=0)
    return cosf, sinf


def kernel(x, c, ctx, c_ctx, w_ada, b_ada, ln_g, ln_b, w_in, conv_a, a_log, dt_bias, norm_a, qk_norm_b, ws_c, bs_c, lb_d, norm_d, w_out, w_router, b_router, w1, w3, w2):
    n_batch, seq, d = x.shape
    ctx_len = ctx.shape[1]
    n_layers = w_in.shape[0]
    n_experts = w_router.shape[1]
    mixw = d // N_MIXERS
    n_heads = mixw // HEAD_DIM
    tb = ctx_len + seq
    t = n_batch * tb
    alpha = (2.0 * n_layers) ** 0.25
    tm = _pick(math.gcd(ctx_len, seq), (256, 128))
    nb, nctx = tb // tm, ctx_len // tm

    col = {"a_qkv": 0, "a_gate": 3 * mixw, "d_f": 4 * mixw, "b_q": 6 * mixw, "b_kv": 7 * mixw, "c_u": 8 * mixw,
           "c_v": 9 * mixw, "d_q": 10 * mixw, "d_i": 11 * mixw, "d_gate": 12 * mixw, "small": 13 * mixw}
    n_proj = 13 * mixw + 256
    src = np.cumsum([0, 3 * mixw, mixw, 2 * n_heads, 2 * n_heads, mixw, mixw, mixw, mixw, mixw, mixw, mixw, 2 * mixw])
    s_qkv, s_ga, s_beta, s_dec, s_bq, s_bkv, s_cu, s_cv, s_dq, s_di, s_dg, s_df, s_end = [int(v) for v in src]

    def permute_w_in(w):
        pad = jnp.zeros((d, n_proj - 13 * mixw - 4 * n_heads), w.dtype)
        return jnp.concatenate([w[:, s_qkv:s_beta], w[:, s_df:s_end], w[:, s_bq:s_df], w[:, s_beta:s_bq], pad],
                               axis=1).astype(BF16)

    assert n_batch + 1 <= SUBLANES
    cc = jnp.zeros((SUBLANES, d), F32).at[:n_batch].set(c).at[n_batch].set(c_ctx)
    mods_all = _mod_vectors(cc, w_ada, b_ada)[:, :n_batch + 1].reshape(n_layers, n_batch + 1, 6, d)

    soft = jax.nn.softmax(lb_d.astype(F32), axis=0)
    lb_all = jnp.cumsum(soft, axis=0) - soft[0]
    cosf, sinf = _axial_tables(seq, ctx_len)
    gate_lanes = jnp.zeros((1, LANES), F32)
    wr_f = w_router.astype(F32)
    wr_hi = wr_f.astype(BF16)
    wr_r = wr_f - wr_hi.astype(F32)
    wr_mid = wr_r.astype(BF16)
    wr_lo = (wr_r - wr_mid.astype(F32)).astype(BF16)
    w_router_p = jnp.zeros((d, 4 * LANES), BF16)
    for blk, part in enumerate((wr_hi, wr_mid, wr_lo)):
        w_router_p = w_router_p.at[:, blk * LANES:blk * LANES + n_experts].set(part)
    b_router_c = jnp.broadcast_to(b_router.astype(F32)[:, None], (n_experts, LANES))

    tmx = EXPERT_BLOCK_ROWS
    n_assign = t * TOP_K
    n_slots = (-(-n_assign // tmx) + n_experts + 2) * tmx

    w1_b, w3_b, w2_b = w1.astype(BF16), w3.astype(BF16), w2.astype(BF16)
    xu = jnp.concatenate([ctx, x], axis=1).reshape(t, d)
    h = _ln_modulate(xu, mods_all[0], tm, nb, nctx, n_batch)
    for l in range(n_layers):
        mods = mods_all[l]
        p = _matmul(h, permute_w_in(w_in[l]))
        neg_a = gate_lanes.at[0, 2 * n_heads:4 * n_heads].set(-jnp.exp(a_log[l].astype(F32)).reshape(-1))
        dtb = gate_lanes.at[0, 2 * n_heads:4 * n_heads].set(dt_bias[l].astype(F32).reshape(-1))
        prep_a = _a_prep(p, conv_a[l], neg_a, dtb, tm, nb, nctx, mixw, col["small"])
        oaf, oar, odf, odr = _recurrent_scans(prep_a, p, lb_all[l][None, :], n_batch, tb, ctx_len, mixw,
                                              col["d_q"], col["d_i"], col["d_f"])
        qb, kb, vb = _b_prep(p, qk_norm_b[l], cosf, sinf, tm, nb, mixw, col["b_q"], col["b_kv"])
        yb = _attention(qb, kb, vb, n_batch, tb, ctx_len, mixw)
        bs_t = jnp.zeros((CHUNK_C, LANES), F32).at[:, :bs_c.shape[1]].set(bs_c[l].T)
        yc = _mixer_c(p, ws_c[l].astype(BF16), bs_t, tm, mixw, col["c_u"], col["c_v"])
        x1, h2, ei, ew = _outproj(oaf, oar, odf, odr, p, yb, yc, norm_a[l][None, :], norm_d[l][None, :],
                                  w_out[l].astype(BF16), xu, mods, ln_g[l, 0][None, :], ln_b[l, 0][None, :],
                                  w_router_p, b_router_c, alpha, tm, nb, nctx, n_batch, mixw,
                                  col["a_gate"], col["d_gate"])
        rank, counts = _slot_ranks(ei, n_experts, tm)
        cnt = counts[:, 0].astype(jnp.int32)
        padded = (cnt + tmx - 1) // tmx * tmx
        pad_end = jnp.cumsum(padded)
        pad_start = pad_end - padded
        eids = jnp.arange(n_experts, dtype=jnp.int32)[:, None, None]
        dest = (rank[:TOP_K] + jnp.sum(jnp.where(ei[None, :TOP_K] == eids, pad_start[:, None, None], 0),
                                       axis=0)).reshape(-1)
        n_blocks = n_slots // tmx
        block_start = jnp.arange(n_blocks, dtype=jnp.int32) * tmx
        block_expert = jnp.minimum(jnp.sum((pad_end[None, :] <= block_start[:, None]).astype(jnp.int32), axis=1),
                                   n_experts - 1)
        n_active = (pad_end[-1:] // tmx).astype(jnp.int32)
        fill_start = (pad_start + cnt) // FILL_ALIGN * FILL_ALIGN
        xs = _dispatch(dest, fill_start, n_active, h2, n_slots, _pick(t, (2 * tm, tm)), tmx)
        ys = _experts(block_expert, n_active, xs, w1_b, w3_b, w2_b, l, tmx)
        last = l == n_layers - 1
        res = _combine(dest, ys, ew, x1, mods, ln_g[l, 1][None, :], ln_b[l, 1][None, :],
                       mods_all[min(l + 1, n_layers - 1)], alpha, tm, nb, nctx, n_batch, latent_only=last)
        if last:
            return res[0].reshape(n_batch, seq, d)
        xu, h = res
```

```python
import functools
import math

import numpy as np
import jax
import jax.numpy as jnp
from jax import lax
from jax.experimental import pallas as pl
from jax.experimental.pallas import tpu as pltpu

F32 = jnp.float32
BF16 = jnp.bfloat16
HIGHEST = lax.Precision.HIGHEST

HEAD_DIM = 128
N_MIXERS = 4
CONV_W = 5
CHUNK = 64
STEP_ROWS = 2 * CHUNK
SOLVE_ROWS = 2 * CHUNK
CHUNK_C = 128
GRID_W = 64
ROPE_THETA = 10000.0
N_EXPERT_GROUPS = 4
TOP_K = 2
EPS = 1e-6
LOG2_E = math.log2(math.e)
GATE_CLIP = 30.0
N_LEVELS = 6
LANES = 128
SUBLANES = 8
HALO = SUBLANES
FILL_ALIGN = SUBLANES
EXPERT_BLOCK_ROWS = 256
V7X_VMEM_LIMIT = 56 * 1024 * 1024


def _pick(n, cands):
    for c in cands:
        if n % c == 0:
            return c
    raise ValueError(f"no tile in {cands} divides {n}")


def _cparams(sem, vmem=None):
    return pltpu.CompilerParams(dimension_semantics=sem, vmem_limit_bytes=vmem or V7X_VMEM_LIMIT)


def _dot(a, b, precision=None):
    return jnp.dot(a, b, preferred_element_type=F32, precision=precision)


def _dot_nt(a, b, precision=None):
    return lax.dot_general(a, b, (((1,), (1,)), ((), ())), preferred_element_type=F32, precision=precision)


def _dot_tn(a, b, precision=None):
    return lax.dot_general(a, b, (((0,), (0,)), ((), ())), preferred_element_type=F32, precision=precision)


def _sigmoid(x):
    return 1.0 / (1.0 + jnp.exp(-x))


def _silu(x):
    return x * _sigmoid(x)


def _gelu_tanh(x):
    return 0.5 * x * (1.0 + jnp.tanh(math.sqrt(2.0 / math.pi) * (x + 0.044715 * (x * x * x))))


def _ln(x):
    mu = jnp.mean(x, axis=-1, keepdims=True)
    xc = x - mu
    return xc * lax.rsqrt(jnp.mean(xc * xc, axis=-1, keepdims=True) + EPS)


def _mod_row(i, nb, nctx, n_batch):
    return jnp.where(i % nb < nctx, n_batch, i // nb)


def _mod_kernel(c_ref, w_ref, b_ref, o_ref):
    o_ref[0] = _dot(_silu(c_ref[...]), w_ref[0], HIGHEST) + b_ref[0]


def _mod_vectors(cc, w_ada, b_ada):
    n_layers, d, n6 = w_ada.shape
    tn = _pick(n6, (1024, 512, 128))
    return pl.pallas_call(
        _mod_kernel,
        grid=(n_layers, n6 // tn),
        in_specs=[pl.BlockSpec((SUBLANES, d), lambda l, j: (0, 0)),
                  pl.BlockSpec((1, d, tn), lambda l, j: (l, 0, j)),
                  pl.BlockSpec((1, 1, tn), lambda l, j: (l, 0, j))],
        out_specs=pl.BlockSpec((1, SUBLANES, tn), lambda l, j: (l, 0, j)),
        out_shape=jax.ShapeDtypeStruct((n_layers, SUBLANES, n6), F32),
        compiler_params=_cparams(("parallel", "parallel")),
        name="mod_vectors",
    )(cc, w_ada, b_ada.reshape(n_layers, 1, n6))


def _lnmod_kernel(x_ref, m_ref, h_ref):
    y = _ln(x_ref[...])
    h_ref[...] = (y * (1.0 + m_ref[0, 1:2, :]) + m_ref[0, 0:1, :]).astype(h_ref.dtype)


def _ln_modulate(x, mods, tm, nb, nctx, n_batch):
    t, d = x.shape
    return pl.pallas_call(
        _lnmod_kernel,
        grid=(t // tm,),
        in_specs=[pl.BlockSpec((tm, d), lambda i: (i, 0)),
                  pl.BlockSpec((1, 6, d), lambda i: (_mod_row(i, nb, nctx, n_batch), 0, 0))],
        out_specs=pl.BlockSpec((tm, d), lambda i: (i, 0)),
        out_shape=jax.ShapeDtypeStruct((t, d), BF16),
        compiler_params=_cparams(("parallel",)),
        name="ln_modulate",
    )(x, mods)


def _matmul_kernel(a_ref, w_ref, o_ref):
    o_ref[...] = _dot(a_ref[...], w_ref[...]).astype(o_ref.dtype)


def _matmul(a, w, out_dtype=F32):
    m, k = a.shape
    n = w.shape[1]
    tm = _pick(m, (2112, 1536, 1024, 768, 512, 256, 128))
    tn = _pick(n, (768, 1152, 1024, 512, 256, 128))
    return pl.pallas_call(
        _matmul_kernel,
        grid=(m // tm, n // tn),
        in_specs=[pl.BlockSpec((tm, k), lambda i, j: (i, 0)),
                  pl.BlockSpec((k, tn), lambda i, j: (0, j))],
        out_specs=pl.BlockSpec((tm, tn), lambda i, j: (i, j)),
        out_shape=jax.ShapeDtypeStruct((m, n), out_dtype),
        compiler_params=_cparams(("parallel", "parallel")),
        name="in_proj",
    )(a, w)


def _unit_tri_inverses(lows, eye, pair_refs):
    ts = [eye - low * pair_ref[0] for low, pair_ref in zip(lows, pair_refs)]
    for lvl in range(1, N_LEVELS):
        tbs = [t.astype(BF16) for t in ts]
        xs = [_dot(tb16, (low * pair_ref[lvl]).astype(BF16)) for tb16, low, pair_ref in zip(tbs, lows, pair_refs)]
        ts = [t - _dot(x.astype(BF16), tb16) for t, x, tb16 in zip(ts, xs, tbs)]
    return ts


def _a_prep_kernel(x_ref, xp_ref, xn_ref, sm_ref, cw_ref, na_ref, dtb_ref, trf_ref, trr_ref, pairf_ref, pairr_ref,
                   uf_ref, wqf_ref, kdf_ref, atf_ref, ur_ref, wqr_ref, kdr_ref, atr_ref, eg_ref, ext_ref,
                   *, tm, nb, nctx, mixw):
    i = pl.program_id(0)
    j = i % nb
    first = jnp.logical_or(j == 0, j == nctx)
    last = jnp.logical_or(j == nctx - 1, j == nb - 1)
    ext_ref[HALO:HALO + tm, :] = x_ref[...]
    ext_ref[0:HALO, :] = jnp.where(first, 0.0, xp_ref[...])
    ext_ref[HALO + tm:2 * HALO + tm, :] = jnp.where(last, 0.0, xn_ref[...])
    off = HALO - CONV_W // 2
    acc = cw_ref[0:1, :] * ext_ref[off:off + tm, :]
    for t in range(1, CONV_W):
        acc = acc + cw_ref[t:t + 1, :] * ext_ref[off + t:off + t + tm, :]
    y = _silu(acc)
    n_heads = mixw // HEAD_DIM
    sm = sm_ref[...]
    lane = lax.broadcasted_iota(jnp.int32, sm.shape, 1)
    zz = sm + dtb_ref[...]
    softplus = jnp.maximum(zz, 0.0) + jnp.log(1.0 + jnp.exp(-jnp.abs(zz)))
    g = jnp.where(jnp.logical_and(lane >= 2 * n_heads, lane < 4 * n_heads), na_ref[...] * softplus, 0.0)
    beta_all = _sigmoid(sm)
    tri_f = trf_ref[...]
    tri_r = trr_ref[...]
    cs_f = _dot(tri_f, g, HIGHEST)
    cs_r = _dot(tri_r, g, HIGHEST)
    gcum = jnp.where(lane < 3 * n_heads, cs_f, cs_r)
    gtot = _dot(jnp.maximum(tri_f, tri_r), g, HIGHEST)
    eg_ref[...] = jnp.exp(gtot)
    gcum_t = gcum.T
    sr = SOLVE_ROWS
    row = lax.broadcasted_iota(jnp.int32, (sr, sr), 0)
    col = lax.broadcasted_iota(jnp.int32, (sr, sr), 1)
    eye = (row == col).astype(F32)
    incls = (tri_f[0:sr, 0:sr] > 0.5, tri_r[0:sr, 0:sr] > 0.5)
    pairs = ([pairf_ref[lvl, 0:sr, 0:sr] for lvl in range(N_LEVELS)],
             [pairr_ref[lvl, 0:sr, 0:sr] for lvl in range(N_LEVELS)])
    outs = ((uf_ref, wqf_ref, kdf_ref, atf_ref), (ur_ref, wqr_ref, kdr_ref, atr_ref))
    items = []
    for sb in range(tm // sr):
        rs = sb * sr
        for h in range(n_heads):
            c0 = h * HEAD_DIM
            qh = y[rs:rs + sr, c0:c0 + HEAD_DIM]
            kh = y[rs:rs + sr, mixw + c0:mixw + c0 + HEAD_DIM]
            vh = y[rs:rs + sr, 2 * mixw + c0:2 * mixw + c0 + HEAD_DIM]
            qh = qh * (lax.rsqrt(jnp.sum(qh * qh, axis=-1, keepdims=True) + EPS) * HEAD_DIM ** -0.5)
            kh = kh * lax.rsqrt(jnp.sum(kh * kh, axis=-1, keepdims=True) + EPS)
            kb16 = kh.astype(BF16)
            qk = _dot_nt(qh.astype(BF16), kb16)
            for d in range(2):
                lb = d * n_heads + h
                lg = (2 + d) * n_heads + h
                beta = beta_all[rs:rs + sr, lb:lb + 1]
                gc = gcum[rs:rs + sr, lg:lg + 1]
                incl = incls[d]
                decay = jnp.where(incl, jnp.exp(jnp.where(incl, gc - gcum_t[lg:lg + 1, rs:rs + sr], 0.0)), 0.0)
                kbeta = kh * beta
                low = jnp.where(row == col, 0.0, _dot_nt(kbeta.astype(BF16), kb16) * decay)
                rhs = jnp.concatenate([vh * beta, kbeta * jnp.exp(gc)], axis=1).astype(BF16)
                items.append(dict(h=h, d=d, rs=rs, low=low, rhs=rhs, attn=(qk * decay).astype(BF16),
                                  qd=(qh * jnp.exp(gc)).astype(BF16),
                                  kd=(kh * jnp.exp(gtot[rs:rs + sr, lg:lg + 1] - gc)).astype(BF16)))
    tinvs = _unit_tri_inverses([it["low"] for it in items], eye, [pairs[it["d"]] for it in items])
    for it, tinv in zip(items, tinvs):
        u_ref, wq_ref, kd_ref, at_ref = outs[it["d"]]
        h, rs = it["h"], it["rs"]
        c0 = h * HEAD_DIM
        sol = _dot(tinv.astype(BF16), it["rhs"])
        u_ref[rs:rs + sr, c0:c0 + HEAD_DIM] = sol[:, :HEAD_DIM]
        w = sol[:, HEAD_DIM:].astype(BF16)
        kd_ref[rs:rs + sr, c0:c0 + HEAD_DIM] = it["kd"]
        for ci in range(sr // CHUNK):
            r0 = ci * CHUNK
            g0 = rs + r0
            wq_ref[2 * g0:2 * g0 + CHUNK, c0:c0 + HEAD_DIM] = w[r0:r0 + CHUNK]
            wq_ref[2 * g0 + CHUNK:2 * g0 + 2 * CHUNK, c0:c0 + HEAD_DIM] = it["qd"][r0:r0 + CHUNK]
            at_ref[g0:g0 + CHUNK, h * CHUNK:(h + 1) * CHUNK] = it["attn"][r0:r0 + CHUNK, r0:r0 + CHUNK]


def _a_prep(p, conv_w, neg_a, dtb, tm, nb, nctx, mixw, col_small):
    t = p.shape[0]
    c3 = 3 * mixw
    n_heads = mixw // HEAD_DIM
    r = np.arange(tm)
    same = (r[:, None] // CHUNK) == (r[None, :] // CHUNK)
    tri_f = jnp.asarray((same & (r[None, :] <= r[:, None])).astype(np.float32))
    tri_r = jnp.asarray((same & (r[None, :] >= r[:, None])).astype(np.float32))
    pair_f = []
    for lvl in range(N_LEVELS):
        s_blk = 1 << lvl
        joined = ((r[:, None] // (2 * s_blk)) == (r[None, :] // (2 * s_blk))) \
            & ((r[:, None] % (2 * s_blk)) >= s_blk) & ((r[None, :] % (2 * s_blk)) < s_blk)
        pair_f.append(joined.astype(np.float32))
    pair_f = np.stack(pair_f)
    pair_r = jnp.asarray(np.transpose(pair_f, (0, 2, 1)))
    pair_f = jnp.asarray(pair_f)
    hb = tm // HALO
    last_hb = t // HALO - 1
    kern = functools.partial(_a_prep_kernel, tm=tm, nb=nb, nctx=nctx, mixw=mixw)
    per_dir_specs = [pl.BlockSpec((tm, mixw), lambda i: (i, 0)),
                     pl.BlockSpec((2 * tm, mixw), lambda i: (i, 0)),
                     pl.BlockSpec((tm, mixw), lambda i: (i, 0)),
                     pl.BlockSpec((tm, n_heads * CHUNK), lambda i: (i, 0))]
    per_dir_shapes = [jax.ShapeDtypeStruct((t, mixw), F32), jax.ShapeDtypeStruct((2 * t, mixw), BF16),
                      jax.ShapeDtypeStruct((t, mixw), BF16), jax.ShapeDtypeStruct((t, n_heads * CHUNK), BF16)]
    return pl.pallas_call(
        kern,
        grid=(t // tm,),
        in_specs=[pl.BlockSpec((tm, c3), lambda i: (i, 0)),
                  pl.BlockSpec((HALO, c3), lambda i: (jnp.maximum(i * hb - 1, 0), 0)),
                  pl.BlockSpec((HALO, c3), lambda i: (jnp.minimum((i + 1) * hb, last_hb), 0)),
                  pl.BlockSpec((tm, LANES), lambda i: (i, col_small // LANES)),
                  pl.BlockSpec((CONV_W, c3), lambda i: (0, 0)),
                  pl.BlockSpec((1, LANES), lambda i: (0, 0)),
                  pl.BlockSpec((1, LANES), lambda i: (0, 0)),
                  pl.BlockSpec((tm, tm), lambda i: (0, 0)),
                  pl.BlockSpec((tm, tm), lambda i: (0, 0)),
                  pl.BlockSpec((N_LEVELS, tm, tm), lambda i: (0, 0, 0)),
                  pl.BlockSpec((N_LEVELS, tm, tm), lambda i: (0, 0, 0))],
        out_specs=per_dir_specs * 2 + [pl.BlockSpec((tm, LANES), lambda i: (i, 0))],
        out_shape=per_dir_shapes * 2 + [jax.ShapeDtypeStruct((t, LANES), F32)],
        scratch_shapes=[pltpu.VMEM((tm + 2 * HALO, c3), F32)],
        compiler_params=_cparams(("parallel",)),
        name="deltanet_prep",
    )(p, p, p, p, conv_w, neg_a, dtb, tri_f, tri_r, pair_f, pair_r)


def _delta_steps(dirs, s_ref, sbd_ref, n_heads):
    mixw = n_heads * HEAD_DIM
    lane_head = lax.broadcasted_iota(jnp.int32, (CHUNK, mixw), 1) // HEAD_DIM
    for pos in range(2):
        r0s = [(pos if d == 0 else 1 - pos) * CHUNK for d in range(2)]
        rs = [_dot(dirs[d][1][2 * r0s[d]:2 * r0s[d] + 2 * CHUNK, :], sbd_ref[d]) for d in range(2)]
        v_news = [dirs[d][0][r0s[d]:r0s[d] + CHUNK, :] - rs[d][:CHUNK] for d in range(2)]
        v_stacks = [jnp.concatenate([jnp.where(lane_head == h, v_news[d], 0.0) for h in range(n_heads)],
                                    axis=0).astype(BF16) for d in range(2)]
        for d in range(2):
            dirs[d][5][r0s[d]:r0s[d] + CHUNK, :] = (rs[d][CHUNK:]
                                                    + _dot(dirs[d][3][r0s[d]:r0s[d] + CHUNK, :], v_stacks[d]))
        vnbs = [v.astype(BF16) for v in v_news]
        for h in range(n_heads):
            c0 = h * HEAD_DIM
            for d in range(2):
                lg = (2 + d) * n_heads + h
                r0 = r0s[d]
                s_new = (s_ref[d, h] * dirs[d][4][r0:r0 + 1, lg:lg + 1]
                         + _dot_tn(dirs[d][2][r0:r0 + CHUNK, c0:c0 + HEAD_DIM], vnbs[d][:, c0:c0 + HEAD_DIM]))
                s_ref[d, h] = s_new
                sbd_ref[d, c0:c0 + HEAD_DIM, c0:c0 + HEAD_DIM] = s_new.astype(BF16)


def _seq_block_maps(nsb, nctx_sb):
    def fwd(b, n):
        return b * nsb + n

    def rev(b, n):
        return b * nsb + jnp.where(n < nctx_sb, nctx_sb - 1 - n, nsb - 1 - (n - nctx_sb))

    return fwd, rev


def _b_prep_kernel(q_ref, kv_ref, gain_ref, cos_ref, sin_ref, qo_ref, ko_ref, vo_ref, *, mixw):
    n_heads = mixw // HEAD_DIM
    n_kv = n_heads // 2
    cosf = cos_ref[...]
    sinf = sin_ref[...]

    def norm_rope(xh, gain, scale):
        yh = xh * lax.rsqrt(jnp.mean(xh * xh, axis=-1, keepdims=True) + EPS) * gain
        return (yh * cosf + pltpu.roll(yh, HEAD_DIM // 2, 1) * sinf) * scale

    for h in range(n_heads):
        c0 = h * HEAD_DIM
        qo_ref[:, c0:c0 + HEAD_DIM] = norm_rope(q_ref[:, c0:c0 + HEAD_DIM], gain_ref[0:1, :],
                                                HEAD_DIM ** -0.5 * LOG2_E).astype(qo_ref.dtype)
    for h in range(n_kv):
        c0 = h * HEAD_DIM
        ko_ref[:, c0:c0 + HEAD_DIM] = norm_rope(kv_ref[:, c0:c0 + HEAD_DIM], gain_ref[1:2, :], 1.0).astype(ko_ref.dtype)
    ones = jnp.ones((q_ref.shape[0], HEAD_DIM), vo_ref.dtype)
    for h in range(n_kv):
        c0 = (n_kv + h) * HEAD_DIM
        vo_ref[:, 2 * h * HEAD_DIM:(2 * h + 1) * HEAD_DIM] = kv_ref[:, c0:c0 + HEAD_DIM].astype(vo_ref.dtype)
        vo_ref[:, (2 * h + 1) * HEAD_DIM:(2 * h + 2) * HEAD_DIM] = ones


def _b_prep(p, qk_gain, cosf, sinf, tm, nb, mixw, col_q, col_kv):
    t = p.shape[0]
    kvw = mixw // 2
    return pl.pallas_call(
        functools.partial(_b_prep_kernel, mixw=mixw),
        grid=(t // tm,),
        in_specs=[pl.BlockSpec((tm, mixw), lambda i: (i, col_q // mixw)),
                  pl.BlockSpec((tm, mixw), lambda i: (i, col_kv // mixw)),
                  pl.BlockSpec((2, HEAD_DIM), lambda i: (0, 0)),
                  pl.BlockSpec((tm, HEAD_DIM), lambda i: (i % nb, 0)),
                  pl.BlockSpec((tm, HEAD_DIM), lambda i: (i % nb, 0))],
        out_specs=[pl.BlockSpec((tm, mixw), lambda i: (i, 0)),
                   pl.BlockSpec((tm, kvw), lambda i: (i, 0)),
                   pl.BlockSpec((tm, 2 * kvw), lambda i: (i, 0))],
        out_shape=[jax.ShapeDtypeStruct((t, mixw), BF16),
                   jax.ShapeDtypeStruct((t, kvw), BF16),
                   jax.ShapeDtypeStruct((t, 2 * kvw), BF16)],
        compiler_params=_cparams(("parallel",)),
        name="gqa_prep",
    )(p, p, qk_gain, cosf, sinf)


def _attn_kernel(q_ref, k_ref, v_ref, o_ref, m_ref, acc_ref, sa_ref, sb_ref, *, tq, tk, nk, nctx_q, ctx_len):
    qi = pl.program_id(2)
    qb = q_ref[...]
    q2 = jnp.concatenate([qb[:, :HEAD_DIM], qb[:, HEAD_DIM:]], axis=0)
    m_ref[...] = jnp.full_like(m_ref, -1e30)
    acc_ref[...] = jnp.zeros_like(acc_ref)

    def scores(i, s_ref):
        s_ref[...] = _dot_nt(q2, k_ref[i * tk:(i + 1) * tk, :])

    def absorb(s_of, vc):
        for g in range(2):
            sl = slice(g * tq, (g + 1) * tq)
            s = s_of(sl)
            m_prev = m_ref[sl, :]
            m_new = jnp.maximum(m_prev, jnp.max(s, axis=-1, keepdims=True))
            p = jnp.exp2(s - m_new)
            acc_ref[sl, :] = jnp.exp2(m_prev - m_new) * acc_ref[sl, :] + _dot(p.astype(BF16), vc)
            m_ref[sl, :] = m_new

    def absorb_chunk(i, s_ref):
        absorb(lambda sl: s_ref[sl, :], v_ref[i * tk:(i + 1) * tk, :])

    @pl.when(qi >= nctx_q)
    def _():
        bufs = (sa_ref, sb_ref)
        scores(0, bufs[0])
        for i in range(nk):
            if i + 1 < nk:
                scores(i + 1, bufs[(i + 1) % 2])
            absorb_chunk(i, bufs[i % 2])

    @pl.when(qi < nctx_q)
    def _():
        s_ctx = _dot_nt(q2, k_ref[0:ctx_len, :])
        absorb(lambda sl: s_ctx[sl, :], v_ref[0:ctx_len, :])

    acc = acc_ref[...]
    o = acc[:, :HEAD_DIM] / acc[:, HEAD_DIM:HEAD_DIM + 1]
    o_ref[...] = jnp.concatenate([o[:tq], o[tq:]], axis=1).astype(o_ref.dtype)


def _attention(q, k, v, n_batch, tb, ctx_len, mixw):
    t = q.shape[0]
    n_kv = mixw // HEAD_DIM // 2
    tq = _pick(math.gcd(ctx_len, tb), (256, 128))
    tk = _pick(tb, (1408, 768, 1024, 512, 640, 384, 256, 128))
    nq, nk = tb // tq, tb // tk
    return pl.pallas_call(
        functools.partial(_attn_kernel, tq=tq, tk=tk, nk=nk, nctx_q=ctx_len // tq, ctx_len=ctx_len),
        grid=(n_batch, n_kv, nq),
        in_specs=[pl.BlockSpec((tq, 2 * HEAD_DIM), lambda b, j, qi: (b * nq + qi, j)),
                  pl.BlockSpec((tb, HEAD_DIM), lambda b, j, qi: (b, j)),
                  pl.BlockSpec((tb, 2 * HEAD_DIM), lambda b, j, qi: (b, j))],
        out_specs=pl.BlockSpec((tq, 2 * HEAD_DIM), lambda b, j, qi: (b * nq + qi, j)),
        out_shape=jax.ShapeDtypeStruct((t, mixw), BF16),
        scratch_shapes=[pltpu.VMEM((2 * tq, 1), F32), pltpu.VMEM((2 * tq, 2 * HEAD_DIM), F32),
                        pltpu.VMEM((2 * tq, tk), F32), pltpu.VMEM((2 * tq, tk), F32)],
        compiler_params=_cparams(("parallel", "parallel", "parallel")),
        name="gqa_attention",
    )(q, k, v)


def _c_kernel(u_ref, v_ref, ws_ref, bs_ref, o_ref, *, tm, mixw):
    n_groups = mixw // HEAD_DIM
    for ci in range(tm // CHUNK_C):
        r0 = ci * CHUNK_C
        u = _gelu_tanh(u_ref[r0:r0 + CHUNK_C, :])
        vn = _ln(_gelu_tanh(v_ref[r0:r0 + CHUNK_C, :])).astype(BF16)
        for g in range(n_groups):
            c0 = g * HEAD_DIM
            vm = _dot(ws_ref[g], vn[:, c0:c0 + HEAD_DIM]) + bs_ref[:, g:g + 1]
            o_ref[r0:r0 + CHUNK_C, c0:c0 + HEAD_DIM] = (u[:, c0:c0 + HEAD_DIM] * vm).astype(o_ref.dtype)


def _mixer_c(p, ws, bs_t, tm, mixw, col_u, col_v):
    t = p.shape[0]
    n_groups = mixw // HEAD_DIM
    return pl.pallas_call(
        functools.partial(_c_kernel, tm=tm, mixw=mixw),
        grid=(t // tm,),
        in_specs=[pl.BlockSpec((tm, mixw), lambda i: (i, col_u // mixw)),
                  pl.BlockSpec((tm, mixw), lambda i: (i, col_v // mixw)),
                  pl.BlockSpec((n_groups, CHUNK_C, CHUNK_C), lambda i: (0, 0, 0)),
                  pl.BlockSpec((CHUNK_C, LANES), lambda i: (0, 0))],
        out_specs=pl.BlockSpec((tm, mixw), lambda i: (i, 0)),
        out_shape=jax.ShapeDtypeStruct((t, mixw), BF16),
        compiler_params=_cparams(("parallel",)),
        name="gmlp",
    )(p, p, ws, bs_t)


def _hgrn_tables():
    tau = np.arange(CHUNK)
    sums_t = [tau[None, :] <= tau[:, None], tau[None, :] > tau[:, None]]
    pair_t = []
    for lvl in range(N_LEVELS):
        m = CHUNK >> (lvl + 1)
        blk = tau // (2 * m)
        upper = (tau % (2 * m)) >= m
        ref = blk * 2 * m + m - 1
        r = tau[None, :]
        w_up = upper[:, None] & (r > ref[:, None]) & (r <= tau[:, None])
        w_lo = (~upper)[:, None] & (r > tau[:, None]) & (r <= ref[:, None])
        sums_t.append(w_up | w_lo)
        pair_t.append((blk[:, None] == blk[None, :]) & upper[:, None] & (~upper)[None, :])
    sums_t = np.concatenate(sums_t, axis=0).astype(np.float32)
    pair_t = np.stack(pair_t).astype(np.float32)
    flip = tau[::-1]
    sums, pair = [], []
    for d in range(2):
        if d == 0:
            s_d, p_d = sums_t, pair_t
        else:
            s_d = sums_t.reshape(-1, CHUNK, CHUNK)[:, flip][:, :, flip].reshape(-1, CHUNK)
            p_d = pair_t[:, flip][:, :, flip]
        sums.append(np.concatenate([s_d, s_d, s_d], axis=1))
        pair.append(p_d)
    return np.stack(sums), np.stack(pair)


def _scan_kernel(uf, wqf, kdf, atf, egf, ur, wqr, kdr, atr, egr, qf, vf, ff, qr, vr, fr, lb_ref, sums_ref, pair_ref,
                 oaf_ref, oar_ref, of_ref, or_ref, s_ref, sbd_ref, st_ref, *, n_heads):
    n = pl.program_id(1)

    @pl.when(n == 0)
    def _():
        st_ref[...] = jnp.zeros_like(st_ref)
        s_ref[...] = jnp.zeros_like(s_ref)
        sbd_ref[...] = jnp.zeros_like(sbd_ref)

    c = CHUNK
    row = lax.broadcasted_iota(jnp.int32, (c, c), 0)
    col = lax.broadcasted_iota(jnp.int32, (c, c), 1)
    eye = (row == col).astype(F32)
    inst = []
    for d, (q_ref, v_ref, f_ref, o_ref) in enumerate(((qf, vf, ff, of_ref), (qr, vr, fr, or_ref))):
        for pos, ci in enumerate((0, 1) if d == 0 else (1, 0)):
            for h in range(n_heads):
                inst.append((d, pos, ci * c, h * HEAD_DIM, h, q_ref, v_ref, f_ref, o_ref))
    qs = [q_ref[r0:r0 + c, c0:c0 + HEAD_DIM] for (d, pos, r0, c0, h, q_ref, v_ref, f_ref, o_ref) in inst]
    vs = [v_ref[r0:r0 + c, c0:c0 + HEAD_DIM].astype(BF16) for (d, pos, r0, c0, h, q_ref, v_ref, f_ref, o_ref) in inst]
    es = [jnp.exp(-jnp.clip(f_ref[r0:r0 + c, c0:c0 + HEAD_DIM], -GATE_CLIP, GATE_CLIP))
          for (d, pos, r0, c0, h, q_ref, v_ref, f_ref, o_ref) in inst]
    sgs = [1.0 / (1.0 + e) for e in es]
    lbs = [lb_ref[0:1, it[3]:it[3] + HEAD_DIM] for it in inst]
    logfs = [jnp.log(lb + (1.0 - lb) * sg) for lb, sg in zip(lbs, sgs)]
    ks = [(1.0 - lb) * (e * sg) for lb, e, sg in zip(lbs, es, sgs)]
    his = [x.astype(BF16) for x in logfs]
    r1s = [x - hi.astype(F32) for x, hi in zip(logfs, his)]
    mids = [x.astype(BF16) for x in r1s]
    los = [(x - mid.astype(F32)).astype(BF16) for x, mid in zip(r1s, mids)]
    cats = [jnp.concatenate([hi, mid, lo], axis=0) for hi, mid, lo in zip(his, mids, los)]
    xs = []
    for i0 in range(0, len(inst), 2):
        x2 = _dot(sums_ref[inst[i0][0]], jnp.concatenate([cats[i0], cats[i0 + 1]], axis=1))
        xs += [x2[:, :HEAD_DIM], x2[:, HEAD_DIM:]]
    kbs = [k.astype(BF16) for k in ks]
    attns = [eye * _dot_nt(q.astype(BF16), kb) for q, kb in zip(qs, kbs)]
    for lvl in range(N_LEVELS):
        zls = [jnp.exp(x[(2 + lvl) * c:(3 + lvl) * c]) for x in xs]
        attns = [a + pair_ref[it[0], lvl] * _dot_nt((q * zl).astype(BF16), (k * zl).astype(BF16))
                 for a, it, q, k, zl in zip(attns, inst, qs, ks, zls)]
    qds = [(q * jnp.exp(x[0:c])).astype(BF16) for q, x in zip(qs, xs)]
    kds = [(k * jnp.exp(x[c:2 * c])).astype(BF16) for k, x in zip(ks, xs)]
    intra = [_dot(a.astype(BF16), v) for a, v in zip(attns, vs)]
    kvs = [_dot_tn(v, kd) for v, kd in zip(vs, kds)]
    _delta_steps(((uf, wqf, kdf, atf, egf, oaf_ref), (ur, wqr, kdr, atr, egr, oar_ref)), s_ref, sbd_ref, n_heads)
    states = [[st_ref[d, h] for h in range(n_heads)] for d in range(2)]
    for pos in range(2):
        for i, (d, p_i, r0, c0, h, q_ref, v_ref, f_ref, o_ref) in enumerate(inst):
            if p_i != pos:
                continue
            st = states[d][h]
            o_ref[r0:r0 + c, c0:c0 + HEAD_DIM] = _dot_nt(qds[i], st.astype(BF16)) + intra[i]
            last = c - 1 if d == 0 else 0
            states[d][h] = st * jnp.exp(xs[i][last:last + 1, :]) + kvs[i]
    for d in range(2):
        for h in range(n_heads):
            st_ref[d, h] = states[d][h]


def _recurrent_scans(prep_a, p, lb, n_batch, tb, ctx_len, mixw, col_q, col_i, col_f):
    uf, wqf, kdf, atf, ur, wqr, kdr, atr, eg = prep_a
    t = p.shape[0]
    n_heads = mixw // HEAD_DIM
    nsb = tb // STEP_ROWS
    fwd, rev = _seq_block_maps(nsb, ctx_len // STEP_ROWS)
    sums_np, pair_np = _hgrn_tables()
    sums = jnp.asarray(sums_np, BF16)
    pair = jnp.asarray(pair_np, F32)
    in_specs = []
    for m in (fwd, rev):
        in_specs += [pl.BlockSpec((STEP_ROWS, mixw), lambda b, n, m=m: (m(b, n), 0)),
                     pl.BlockSpec((2 * STEP_ROWS, mixw), lambda b, n, m=m: (m(b, n), 0)),
                     pl.BlockSpec((STEP_ROWS, mixw), lambda b, n, m=m: (m(b, n), 0)),
                     pl.BlockSpec((STEP_ROWS, n_heads * CHUNK), lambda b, n, m=m: (m(b, n), 0)),
                     pl.BlockSpec((STEP_ROWS, LANES), lambda b, n, m=m: (m(b, n), 0))]
    for d, m in enumerate((fwd, rev)):
        in_specs += [pl.BlockSpec((STEP_ROWS, mixw), lambda b, n, m=m: (m(b, n), col_q // mixw)),
                     pl.BlockSpec((STEP_ROWS, mixw), lambda b, n, m=m: (m(b, n), col_i // mixw)),
                     pl.BlockSpec((STEP_ROWS, mixw), lambda b, n, m=m, d=d: (m(b, n), col_f // mixw + d))]
    in_specs += [pl.BlockSpec((1, mixw), lambda b, n: (0, 0)),
                 pl.BlockSpec(sums.shape, lambda b, n: (0, 0, 0)),
                 pl.BlockSpec(pair.shape, lambda b, n: (0, 0, 0, 0))]
    return pl.pallas_call(
        functools.partial(_scan_kernel, n_heads=n_heads),
        grid=(n_batch, nsb),
        in_specs=in_specs,
        out_specs=[pl.BlockSpec((STEP_ROWS, mixw), lambda b, n: (fwd(b, n), 0)),
                   pl.BlockSpec((STEP_ROWS, mixw), lambda b, n: (rev(b, n), 0))] * 2,
        out_shape=[jax.ShapeDtypeStruct((t, mixw), F32)] * 4,
        scratch_shapes=[pltpu.VMEM((2, n_heads, HEAD_DIM, HEAD_DIM), F32),
                        pltpu.VMEM((2, mixw, mixw), BF16),
                        pltpu.VMEM((2, n_heads, HEAD_DIM, HEAD_DIM), F32)],
        compiler_params=_cparams(("arbitrary", "arbitrary")),
        name="recurrent_scans",
    )(uf, wqf, kdf, atf, eg, ur, wqr, kdr, atr, eg, p, p, p, p, p, p, lb, sums, pair)


def _route(logits_t, bias, n_experts):
    per = n_experts // N_EXPERT_GROUPS
    scores = _sigmoid(logits_t)
    sel = scores + bias
    rows_sel = [sel[e:e + 1, :] for e in range(n_experts)]
    rows_sc = [scores[e:e + 1, :] for e in range(n_experts)]

    def top2_sum(vals):
        hi = vals[0]
        lo = jnp.full_like(hi, -jnp.inf)
        for x in vals[1:]:
            lo = jnp.maximum(lo, jnp.minimum(hi, x))
            hi = jnp.maximum(hi, x)
        return hi + lo

    best = jnp.zeros_like(rows_sel[0], dtype=jnp.int32)
    best_score = top2_sum(rows_sel[0:per])
    for g in range(1, N_EXPERT_GROUPS):
        gs = top2_sum(rows_sel[g * per:(g + 1) * per])
        better = gs > best_score
        best = jnp.where(better, g, best)
        best_score = jnp.where(better, gs, best_score)
    in_sel, in_sc = [], []
    for j in range(per):
        a = rows_sel[j]
        c = rows_sc[j]
        for g in range(1, N_EXPERT_GROUPS):
            a = jnp.where(best == g, rows_sel[g * per + j], a)
            c = jnp.where(best == g, rows_sc[g * per + j], c)
        in_sel.append(a)
        in_sc.append(c)

    def first_argmax(vals):
        idx = jnp.zeros_like(best)
        top = vals[0]
        for j in range(1, per):
            better = vals[j] > top
            idx = jnp.where(better, j, idx)
            top = jnp.where(better, vals[j], top)
        return idx

    i1 = first_argmax(in_sel)
    i2 = first_argmax([jnp.where(i1 == j, -jnp.inf, in_sel[j]) for j in range(per)])

    def pick(idx):
        w = in_sc[0]
        for j in range(1, per):
            w = jnp.where(idx == j, in_sc[j], w)
        return w

    w1, w2 = pick(i1), pick(i2)
    tot = w1 + w2
    return best * per + i1, best * per + i2, w1 / tot, w2 / tot


def _outproj_kernel(oaf, oar, ga, odf, odr, gd, yb, yc, na, nd, wo, x_ref, m_ref, lng, lnb, wr, br,
                    x1_ref, h2_ref, ei_ref, ew_ref, *, alpha, mixw, tm, n_experts):
    n_heads = mixw // HEAD_DIM

    def gated(of_ref, or_ref, g_ref, gain_ref):
        parts = []
        for h in range(n_heads):
            c0 = h * HEAD_DIM
            o = of_ref[:, c0:c0 + HEAD_DIM] + or_ref[:, c0:c0 + HEAD_DIM]
            y = o * lax.rsqrt(jnp.mean(o * o, axis=-1, keepdims=True) + EPS) * gain_ref[...]
            parts.append((y * _silu(g_ref[:, c0:c0 + HEAD_DIM])).astype(BF16))
        return jnp.concatenate(parts, axis=1)

    ya = gated(oaf, oar, ga, na)
    yd = gated(odf, odr, gd, nd)
    y = (_dot(ya, wo[0:mixw, :]) + _dot(yb[...], wo[mixw:2 * mixw, :])
         + _dot(yc[...], wo[2 * mixw:3 * mixw, :]) + _dot(yd, wo[3 * mixw:4 * mixw, :]))
    x1 = _ln(alpha * x_ref[...] + m_ref[0, 2:3, :] * y) * lng[...] + lnb[...]
    x1_ref[...] = x1
    h2 = _ln(x1) * (1.0 + m_ref[0, 4:5, :]) + m_ref[0, 3:4, :]
    h2_ref[...] = h2
    h_hi = h2.astype(BF16)
    h_r = h2 - h_hi.astype(F32)
    h_mid = h_r.astype(BF16)
    h_lo = (h_r - h_mid.astype(F32)).astype(BF16)
    pa = _dot(h_hi, wr[:, 0:2 * LANES])
    pb = _dot(h_mid, wr[:, 0:2 * LANES])
    pc = _dot(h_hi, wr[:, 2 * LANES:4 * LANES])
    pd = _dot(h_lo, wr[:, 0:2 * LANES])
    logits = (((pd[:, :LANES] + pb[:, LANES:] + pc[:, :LANES]) + (pb[:, :LANES] + pa[:, LANES:]))
              + pa[:, :LANES])
    logits_t = logits.T[:n_experts]
    e1, e2, w1, w2 = _route(logits_t, br[:, 0:1], n_experts)
    zi = jnp.zeros((SUBLANES - TOP_K, tm), jnp.int32)
    ei_ref[...] = jnp.concatenate([e1, e2, zi], axis=0)
    ew_ref[...] = jnp.concatenate([w1, w2, zi.astype(F32)], axis=0)


def _outproj(oaf, oar, odf, odr, p, yb, yc, norm_a, norm_d, w_out, x, mods, ln_g, ln_b, w_router_p, b_router,
             alpha, tm, nb, nctx, n_batch, mixw, col_ga, col_gd):
    t, d = x.shape
    n_experts = b_router.shape[0]
    row = lambda i: (i, 0)
    const = lambda i: (0, 0)
    kern = functools.partial(_outproj_kernel, alpha=alpha, mixw=mixw, tm=tm, n_experts=n_experts)
    return pl.pallas_call(
        kern,
        grid=(t // tm,),
        in_specs=[pl.BlockSpec((tm, mixw), row), pl.BlockSpec((tm, mixw), row),
                  pl.BlockSpec((tm, mixw), lambda i: (i, col_ga // mixw)),
                  pl.BlockSpec((tm, mixw), row), pl.BlockSpec((tm, mixw), row),
                  pl.BlockSpec((tm, mixw), lambda i: (i, col_gd // mixw)),
                  pl.BlockSpec((tm, mixw), row), pl.BlockSpec((tm, mixw), row),
                  pl.BlockSpec((1, HEAD_DIM), const), pl.BlockSpec((1, HEAD_DIM), const),
                  pl.BlockSpec((N_MIXERS * mixw, d), const),
                  pl.BlockSpec((tm, d), row),
                  pl.BlockSpec((1, 6, d), lambda i: (_mod_row(i, nb, nctx, n_batch), 0, 0)),
                  pl.BlockSpec((1, d), const), pl.BlockSpec((1, d), const),
                  pl.BlockSpec((d, 4 * LANES), const), pl.BlockSpec((n_experts, LANES), const)],
        out_specs=[pl.BlockSpec((tm, d), row), pl.BlockSpec((tm, d), row),
                   pl.BlockSpec((SUBLANES, tm), lambda i: (0, i)), pl.BlockSpec((SUBLANES, tm), lambda i: (0, i))],
        out_shape=[jax.ShapeDtypeStruct((t, d), F32), jax.ShapeDtypeStruct((t, d), F32),
                   jax.ShapeDtypeStruct((SUBLANES, t), jnp.int32), jax.ShapeDtypeStruct((SUBLANES, t), F32)],
        compiler_params=_cparams(("parallel",)),
        name="out_proj_router",
    )(oaf, oar, p, odf, odr, p, yb, yc, norm_a, norm_d, w_out, x, mods, ln_g, ln_b, w_router_p, b_router)


def _rank_kernel(ei_ref, su_ref, rank_ref, cnt_ref, base_ref, *, n_experts, tm):
    i = pl.program_id(0)

    @pl.when(i == 0)
    def _():
        base_ref[...] = jnp.zeros_like(base_ref)

    eid = lax.broadcasted_iota(jnp.int32, (n_experts, tm), 0)
    o1 = (eid == ei_ref[0:1, :]).astype(F32)
    o2 = (eid == ei_ref[1:2, :]).astype(F32)
    cnt = o1 + o2
    before = _dot(cnt.astype(BF16), su_ref[...]) + base_ref[:, 0:1]
    r1 = jnp.sum(o1 * before, axis=0, keepdims=True)
    r2 = jnp.sum(o2 * before, axis=0, keepdims=True)
    rank_ref[...] = jnp.concatenate([r1, r2, jnp.zeros((SUBLANES - TOP_K, tm), F32)], axis=0).astype(jnp.int32)
    base_ref[...] = base_ref[...] + jnp.sum(cnt, axis=1, keepdims=True)
    cnt_ref[...] = base_ref[...]


def _slot_ranks(ei, n_experts, tm):
    t = ei.shape[1]
    r = np.arange(tm)
    su = jnp.asarray((r[:, None] < r[None, :]).astype(np.float32), BF16)
    return pl.pallas_call(
        functools.partial(_rank_kernel, n_experts=n_experts, tm=tm),
        grid=(t // tm,),
        in_specs=[pl.BlockSpec((SUBLANES, tm), lambda i: (0, i)), pl.BlockSpec((tm, tm), lambda i: (0, 0))],
        out_specs=[pl.BlockSpec((SUBLANES, tm), lambda i: (0, i)), pl.BlockSpec((n_experts, LANES), lambda i: (0, 0))],
        out_shape=[jax.ShapeDtypeStruct((SUBLANES, t), jnp.int32), jax.ShapeDtypeStruct((n_experts, LANES), F32)],
        scratch_shapes=[pltpu.VMEM((n_experts, LANES), F32)],
        compiler_params=_cparams(("arbitrary",)),
        name="moe_slot_ranks",
    )(ei, su)


def _row_copy(src_ref, src_row, dst_ref, dst_row, sem):
    return pltpu.make_async_copy(src_ref.at[pl.ds(src_row, 1)], dst_ref.at[pl.ds(dst_row, 1)], sem)


def _dispatch_kernel(dest_ref, zs_ref, na_ref, h_ref, xs_ref, zero_ref, sem, zsem, *, tm, t, n_experts, tmx, n_blocks):
    base = pl.program_id(0) * tm

    @pl.when(pl.program_id(0) == 0)
    def _():
        zero_ref[...] = jnp.zeros_like(zero_ref)
        for e in range(n_experts):
            fill = pltpu.make_async_copy(zero_ref, xs_ref.at[pl.ds(pl.multiple_of(zs_ref[e], FILL_ALIGN),
                                                                  tmx + FILL_ALIGN)], zsem)
            fill.start()
            fill.wait()

        def tail(b, carry):
            fill = pltpu.make_async_copy(zero_ref.at[pl.ds(0, tmx)],
                                         xs_ref.at[pl.ds(pl.multiple_of(b * tmx, tmx), tmx)], zsem)
            fill.start()
            fill.wait()
            return carry

        lax.fori_loop(na_ref[0], n_blocks, tail, 0)

    for r in range(tm):
        for kk in range(TOP_K):
            _row_copy(h_ref, r, xs_ref, dest_ref[kk * t + base + r], sem).start()
    for kk in range(TOP_K):
        pltpu.make_async_copy(h_ref, xs_ref.at[pl.ds(0, tm)], sem).wait()


def _dispatch(dest, zero_start, n_active, h2, n_slots, tm, tmx):
    t, d = h2.shape
    n_experts = zero_start.shape[0]
    return pl.pallas_call(
        functools.partial(_dispatch_kernel, tm=tm, t=t, n_experts=n_experts, tmx=tmx, n_blocks=n_slots // tmx),
        grid_spec=pltpu.PrefetchScalarGridSpec(
            num_scalar_prefetch=3,
            grid=(t // tm,),
            in_specs=[pl.BlockSpec((tm, d), lambda i, dest, zs, na: (i, 0))],
            out_specs=pl.BlockSpec(memory_space=pl.ANY),
            scratch_shapes=[pltpu.VMEM((tmx + FILL_ALIGN, d), h2.dtype), pltpu.SemaphoreType.DMA(()),
                            pltpu.SemaphoreType.DMA(())]),
        out_shape=jax.ShapeDtypeStruct((n_slots, d), h2.dtype),
        compiler_params=_cparams(("arbitrary",)),
        name="moe_dispatch",
    )(dest, zero_start, n_active, h2)


def _expert_kernel(be_ref, na_ref, x_ref, w1_ref, w3_ref, w2_ref, y_ref):
    i = pl.program_id(0)

    @pl.when(i < na_ref[0])
    def _():
        xb = x_ref[...].astype(BF16)
        h1 = _dot(xb, w1_ref[0, 0])
        h3 = _dot(xb, w3_ref[0, 0])
        y_ref[...] = _dot((_silu(h1) * h3).astype(BF16), w2_ref[0, 0])

    @pl.when(i >= na_ref[0])
    def _():
        y_ref[...] = jnp.zeros_like(y_ref)


def _experts(block_expert, n_active, xs, w1, w3, w2, layer, tmx):
    n_slots, d = xs.shape
    f = w1.shape[3]
    return pl.pallas_call(
        _expert_kernel,
        grid_spec=pltpu.PrefetchScalarGridSpec(
            num_scalar_prefetch=2,
            grid=(n_slots // tmx,),
            in_specs=[pl.BlockSpec((tmx, d), lambda i, be, na: (jnp.minimum(i, na[0] - 1), 0)),
                      pl.BlockSpec((1, 1, d, f), lambda i, be, na: (layer, be[i], 0, 0)),
                      pl.BlockSpec((1, 1, d, f), lambda i, be, na: (layer, be[i], 0, 0)),
                      pl.BlockSpec((1, 1, f, d), lambda i, be, na: (layer, be[i], 0, 0))],
            out_specs=pl.BlockSpec((tmx, d), lambda i, be, na: (i, 0))),
        out_shape=jax.ShapeDtypeStruct((n_slots, d), F32),
        compiler_params=_cparams(("arbitrary",)),
        name="moe_experts",
    )(block_expert, n_active, xs, w1, w3, w2)


def _combine_kernel(dest_ref, ys_ref, ew_ref, x1_ref, m_ref, lng, lnb, mn_ref, x2_ref, *rest,
                    tm, t, alpha, block_of_step, emit_next):
    hn_ref = rest[0] if emit_next else None
    g_ref, sem = rest[-2:]
    i = pl.program_id(0)
    n = pl.num_programs(0)

    def issue(step, slot):
        base = block_of_step(step) * tm

        for r in range(tm):
            for kk in range(TOP_K):
                _row_copy(ys_ref, dest_ref[kk * t + base + r], g_ref.at[slot, kk], r, sem.at[slot]).start()

    @pl.when(i == 0)
    def _():
        issue(0, 0)

    @pl.when(i + 1 < n)
    def _():
        issue(i + 1, (i + 1) % 2)

    slot = i % 2
    for kk in range(TOP_K):
        pltpu.make_async_copy(ys_ref.at[pl.ds(0, tm)], g_ref.at[slot, kk], sem.at[slot]).wait()
    row = lax.broadcasted_iota(jnp.int32, (tm, tm), 0)
    col = lax.broadcasted_iota(jnp.int32, (tm, tm), 1)
    w_t = _dot_nt((row == col).astype(F32), ew_ref[...], HIGHEST)
    f = w_t[:, 0:1] * g_ref[slot, 0] + w_t[:, 1:2] * g_ref[slot, 1]
    x2 = _ln(alpha * x1_ref[...] + m_ref[0, 5:6, :] * f) * lng[...] + lnb[...]
    x2_ref[...] = x2
    if emit_next:
        hn_ref[...] = (_ln(x2) * (1.0 + mn_ref[0, 1:2, :]) + mn_ref[0, 0:1, :]).astype(hn_ref.dtype)


def _combine(dest, ys, ew, x1, mods, ln_g, ln_b, mods_next, alpha, tm, nb, nctx, n_batch, latent_only):
    t, d = x1.shape
    nlat = nb - nctx
    if latent_only:
        n_steps = n_batch * nlat
        block_of_step = lambda i: (i // nlat) * nb + nctx + i % nlat
    else:
        n_steps = t // tm
        block_of_step = lambda i: i
    mrow = lambda i, dest: (_mod_row(block_of_step(i), nb, nctx, n_batch), 0, 0)
    const = lambda i, dest: (0, 0)
    out_specs = [pl.BlockSpec((tm, d), lambda i, dest: (i, 0))]
    out_shape = [jax.ShapeDtypeStruct((n_steps * tm, d), F32)]
    if not latent_only:
        out_specs.append(pl.BlockSpec((tm, d), lambda i, dest: (i, 0)))
        out_shape.append(jax.ShapeDtypeStruct((t, d), BF16))
    return pl.pallas_call(
        functools.partial(_combine_kernel, tm=tm, t=t, alpha=alpha, block_of_step=block_of_step,
                          emit_next=not latent_only),
        grid_spec=pltpu.PrefetchScalarGridSpec(
            num_scalar_prefetch=1,
            grid=(n_steps,),
            in_specs=[pl.BlockSpec(memory_space=pl.ANY),
                      pl.BlockSpec((SUBLANES, tm), lambda i, dest: (0, block_of_step(i))),
                      pl.BlockSpec((tm, d), lambda i, dest: (block_of_step(i), 0)),
                      pl.BlockSpec((1, 6, d), mrow),
                      pl.BlockSpec((1, d), const), pl.BlockSpec((1, d), const),
                      pl.BlockSpec((1, 6, d), mrow)],
            out_specs=out_specs,
            scratch_shapes=[pltpu.VMEM((2, TOP_K, tm, d), F32), pltpu.SemaphoreType.DMA((2,))]),
        out_shape=out_shape,
        compiler_params=_cparams(("arbitrary",)),
        name="moe_combine",
    )(dest, ys, ew, x1, mods, ln_g, ln_b, mods_next)


def _axial_tables(seq, ctx_len):
    rows = seq // GRID_W
    row = jnp.repeat(jnp.arange(rows, dtype=F32), GRID_W)
    col = jnp.tile(jnp.arange(GRID_W, dtype=F32), rows)
    n_freq = HEAD_DIM // 4
    inv_freq = ROPE_THETA ** (-jnp.arange(n_freq, dtype=F32) / n_freq)
    ang = jnp.concatenate([row[:, None] * inv_freq, col[:, None] * inv_freq], axis=-1)
    cos, sin = jnp.cos(ang), jnp.sin(ang)
    cosf = jnp.concatenate([cos, cos], axis=-1)
    sinf = jnp.concatenate([-sin, sin], axis=-1)
    cosf = jnp.concatenate([jnp.ones((ctx_len, HEAD_DIM), F32), cosf], axis=0)
    sinf = jnp.concatenate([jnp.zeros((ctx_len, HEAD_DIM), F32), sinf], axis=0)
    return cosf, sinf


def kernel(x, c, ctx, c_ctx, w_ada, b_ada, ln_g, ln_b, w_in, conv_a, a_log, dt_bias, norm_a, qk_norm_b, ws_c, bs_c, lb_d, norm_d, w_out, w_router, b_router, w1, w3, w2):
    n_batch, seq, d = x.shape
    ctx_len = ctx.shape[1]
    n_layers = w_in.shape[0]
    n_experts = w_router.shape[1]
    mixw = d // N_MIXERS
    n_heads = mixw // HEAD_DIM
    tb = ctx_len + seq
    t = n_batch * tb
    alpha = (2.0 * n_layers) ** 0.25
    tm = _pick(math.gcd(ctx_len, seq), (256, 128))
    nb, nctx = tb // tm, ctx_len // tm

    col = {"a_qkv": 0, "a_gate": 3 * mixw, "d_f": 4 * mixw, "b_q": 6 * mixw, "b_kv": 7 * mixw, "c_u": 8 * mixw,
           "c_v": 9 * mixw, "d_q": 10 * mixw, "d_i": 11 * mixw, "d_gate": 12 * mixw, "small": 13 * mixw}
    n_proj = 13 * mixw + 256
    src = np.cumsum([0, 3 * mixw, mixw, 2 * n_heads, 2 * n_heads, mixw, mixw, mixw, mixw, mixw, mixw, mixw, 2 * mixw])
    s_qkv, s_ga, s_beta, s_dec, s_bq, s_bkv, s_cu, s_cv, s_dq, s_di, s_dg, s_df, s_end = [int(v) for v in src]

    def permute_w_in(w):
        pad = jnp.zeros((d, n_proj - 13 * mixw - 4 * n_heads), w.dtype)
        return jnp.concatenate([w[:, s_qkv:s_beta], w[:, s_df:s_end], w[:, s_bq:s_df], w[:, s_beta:s_bq], pad],
                               axis=1).astype(BF16)

    assert n_batch + 1 <= SUBLANES
    cc = jnp.zeros((SUBLANES, d), F32).at[:n_batch].set(c).at[n_batch].set(c_ctx)
    mods_all = _mod_vectors(cc, w_ada, b_ada)[:, :n_batch + 1].reshape(n_layers, n_batch + 1, 6, d)

    soft = jax.nn.softmax(lb_d.astype(F32), axis=0)
    lb_all = jnp.cumsum(soft, axis=0) - soft[0]
    cosf, sinf = _axial_tables(seq, ctx_len)
    gate_lanes = jnp.zeros((1, LANES), F32)
    wr_f = w_router.astype(F32)
    wr_hi = wr_f.astype(BF16)
    wr_r = wr_f - wr_hi.astype(F32)
    wr_mid = wr_r.astype(BF16)
    wr_lo = (wr_r - wr_mid.astype(F32)).astype(BF16)
    w_router_p = jnp.zeros((d, 4 * LANES), BF16)
    for blk, part in enumerate((wr_hi, wr_mid, wr_lo)):
        w_router_p = w_router_p.at[:, blk * LANES:blk * LANES + n_experts].set(part)
    b_router_c = jnp.broadcast_to(b_router.astype(F32)[:, None], (n_experts, LANES))

    tmx = EXPERT_BLOCK_ROWS
    n_assign = t * TOP_K
    n_slots = (-(-n_assign // tmx) + n_experts + 2) * tmx

    w1_b, w3_b, w2_b = w1.astype(BF16), w3.astype(BF16), w2.astype(BF16)
    xu = jnp.concatenate([ctx, x], axis=1).reshape(t, d)
    h = _ln_modulate(xu, mods_all[0], tm, nb, nctx, n_batch)
    for l in range(n_layers):
        mods = mods_all[l]
        p = _matmul(h, permute_w_in(w_in[l]))
        neg_a = gate_lanes.at[0, 2 * n_heads:4 * n_heads].set(-jnp.exp(a_log[l].astype(F32)).reshape(-1))
        dtb = gate_lanes.at[0, 2 * n_heads:4 * n_heads].set(dt_bias[l].astype(F32).reshape(-1))
        prep_a = _a_prep(p, conv_a[l], neg_a, dtb, tm, nb, nctx, mixw, col["small"])
        oaf, oar, odf, odr = _recurrent_scans(prep_a, p, lb_all[l][None, :], n_batch, tb, ctx_len, mixw,
                                              col["d_q"], col["d_i"], col["d_f"])
        qb, kb, vb = _b_prep(p, qk_norm_b[l], cosf, sinf, tm, nb, mixw, col["b_q"], col["b_kv"])
        yb = _attention(qb, kb, vb, n_batch, tb, ctx_len, mixw)
        bs_t = jnp.zeros((CHUNK_C, LANES), F32).at[:, :bs_c.shape[1]].set(bs_c[l].T)
        yc = _mixer_c(p, ws_c[l].astype(BF16), bs_t, tm, mixw, col["c_u"], col["c_v"])
        x1, h2, ei, ew = _outproj(oaf, oar, odf, odr, p, yb, yc, norm_a[l][None, :], norm_d[l][None, :],
                                  w_out[l].astype(BF16), xu, mods, ln_g[l, 0][None, :], ln_b[l, 0][None, :],
                                  w_router_p, b_router_c, alpha, tm, nb, nctx, n_batch, mixw,
                                  col["a_gate"], col["d_gate"])
        rank, counts = _slot_ranks(ei, n_experts, tm)
        cnt = counts[:, 0].astype(jnp.int32)
        padded = (cnt + tmx - 1) // tmx * tmx
        pad_end = jnp.cumsum(padded)
        pad_start = pad_end - padded
        eids = jnp.arange(n_experts, dtype=jnp.int32)[:, None, None]
        dest = (rank[:TOP_K] + jnp.sum(jnp.where(ei[None, :TOP_K] == eids, pad_start[:, None, None], 0),
                                       axis=0)).reshape(-1)
        n_blocks = n_slots // tmx
        block_start = jnp.arange(n_blocks, dtype=jnp.int32) * tmx
        block_expert = jnp.minimum(jnp.sum((pad_end[None, :] <= block_start[:, None]).astype(jnp.int32), axis=1),
                                   n_experts - 1)
        n_active = (pad_end[-1:] // tmx).astype(jnp.int32)
        fill_start = (pad_start + cnt) // FILL_ALIGN * FILL_ALIGN
        xs = _dispatch(dest, fill_start, n_active, h2, n_slots, _pick(t, (2 * tm, tm)), tmx)
        ys = _experts(block_expert, n_active, xs, w1_b, w3_b, w2_b, l, tmx)
        last = l == n_layers - 1
        res = _combine(dest, ys, ew, x1, mods, ln_g[l, 1][None, :], ln_b[l, 1][None, :],
                       mods_all[min(l + 1, n_layers - 1)], alpha, tm, nb, nctx, n_batch, latent_only=last)
        if last:
            return res[0].reshape(n_batch, seq, d)
        xu, h = res
```

```python
import functools
import math

import numpy as np
import jax
import jax.numpy as jnp
from jax import lax
from jax.experimental import pallas as pl
from jax.experimental.pallas import tpu as pltpu

F32 = jnp.float32
BF16 = jnp.bfloat16
HIGHEST = lax.Precision.HIGHEST

HEAD_DIM = 128
N_MIXERS = 4
CONV_W = 5
CHUNK = 64
STEP_ROWS = 2 * CHUNK
SOLVE_ROWS = 2 * CHUNK
CHUNK_C = 128
GRID_W = 64
ROPE_THETA = 10000.0
N_EXPERT_GROUPS = 4
TOP_K = 2
EPS = 1e-6
LOG2_E = math.log2(math.e)
GATE_CLIP = 30.0
N_LEVELS = 6
LANES = 128
SUBLANES = 8
HALO = SUBLANES
FILL_ALIGN = SUBLANES
EXPERT_BLOCK_ROWS = 256
V7X_VMEM_LIMIT = 56 * 1024 * 1024


def _pick(n, cands):
    for c in cands:
        if n % c == 0:
            return c
    raise ValueError(f"no tile in {cands} divides {n}")


def _cparams(sem, vmem=None):
    return pltpu.CompilerParams(dimension_semantics=sem, vmem_limit_bytes=vmem or V7X_VMEM_LIMIT)


def _dot(a, b, precision=None):
    return jnp.dot(a, b, preferred_element_type=F32, precision=precision)


def _dot_nt(a, b, precision=None):
    return lax.dot_general(a, b, (((1,), (1,)), ((), ())), preferred_element_type=F32, precision=precision)


def _dot_tn(a, b, precision=None):
    return lax.dot_general(a, b, (((0,), (0,)), ((), ())), preferred_element_type=F32, precision=precision)


def _sigmoid(x):
    return 1.0 / (1.0 + jnp.exp(-x))


def _silu(x):
    return x * _sigmoid(x)


def _gelu_tanh(x):
    return 0.5 * x * (1.0 + jnp.tanh(math.sqrt(2.0 / math.pi) * (x + 0.044715 * (x * x * x))))


def _ln(x):
    mu = jnp.mean(x, axis=-1, keepdims=True)
    xc = x - mu
    return xc * lax.rsqrt(jnp.mean(xc * xc, axis=-1, keepdims=True) + EPS)


def _mod_row(i, nb, nctx, n_batch):
    return jnp.where(i % nb < nctx, n_batch, i // nb)


def _mod_kernel(c_ref, w_ref, b_ref, o_ref):
    o_ref[0] = _dot(_silu(c_ref[...]), w_ref[0], HIGHEST) + b_ref[0]


def _mod_vectors(cc, w_ada, b_ada):
    n_layers, d, n6 = w_ada.shape
    tn = _pick(n6, (1024, 512, 128))
    return pl.pallas_call(
        _mod_kernel,
        grid=(n_layers, n6 // tn),
        in_specs=[pl.BlockSpec((SUBLANES, d), lambda l, j: (0, 0)),
                  pl.BlockSpec((1, d, tn), lambda l, j: (l, 0, j)),
                  pl.BlockSpec((1, 1, tn), lambda l, j: (l, 0, j))],
        out_specs=pl.BlockSpec((1, SUBLANES, tn), lambda l, j: (l, 0, j)),
        out_shape=jax.ShapeDtypeStruct((n_layers, SUBLANES, n6), F32),
        compiler_params=_cparams(("parallel", "parallel")),
        name="mod_vectors",
    )(cc, w_ada, b_ada.reshape(n_layers, 1, n6))


def _lnmod_kernel(x_ref, m_ref, h_ref):
    y = _ln(x_ref[...])
    h_ref[...] = (y * (1.0 + m_ref[0, 1:2, :]) + m_ref[0, 0:1, :]).astype(h_ref.dtype)


def _ln_modulate(x, mods, tm, nb, nctx, n_batch):
    t, d = x.shape
    return pl.pallas_call(
        _lnmod_kernel,
        grid=(t // tm,),
        in_specs=[pl.BlockSpec((tm, d), lambda i: (i, 0)),
                  pl.BlockSpec((1, 6, d), lambda i: (_mod_row(i, nb, nctx, n_batch), 0, 0))],
        out_specs=pl.BlockSpec((tm, d), lambda i: (i, 0)),
        out_shape=jax.ShapeDtypeStruct((t, d), BF16),
        compiler_params=_cparams(("parallel",)),
        name="ln_modulate",
    )(x, mods)


def _matmul_kernel(a_ref, w_ref, o_ref):
    o_ref[...] = _dot(a_ref[...], w_ref[...]).astype(o_ref.dtype)


def _matmul(a, w, out_dtype=F32):
    m, k = a.shape
    n = w.shape[1]
    tm = _pick(m, (2112, 1536, 1024, 768, 512, 256, 128))
    tn = _pick(n, (768, 1152, 1024, 512, 256, 128))
    return pl.pallas_call(
        _matmul_kernel,
        grid=(m // tm, n // tn),
        in_specs=[pl.BlockSpec((tm, k), lambda i, j: (i, 0)),
                  pl.BlockSpec((k, tn), lambda i, j: (0, j))],
        out_specs=pl.BlockSpec((tm, tn), lambda i, j: (i, j)),
        out_shape=jax.ShapeDtypeStruct((m, n), out_dtype),
        compiler_params=_cparams(("parallel", "parallel")),
        name="in_proj",
    )(a, w)


def _unit_tri_inverses(lows, eye, pair_refs):
    ts = [eye - low * pair_ref[0] for low, pair_ref in zip(lows, pair_refs)]
    for lvl in range(1, N_LEVELS):
        tbs = [t.astype(BF16) for t in ts]
        xs = [_dot(tb16, (low * pair_ref[lvl]).astype(BF16)) for tb16, low, pair_ref in zip(tbs, lows, pair_refs)]
        ts = [t - _dot(x.astype(BF16), tb16) for t, x, tb16 in zip(ts, xs, tbs)]
    return ts


def _a_prep_kernel(x_ref, xp_ref, xn_ref, sm_ref, cw_ref, na_ref, dtb_ref, trf_ref, trr_ref, pairf_ref, pairr_ref,
                   uf_ref, wqf_ref, kdf_ref, atf_ref, ur_ref, wqr_ref, kdr_ref, atr_ref, eg_ref, ext_ref,
                   *, tm, nb, nctx, mixw):
    i = pl.program_id(0)
    j = i % nb
    first = jnp.logical_or(j == 0, j == nctx)
    last = jnp.logical_or(j == nctx - 1, j == nb - 1)
    ext_ref[HALO:HALO + tm, :] = x_ref[...]
    ext_ref[0:HALO, :] = jnp.where(first, 0.0, xp_ref[...])
    ext_ref[HALO + tm:2 * HALO + tm, :] = jnp.where(last, 0.0, xn_ref[...])
    off = HALO - CONV_W // 2
    big = ext_ref[...]
    nrow = tm + 2 * HALO
    acc = None
    for t in range(CONV_W):
        term = cw_ref[t:t + 1, :] * pltpu.roll(big, nrow - (off + t), 0)[0:tm, :]
        acc = term if acc is None else acc + term
    y = _silu(acc)
    n_heads = mixw // HEAD_DIM
    sm = sm_ref[...]
    lane = lax.broadcasted_iota(jnp.int32, sm.shape, 1)
    zz = sm + dtb_ref[...]
    softplus = jnp.maximum(zz, 0.0) + jnp.log(1.0 + jnp.exp(-jnp.abs(zz)))
    g = jnp.where(jnp.logical_and(lane >= 2 * n_heads, lane < 4 * n_heads), na_ref[...] * softplus, 0.0)
    beta_all = _sigmoid(sm)
    tri_f = trf_ref[...]
    tri_r = trr_ref[...]
    cs_f = _dot(tri_f, g, HIGHEST)
    cs_r = _dot(tri_r, g, HIGHEST)
    gcum = jnp.where(lane < 3 * n_heads, cs_f, cs_r)
    gtot = _dot(jnp.maximum(tri_f, tri_r), g, HIGHEST)
    eg_ref[...] = jnp.exp(gtot)
    gcum_t = gcum.T
    sr = SOLVE_ROWS
    row = lax.broadcasted_iota(jnp.int32, (sr, sr), 0)
    col = lax.broadcasted_iota(jnp.int32, (sr, sr), 1)
    eye = (row == col).astype(F32)
    incls = (tri_f[0:sr, 0:sr] > 0.5, tri_r[0:sr, 0:sr] > 0.5)
    pairs = ([pairf_ref[lvl, 0:sr, 0:sr] for lvl in range(N_LEVELS)],
             [pairr_ref[lvl, 0:sr, 0:sr] for lvl in range(N_LEVELS)])
    outs = ((uf_ref, wqf_ref, kdf_ref, atf_ref), (ur_ref, wqr_ref, kdr_ref, atr_ref))
    items = []
    for sb in range(tm // sr):
        rs = sb * sr
        for h in range(n_heads):
            c0 = h * HEAD_DIM
            qh = y[rs:rs + sr, c0:c0 + HEAD_DIM]
            kh = y[rs:rs + sr, mixw + c0:mixw + c0 + HEAD_DIM]
            vh = y[rs:rs + sr, 2 * mixw + c0:2 * mixw + c0 + HEAD_DIM]
            qh = qh * (lax.rsqrt(jnp.sum(qh * qh, axis=-1, keepdims=True) + EPS) * HEAD_DIM ** -0.5)
            kh = kh * lax.rsqrt(jnp.sum(kh * kh, axis=-1, keepdims=True) + EPS)
            kb16 = kh.astype(BF16)
            qk = _dot_nt(qh.astype(BF16), kb16)
            for d in range(2):
                lb = d * n_heads + h
                lg = (2 + d) * n_heads + h
                beta = beta_all[rs:rs + sr, lb:lb + 1]
                gc = gcum[rs:rs + sr, lg:lg + 1]
                incl = incls[d]
                decay = jnp.where(incl, jnp.exp(jnp.where(incl, gc - gcum_t[lg:lg + 1, rs:rs + sr], 0.0)), 0.0)
                kbeta = kh * beta
                low = jnp.where(row == col, 0.0, _dot_nt(kbeta.astype(BF16), kb16) * decay)
                rhs = jnp.concatenate([vh * beta, kbeta * jnp.exp(gc)], axis=1).astype(BF16)
                items.append(dict(h=h, d=d, rs=rs, low=low, rhs=rhs, attn=(qk * decay).astype(BF16),
                                  qd=(qh * jnp.exp(gc)).astype(BF16),
                                  kd=(kh * jnp.exp(gtot[rs:rs + sr, lg:lg + 1] - gc)).astype(BF16)))
    tinvs = _unit_tri_inverses([it["low"] for it in items], eye, [pairs[it["d"]] for it in items])
    for it, tinv in zip(items, tinvs):
        u_ref, wq_ref, kd_ref, at_ref = outs[it["d"]]
        h, rs = it["h"], it["rs"]
        c0 = h * HEAD_DIM
        sol = _dot(tinv.astype(BF16), it["rhs"])
        u_ref[rs:rs + sr, c0:c0 + HEAD_DIM] = sol[:, :HEAD_DIM]
        w = sol[:, HEAD_DIM:].astype(BF16)
        kd_ref[rs:rs + sr, c0:c0 + HEAD_DIM] = it["kd"]
        for ci in range(sr // CHUNK):
            r0 = ci * CHUNK
            g0 = rs + r0
            wq_ref[2 * g0:2 * g0 + CHUNK, c0:c0 + HEAD_DIM] = w[r0:r0 + CHUNK]
            wq_ref[2 * g0 + CHUNK:2 * g0 + 2 * CHUNK, c0:c0 + HEAD_DIM] = it["qd"][r0:r0 + CHUNK]
            at_ref[g0:g0 + CHUNK, h * CHUNK:(h + 1) * CHUNK] = it["attn"][r0:r0 + CHUNK, r0:r0 + CHUNK]


def _a_prep(p, conv_w, neg_a, dtb, tm, nb, nctx, mixw, col_small):
    t = p.shape[0]
    c3 = 3 * mixw
    n_heads = mixw // HEAD_DIM
    r = np.arange(tm)
    same = (r[:, None] // CHUNK) == (r[None, :] // CHUNK)
    tri_f = jnp.asarray((same & (r[None, :] <= r[:, None])).astype(np.float32))
    tri_r = jnp.asarray((same & (r[None, :] >= r[:, None])).astype(np.float32))
    pair_f = []
    for lvl in range(N_LEVELS):
        s_blk = 1 << lvl
        joined = ((r[:, None] // (2 * s_blk)) == (r[None, :] // (2 * s_blk))) \
            & ((r[:, None] % (2 * s_blk)) >= s_blk) & ((r[None, :] % (2 * s_blk)) < s_blk)
        pair_f.append(joined.astype(np.float32))
    pair_f = np.stack(pair_f)
    pair_r = jnp.asarray(np.transpose(pair_f, (0, 2, 1)))
    pair_f = jnp.asarray(pair_f)
    hb = tm // HALO
    last_hb = t // HALO - 1
    kern = functools.partial(_a_prep_kernel, tm=tm, nb=nb, nctx=nctx, mixw=mixw)
    per_dir_specs = [pl.BlockSpec((tm, mixw), lambda i: (i, 0)),
                     pl.BlockSpec((2 * tm, mixw), lambda i: (i, 0)),
                     pl.BlockSpec((tm, mixw), lambda i: (i, 0)),
                     pl.BlockSpec((tm, n_heads * CHUNK), lambda i: (i, 0))]
    per_dir_shapes = [jax.ShapeDtypeStruct((t, mixw), F32), jax.ShapeDtypeStruct((2 * t, mixw), BF16),
                      jax.ShapeDtypeStruct((t, mixw), BF16), jax.ShapeDtypeStruct((t, n_heads * CHUNK), BF16)]
    return pl.pallas_call(
        kern,
        grid=(t // tm,),
        in_specs=[pl.BlockSpec((tm, c3), lambda i: (i, 0)),
                  pl.BlockSpec((HALO, c3), lambda i: (jnp.maximum(i * hb - 1, 0), 0)),
                  pl.BlockSpec((HALO, c3), lambda i: (jnp.minimum((i + 1) * hb, last_hb), 0)),
                  pl.BlockSpec((tm, LANES), lambda i: (i, col_small // LANES)),
                  pl.BlockSpec((CONV_W, c3), lambda i: (0, 0)),
                  pl.BlockSpec((1, LANES), lambda i: (0, 0)),
                  pl.BlockSpec((1, LANES), lambda i: (0, 0)),
                  pl.BlockSpec((tm, tm), lambda i: (0, 0)),
                  pl.BlockSpec((tm, tm), lambda i: (0, 0)),
                  pl.BlockSpec((N_LEVELS, tm, tm), lambda i: (0, 0, 0)),
                  pl.BlockSpec((N_LEVELS, tm, tm), lambda i: (0, 0, 0))],
        out_specs=per_dir_specs * 2 + [pl.BlockSpec((tm, LANES), lambda i: (i, 0))],
        out_shape=per_dir_shapes * 2 + [jax.ShapeDtypeStruct((t, LANES), F32)],
        scratch_shapes=[pltpu.VMEM((tm + 2 * HALO, c3), F32)],
        compiler_params=_cparams(("parallel",)),
        name="deltanet_prep",
    )(p, p, p, p, conv_w, neg_a, dtb, tri_f, tri_r, pair_f, pair_r)


def _delta_steps(dirs, s_ref, sbd_ref, n_heads):
    mixw = n_heads * HEAD_DIM
    lane_head = lax.broadcasted_iota(jnp.int32, (CHUNK, mixw), 1) // HEAD_DIM
    for pos in range(2):
        r0s = [(pos if d == 0 else 1 - pos) * CHUNK for d in range(2)]
        rs = [_dot(dirs[d][1][2 * r0s[d]:2 * r0s[d] + 2 * CHUNK, :], sbd_ref[d]) for d in range(2)]
        v_news = [dirs[d][0][r0s[d]:r0s[d] + CHUNK, :] - rs[d][:CHUNK] for d in range(2)]
        v_stacks = [jnp.concatenate([jnp.where(lane_head == h, v_news[d], 0.0) for h in range(n_heads)],
                                    axis=0).astype(BF16) for d in range(2)]
        for d in range(2):
            dirs[d][5][r0s[d]:r0s[d] + CHUNK, :] = (rs[d][CHUNK:]
                                                    + _dot(dirs[d][3][r0s[d]:r0s[d] + CHUNK, :], v_stacks[d]))
        vnbs = [v.astype(BF16) for v in v_news]
        for h in range(n_heads):
            c0 = h * HEAD_DIM
            for d in range(2):
                lg = (2 + d) * n_heads + h
                r0 = r0s[d]
                s_new = (s_ref[d, h] * dirs[d][4][r0:r0 + 1, lg:lg + 1]
                         + _dot_tn(dirs[d][2][r0:r0 + CHUNK, c0:c0 + HEAD_DIM], vnbs[d][:, c0:c0 + HEAD_DIM]))
                s_ref[d, h] = s_new
                sbd_ref[d, c0:c0 + HEAD_DIM, c0:c0 + HEAD_DIM] = s_new.astype(BF16)


def _seq_block_maps(nsb, nctx_sb):
    def fwd(b, n):
        return b * nsb + n

    def rev(b, n):
        return b * nsb + jnp.where(n < nctx_sb, nctx_sb - 1 - n, nsb - 1 - (n - nctx_sb))

    return fwd, rev


def _b_prep_kernel(q_ref, kv_ref, gain_ref, cos_ref, sin_ref, qo_ref, ko_ref, vo_ref, *, mixw):
    n_heads = mixw // HEAD_DIM
    n_kv = n_heads // 2
    cosf = cos_ref[...]
    sinf = sin_ref[...]

    def norm_rope(xh, gain, scale):
        yh = xh * lax.rsqrt(jnp.mean(xh * xh, axis=-1, keepdims=True) + EPS) * gain
        return (yh * cosf + pltpu.roll(yh, HEAD_DIM // 2, 1) * sinf) * scale

    for h in range(n_heads):
        c0 = h * HEAD_DIM
        qo_ref[:, c0:c0 + HEAD_DIM] = norm_rope(q_ref[:, c0:c0 + HEAD_DIM], gain_ref[0:1, :],
                                                HEAD_DIM ** -0.5 * LOG2_E).astype(qo_ref.dtype)
    for h in range(n_kv):
        c0 = h * HEAD_DIM
        ko_ref[:, c0:c0 + HEAD_DIM] = norm_rope(kv_ref[:, c0:c0 + HEAD_DIM], gain_ref[1:2, :], 1.0).astype(ko_ref.dtype)
    ones = jnp.ones((q_ref.shape[0], HEAD_DIM), vo_ref.dtype)
    for h in range(n_kv):
        c0 = (n_kv + h) * HEAD_DIM
        vo_ref[:, 2 * h * HEAD_DIM:(2 * h + 1) * HEAD_DIM] = kv_ref[:, c0:c0 + HEAD_DIM].astype(vo_ref.dtype)
        vo_ref[:, (2 * h + 1) * HEAD_DIM:(2 * h + 2) * HEAD_DIM] = ones


def _b_prep(p, qk_gain, cosf, sinf, tm, nb, mixw, col_q, col_kv):
    t = p.shape[0]
    kvw = mixw // 2
    return pl.pallas_call(
        functools.partial(_b_prep_kernel, mixw=mixw),
        grid=(t // tm,),
        in_specs=[pl.BlockSpec((tm, mixw), lambda i: (i, col_q // mixw)),
                  pl.BlockSpec((tm, mixw), lambda i: (i, col_kv // mixw)),
                  pl.BlockSpec((2, HEAD_DIM), lambda i: (0, 0)),
                  pl.BlockSpec((tm, HEAD_DIM), lambda i: (i % nb, 0)),
                  pl.BlockSpec((tm, HEAD_DIM), lambda i: (i % nb, 0))],
        out_specs=[pl.BlockSpec((tm, mixw), lambda i: (i, 0)),
                   pl.BlockSpec((tm, kvw), lambda i: (i, 0)),
                   pl.BlockSpec((tm, 2 * kvw), lambda i: (i, 0))],
        out_shape=[jax.ShapeDtypeStruct((t, mixw), BF16),
                   jax.ShapeDtypeStruct((t, kvw), BF16),
                   jax.ShapeDtypeStruct((t, 2 * kvw), BF16)],
        compiler_params=_cparams(("parallel",)),
        name="gqa_prep",
    )(p, p, qk_gain, cosf, sinf)


def _attn_kernel(q_ref, k_ref, v_ref, o_ref, m_ref, acc_ref, sa_ref, sb_ref, *, tq, tk, nk, nctx_q, ctx_len):
    qi = pl.program_id(2)
    qb = q_ref[...]
    q2 = jnp.concatenate([qb[:, :HEAD_DIM], qb[:, HEAD_DIM:]], axis=0)
    m_ref[...] = jnp.full_like(m_ref, -1e30)
    acc_ref[...] = jnp.zeros_like(acc_ref)

    def scores(i, s_ref):
        s_ref[...] = _dot_nt(q2, k_ref[i * tk:(i + 1) * tk, :])

    def absorb(s_of, vc):
        for g in range(2):
            sl = slice(g * tq, (g + 1) * tq)
            s = s_of(sl)
            m_prev = m_ref[sl, :]
            m_new = jnp.maximum(m_prev, jnp.max(s, axis=-1, keepdims=True))
            p = jnp.exp2(s - m_new)
            acc_ref[sl, :] = jnp.exp2(m_prev - m_new) * acc_ref[sl, :] + _dot(p.astype(BF16), vc)
            m_ref[sl, :] = m_new

    def absorb_chunk(i, s_ref):
        absorb(lambda sl: s_ref[sl, :], v_ref[i * tk:(i + 1) * tk, :])

    @pl.when(qi >= nctx_q)
    def _():
        bufs = (sa_ref, sb_ref)
        scores(0, bufs[0])
        for i in range(nk):
            if i + 1 < nk:
                scores(i + 1, bufs[(i + 1) % 2])
            absorb_chunk(i, bufs[i % 2])

    @pl.when(qi < nctx_q)
    def _():
        s_ctx = _dot_nt(q2, k_ref[0:ctx_len, :])
        absorb(lambda sl: s_ctx[sl, :], v_ref[0:ctx_len, :])

    acc = acc_ref[...]
    o = acc[:, :HEAD_DIM] / acc[:, HEAD_DIM:HEAD_DIM + 1]
    o_ref[...] = jnp.concatenate([o[:tq], o[tq:]], axis=1).astype(o_ref.dtype)


def _attention(q, k, v, n_batch, tb, ctx_len, mixw):
    t = q.shape[0]
    n_kv = mixw // HEAD_DIM // 2
    tq = _pick(math.gcd(ctx_len, tb), (256, 128))
    tk = _pick(tb, (1408, 768, 1024, 512, 640, 384, 256, 128))
    nq, nk = tb // tq, tb // tk
    return pl.pallas_call(
        functools.partial(_attn_kernel, tq=tq, tk=tk, nk=nk, nctx_q=ctx_len // tq, ctx_len=ctx_len),
        grid=(n_batch, n_kv, nq),
        in_specs=[pl.BlockSpec((tq, 2 * HEAD_DIM), lambda b, j, qi: (b * nq + qi, j)),
                  pl.BlockSpec((tb, HEAD_DIM), lambda b, j, qi: (b, j)),
                  pl.BlockSpec((tb, 2 * HEAD_DIM), lambda b, j, qi: (b, j))],
        out_specs=pl.BlockSpec((tq, 2 * HEAD_DIM), lambda b, j, qi: (b * nq + qi, j)),
        out_shape=jax.ShapeDtypeStruct((t, mixw), BF16),
        scratch_shapes=[pltpu.VMEM((2 * tq, 1), F32), pltpu.VMEM((2 * tq, 2 * HEAD_DIM), F32),
                        pltpu.VMEM((2 * tq, tk), F32), pltpu.VMEM((2 * tq, tk), F32)],
        compiler_params=_cparams(("parallel", "parallel", "parallel")),
        name="gqa_attention",
    )(q, k, v)


def _c_kernel(u_ref, v_ref, ws_ref, bs_ref, o_ref, *, tm, mixw):
    n_groups = mixw // HEAD_DIM
    for ci in range(tm // CHUNK_C):
        r0 = ci * CHUNK_C
        u = _gelu_tanh(u_ref[r0:r0 + CHUNK_C, :])
        vn = _ln(_gelu_tanh(v_ref[r0:r0 + CHUNK_C, :])).astype(BF16)
        for g in range(n_groups):
            c0 = g * HEAD_DIM
            vm = _dot(ws_ref[g], vn[:, c0:c0 + HEAD_DIM]) + bs_ref[:, g:g + 1]
            o_ref[r0:r0 + CHUNK_C, c0:c0 + HEAD_DIM] = (u[:, c0:c0 + HEAD_DIM] * vm).astype(o_ref.dtype)


def _mixer_c(p, ws, bs_t, tm, mixw, col_u, col_v):
    t = p.shape[0]
    n_groups = mixw // HEAD_DIM
    return pl.pallas_call(
        functools.partial(_c_kernel, tm=tm, mixw=mixw),
        grid=(t // tm,),
        in_specs=[pl.BlockSpec((tm, mixw), lambda i: (i, col_u // mixw)),
                  pl.BlockSpec((tm, mixw), lambda i: (i, col_v // mixw)),
                  pl.BlockSpec((n_groups, CHUNK_C, CHUNK_C), lambda i: (0, 0, 0)),
                  pl.BlockSpec((CHUNK_C, LANES), lambda i: (0, 0))],
        out_specs=pl.BlockSpec((tm, mixw), lambda i: (i, 0)),
        out_shape=jax.ShapeDtypeStruct((t, mixw), BF16),
        compiler_params=_cparams(("parallel",)),
        name="gmlp",
    )(p, p, ws, bs_t)


def _hgrn_tables():
    tau = np.arange(CHUNK)
    sums_t = [tau[None, :] <= tau[:, None], tau[None, :] > tau[:, None]]
    pair_t = []
    for lvl in range(N_LEVELS):
        m = CHUNK >> (lvl + 1)
        blk = tau // (2 * m)
        upper = (tau % (2 * m)) >= m
        ref = blk * 2 * m + m - 1
        r = tau[None, :]
        w_up = upper[:, None] & (r > ref[:, None]) & (r <= tau[:, None])
        w_lo = (~upper)[:, None] & (r > tau[:, None]) & (r <= ref[:, None])
        sums_t.append(w_up | w_lo)
        pair_t.append((blk[:, None] == blk[None, :]) & upper[:, None] & (~upper)[None, :])
    sums_t = np.concatenate(sums_t, axis=0).astype(np.float32)
    pair_t = np.stack(pair_t).astype(np.float32)
    flip = tau[::-1]
    sums, pair = [], []
    for d in range(2):
        if d == 0:
            s_d, p_d = sums_t, pair_t
        else:
            s_d = sums_t.reshape(-1, CHUNK, CHUNK)[:, flip][:, :, flip].reshape(-1, CHUNK)
            p_d = pair_t[:, flip][:, :, flip]
        sums.append(np.concatenate([s_d, s_d, s_d], axis=1))
        pair.append(p_d)
    return np.stack(sums), np.stack(pair)


def _scan_kernel(uf, wqf, kdf, atf, egf, ur, wqr, kdr, atr, egr, qf, vf, ff, qr, vr, fr, lb_ref, sums_ref, pair_ref,
                 oaf_ref, oar_ref, of_ref, or_ref, s_ref, sbd_ref, st_ref, *, n_heads):
    n = pl.program_id(1)

    @pl.when(n == 0)
    def _():
        st_ref[...] = jnp.zeros_like(st_ref)
        s_ref[...] = jnp.zeros_like(s_ref)
        sbd_ref[...] = jnp.zeros_like(sbd_ref)

    c = CHUNK
    row = lax.broadcasted_iota(jnp.int32, (c, c), 0)
    col = lax.broadcasted_iota(jnp.int32, (c, c), 1)
    eye = (row == col).astype(F32)
    inst = []
    for d, (q_ref, v_ref, f_ref, o_ref) in enumerate(((qf, vf, ff, of_ref), (qr, vr, fr, or_ref))):
        for pos, ci in enumerate((0, 1) if d == 0 else (1, 0)):
            for h in range(n_heads):
                inst.append((d, pos, ci * c, h * HEAD_DIM, h, q_ref, v_ref, f_ref, o_ref))
    qs = [q_ref[r0:r0 + c, c0:c0 + HEAD_DIM] for (d, pos, r0, c0, h, q_ref, v_ref, f_ref, o_ref) in inst]
    vs = [v_ref[r0:r0 + c, c0:c0 + HEAD_DIM].astype(BF16) for (d, pos, r0, c0, h, q_ref, v_ref, f_ref, o_ref) in inst]
    es = [jnp.exp(-jnp.clip(f_ref[r0:r0 + c, c0:c0 + HEAD_DIM], -GATE_CLIP, GATE_CLIP))
          for (d, pos, r0, c0, h, q_ref, v_ref, f_ref, o_ref) in inst]
    sgs = [1.0 / (1.0 + e) for e in es]
    lbs = [lb_ref[0:1, it[3]:it[3] + HEAD_DIM] for it in inst]
    logfs = [jnp.log(lb + (1.0 - lb) * sg) for lb, sg in zip(lbs, sgs)]
    ks = [(1.0 - lb) * (e * sg) for lb, e, sg in zip(lbs, es, sgs)]
    his = [x.astype(BF16) for x in logfs]
    r1s = [x - hi.astype(F32) for x, hi in zip(logfs, his)]
    mids = [x.astype(BF16) for x in r1s]
    los = [(x - mid.astype(F32)).astype(BF16) for x, mid in zip(r1s, mids)]
    cats = [jnp.concatenate([hi, mid, lo], axis=0) for hi, mid, lo in zip(his, mids, los)]
    xs = []
    for i0 in range(0, len(inst), 2):
        x2 = _dot(sums_ref[inst[i0][0]], jnp.concatenate([cats[i0], cats[i0 + 1]], axis=1))
        xs += [x2[:, :HEAD_DIM], x2[:, HEAD_DIM:]]
    kbs = [k.astype(BF16) for k in ks]
    attns = [eye * _dot_nt(q.astype(BF16), kb) for q, kb in zip(qs, kbs)]
    for lvl in range(N_LEVELS):
        zls = [jnp.exp(x[(2 + lvl) * c:(3 + lvl) * c]) for x in xs]
        attns = [a + pair_ref[it[0], lvl] * _dot_nt((q * zl).astype(BF16), (k * zl).astype(BF16))
                 for a, it, q, k, zl in zip(attns, inst, qs, ks, zls)]
    qds = [(q * jnp.exp(x[0:c])).astype(BF16) for q, x in zip(qs, xs)]
    kds = [(k * jnp.exp(x[c:2 * c])).astype(BF16) for k, x in zip(ks, xs)]
    intra = [_dot(a.astype(BF16), v) for a, v in zip(attns, vs)]
    kvs = [_dot_tn(v, kd) for v, kd in zip(vs, kds)]
    _delta_steps(((uf, wqf, kdf, atf, egf, oaf_ref), (ur, wqr, kdr, atr, egr, oar_ref)), s_ref, sbd_ref, n_heads)
    states = [[st_ref[d, h] for h in range(n_heads)] for d in range(2)]
    for pos in range(2):
        for i, (d, p_i, r0, c0, h, q_ref, v_ref, f_ref, o_ref) in enumerate(inst):
            if p_i != pos:
                continue
            st = states[d][h]
            o_ref[r0:r0 + c, c0:c0 + HEAD_DIM] = _dot_nt(qds[i], st.astype(BF16)) + intra[i]
            last = c - 1 if d == 0 else 0
            states[d][h] = st * jnp.exp(xs[i][last:last + 1, :]) + kvs[i]
    for d in range(2):
        for h in range(n_heads):
            st_ref[d, h] = states[d][h]


def _recurrent_scans(prep_a, p, lb, n_batch, tb, ctx_len, mixw, col_q, col_i, col_f):
    uf, wqf, kdf, atf, ur, wqr, kdr, atr, eg = prep_a
    t = p.shape[0]
    n_heads = mixw // HEAD_DIM
    nsb = tb // STEP_ROWS
    fwd, rev = _seq_block_maps(nsb, ctx_len // STEP_ROWS)
    sums_np, pair_np = _hgrn_tables()
    sums = jnp.asarray(sums_np, BF16)
    pair = jnp.asarray(pair_np, F32)
    in_specs = []
    for m in (fwd, rev):
        in_specs += [pl.BlockSpec((STEP_ROWS, mixw), lambda b, n, m=m: (m(b, n), 0)),
                     pl.BlockSpec((2 * STEP_ROWS, mixw), lambda b, n, m=m: (m(b, n), 0)),
                     pl.BlockSpec((STEP_ROWS, mixw), lambda b, n, m=m: (m(b, n), 0)),
                     pl.BlockSpec((STEP_ROWS, n_heads * CHUNK), lambda b, n, m=m: (m(b, n), 0)),
                     pl.BlockSpec((STEP_ROWS, LANES), lambda b, n, m=m: (m(b, n), 0))]
    for d, m in enumerate((fwd, rev)):
        in_specs += [pl.BlockSpec((STEP_ROWS, mixw), lambda b, n, m=m: (m(b, n), col_q // mixw)),
                     pl.BlockSpec((STEP_ROWS, mixw), lambda b, n, m=m: (m(b, n), col_i // mixw)),
                     pl.BlockSpec((STEP_ROWS, mixw), lambda b, n, m=m, d=d: (m(b, n), col_f // mixw + d))]
    in_specs += [pl.BlockSpec((1, mixw), lambda b, n: (0, 0)),
                 pl.BlockSpec(sums.shape, lambda b, n: (0, 0, 0)),
                 pl.BlockSpec(pair.shape, lambda b, n: (0, 0, 0, 0))]
    return pl.pallas_call(
        functools.partial(_scan_kernel, n_heads=n_heads),
        grid=(n_batch, nsb),
        in_specs=in_specs,
        out_specs=[pl.BlockSpec((STEP_ROWS, mixw), lambda b, n: (fwd(b, n), 0)),
                   pl.BlockSpec((STEP_ROWS, mixw), lambda b, n: (rev(b, n), 0))] * 2,
        out_shape=[jax.ShapeDtypeStruct((t, mixw), F32)] * 4,
        scratch_shapes=[pltpu.VMEM((2, n_heads, HEAD_DIM, HEAD_DIM), F32),
                        pltpu.VMEM((2, mixw, mixw), BF16),
                        pltpu.VMEM((2, n_heads, HEAD_DIM, HEAD_DIM), F32)],
        compiler_params=_cparams(("arbitrary", "arbitrary")),
        name="recurrent_scans",
    )(uf, wqf, kdf, atf, eg, ur, wqr, kdr, atr, eg, p, p, p, p, p, p, lb, sums, pair)


def _route(logits_t, bias, n_experts):
    per = n_experts // N_EXPERT_GROUPS
    scores = _sigmoid(logits_t)
    sel = scores + bias
    rows_sel = [sel[e:e + 1, :] for e in range(n_experts)]
    rows_sc = [scores[e:e + 1, :] for e in range(n_experts)]

    def top2_sum(vals):
        hi = vals[0]
        lo = jnp.full_like(hi, -jnp.inf)
        for x in vals[1:]:
            lo = jnp.maximum(lo, jnp.minimum(hi, x))
            hi = jnp.maximum(hi, x)
        return hi + lo

    best = jnp.zeros_like(rows_sel[0], dtype=jnp.int32)
    best_score = top2_sum(rows_sel[0:per])
    for g in range(1, N_EXPERT_GROUPS):
        gs = top2_sum(rows_sel[g * per:(g + 1) * per])
        better = gs > best_score
        best = jnp.where(better, g, best)
        best_score = jnp.where(better, gs, best_score)
    in_sel, in_sc = [], []
    for j in range(per):
        a = rows_sel[j]
        c = rows_sc[j]
        for g in range(1, N_EXPERT_GROUPS):
            a = jnp.where(best == g, rows_sel[g * per + j], a)
            c = jnp.where(best == g, rows_sc[g * per + j], c)
        in_sel.append(a)
        in_sc.append(c)

    def first_argmax(vals):
        idx = jnp.zeros_like(best)
        top = vals[0]
        for j in range(1, per):
            better = vals[j] > top
            idx = jnp.where(better, j, idx)
            top = jnp.where(better, vals[j], top)
        return idx

    i1 = first_argmax(in_sel)
    i2 = first_argmax([jnp.where(i1 == j, -jnp.inf, in_sel[j]) for j in range(per)])

    def pick(idx):
        w = in_sc[0]
        for j in range(1, per):
            w = jnp.where(idx == j, in_sc[j], w)
        return w

    w1, w2 = pick(i1), pick(i2)
    tot = w1 + w2
    return best * per + i1, best * per + i2, w1 / tot, w2 / tot


def _outproj_kernel(oaf, oar, ga, odf, odr, gd, yb, yc, na, nd, wo, x_ref, m_ref, lng, lnb, wr, br,
                    x1_ref, h2_ref, ei_ref, ew_ref, *, alpha, mixw, tm, n_experts):
    n_heads = mixw // HEAD_DIM

    def gated(of_ref, or_ref, g_ref, gain_ref):
        parts = []
        for h in range(n_heads):
            c0 = h * HEAD_DIM
            o = of_ref[:, c0:c0 + HEAD_DIM] + or_ref[:, c0:c0 + HEAD_DIM]
            y = o * lax.rsqrt(jnp.mean(o * o, axis=-1, keepdims=True) + EPS) * gain_ref[...]
            parts.append((y * _silu(g_ref[:, c0:c0 + HEAD_DIM])).astype(BF16))
        return jnp.concatenate(parts, axis=1)

    ya = gated(oaf, oar, ga, na)
    yd = gated(odf, odr, gd, nd)
    y = (_dot(ya, wo[0:mixw, :]) + _dot(yb[...], wo[mixw:2 * mixw, :])
         + _dot(yc[...], wo[2 * mixw:3 * mixw, :]) + _dot(yd, wo[3 * mixw:4 * mixw, :]))
    x1 = _ln(alpha * x_ref[...] + m_ref[0, 2:3, :] * y) * lng[...] + lnb[...]
    x1_ref[...] = x1
    h2 = _ln(x1) * (1.0 + m_ref[0, 4:5, :]) + m_ref[0, 3:4, :]
    h2_ref[...] = h2
    h_hi = h2.astype(BF16)
    h_r = h2 - h_hi.astype(F32)
    h_mid = h_r.astype(BF16)
    h_lo = (h_r - h_mid.astype(F32)).astype(BF16)
    pa = _dot(h_hi, wr[:, 0:2 * LANES])
    pb = _dot(h_mid, wr[:, 0:2 * LANES])
    pc = _dot(h_hi, wr[:, 2 * LANES:4 * LANES])
    pd = _dot(h_lo, wr[:, 0:2 * LANES])
    logits = (((pd[:, :LANES] + pb[:, LANES:] + pc[:, :LANES]) + (pb[:, :LANES] + pa[:, LANES:]))
              + pa[:, :LANES])
    logits_t = logits.T[:n_experts]
    e1, e2, w1, w2 = _route(logits_t, br[:, 0:1], n_experts)
    zi = jnp.zeros((SUBLANES - TOP_K, tm), jnp.int32)
    ei_ref[...] = jnp.concatenate([e1, e2, zi], axis=0)
    ew_ref[...] = jnp.concatenate([w1, w2, zi.astype(F32)], axis=0)


def _outproj(oaf, oar, odf, odr, p, yb, yc, norm_a, norm_d, w_out, x, mods, ln_g, ln_b, w_router_p, b_router,
             alpha, tm, nb, nctx, n_batch, mixw, col_ga, col_gd):
    t, d = x.shape
    n_experts = b_router.shape[0]
    row = lambda i: (i, 0)
    const = lambda i: (0, 0)
    kern = functools.partial(_outproj_kernel, alpha=alpha, mixw=mixw, tm=tm, n_experts=n_experts)
    return pl.pallas_call(
        kern,
        grid=(t // tm,),
        in_specs=[pl.BlockSpec((tm, mixw), row), pl.BlockSpec((tm, mixw), row),
                  pl.BlockSpec((tm, mixw), lambda i: (i, col_ga // mixw)),
                  pl.BlockSpec((tm, mixw), row), pl.BlockSpec((tm, mixw), row),
                  pl.BlockSpec((tm, mixw), lambda i: (i, col_gd // mixw)),
                  pl.BlockSpec((tm, mixw), row), pl.BlockSpec((tm, mixw), row),
                  pl.BlockSpec((1, HEAD_DIM), const), pl.BlockSpec((1, HEAD_DIM), const),
                  pl.BlockSpec((N_MIXERS * mixw, d), const),
                  pl.BlockSpec((tm, d), row),
                  pl.BlockSpec((1, 6, d), lambda i: (_mod_row(i, nb, nctx, n_batch), 0, 0)),
                  pl.BlockSpec((1, d), const), pl.BlockSpec((1, d), const),
                  pl.BlockSpec((d, 4 * LANES), const), pl.BlockSpec((n_experts, LANES), const)],
        out_specs=[pl.BlockSpec((tm, d), row), pl.BlockSpec((tm, d), row),
                   pl.BlockSpec((SUBLANES, tm), lambda i: (0, i)), pl.BlockSpec((SUBLANES, tm), lambda i: (0, i))],
        out_shape=[jax.ShapeDtypeStruct((t, d), F32), jax.ShapeDtypeStruct((t, d), F32),
                   jax.ShapeDtypeStruct((SUBLANES, t), jnp.int32), jax.ShapeDtypeStruct((SUBLANES, t), F32)],
        compiler_params=_cparams(("parallel",)),
        name="out_proj_router",
    )(oaf, oar, p, odf, odr, p, yb, yc, norm_a, norm_d, w_out, x, mods, ln_g, ln_b, w_router_p, b_router)


def _rank_kernel(ei_ref, su_ref, rank_ref, cnt_ref, base_ref, *, n_experts, tm):
    i = pl.program_id(0)

    @pl.when(i == 0)
    def _():
        base_ref[...] = jnp.zeros_like(base_ref)

    eid = lax.broadcasted_iota(jnp.int32, (n_experts, tm), 0)
    o1 = (eid == ei_ref[0:1, :]).astype(F32)
    o2 = (eid == ei_ref[1:2, :]).astype(F32)
    cnt = o1 + o2
    before = _dot(cnt.astype(BF16), su_ref[...]) + base_ref[:, 0:1]
    r1 = jnp.sum(o1 * before, axis=0, keepdims=True)
    r2 = jnp.sum(o2 * before, axis=0, keepdims=True)
    rank_ref[...] = jnp.concatenate([r1, r2, jnp.zeros((SUBLANES - TOP_K, tm), F32)], axis=0).astype(jnp.int32)
    base_ref[...] = base_ref[...] + jnp.sum(cnt, axis=1, keepdims=True)
    cnt_ref[...] = base_ref[...]


def _slot_ranks(ei, n_experts, tm):
    t = ei.shape[1]
    r = np.arange(tm)
    su = jnp.asarray((r[:, None] < r[None, :]).astype(np.float32), BF16)
    return pl.pallas_call(
        functools.partial(_rank_kernel, n_experts=n_experts, tm=tm),
        grid=(t // tm,),
        in_specs=[pl.BlockSpec((SUBLANES, tm), lambda i: (0, i)), pl.BlockSpec((tm, tm), lambda i: (0, 0))],
        out_specs=[pl.BlockSpec((SUBLANES, tm), lambda i: (0, i)), pl.BlockSpec((n_experts, LANES), lambda i: (0, 0))],
        out_shape=[jax.ShapeDtypeStruct((SUBLANES, t), jnp.int32), jax.ShapeDtypeStruct((n_experts, LANES), F32)],
        scratch_shapes=[pltpu.VMEM((n_experts, LANES), F32)],
        compiler_params=_cparams(("arbitrary",)),
        name="moe_slot_ranks",
    )(ei, su)


def _row_copy(src_ref, src_row, dst_ref, dst_row, sem):
    return pltpu.make_async_copy(src_ref.at[pl.ds(src_row, 1)], dst_ref.at[pl.ds(dst_row, 1)], sem)


def _dispatch_kernel(dest_ref, zs_ref, na_ref, h_ref, xs_ref, zero_ref, sem, zsem, *, tm, t, n_experts, tmx, n_blocks):
    base = pl.program_id(0) * tm

    @pl.when(pl.program_id(0) == 0)
    def _():
        zero_ref[...] = jnp.zeros_like(zero_ref)
        for e in range(n_experts):
            fill = pltpu.make_async_copy(zero_ref, xs_ref.at[pl.ds(pl.multiple_of(zs_ref[e], FILL_ALIGN),
                                                                  tmx + FILL_ALIGN)], zsem)
            fill.start()
            fill.wait()

        def tail(b, carry):
            fill = pltpu.make_async_copy(zero_ref.at[pl.ds(0, tmx)],
                                         xs_ref.at[pl.ds(pl.multiple_of(b * tmx, tmx), tmx)], zsem)
            fill.start()
            fill.wait()
            return carry

        lax.fori_loop(na_ref[0], n_blocks, tail, 0)

    for r in range(tm):
        for kk in range(TOP_K):
            _row_copy(h_ref, r, xs_ref, dest_ref[kk * t + base + r], sem).start()
    for kk in range(TOP_K):
        pltpu.make_async_copy(h_ref, xs_ref.at[pl.ds(0, tm)], sem).wait()


def _dispatch(dest, zero_start, n_active, h2, n_slots, tm, tmx):
    t, d = h2.shape
    n_experts = zero_start.shape[0]
    return pl.pallas_call(
        functools.partial(_dispatch_kernel, tm=tm, t=t, n_experts=n_experts, tmx=tmx, n_blocks=n_slots // tmx),
        grid_spec=pltpu.PrefetchScalarGridSpec(
            num_scalar_prefetch=3,
            grid=(t // tm,),
            in_specs=[pl.BlockSpec((tm, d), lambda i, dest, zs, na: (i, 0))],
            out_specs=pl.BlockSpec(memory_space=pl.ANY),
            scratch_shapes=[pltpu.VMEM((tmx + FILL_ALIGN, d), h2.dtype), pltpu.SemaphoreType.DMA(()),
                            pltpu.SemaphoreType.DMA(())]),
        out_shape=jax.ShapeDtypeStruct((n_slots, d), h2.dtype),
        compiler_params=_cparams(("arbitrary",)),
        name="moe_dispatch",
    )(dest, zero_start, n_active, h2)


def _expert_kernel(be_ref, na_ref, x_ref, w1_ref, w3_ref, w2_ref, y_ref):
    i = pl.program_id(0)

    @pl.when(i < na_ref[0])
    def _():
        xb = x_ref[...].astype(BF16)
        h1 = _dot(xb, w1_ref[0, 0])
        h3 = _dot(xb, w3_ref[0, 0])
        y_ref[...] = _dot((_silu(h1) * h3).astype(BF16), w2_ref[0, 0])

    @pl.when(i >= na_ref[0])
    def _():
        y_ref[...] = jnp.zeros_like(y_ref)


def _experts(block_expert, n_active, xs, w1, w3, w2, layer, tmx):
    n_slots, d = xs.shape
    f = w1.shape[3]
    return pl.pallas_call(
        _expert_kernel,
        grid_spec=pltpu.PrefetchScalarGridSpec(
            num_scalar_prefetch=2,
            grid=(n_slots // tmx,),
            in_specs=[pl.BlockSpec((tmx, d), lambda i, be, na: (jnp.minimum(i, na[0] - 1), 0)),
                      pl.BlockSpec((1, 1, d, f), lambda i, be, na: (layer, be[i], 0, 0)),
                      pl.BlockSpec((1, 1, d, f), lambda i, be, na: (layer, be[i], 0, 0)),
                      pl.BlockSpec((1, 1, f, d), lambda i, be, na: (layer, be[i], 0, 0))],
            out_specs=pl.BlockSpec((tmx, d), lambda i, be, na: (i, 0))),
        out_shape=jax.ShapeDtypeStruct((n_slots, d), F32),
        compiler_params=_cparams(("arbitrary",)),
        name="moe_experts",
    )(block_expert, n_active, xs, w1, w3, w2)


def _combine_kernel(dest_ref, ys_ref, ew_ref, x1_ref, m_ref, lng, lnb, mn_ref, x2_ref, *rest,
                    tm, t, alpha, block_of_step, emit_next):
    hn_ref = rest[0] if emit_next else None
    g_ref, sem = rest[-2:]
    i = pl.program_id(0)
    n = pl.num_programs(0)

    def issue(step, slot):
        base = block_of_step(step) * tm

        for r in range(tm):
            for kk in range(TOP_K):
                _row_copy(ys_ref, dest_ref[kk * t + base + r], g_ref.at[slot, kk], r, sem.at[slot]).start()

    @pl.when(i == 0)
    def _():
        issue(0, 0)

    @pl.when(i + 1 < n)
    def _():
        issue(i + 1, (i + 1) % 2)

    slot = i % 2
    for kk in range(TOP_K):
        pltpu.make_async_copy(ys_ref.at[pl.ds(0, tm)], g_ref.at[slot, kk], sem.at[slot]).wait()
    row = lax.broadcasted_iota(jnp.int32, (tm, tm), 0)
    col = lax.broadcasted_iota(jnp.int32, (tm, tm), 1)
    w_t = _dot_nt((row == col).astype(F32), ew_ref[...], HIGHEST)
    f = w_t[:, 0:1] * g_ref[slot, 0] + w_t[:, 1:2] * g_ref[slot, 1]
    x2 = _ln(alpha * x1_ref[...] + m_ref[0, 5:6, :] * f) * lng[...] + lnb[...]
    x2_ref[...] = x2
    if emit_next:
        hn_ref[...] = (_ln(x2) * (1.0 + mn_ref[0, 1:2, :]) + mn_ref[0, 0:1, :]).astype(hn_ref.dtype)


def _combine(dest, ys, ew, x1, mods, ln_g, ln_b, mods_next, alpha, tm, nb, nctx, n_batch, latent_only):
    t, d = x1.shape
    nlat = nb - nctx
    if latent_only:
        n_steps = n_batch * nlat
        block_of_step = lambda i: (i // nlat) * nb + nctx + i % nlat
    else:
        n_steps = t // tm
        block_of_step = lambda i: i
    mrow = lambda i, dest: (_mod_row(block_of_step(i), nb, nctx, n_batch), 0, 0)
    const = lambda i, dest: (0, 0)
    out_specs = [pl.BlockSpec((tm, d), lambda i, dest: (i, 0))]
    out_shape = [jax.ShapeDtypeStruct((n_steps * tm, d), F32)]
    if not latent_only:
        out_specs.append(pl.BlockSpec((tm, d), lambda i, dest: (i, 0)))
        out_shape.append(jax.ShapeDtypeStruct((t, d), BF16))
    return pl.pallas_call(
        functools.partial(_combine_kernel, tm=tm, t=t, alpha=alpha, block_of_step=block_of_step,
                          emit_next=not latent_only),
        grid_spec=pltpu.PrefetchScalarGridSpec(
            num_scalar_prefetch=1,
            grid=(n_steps,),
            in_specs=[pl.BlockSpec(memory_space=pl.ANY),
                      pl.BlockSpec((SUBLANES, tm), lambda i, dest: (0, block_of_step(i))),
                      pl.BlockSpec((tm, d), lambda i, dest: (block_of_step(i), 0)),
                      pl.BlockSpec((1, 6, d), mrow),
                      pl.BlockSpec((1, d), const), pl.BlockSpec((1, d), const),
                      pl.BlockSpec((1, 6, d), mrow)],
            out_specs=out_specs,
            scratch_shapes=[pltpu.VMEM((2, TOP_K, tm, d), F32), pltpu.SemaphoreType.DMA((2,))]),
        out_shape=out_shape,
        compiler_params=_cparams(("arbitrary",)),
        name="moe_combine",
    )(dest, ys, ew, x1, mods, ln_g, ln_b, mods_next)


def _axial_tables(seq, ctx_len):
    rows = seq // GRID_W
    row = jnp.repeat(jnp.arange(rows, dtype=F32), GRID_W)
    col = jnp.tile(jnp.arange(GRID_W, dtype=F32), rows)
    n_freq = HEAD_DIM // 4
    inv_freq = ROPE_THETA ** (-jnp.arange(n_freq, dtype=F32) / n_freq)
    ang = jnp.concatenate([row[:, None] * inv_freq, col[:, None] * inv_freq], axis=-1)
    cos, sin = jnp.cos(ang), jnp.sin(ang)
    cosf = jnp.concatenate([cos, cos], axis=-1)
    sinf = jnp.concatenate([-sin, sin], axis=-1)
    cosf = jnp.concatenate([jnp.ones((ctx_len, HEAD_DIM), F32), cosf], axis=0)
    sinf = jnp.concatenate([jnp.zeros((ctx_len, HEAD_DIM), F32), sinf], axis=0)
    return cosf, sinf


def kernel(x, c, ctx, c_ctx, w_ada, b_ada, ln_g, ln_b, w_in, conv_a, a_log, dt_bias, norm_a, qk_norm_b, ws_c, bs_c, lb_d, norm_d, w_out, w_router, b_router, w1, w3, w2):
    n_batch, seq, d = x.shape
    ctx_len = ctx.shape[1]
    n_layers = w_in.shape[0]
    n_experts = w_router.shape[1]
    mixw = d // N_MIXERS
    n_heads = mixw // HEAD_DIM
    tb = ctx_len + seq
    t = n_batch * tb
    alpha = (2.0 * n_layers) ** 0.25
    tm = _pick(math.gcd(ctx_len, seq), (256, 128))
    nb, nctx = tb // tm, ctx_len // tm

    col = {"a_qkv": 0, "a_gate": 3 * mixw, "d_f": 4 * mixw, "b_q": 6 * mixw, "b_kv": 7 * mixw, "c_u": 8 * mixw,
           "c_v": 9 * mixw, "d_q": 10 * mixw, "d_i": 11 * mixw, "d_gate": 12 * mixw, "small": 13 * mixw}
    n_proj = 13 * mixw + 256
    src = np.cumsum([0, 3 * mixw, mixw, 2 * n_heads, 2 * n_heads, mixw, mixw, mixw, mixw, mixw, mixw, mixw, 2 * mixw])
    s_qkv, s_ga, s_beta, s_dec, s_bq, s_bkv, s_cu, s_cv, s_dq, s_di, s_dg, s_df, s_end = [int(v) for v in src]

    def permute_w_in(w):
        pad = jnp.zeros((d, n_proj - 13 * mixw - 4 * n_heads), w.dtype)
        return jnp.concatenate([w[:, s_qkv:s_beta], w[:, s_df:s_end], w[:, s_bq:s_df], w[:, s_beta:s_bq], pad],
                               axis=1).astype(BF16)

    assert n_batch + 1 <= SUBLANES
    cc = jnp.zeros((SUBLANES, d), F32).at[:n_batch].set(c).at[n_batch].set(c_ctx)
    mods_all = _mod_vectors(cc, w_ada, b_ada)[:, :n_batch + 1].reshape(n_layers, n_batch + 1, 6, d)

    soft = jax.nn.softmax(lb_d.astype(F32), axis=0)
    lb_all = jnp.cumsum(soft, axis=0) - soft[0]
    cosf, sinf = _axial_tables(seq, ctx_len)
    gate_lanes = jnp.zeros((1, LANES), F32)
    wr_f = w_router.astype(F32)
    wr_hi = wr_f.astype(BF16)
    wr_r = wr_f - wr_hi.astype(F32)
    wr_mid = wr_r.astype(BF16)
    wr_lo = (wr_r - wr_mid.astype(F32)).astype(BF16)
    w_router_p = jnp.zeros((d, 4 * LANES), BF16)
    for blk, part in enumerate((wr_hi, wr_mid, wr_lo)):
        w_router_p = w_router_p.at[:, blk * LANES:blk * LANES + n_experts].set(part)
    b_router_c = jnp.broadcast_to(b_router.astype(F32)[:, None], (n_experts, LANES))

    tmx = EXPERT_BLOCK_ROWS
    n_assign = t * TOP_K
    n_slots = (-(-n_assign // tmx) + n_experts + 2) * tmx

    w1_b, w3_b, w2_b = w1.astype(BF16), w3.astype(BF16), w2.astype(BF16)
    xu = jnp.concatenate([ctx, x], axis=1).reshape(t, d)
    h = _ln_modulate(xu, mods_all[0], tm, nb, nctx, n_batch)
    for l in range(n_layers):
        mods = mods_all[l]
        p = _matmul(h, permute_w_in(w_in[l]))
        neg_a = gate_lanes.at[0, 2 * n_heads:4 * n_heads].set(-jnp.exp(a_log[l].astype(F32)).reshape(-1))
        dtb = gate_lanes.at[0, 2 * n_heads:4 * n_heads].set(dt_bias[l].astype(F32).reshape(-1))
        prep_a = _a_prep(p, conv_a[l], neg_a, dtb, tm, nb, nctx, mixw, col["small"])
        oaf, oar, odf, odr = _recurrent_scans(prep_a, p, lb_all[l][None, :], n_batch, tb, ctx_len, mixw,
                                              col["d_q"], col["d_i"], col["d_f"])
        qb, kb, vb = _b_prep(p, qk_norm_b[l], cosf, sinf, tm, nb, mixw, col["b_q"], col["b_kv"])
        yb = _attention(qb, kb, vb, n_batch, tb, ctx_len, mixw)
        bs_t = jnp.zeros((CHUNK_C, LANES), F32).at[:, :bs_c.shape[1]].set(bs_c[l].T)
        yc = _mixer_c(p, ws_c[l].astype(BF16), bs_t, tm, mixw, col["c_u"], col["c_v"])
        x1, h2, ei, ew = _outproj(oaf, oar, odf, odr, p, yb, yc, norm_a[l][None, :], norm_d[l][None, :],
                                  w_out[l].astype(BF16), xu, mods, ln_g[l, 0][None, :], ln_b[l, 0][None, :],
                                  w_router_p, b_router_c, alpha, tm, nb, nctx, n_batch, mixw,
                                  col["a_gate"], col["d_gate"])
        rank, counts = _slot_ranks(ei, n_experts, tm)
        cnt = counts[:, 0].astype(jnp.int32)
        padded = (cnt + tmx - 1) // tmx * tmx
        pad_end = jnp.cumsum(padded)
        pad_start = pad_end - padded
        eids = jnp.arange(n_experts, dtype=jnp.int32)[:, None, None]
        dest = (rank[:TOP_K] + jnp.sum(jnp.where(ei[None, :TOP_K] == eids, pad_start[:, None, None], 0),
                                       axis=0)).reshape(-1)
        n_blocks = n_slots // tmx
        block_start = jnp.arange(n_blocks, dtype=jnp.int32) * tmx
        block_expert = jnp.minimum(jnp.sum((pad_end[None, :] <= block_start[:, None]).astype(jnp.int32), axis=1),
                                   n_experts - 1)
        n_active = (pad_end[-1:] // tmx).astype(jnp.int32)
        fill_start = (pad_start + cnt) // FILL_ALIGN * FILL_ALIGN
        xs = _dispatch(dest, fill_start, n_active, h2, n_slots, _pick(t, (2 * tm, tm)), tmx)
        ys = _experts(block_expert, n_active, xs, w1_b, w3_b, w2_b, l, tmx)
        last = l == n_layers - 1
        res = _combine(dest, ys, ew, x1, mods, ln_g[l, 1][None, :], ln_b[l, 1][None, :],
                       mods_all[min(l + 1, n_layers - 1)], alpha, tm, nb, nctx, n_batch, latent_only=last)
        if last:
            return res[0].reshape(n_batch, seq, d)
        xu, h = res
```
